```python
import jax, jax.numpy as jnp
from jax import lax
import numpy as np

D_MODEL = 1024
BATCH = 8
SEQ = 8192
DEPTH = 2

HEAD_DIM = 64
N_Q_HEADS = 8
N_KV_HEADS = 2
Q_PER_KV = N_Q_HEADS // N_KV_HEADS
ATTN_WIDTH = N_Q_HEADS * HEAD_DIM
KV_WIDTH = N_KV_HEADS * HEAD_DIM
WINDOW = 128
ATTN_BLOCK = WINDOW
ROT_DIM = HEAD_DIM // 4
ROPE_THETA = 500000.0
POOL_WINDOWS = (2, 4, 8, 16)
N_POOL_GROUPS = len(POOL_WINDOWS)
POOL_WIDTH = D_MODEL - ATTN_WIDTH
POOL_GROUP_DIM = POOL_WIDTH // N_POOL_GROUPS
MIX_WIDTH = ATTN_WIDTH + POOL_WIDTH
IN_WIDTH = ATTN_WIDTH + 2 * KV_WIDTH + POOL_WIDTH
N_EXPERTS = 32
TOP_K = 4
D_FF = D_MODEL
SWIGLU_LIMIT = 7.0
SWIGLU_ALPHA = 1.702
MOE_BLOCK = 256
EPS = 1e-6
MAX_POS_OFFSET = 4096

kernel_name = "hybrid_swa_sink_pool_moe_adaln"

F32 = jnp.float32


def rmsnorm(t, g):
    tf = t.astype(F32)
    y = tf * lax.rsqrt(jnp.mean(tf * tf, axis=-1, keepdims=True) + EPS)
    return (y * g.astype(F32)).astype(t.dtype)


def partial_rope(t, cos, sin):
    half = ROT_DIM // 2
    tf = t.astype(F32)
    t1 = tf[..., :half]
    t2 = tf[..., half:ROT_DIM]
    out = jnp.concatenate([t1 * cos - t2 * sin, t2 * cos + t1 * sin, tf[..., ROT_DIM:]], axis=-1)
    return out.astype(t.dtype)


def sliding_window_attention(q, k, v, sinks):
    b, s_len = q.shape[0], q.shape[1]
    T = ATTN_BLOCK
    nb = s_len // T
    qb = q.reshape(b, nb, T, N_KV_HEADS, Q_PER_KV, HEAD_DIM)

    def band(t):
        tb = t.reshape(b, nb, T, N_KV_HEADS, HEAD_DIM)
        prev = jnp.pad(tb[:, :-1], ((0, 0), (1, 0), (0, 0), (0, 0), (0, 0)))
        return jnp.concatenate([prev, tb], axis=2)

    kk, vv = band(k), band(v)
    scores = jnp.einsum('bnqhgd,bnkhd->bnhgqk', qb, kk,
                        preferred_element_type=F32) * (HEAD_DIM ** -0.5)
    qi = jnp.arange(T)[:, None]
    kj = jnp.arange(2 * T)[None, :]
    diff = qi + T - kj
    blk = jnp.arange(nb)[:, None, None]
    allowed = (diff >= 0) & (diff < WINDOW) & ((blk > 0) | (kj >= T))
    scores = jnp.where(allowed[None, :, None, None], scores, -jnp.inf)
    sink = sinks.astype(F32).reshape(1, 1, N_KV_HEADS, Q_PER_KV, 1, 1)
    m = jnp.maximum(jnp.max(scores, axis=-1, keepdims=True), sink)
    p = jnp.exp(scores - m)
    denom = jnp.sum(p, axis=-1, keepdims=True) + jnp.exp(sink - m)
    o = jnp.einsum('bnhgqk,bnkhd->bnhgqd', p.astype(v.dtype), vv,
                   preferred_element_type=F32) / denom
    o = jnp.transpose(o, (0, 1, 4, 2, 3, 5))
    return o.reshape(b, s_len, ATTN_WIDTH).astype(q.dtype)


def multiscale_pool(u, w_pool, b_pool, scale):
    b, s_len = u.shape[0], u.shape[1]
    ug = u.astype(F32).reshape(b, s_len, N_POOL_GROUPS, POOL_GROUP_DIM)
    cs = jnp.cumsum(ug, axis=1)
    count = jnp.arange(1, s_len + 1, dtype=F32)
    outs = []
    for g, w in enumerate(POOL_WINDOWS):
        c_g = cs[:, :, g]
        lag = jnp.pad(c_g, ((0, 0), (w, 0), (0, 0)))[:, :s_len]
        mean = (c_g - lag) / jnp.minimum(count, float(w))[None, :, None]
        outs.append(mean - ug[:, :, g])
    pooled = jnp.stack(outs, axis=2).astype(u.dtype)
    y = jnp.einsum('bsgc,gcd->bsgd', pooled, w_pool) + b_pool
    y = y * scale.reshape(N_POOL_GROUPS, POOL_GROUP_DIM)
    return y.reshape(b, s_len, POOL_WIDTH)


def moe(h, router_w, router_b, w_gu, b_gu, w_down, b_down):
    b, s_len, d = h.shape
    n_tok = b * s_len
    hf = h.reshape(n_tok, d)
    logits = (hf @ router_w).astype(F32) + router_b.astype(F32)
    top_vals, top_idx = lax.top_k(logits, TOP_K)
    gates = jax.nn.softmax(top_vals, axis=-1)
    n_asg = n_tok * TOP_K
    e_flat = top_idx.reshape(n_asg)
    tok_flat = jnp.arange(n_asg, dtype=jnp.int32) // TOP_K
    g_flat = gates.reshape(n_asg)
    order = jnp.argsort(e_flat)
    e_sorted = e_flat[order]
    counts = jnp.bincount(e_flat, length=N_EXPERTS)
    padded = (counts + MOE_BLOCK - 1) // MOE_BLOCK * MOE_BLOCK
    starts = jnp.cumsum(counts) - counts
    pends = jnp.cumsum(padded)
    pstarts = pends - padded
    rank = jnp.arange(n_asg, dtype=jnp.int32) - starts[e_sorted]
    dest = pstarts[e_sorted] + rank
    n_blocks = -(-n_asg // MOE_BLOCK) + N_EXPERTS
    n_slots = n_blocks * MOE_BLOCK
    slot_tok = jnp.zeros((n_slots,), jnp.int32).at[dest].set(tok_flat[order])
    slot_gate = jnp.zeros((n_slots,), F32).at[dest].set(g_flat[order])
    block_expert = jnp.minimum(
        jnp.searchsorted(pends, jnp.arange(n_blocks) * MOE_BLOCK, side='right'),
        N_EXPERTS - 1).astype(jnp.int32)

    def expert_block(args):
        e, tok, gate = args
        xb = hf[tok]
        gu = xb @ w_gu[e] + b_gu[e]
        g_lin = jnp.minimum(gu[:, :D_FF], SWIGLU_LIMIT)
        u_lin = jnp.clip(gu[:, D_FF:], -SWIGLU_LIMIT, SWIGLU_LIMIT)
        act = g_lin * jax.nn.sigmoid(g_lin * SWIGLU_ALPHA) * (u_lin + 1.0)
        yb = act @ w_down[e] + b_down[e]
        return yb * gate[:, None].astype(yb.dtype)

    yb = lax.map(expert_block, (block_expert,
                                slot_tok.reshape(n_blocks, MOE_BLOCK),
                                slot_gate.reshape(n_blocks, MOE_BLOCK)))
    y = jnp.zeros((n_tok, d), yb.dtype).at[slot_tok].add(yb.reshape(n_slots, d))
    return y.reshape(b, s_len, d).astype(h.dtype)


def setup_inputs(seed: int = 0) -> dict:
    key = jax.random.key(seed)
    ks = jax.random.split(key, 22)
    nrm = lambda k, shape, s: jax.random.normal(k, shape, F32) * s
    L, D, E, F = DEPTH, D_MODEL, N_EXPERTS, D_FF
    x = jax.random.normal(ks[0], (BATCH, SEQ, D), F32)
    c = jax.random.normal(ks[1], (BATCH, D), F32)
    positions = (jnp.arange(SEQ, dtype=jnp.int32)[None, :]
                 + jax.random.randint(ks[2], (BATCH, 1), 0, MAX_POS_OFFSET, dtype=jnp.int32))
    return {
        "x": x,
        "c": c,
        "positions": positions,
        "ada_w": nrm(ks[3], (L, D, 6 * D), 0.5 * D ** -0.5),
        "ada_b": nrm(ks[4], (L, 6 * D), 0.02),
        "norm1_g": 1.0 + nrm(ks[5], (L, D), 0.02),
        "w_in": nrm(ks[6], (L, D, IN_WIDTH), D ** -0.5),
        "q_norm_g": 1.0 + nrm(ks[7], (L, HEAD_DIM), 0.02),
        "k_norm_g": 1.0 + nrm(ks[8], (L, HEAD_DIM), 0.02),
        "attn_sinks": nrm(ks[9], (L, N_Q_HEADS), 0.5),
        "pool_w": nrm(ks[10], (L, N_POOL_GROUPS, POOL_GROUP_DIM, POOL_GROUP_DIM), POOL_GROUP_DIM ** -0.5),
        "pool_b": nrm(ks[11], (L, N_POOL_GROUPS, POOL_GROUP_DIM), 0.02),
        "pool_scale": 1.0 + nrm(ks[12], (L, POOL_WIDTH), 0.05),
        "w_out": nrm(ks[13], (L, MIX_WIDTH, D), MIX_WIDTH ** -0.5),
        "norm2_g": 1.0 + nrm(ks[14], (L, D), 0.02),
        "router_w": nrm(ks[15], (L, D, E), D ** -0.5),
        "router_b": nrm(ks[16], (L, E), 0.01),
        "expert_w_gu": nrm(ks[17], (L, E, D, 2 * F), D ** -0.5),
        "expert_b_gu": nrm(ks[18], (L, E, 2 * F), 0.02),
        "expert_w_down": nrm(ks[19], (L, E, F, D), F ** -0.5),
        "expert_b_down": nrm(ks[20], (L, E, D), 0.02),
    }


def reference(x, c, positions, ada_w, ada_b, norm1_g, w_in, q_norm_g, k_norm_g, attn_sinks,
              pool_w, pool_b, pool_scale, w_out, norm2_g, router_w, router_b,
              expert_w_gu, expert_b_gu, expert_w_down, expert_b_down):
    b, s_len, _ = x.shape
    inv_freq = ROPE_THETA ** (-jnp.arange(0, ROT_DIM, 2, dtype=F32) / ROT_DIM)
    ang = positions.astype(F32)[..., None] * inv_freq
    cos = jnp.cos(ang)[:, :, None, :]
    sin = jnp.sin(ang)[:, :, None, :]
    c_act = jax.nn.silu(c.astype(F32))
    for l in range(DEPTH):
        mod = (c_act @ ada_w[l].astype(F32) + ada_b[l].astype(F32)).astype(x.dtype)
        shift1, scale1, gate1, shift2, scale2, gate2 = jnp.split(mod[:, None, :], 6, axis=-1)

        h = rmsnorm(x, norm1_g[l]) * (1.0 + scale1) + shift1
        proj = h @ w_in[l]
        q, k, v, u = jnp.split(proj, [ATTN_WIDTH, ATTN_WIDTH + KV_WIDTH,
                                      ATTN_WIDTH + 2 * KV_WIDTH], axis=-1)
        q = q.reshape(b, s_len, N_Q_HEADS, HEAD_DIM)
        k = k.reshape(b, s_len, N_KV_HEADS, HEAD_DIM)
        v = v.reshape(b, s_len, N_KV_HEADS, HEAD_DIM)
        q = partial_rope(rmsnorm(q, q_norm_g[l]), cos, sin)
        k = partial_rope(rmsnorm(k, k_norm_g[l]), cos, sin)
        attn_out = sliding_window_attention(q, k, v, attn_sinks[l])
        pool_out = multiscale_pool(u, pool_w[l], pool_b[l], pool_scale[l])
        mixed = jnp.concatenate([attn_out, pool_out.astype(attn_out.dtype)], axis=-1) @ w_out[l]
        x = x + gate1 * mixed

        h2 = rmsnorm(x, norm2_g[l]) * (1.0 + scale2) + shift2
        x = x + gate2 * moe(h2, router_w[l], router_b[l], expert_w_gu[l], expert_b_gu[l],
                            expert_w_down[l], expert_b_down[l])
    return x
```

```python
import functools

import numpy as np
import jax
import jax.numpy as jnp
from jax import lax
from jax.experimental import pallas as pl
from jax.experimental.pallas import tpu as pltpu

F32 = jnp.float32
BF16 = jnp.bfloat16
I32 = jnp.int32

HEAD_DIM = 64
N_Q_HEADS = 8
N_KV_HEADS = 2
Q_PER_KV = N_Q_HEADS // N_KV_HEADS
ATTN_WIDTH = N_Q_HEADS * HEAD_DIM
KV_WIDTH = N_KV_HEADS * HEAD_DIM
WINDOW = 128
ROT_DIM = HEAD_DIM // 4
ROT_HALF = ROT_DIM // 2
ROPE_THETA = 500000.0
POOL_WINDOWS = (2, 4, 8, 16)
N_POOL_GROUPS = len(POOL_WINDOWS)
POOL_GROUP_DIM = 128
POOL_WIDTH = N_POOL_GROUPS * POOL_GROUP_DIM
MAX_POOL_WINDOW = max(POOL_WINDOWS)
N_EXPERTS = 32
TOP_K = 4
SWIGLU_LIMIT = 7.0
SWIGLU_ALPHA = 1.702
EPS = 1e-6

LANES = 128
VMEM_LIMIT_BYTES = 56 * 1024 * 1024

TOKEN_TILE = 512
ROW_DMA_TILE = 256
EXPERT_BLOCK = 512
FF_CHUNK = 512


def _params(*sem):
    return pltpu.CompilerParams(dimension_semantics=sem, vmem_limit_bytes=VMEM_LIMIT_BYTES)


def _ada_kernel(c_ref, w_ref, b_ref, o_ref):
    c = c_ref[...]
    c_act = c * (1.0 / (1.0 + jnp.exp(-c)))
    o_ref[0] = jnp.dot(c_act, w_ref[0], preferred_element_type=F32,
                       precision=lax.Precision.HIGHEST) + b_ref[0]


def _ada(c, ada_w, ada_b):
    n_layers, d, d6 = ada_w.shape
    b = c.shape[0]
    tn = 1536
    return pl.pallas_call(
        _ada_kernel,
        out_shape=jax.ShapeDtypeStruct((n_layers, b, d6), F32),
        grid=(n_layers, d6 // tn),
        in_specs=[pl.BlockSpec((b, d), lambda l, j: (0, 0)),
                  pl.BlockSpec((1, d, tn), lambda l, j: (l, 0, j)),
                  pl.BlockSpec((1, 1, tn), lambda l, j: (l, 0, j))],
        out_specs=pl.BlockSpec((1, b, tn), lambda l, j: (l, 0, j)),
        compiler_params=_params("arbitrary", "arbitrary"),
        name="ada",
    )(c, ada_w, ada_b.reshape(n_layers, 1, d6))


def _rope_kernel(pos_ref, freq_ref, ch_ref, cl_ref, sh_ref, sl_ref):
    ang = pos_ref[...].astype(F32) * freq_ref[...]
    for fn, hi_ref, lo_ref in ((jnp.cos, ch_ref, cl_ref), (jnp.sin, sh_ref, sl_ref)):
        val = fn(ang)
        hi = val.astype(BF16)
        hi_ref[...] = hi
        lo_ref[...] = (val - hi.astype(F32)).astype(BF16)


def _rope_tables(positions):
    n_tok = positions.size
    inv_freq = ROPE_THETA ** (-jnp.arange(0, ROT_DIM, 2, dtype=F32) / ROT_DIM)
    rows = n_tok * ROT_HALF // LANES
    pos_rep = jnp.repeat(positions.reshape(-1), ROT_HALF).reshape(rows, LANES)
    freq = jnp.tile(inv_freq, LANES // ROT_HALF).reshape(1, LANES)
    tr = min(rows, 512)
    spec = pl.BlockSpec((tr, LANES), lambda i: (i, 0))
    outs = pl.pallas_call(
        _rope_kernel,
        out_shape=[jax.ShapeDtypeStruct((rows, LANES), BF16)] * 4,
        grid=(rows // tr,),
        in_specs=[spec, pl.BlockSpec((1, LANES), lambda i: (0, 0))],
        out_specs=[spec] * 4,
        compiler_params=_params("arbitrary"),
        name="rope",
    )(pos_rep, freq)
    return jnp.concatenate([o.reshape(n_tok, ROT_HALF) for o in outs], axis=1)


def _rope_select_matrix():
    sel = np.zeros((4 * ROT_HALF, 3 * LANES), np.float32)
    for d in range(LANES):
        dd = d % HEAD_DIM
        if dd < ROT_DIM:
            f = dd % ROT_HALF
            sel[f, d] = 1.0
            sel[ROT_HALF + f, d] = 1.0
        if dd < ROT_HALF:
            sel[2 * ROT_HALF + dd, LANES + d] = -1.0
            sel[3 * ROT_HALF + dd, LANES + d] = -1.0
        elif dd < ROT_DIM:
            sel[2 * ROT_HALF + dd - ROT_HALF, 2 * LANES + d] = 1.0
            sel[3 * ROT_HALF + dd - ROT_HALF, 2 * LANES + d] = 1.0
    unrotated = np.array([[1.0 if (d % HEAD_DIM) >= ROT_DIM else 0.0 for d in range(LANES)]], np.float32)
    return jnp.asarray(sel, BF16), jnp.asarray(unrotated, F32)


def _head_mean_matrix(width):
    m = np.zeros((width, width), np.float32)
    for h in range(width // HEAD_DIM):
        m[h * HEAD_DIM:(h + 1) * HEAD_DIM, h * HEAD_DIM:(h + 1) * HEAD_DIM] = 1.0 / HEAD_DIM
    return jnp.asarray(m, BF16)


def _inproj_kernel(x_ref, shift_ref, scale_ref, g1_ref, w_ref, qg_ref, kg_ref, rope_ref, sel_ref,
                   unrot_ref, bdq_ref, bdk_ref, wp_ref, bp_ref, ps_ref,
                   q_ref, k_ref, v_ref, pool_ref, carry_ref, *, ts):
    s_idx = pl.program_id(1)

    @pl.when(s_idx == 0)
    def _():
        carry_ref[...] = jnp.zeros_like(carry_ref)

    x = x_ref[...]
    ms = jnp.mean(x * x, axis=-1, keepdims=True)
    h = x * lax.rsqrt(ms + EPS) * (g1_ref[...] * (1.0 + scale_ref[...])) + shift_ref[...]
    hb = h.astype(BF16)

    tab = jnp.dot(rope_ref[...], sel_ref[...], preferred_element_type=F32)
    cos_t = tab[:, :LANES] + unrot_ref[...]
    sin_a = tab[:, LANES:2 * LANES]
    sin_b = tab[:, 2 * LANES:]

    def norm_rope(t, bd_ref, g_ref, out_scale):
        msq = jnp.dot((t * t).astype(BF16), bd_ref[...], preferred_element_type=F32)
        tn = t * lax.rsqrt(msq + EPS) * g_ref[...]
        chunks = []
        for c in range(t.shape[1] // LANES):
            tc = tn[:, c * LANES:(c + 1) * LANES]
            rot = (tc * cos_t + pltpu.roll(tc, LANES - ROT_HALF, 1) * sin_a
                   + pltpu.roll(tc, ROT_HALF, 1) * sin_b)
            chunks.append(rot * out_scale)
        return chunks[0] if len(chunks) == 1 else jnp.concatenate(chunks, axis=1)

    q = jnp.dot(hb, w_ref[:, :ATTN_WIDTH], preferred_element_type=F32)
    q_ref[...] = norm_rope(q, bdq_ref, qg_ref, HEAD_DIM ** -0.5).astype(BF16)
    k = jnp.dot(hb, w_ref[:, ATTN_WIDTH:ATTN_WIDTH + KV_WIDTH], preferred_element_type=F32)
    k_ref[...] = norm_rope(k, bdk_ref, kg_ref, 1.0).astype(BF16)
    v = jnp.dot(hb, w_ref[:, ATTN_WIDTH + KV_WIDTH:ATTN_WIDTH + 2 * KV_WIDTH], preferred_element_type=F32)
    v_ref[...] = v.astype(BF16)

    u = jnp.dot(hb, w_ref[:, ATTN_WIDTH + 2 * KV_WIDTH:], preferred_element_type=F32)
    ext = jnp.concatenate([carry_ref[...], u], axis=0)
    carry_ref[...] = u[ts - MAX_POOL_WINDOW:, :]
    count = (lax.broadcasted_iota(I32, (ts, POOL_GROUP_DIM), 0) + (s_idx * ts + 1)).astype(F32)
    outs = []
    for g, w in enumerate(POOL_WINDOWS):
        win = ext[:, g * POOL_GROUP_DIM:(g + 1) * POOL_GROUP_DIM]
        span = 1
        while span < w:
            win = win + pltpu.roll(win, span, 0)
            span *= 2
        mean = win[MAX_POOL_WINDOW:, :] * (1.0 / jnp.minimum(count, float(w)))
        pooled = mean - u[:, g * POOL_GROUP_DIM:(g + 1) * POOL_GROUP_DIM]
        outs.append(jnp.dot(pooled.astype(BF16), wp_ref[g], preferred_element_type=F32))
    y = (jnp.concatenate(outs, axis=1) + bp_ref[...]) * ps_ref[...]
    pool_ref[...] = y.astype(BF16)


def _inproj(x2, shift1, scale1, g1, w_in, qg, kg, rope, sel, unrot, bdq, bdk, wp, bp, ps, *, batch, seq):
    n_tok, d = x2.shape
    ts = min(TOKEN_TILE, seq)
    spt = seq // ts
    row = lambda width: pl.BlockSpec((ts, width), lambda b, s: (b * spt + s, 0))
    per_batch = pl.BlockSpec((None, 1, d), lambda b, s: (b, 0, 0))
    full = lambda a: pl.BlockSpec(a.shape, lambda b, s: (0,) * a.ndim)
    return pl.pallas_call(
        functools.partial(_inproj_kernel, ts=ts),
        out_shape=[jax.ShapeDtypeStruct((n_tok, ATTN_WIDTH), BF16),
                   jax.ShapeDtypeStruct((n_tok, KV_WIDTH), BF16),
                   jax.ShapeDtypeStruct((n_tok, KV_WIDTH), BF16),
                   jax.ShapeDtypeStruct((n_tok, POOL_WIDTH), BF16)],
        grid=(batch, spt),
        in_specs=[row(d), per_batch, per_batch, full(g1), full(w_in), full(qg), full(kg),
                  row(rope.shape[1]), full(sel), full(unrot), full(bdq), full(bdk),
                  full(wp), full(bp), full(ps)],
        out_specs=[row(ATTN_WIDTH), row(KV_WIDTH), row(KV_WIDTH), row(POOL_WIDTH)],
        scratch_shapes=[pltpu.VMEM((MAX_POOL_WINDOW, POOL_WIDTH), F32)],
        compiler_params=_params("arbitrary", "arbitrary"),
        name="inproj",
    )(x2, shift1, scale1, g1, w_in, qg, kg, rope, sel, unrot, bdq, bdk, wp, bp, ps)


def _attn_kernel(sink_ref, q_ref, kc_ref, vc_ref, kp_ref, vp_ref, o_ref, *, tq, tiles_per_seq):
    first = (pl.program_id(0) % tiles_per_seq) == 0
    kfull = jnp.concatenate([kp_ref[...], kc_ref[...]], axis=0)
    vfull = jnp.concatenate([vp_ref[...], vc_ref[...]], axis=0)
    rows = Q_PER_KV * WINDOW
    qi = lax.broadcasted_iota(I32, (rows, 2 * WINDOW), 0) % WINDOW
    kj = lax.broadcasted_iota(I32, (rows, 2 * WINDOW), 1)
    allowed = (kj > qi) & (kj <= qi + WINDOW)
    allowed_first = allowed & (jnp.logical_not(first) | (kj >= WINDOW))
    row_head = lax.broadcasted_iota(I32, (rows, 1), 0) // WINDOW
    for j in range(tq // WINDOW):
        kb = kfull[j * WINDOW:(j + 2) * WINDOW, :]
        vb = vfull[j * WINDOW:(j + 2) * WINDOW, :]
        mask = allowed_first if j == 0 else allowed
        for h in range(N_KV_HEADS):
            qs = jnp.concatenate(
                [q_ref[j * WINDOW:(j + 1) * WINDOW, (h * Q_PER_KV + g) * HEAD_DIM:(h * Q_PER_KV + g + 1) * HEAD_DIM]
                 for g in range(Q_PER_KV)], axis=0)
            s = lax.dot_general(qs, kb[:, h * HEAD_DIM:(h + 1) * HEAD_DIM], (((1,), (1,)), ((), ())),
                                preferred_element_type=F32)
            s = jnp.where(mask, s, -jnp.inf)
            sink = jnp.zeros((rows, 1), F32)
            for g in range(Q_PER_KV):
                sink = jnp.where(row_head == g, sink_ref[h * Q_PER_KV + g], sink)
            m = jnp.maximum(jnp.max(s, axis=-1, keepdims=True), sink)
            p = jnp.exp(s - m)
            denom = jnp.sum(p, axis=-1, keepdims=True) + jnp.exp(sink - m)
            o = jnp.dot(p.astype(BF16), vb[:, h * HEAD_DIM:(h + 1) * HEAD_DIM], preferred_element_type=F32)
            o = o * (1.0 / denom)
            o_ref[j * WINDOW:(j + 1) * WINDOW, h * Q_PER_KV * HEAD_DIM:(h + 1) * Q_PER_KV * HEAD_DIM] = (
                jnp.concatenate([o[g * WINDOW:(g + 1) * WINDOW, :] for g in range(Q_PER_KV)], axis=1).astype(BF16))


def _attention(q, k, v, sinks, *, seq):
    n_tok = q.shape[0]
    tq = min(TOKEN_TILE, seq)
    per = tq // WINDOW
    cur = lambda width: pl.BlockSpec((tq, width), lambda i: (i, 0))
    prev = pl.BlockSpec((WINDOW, KV_WIDTH), lambda i: (jnp.maximum(i * per - 1, 0), 0))
    return pl.pallas_call(
        functools.partial(_attn_kernel, tq=tq, tiles_per_seq=seq // tq),
        out_shape=jax.ShapeDtypeStruct((n_tok, ATTN_WIDTH), BF16),
        grid=(n_tok // tq,),
        in_specs=[pl.BlockSpec(memory_space=pltpu.SMEM), cur(ATTN_WIDTH), cur(KV_WIDTH), cur(KV_WIDTH), prev, prev],
        out_specs=cur(ATTN_WIDTH),
        compiler_params=_params("arbitrary"),
        name="attn",
    )(sinks, q, k, v, k, v)


def _outproj_kernel(attn_ref, pool_ref, x_ref, gate1_ref, shift2_ref, scale2_ref, g2_ref, wa_ref, wpo_ref,
                    rw_ref, rb_ref, tri_ref,
                    xo_ref, h2_ref, idx_ref, rank_ref, gates_ref, counts_ref, carry_ref):
    i = pl.program_id(0)

    @pl.when(i == 0)
    def _():
        carry_ref[...] = jnp.zeros_like(carry_ref)

    mixed = (jnp.dot(attn_ref[...], wa_ref[...], preferred_element_type=F32)
             + jnp.dot(pool_ref[...], wpo_ref[...], preferred_element_type=F32))
    x = x_ref[...] + gate1_ref[...] * mixed
    xo_ref[...] = x
    ms = jnp.mean(x * x, axis=-1, keepdims=True)
    h2 = x * lax.rsqrt(ms + EPS) * (g2_ref[...] * (1.0 + scale2_ref[...])) + shift2_ref[...]
    h2_ref[...] = h2

    logits = jnp.dot(h2, rw_ref[...], preferred_element_type=F32,
                     precision=lax.Precision.HIGHEST) + rb_ref[...]
    lane = lax.broadcasted_iota(I32, logits.shape, 1)
    work = logits
    vals, picks, chosen = [], [], []
    for _ in range(TOP_K):
        m = jnp.max(work, axis=-1, keepdims=True)
        pick = jnp.min(jnp.where(work == m, lane, N_EXPERTS), axis=-1, keepdims=True)
        sel = lane == pick
        vals.append(m)
        picks.append(pick)
        chosen.append(sel)
        work = jnp.where(sel, -jnp.inf, work)
    exps = [jnp.exp(v - vals[0]) for v in vals]
    inv_total = 1.0 / (exps[0] + exps[1] + exps[2] + exps[3])

    multi = jnp.zeros(logits.shape, F32)
    for sel in chosen:
        multi = multi + sel.astype(F32)
    before = jnp.dot(tri_ref[...], multi.astype(BF16), preferred_element_type=F32) + carry_ref[...]
    for kk in range(TOP_K):
        idx_ref[:, kk:kk + 1] = picks[kk]
        gates_ref[:, kk:kk + 1] = exps[kk] * inv_total
        rank_ref[:, kk:kk + 1] = jnp.sum(jnp.where(chosen[kk], before, 0.0), axis=-1, keepdims=True).astype(I32)
    carry_ref[...] = carry_ref[...] + jnp.sum(multi, axis=0, keepdims=True)
    counts_ref[...] = carry_ref[...]


def _outproj(attn, pool, x2, gate1, shift2, scale2, g2, wa, wpo, rw, rb, tri, *, seq):
    n_tok, d = x2.shape
    ts = tri.shape[0]
    spt = seq // ts
    row = lambda width: pl.BlockSpec((ts, width), lambda i: (i, 0))
    per_batch = pl.BlockSpec((None, 1, d), lambda i: (i // spt, 0, 0))
    full = lambda a: pl.BlockSpec(a.shape, lambda i: (0,) * a.ndim)
    return pl.pallas_call(
        _outproj_kernel,
        out_shape=[jax.ShapeDtypeStruct((n_tok, d), F32),
                   jax.ShapeDtypeStruct((n_tok, d), F32),
                   jax.ShapeDtypeStruct((n_tok, TOP_K), I32),
                   jax.ShapeDtypeStruct((n_tok, TOP_K), I32),
                   jax.ShapeDtypeStruct((n_tok, TOP_K), F32),
                   jax.ShapeDtypeStruct((1, N_EXPERTS), F32)],
        grid=(n_tok // ts,),
        in_specs=[row(ATTN_WIDTH), row(POOL_WIDTH), row(d), per_batch, per_batch, per_batch, full(g2),
                  full(wa), full(wpo), full(rw), full(rb), full(tri)],
        out_specs=[row(d), row(d), row(TOP_K), row(TOP_K), row(TOP_K),
                   pl.BlockSpec((1, N_EXPERTS), lambda i: (0, 0))],
        scratch_shapes=[pltpu.VMEM((1, N_EXPERTS), F32)],
        compiler_params=_params("arbitrary"),
        name="outproj",
    )(attn, pool, x2, gate1, shift2, scale2, g2, wa, wpo, rw, rb, tri)


def _dest_kernel(pstart_ref, idx_ref, rank_ref, dest_ref):
    idx = idx_ref[...]
    dest = rank_ref[...]
    for e in range(N_EXPERTS):
        dest = dest + jnp.where(idx == e, pstart_ref[e], 0)
    dest_ref[...] = dest


def _dest(pstart, idx_t, rank_t):
    n_tok = idx_t.shape[1]
    tl = min(n_tok, 8192)
    spec = pl.BlockSpec((TOP_K, tl), lambda i: (0, i))
    return pl.pallas_call(
        _dest_kernel,
        out_shape=jax.ShapeDtypeStruct((TOP_K, n_tok), I32),
        grid=(n_tok // tl,),
        in_specs=[pl.BlockSpec(memory_space=pltpu.SMEM), spec, spec],
        out_specs=spec,
        compiler_params=_params("arbitrary"),
        name="dest",
    )(pstart, idx_t, rank_t)


def _dispatch_kernel(dest_ref, h_ref, xs_in_ref, xs_ref, sem, *, td):
    del xs_in_ref

    def row_copy(r, k):
        return pltpu.make_async_copy(h_ref.at[pl.ds(r, 1)], xs_ref.at[pl.ds(dest_ref[k, r], 1)], sem)

    def issue(r, carry):
        for k in range(TOP_K):
            row_copy(r, k).start()
        return carry

    def drain(r, carry):
        for k in range(TOP_K):
            row_copy(r, k).wait()
        return carry

    lax.fori_loop(0, td, issue, 0)
    lax.fori_loop(0, td, drain, 0)


def _dispatch(dest3, h2, xs_zero):
    n_tok, d = h2.shape
    td = dest3.shape[2]
    return pl.pallas_call(
        functools.partial(_dispatch_kernel, td=td),
        out_shape=jax.ShapeDtypeStruct(xs_zero.shape, xs_zero.dtype),
        grid=(n_tok // td,),
        in_specs=[pl.BlockSpec((None, TOP_K, td), lambda i: (i, 0, 0), memory_space=pltpu.SMEM),
                  pl.BlockSpec((td, d), lambda i: (i, 0)),
                  pl.BlockSpec(memory_space=pl.ANY)],
        out_specs=pl.BlockSpec(memory_space=pl.ANY),
        scratch_shapes=[pltpu.SemaphoreType.DMA],
        input_output_aliases={2: 0},
        compiler_params=pltpu.CompilerParams(dimension_semantics=("arbitrary",),
                                             vmem_limit_bytes=VMEM_LIMIT_BYTES, has_side_effects=True),
        name="dispatch",
    )(dest3, h2, xs_zero)


def _experts_kernel(be_ref, nused_ref, xs_ref, wgu_ref, bgu_ref, wd_ref, bd_ref, yb_ref, *, d_ff):
    del be_ref

    @pl.when(pl.program_id(0) < nused_ref[0])
    def _():
        xb = xs_ref[...].astype(BF16)
        acc = None
        for j in range(d_ff // FF_CHUNK):
            lo, hi = j * FF_CHUNK, (j + 1) * FF_CHUNK
            g = jnp.dot(xb, wgu_ref[:, lo:hi], preferred_element_type=F32) + bgu_ref[:, lo:hi]
            u = jnp.dot(xb, wgu_ref[:, d_ff + lo:d_ff + hi], preferred_element_type=F32) + bgu_ref[:, d_ff + lo:d_ff + hi]
            g = jnp.minimum(g, SWIGLU_LIMIT)
            u = jnp.clip(u, -SWIGLU_LIMIT, SWIGLU_LIMIT)
            act = g * (1.0 / (1.0 + jnp.exp(-SWIGLU_ALPHA * g))) * (u + 1.0)
            part = jnp.dot(act.astype(BF16), wd_ref[lo:hi, :], preferred_element_type=F32)
            acc = part if acc is None else acc + part
        yb_ref[...] = acc + bd_ref[...]


def _experts(block_expert, n_used, xs, wgu, bgu, wd, bd):
    n_slots, d = xs.shape
    d_ff = wd.shape[1]
    m = EXPERT_BLOCK
    blk = lambda i, be, nu: (jnp.minimum(i, nu[0] - 1), 0)
    per_expert = lambda i, be, nu: (be[i], 0, 0)
    grid_spec = pltpu.PrefetchScalarGridSpec(
        num_scalar_prefetch=2,
        grid=(n_slots // m,),
        in_specs=[pl.BlockSpec((m, d), blk),
                  pl.BlockSpec((None, d, 2 * d_ff), per_expert),
                  pl.BlockSpec((None, 1, 2 * d_ff), per_expert),
                  pl.BlockSpec((None, d_ff, d), per_expert),
                  pl.BlockSpec((None, 1, d), per_expert)],
        out_specs=pl.BlockSpec((m, d), blk),
    )
    return pl.pallas_call(
        functools.partial(_experts_kernel, d_ff=d_ff),
        out_shape=jax.ShapeDtypeStruct((n_slots, d), F32),
        grid_spec=grid_spec,
        compiler_params=_params("arbitrary"),
        name="experts",
    )(block_expert, n_used, xs, wgu, bgu, wd, bd)


def _combine_kernel(dest_ref, gates_ref, x_ref, gate2_ref, yb_ref, xo_ref, buf_ref, sem, *, tc):
    def row_copy(r, k):
        return pltpu.make_async_copy(yb_ref.at[pl.ds(dest_ref[k, r], 1)], buf_ref.at[k, pl.ds(r, 1)], sem)

    def issue(r, carry):
        for k in range(TOP_K):
            row_copy(r, k).start()
        return carry

    def drain(r, carry):
        for k in range(TOP_K):
            row_copy(r, k).wait()
        return carry

    lax.fori_loop(0, tc, issue, 0)
    lax.fori_loop(0, tc, drain, 0)
    gates = gates_ref[...]
    y = gates[:, 0:1] * buf_ref[0]
    for k in range(1, TOP_K):
        y = y + gates[:, k:k + 1] * buf_ref[k]
    xo_ref[...] = x_ref[...] + gate2_ref[...] * y


def _combine(dest3, gates, x2, gate2, yb, *, seq):
    n_tok, d = x2.shape
    tc = dest3.shape[2]
    spt = seq // tc
    return pl.pallas_call(
        functools.partial(_combine_kernel, tc=tc),
        out_shape=jax.ShapeDtypeStruct((n_tok, d), F32),
        grid=(n_tok // tc,),
        in_specs=[pl.BlockSpec((None, TOP_K, tc), lambda i: (i, 0, 0), memory_space=pltpu.SMEM),
                  pl.BlockSpec((tc, TOP_K), lambda i: (i, 0)),
                  pl.BlockSpec((tc, d), lambda i: (i, 0)),
                  pl.BlockSpec((None, 1, d), lambda i: (i // spt, 0, 0)),
                  pl.BlockSpec(memory_space=pl.ANY)],
        out_specs=pl.BlockSpec((tc, d), lambda i: (i, 0)),
        scratch_shapes=[pltpu.VMEM((TOP_K, tc, d), F32), pltpu.SemaphoreType.DMA],
        compiler_params=_params("arbitrary"),
        name="combine",
    )(dest3, gates, x2, gate2, yb)


def kernel(x, c, positions, ada_w, ada_b, norm1_g, w_in, q_norm_g, k_norm_g, attn_sinks, pool_w, pool_b,
           pool_scale, w_out, norm2_g, router_w, router_b, expert_w_gu, expert_b_gu, expert_w_down,
           expert_b_down):
    batch, seq, d = x.shape
    n_layers = ada_w.shape[0]
    n_tok = batch * seq
    d_ff = expert_w_down.shape[2]
    assert seq % WINDOW == 0 and seq % min(TOKEN_TILE, seq) == 0 and seq % ROW_DMA_TILE == 0

    mod = _ada(c, ada_w, ada_b).reshape(n_layers, batch, 6, 1, d)
    rope = _rope_tables(positions)
    sel, unrot = _rope_select_matrix()
    bdq, bdk = _head_mean_matrix(ATTN_WIDTH), _head_mean_matrix(KV_WIDTH)
    ts = min(TOKEN_TILE, seq)
    tri = jnp.asarray(np.tril(np.ones((ts, ts), np.float32), -1), BF16)

    m = EXPERT_BLOCK
    n_asg = n_tok * TOP_K
    n_blocks = -(-n_asg // m) + N_EXPERTS
    n_slots = n_blocks * m
    n_row_tiles = n_tok // ROW_DMA_TILE

    x2 = x.reshape(n_tok, d)
    for l in range(n_layers):
        shift1, scale1, gate1, shift2, scale2, gate2 = (mod[l, :, j] for j in range(6))
        q, k, v, pool = _inproj(
            x2, shift1, scale1, norm1_g[l].reshape(1, d), w_in[l].astype(BF16),
            jnp.tile(q_norm_g[l], N_Q_HEADS).reshape(1, ATTN_WIDTH),
            jnp.tile(k_norm_g[l], N_KV_HEADS).reshape(1, KV_WIDTH),
            rope, sel, unrot, bdq, bdk, pool_w[l].astype(BF16), pool_b[l].reshape(1, POOL_WIDTH),
            pool_scale[l].reshape(1, POOL_WIDTH), batch=batch, seq=seq)
        attn = _attention(q, k, v, attn_sinks[l], seq=seq)
        w_out_b = w_out[l].astype(BF16)
        x2, h2, idx, rank, gates, counts = _outproj(
            attn, pool, x2, gate1, shift2, scale2, norm2_g[l].reshape(1, d),
            w_out_b[:ATTN_WIDTH], w_out_b[ATTN_WIDTH:], router_w[l], router_b[l].reshape(1, N_EXPERTS),
            tri, seq=seq)

        counts = counts.reshape(N_EXPERTS).astype(I32)
        padded = (counts + m - 1) // m * m
        pends = jnp.cumsum(padded)
        pstart = (pends - padded).astype(I32)
        n_used = (pends[-1] // m).astype(I32).reshape(1)
        block_row = jnp.minimum(jnp.arange(n_blocks, dtype=I32), n_used[0] - 1) * m
        block_expert = jnp.minimum(jnp.searchsorted(pends, block_row, side='right'), N_EXPERTS - 1).astype(I32)

        dest = _dest(pstart, idx.T, rank.T)
        dest3 = dest.reshape(TOP_K, n_row_tiles, ROW_DMA_TILE).transpose(1, 0, 2)
        xs = _dispatch(dest3, h2, jnp.zeros((n_slots, d), F32))
        yb = _experts(block_expert, n_used, xs, expert_w_gu[l].astype(BF16),
                      expert_b_gu[l].reshape(N_EXPERTS, 1, 2 * d_ff), expert_w_down[l].astype(BF16),
                      expert_b_down[l].reshape(N_EXPERTS, 1, d))
        x2 = _combine(dest3, gates, x2, gate2, yb, seq=seq)
    return x2.reshape(batch, seq, d)
```

```python
import functools

import numpy as np
import jax
import jax.numpy as jnp
from jax import lax
from jax.experimental import pallas as pl
from jax.experimental.pallas import tpu as pltpu

F32 = jnp.float32
BF16 = jnp.bfloat16
I32 = jnp.int32

HEAD_DIM = 64
N_Q_HEADS = 8
N_KV_HEADS = 2
Q_PER_KV = N_Q_HEADS // N_KV_HEADS
ATTN_WIDTH = N_Q_HEADS * HEAD_DIM
KV_WIDTH = N_KV_HEADS * HEAD_DIM
WINDOW = 128
ROT_DIM = HEAD_DIM // 4
ROT_HALF = ROT_DIM // 2
ROPE_THETA = 500000.0
POOL_WINDOWS = (2, 4, 8, 16)
N_POOL_GROUPS = len(POOL_WINDOWS)
POOL_GROUP_DIM = 128
POOL_WIDTH = N_POOL_GROUPS * POOL_GROUP_DIM
MAX_POOL_WINDOW = max(POOL_WINDOWS)
N_EXPERTS = 32
TOP_K = 4
SWIGLU_LIMIT = 7.0
SWIGLU_ALPHA = 1.702
EPS = 1e-6

LANES = 128
SUBLANES = 8
VMEM_LIMIT_BYTES = 56 * 1024 * 1024

TOKEN_TILE = 512
ROUTE_TILE = 256
EXPERT_BLOCK = 512
FF_CHUNK = 512
DISPATCH_CHUNK = 16
COMBINE_CHUNK = 8


def _params(*sem):
    return pltpu.CompilerParams(dimension_semantics=sem, vmem_limit_bytes=VMEM_LIMIT_BYTES)


def _ada_kernel(c_ref, w_ref, b_ref, o_ref):
    c = c_ref[...]
    c_act = c * (1.0 / (1.0 + jnp.exp(-c)))
    o_ref[0] = jnp.dot(c_act, w_ref[0], preferred_element_type=F32,
                       precision=lax.Precision.HIGHEST) + b_ref[0]


def _ada(c, ada_w, ada_b):
    n_layers, d, d6 = ada_w.shape
    b = c.shape[0]
    tn = 1536
    return pl.pallas_call(
        _ada_kernel,
        out_shape=jax.ShapeDtypeStruct((n_layers, b, d6), F32),
        grid=(n_layers, d6 // tn),
        in_specs=[pl.BlockSpec((b, d), lambda l, j: (0, 0)),
                  pl.BlockSpec((1, d, tn), lambda l, j: (l, 0, j)),
                  pl.BlockSpec((1, 1, tn), lambda l, j: (l, 0, j))],
        out_specs=pl.BlockSpec((1, b, tn), lambda l, j: (l, 0, j)),
        compiler_params=_params("arbitrary", "arbitrary"),
        name="ada",
    )(c, ada_w, ada_b.reshape(n_layers, 1, d6))


def _rope_kernel(pos_ref, freq_ref, ch_ref, cl_ref, sh_ref, sl_ref):
    ang = pos_ref[...].astype(F32) * freq_ref[...]
    for fn, hi_ref, lo_ref in ((jnp.cos, ch_ref, cl_ref), (jnp.sin, sh_ref, sl_ref)):
        val = fn(ang)
        hi = val.astype(BF16)
        hi_ref[...] = hi
        lo_ref[...] = (val - hi.astype(F32)).astype(BF16)


def _rope_tables(positions):
    n_tok = positions.size
    inv_freq = ROPE_THETA ** (-jnp.arange(0, ROT_DIM, 2, dtype=F32) / ROT_DIM)
    rows = n_tok * ROT_HALF // LANES
    pos_rep = jnp.repeat(positions.reshape(-1), ROT_HALF).reshape(rows, LANES)
    freq = jnp.tile(inv_freq, LANES // ROT_HALF).reshape(1, LANES)
    tr = min(rows, 512)
    spec = pl.BlockSpec((tr, LANES), lambda i: (i, 0))
    outs = pl.pallas_call(
        _rope_kernel,
        out_shape=[jax.ShapeDtypeStruct((rows, LANES), BF16)] * 4,
        grid=(rows // tr,),
        in_specs=[spec, pl.BlockSpec((1, LANES), lambda i: (0, 0))],
        out_specs=[spec] * 4,
        compiler_params=_params("arbitrary"),
        name="rope",
    )(pos_rep, freq)
    return jnp.concatenate([o.reshape(n_tok, ROT_HALF) for o in outs], axis=1)


def _rope_select_matrix():
    sel = np.zeros((4 * ROT_HALF, 3 * LANES), np.float32)
    for d in range(LANES):
        dd = d % HEAD_DIM
        if dd < ROT_DIM:
            f = dd % ROT_HALF
            sel[f, d] = 1.0
            sel[ROT_HALF + f, d] = 1.0
        if dd < ROT_HALF:
            sel[2 * ROT_HALF + dd, LANES + d] = -1.0
            sel[3 * ROT_HALF + dd, LANES + d] = -1.0
        elif dd < ROT_DIM:
            sel[2 * ROT_HALF + dd - ROT_HALF, 2 * LANES + d] = 1.0
            sel[3 * ROT_HALF + dd - ROT_HALF, 2 * LANES + d] = 1.0
    unrotated = np.array([[1.0 if (d % HEAD_DIM) >= ROT_DIM else 0.0 for d in range(LANES)]], np.float32)
    return jnp.asarray(sel, BF16), jnp.asarray(unrotated, F32)


def _head_mean_matrix(width):
    m = np.zeros((width, width), np.float32)
    for h in range(width // HEAD_DIM):
        m[h * HEAD_DIM:(h + 1) * HEAD_DIM, h * HEAD_DIM:(h + 1) * HEAD_DIM] = 1.0 / HEAD_DIM
    return jnp.asarray(m, BF16)


def _inproj_kernel(x_ref, shift_ref, scale_ref, g1_ref, w_ref, qg_ref, kg_ref, rope_ref, sel_ref,
                   unrot_ref, bdq_ref, bdk_ref, wp_ref, bp_ref, ps_ref,
                   q_ref, k_ref, v_ref, pool_ref, carry_ref, *, ts):
    s_idx = pl.program_id(1)

    @pl.when(s_idx == 0)
    def _():
        carry_ref[...] = jnp.zeros_like(carry_ref)

    x = x_ref[...]
    ms = jnp.mean(x * x, axis=-1, keepdims=True)
    h = x * lax.rsqrt(ms + EPS) * (g1_ref[...] * (1.0 + scale_ref[...])) + shift_ref[...]
    hb = h.astype(BF16)

    tab = jnp.dot(rope_ref[...], sel_ref[...], preferred_element_type=F32)
    cos_t = tab[:, :LANES] + unrot_ref[...]
    sin_a = tab[:, LANES:2 * LANES]
    sin_b = tab[:, 2 * LANES:]

    def norm_rope(t, bd_ref, g_ref, out_scale):
        msq = jnp.dot((t * t).astype(BF16), bd_ref[...], preferred_element_type=F32)
        tn = t * lax.rsqrt(msq + EPS) * g_ref[...]
        chunks = []
        for c in range(t.shape[1] // LANES):
            tc = tn[:, c * LANES:(c + 1) * LANES]
            rot = (tc * cos_t + pltpu.roll(tc, LANES - ROT_HALF, 1) * sin_a
                   + pltpu.roll(tc, ROT_HALF, 1) * sin_b)
            chunks.append(rot * out_scale)
        return chunks[0] if len(chunks) == 1 else jnp.concatenate(chunks, axis=1)

    q = jnp.dot(hb, w_ref[:, :ATTN_WIDTH], preferred_element_type=F32)
    q_ref[...] = norm_rope(q, bdq_ref, qg_ref, HEAD_DIM ** -0.5).astype(BF16)
    k = jnp.dot(hb, w_ref[:, ATTN_WIDTH:ATTN_WIDTH + KV_WIDTH], preferred_element_type=F32)
    k_ref[...] = norm_rope(k, bdk_ref, kg_ref, 1.0).astype(BF16)
    v = jnp.dot(hb, w_ref[:, ATTN_WIDTH + KV_WIDTH:ATTN_WIDTH + 2 * KV_WIDTH], preferred_element_type=F32)
    v_ref[...] = v.astype(BF16)

    u = jnp.dot(hb, w_ref[:, ATTN_WIDTH + 2 * KV_WIDTH:], preferred_element_type=F32)
    ext = jnp.concatenate([carry_ref[...], u], axis=0)
    carry_ref[...] = u[ts - MAX_POOL_WINDOW:, :]
    count = (lax.broadcasted_iota(I32, (ts, POOL_GROUP_DIM), 0) + (s_idx * ts + 1)).astype(F32)
    outs = []
    for g, w in enumerate(POOL_WINDOWS):
        win = ext[:, g * POOL_GROUP_DIM:(g + 1) * POOL_GROUP_DIM]
        span = 1
        while span < w:
            win = win + pltpu.roll(win, span, 0)
            span *= 2
        mean = win[MAX_POOL_WINDOW:, :] * (1.0 / jnp.minimum(count, float(w)))
        pooled = mean - u[:, g * POOL_GROUP_DIM:(g + 1) * POOL_GROUP_DIM]
        outs.append(jnp.dot(pooled.astype(BF16), wp_ref[g], preferred_element_type=F32))
    y = (jnp.concatenate(outs, axis=1) + bp_ref[...]) * ps_ref[...]
    pool_ref[...] = y.astype(BF16)


def _inproj(x2, shift1, scale1, g1, w_in, qg, kg, rope, sel, unrot, bdq, bdk, wp, bp, ps, *, batch, seq):
    n_tok, d = x2.shape
    ts = min(TOKEN_TILE, seq)
    spt = seq // ts
    row = lambda width: pl.BlockSpec((ts, width), lambda b, s: (b * spt + s, 0))
    per_batch = pl.BlockSpec((None, 1, d), lambda b, s: (b, 0, 0))
    full = lambda a: pl.BlockSpec(a.shape, lambda b, s: (0,) * a.ndim)
    return pl.pallas_call(
        functools.partial(_inproj_kernel, ts=ts),
        out_shape=[jax.ShapeDtypeStruct((n_tok, ATTN_WIDTH), BF16),
                   jax.ShapeDtypeStruct((n_tok, KV_WIDTH), BF16),
                   jax.ShapeDtypeStruct((n_tok, KV_WIDTH), BF16),
                   jax.ShapeDtypeStruct((n_tok, POOL_WIDTH), BF16)],
        grid=(batch, spt),
        in_specs=[row(d), per_batch, per_batch, full(g1), full(w_in), full(qg), full(kg),
                  row(rope.shape[1]), full(sel), full(unrot), full(bdq), full(bdk),
                  full(wp), full(bp), full(ps)],
        out_specs=[row(ATTN_WIDTH), row(KV_WIDTH), row(KV_WIDTH), row(POOL_WIDTH)],
        scratch_shapes=[pltpu.VMEM((MAX_POOL_WINDOW, POOL_WIDTH), F32)],
        compiler_params=_params("arbitrary", "arbitrary"),
        name="inproj",
    )(x2, shift1, scale1, g1, w_in, qg, kg, rope, sel, unrot, bdq, bdk, wp, bp, ps)


def _attn_kernel(sink_ref, q_ref, kc_ref, vc_ref, kp_ref, vp_ref, o_ref, *, tq, tiles_per_seq):
    first = (pl.program_id(0) % tiles_per_seq) == 0
    kfull = jnp.concatenate([kp_ref[...], kc_ref[...]], axis=0)
    vfull = jnp.concatenate([vp_ref[...], vc_ref[...]], axis=0)
    rows = Q_PER_KV * WINDOW
    qi = lax.broadcasted_iota(I32, (rows, 2 * WINDOW), 0) % WINDOW
    kj = lax.broadcasted_iota(I32, (rows, 2 * WINDOW), 1)
    allowed = (kj > qi) & (kj <= qi + WINDOW)
    allowed_first = allowed & (jnp.logical_not(first) | (kj >= WINDOW))
    row_head = lax.broadcasted_iota(I32, (rows, 1), 0) // WINDOW
    for j in range(tq // WINDOW):
        kb = kfull[j * WINDOW:(j + 2) * WINDOW, :]
        vb = vfull[j * WINDOW:(j + 2) * WINDOW, :]
        mask = allowed_first if j == 0 else allowed
        for h in range(N_KV_HEADS):
            qs = jnp.concatenate(
                [q_ref[j * WINDOW:(j + 1) * WINDOW, (h * Q_PER_KV + g) * HEAD_DIM:(h * Q_PER_KV + g + 1) * HEAD_DIM]
                 for g in range(Q_PER_KV)], axis=0)
            s = lax.dot_general(qs, kb[:, h * HEAD_DIM:(h + 1) * HEAD_DIM], (((1,), (1,)), ((), ())),
                                preferred_element_type=F32)
            s = jnp.where(mask, s, -jnp.inf)
            sink = jnp.zeros((rows, 1), F32)
            for g in range(Q_PER_KV):
                sink = jnp.where(row_head == g, sink_ref[h * Q_PER_KV + g], sink)
            m = jnp.maximum(jnp.max(s, axis=-1, keepdims=True), sink)
            p = jnp.exp(s - m)
            denom = jnp.sum(p, axis=-1, keepdims=True) + jnp.exp(sink - m)
            o = jnp.dot(p.astype(BF16), vb[:, h * HEAD_DIM:(h + 1) * HEAD_DIM], preferred_element_type=F32)
            o = o * (1.0 / denom)
            o_ref[j * WINDOW:(j + 1) * WINDOW, h * Q_PER_KV * HEAD_DIM:(h + 1) * Q_PER_KV * HEAD_DIM] = (
                jnp.concatenate([o[g * WINDOW:(g + 1) * WINDOW, :] for g in range(Q_PER_KV)], axis=1).astype(BF16))


def _attention(q, k, v, sinks, *, seq):
    n_tok = q.shape[0]
    tq = min(TOKEN_TILE, seq)
    per = tq // WINDOW
    cur = lambda width: pl.BlockSpec((tq, width), lambda i: (i, 0))
    prev = pl.BlockSpec((WINDOW, KV_WIDTH), lambda i: (jnp.maximum(i * per - 1, 0), 0))
    return pl.pallas_call(
        functools.partial(_attn_kernel, tq=tq, tiles_per_seq=seq // tq),
        out_shape=jax.ShapeDtypeStruct((n_tok, ATTN_WIDTH), BF16),
        grid=(n_tok // tq,),
        in_specs=[pl.BlockSpec(memory_space=pltpu.SMEM), cur(ATTN_WIDTH), cur(KV_WIDTH), cur(KV_WIDTH), prev, prev],
        out_specs=cur(ATTN_WIDTH),
        compiler_params=_params("arbitrary"),
        name="attn",
    )(sinks, q, k, v, k, v)


def _outproj_kernel(attn_ref, pool_ref, x_ref, gate1_ref, shift2_ref, scale2_ref, g2_ref, wa_ref, wpo_ref,
                    rw_ref, rb_ref, tri_ref,
                    xo_ref, h2_ref, idx_ref, rank_ref, gates_ref, cnt_ref):
    mixed = (jnp.dot(attn_ref[...], wa_ref[...], preferred_element_type=F32)
             + jnp.dot(pool_ref[...], wpo_ref[...], preferred_element_type=F32))
    x = x_ref[...] + gate1_ref[...] * mixed
    xo_ref[...] = x
    ms = jnp.mean(x * x, axis=-1, keepdims=True)
    h2 = x * lax.rsqrt(ms + EPS) * (g2_ref[...] * (1.0 + scale2_ref[...])) + shift2_ref[...]
    h2_ref[...] = h2.astype(BF16)

    logits = jnp.dot(h2, rw_ref[...], preferred_element_type=F32,
                     precision=lax.Precision.HIGHEST) + rb_ref[...]
    lane = lax.broadcasted_iota(I32, logits.shape, 1)
    work = logits
    vals, picks, chosen = [], [], []
    for _ in range(TOP_K):
        m = jnp.max(work, axis=-1, keepdims=True)
        pick = jnp.min(jnp.where(work == m, lane, N_EXPERTS), axis=-1, keepdims=True)
        sel = lane == pick
        vals.append(m)
        picks.append(pick)
        chosen.append(sel)
        work = jnp.where(sel, -jnp.inf, work)
    exps = [jnp.exp(v - vals[0]) for v in vals]
    inv_total = 1.0 / (exps[0] + exps[1] + exps[2] + exps[3])

    multi = jnp.zeros(logits.shape, F32)
    for sel in chosen:
        multi = multi + sel.astype(F32)
    before = jnp.dot(tri_ref[...], multi.astype(BF16), preferred_element_type=F32)
    for kk in range(TOP_K):
        idx_ref[:, kk:kk + 1] = picks[kk]
        gates_ref[:, kk:kk + 1] = exps[kk] * inv_total
        rank_ref[:, kk:kk + 1] = jnp.sum(jnp.where(chosen[kk], before, 0.0), axis=-1, keepdims=True).astype(I32)
    cnt_ref[...] = jnp.sum(multi, axis=0, keepdims=True)


def _outproj(attn, pool, x2, gate1, shift2, scale2, g2, wa, wpo, rw, rb, tri, *, seq):
    n_tok, d = x2.shape
    ts = tri.shape[0]
    spt = seq // ts
    row = lambda width: pl.BlockSpec((ts, width), lambda i: (i, 0))
    per_batch = pl.BlockSpec((None, 1, d), lambda i: (i // spt, 0, 0))
    full = lambda a: pl.BlockSpec(a.shape, lambda i: (0,) * a.ndim)
    return pl.pallas_call(
        _outproj_kernel,
        out_shape=[jax.ShapeDtypeStruct((n_tok, d), F32),
                   jax.ShapeDtypeStruct((n_tok, d), BF16),
                   jax.ShapeDtypeStruct((n_tok, TOP_K), I32),
                   jax.ShapeDtypeStruct((n_tok, TOP_K), I32),
                   jax.ShapeDtypeStruct((n_tok, TOP_K), F32),
                   jax.ShapeDtypeStruct((n_tok // ts, 1, N_EXPERTS), F32)],
        grid=(n_tok // ts,),
        in_specs=[row(ATTN_WIDTH), row(POOL_WIDTH), row(d), per_batch, per_batch, per_batch, full(g2),
                  full(wa), full(wpo), full(rw), full(rb), full(tri)],
        out_specs=[row(d), row(d), row(TOP_K), row(TOP_K), row(TOP_K),
                   pl.BlockSpec((None, 1, N_EXPERTS), lambda i: (i, 0, 0))],
        compiler_params=_params("arbitrary"),
        name="outproj",
    )(attn, pool, x2, gate1, shift2, scale2, g2, wa, wpo, rw, rb, tri)


def _chunk_loops(cnt_ref, tile, chunk, body):
    def per_expert(e, carry):
        n_chunks = (cnt_ref[tile * N_EXPERTS + e] + chunk - 1) // chunk

        def per_chunk(i, c):
            body(e, i)
            return c

        return lax.fori_loop(0, n_chunks, per_chunk, carry)

    lax.fori_loop(0, N_EXPERTS, per_expert, 0)


def _dispatch_kernel(cnt_ref, lo_ref, gs_ref, fill_ref, idx_t_ref, rank_t_ref, lo_col_ref, h_ref, xs_ref,
                     stage_ref, sem, *, tt):
    t = pl.program_id(0)
    slot = t % 2
    n_rows = TOP_K * tt
    u = DISPATCH_CHUNK
    m = EXPERT_BLOCK

    @pl.when(t == 0)
    def _():
        stage_ref[...] = jnp.zeros_like(stage_ref)

        def pad_copy(e, i):
            return pltpu.make_async_copy(stage_ref.at[1, pl.ds(0, u)],
                                         xs_ref.at[pl.ds(fill_ref[e] - (i + 1) * u, u)], sem)

        def tail_copy(i):
            return pltpu.make_async_copy(stage_ref.at[1, pl.ds(0, m)],
                                         xs_ref.at[pl.ds(fill_ref[N_EXPERTS - 1] + i * m, m)], sem)

        n_tail = (xs_ref.shape[0] - fill_ref[N_EXPERTS - 1]) // m
        for act in ("start", "wait"):
            _chunk_loops(fill_ref, 1, u, lambda e, i: getattr(pad_copy(e, i), act)())
            lax.fori_loop(0, n_tail, lambda i, c: (getattr(tail_copy(i), act)(), c)[1], 0)

    expert = lax.broadcasted_iota(I32, (N_EXPERTS, tt), 0)
    rows = lax.broadcasted_iota(I32, (n_rows, tt), 0)
    hit = None
    for k in range(TOP_K):
        onehot = expert == idx_t_ref[k:k + 1, :]
        base = jnp.sum(jnp.where(onehot, lo_col_ref[...], 0.0), axis=0, keepdims=True)
        slot_row = base.astype(I32) + rank_t_ref[k:k + 1, :]
        match = rows == slot_row
        hit = match if hit is None else (hit | match)
    perm = jnp.where(hit, 1.0, 0.0).astype(BF16)
    grouped = jnp.dot(perm, h_ref[...], preferred_element_type=F32)
    for c in range(SUBLANES):
        stage_ref[slot, 0:n_rows, c, :] = grouped[:, c * LANES:(c + 1) * LANES]

    def chunk_copy(tile, buf, e, i):
        src = lo_ref[tile * N_EXPERTS + e] + i * u
        dst = gs_ref[tile * N_EXPERTS + e] + i * u
        return pltpu.make_async_copy(stage_ref.at[buf, pl.ds(src, u)], xs_ref.at[pl.ds(dst, u)], sem)

    @pl.when(t > 0)
    def _():
        _chunk_loops(cnt_ref, t - 1, u, lambda e, i: chunk_copy(t - 1, 1 - slot, e, i).wait())

    _chunk_loops(cnt_ref, t, u, lambda e, i: chunk_copy(t, slot, e, i).start())

    @pl.when(t == pl.num_programs(0) - 1)
    def _():
        _chunk_loops(cnt_ref, t, u, lambda e, i: chunk_copy(t, slot, e, i).wait())


def _dispatch(cnt, lo, gstart, fill, idx_t, rank_t, lo_col, h2, n_slots):
    n_tok, d = h2.shape
    tt = ROUTE_TILE
    grid_spec = pltpu.PrefetchScalarGridSpec(
        num_scalar_prefetch=4,
        grid=(n_tok // tt,),
        in_specs=[pl.BlockSpec((TOP_K, tt), lambda i, *_: (0, i)),
                  pl.BlockSpec((TOP_K, tt), lambda i, *_: (0, i)),
                  pl.BlockSpec((None, N_EXPERTS, 1), lambda i, *_: (i, 0, 0)),
                  pl.BlockSpec((tt, d), lambda i, *_: (i, 0))],
        out_specs=pl.BlockSpec(memory_space=pl.ANY),
        scratch_shapes=[pltpu.VMEM((2, TOP_K * tt + DISPATCH_CHUNK, SUBLANES, LANES), F32),
                        pltpu.SemaphoreType.DMA],
    )
    return pl.pallas_call(
        functools.partial(_dispatch_kernel, tt=tt),
        out_shape=jax.ShapeDtypeStruct((n_slots, SUBLANES, d // SUBLANES), F32),
        grid_spec=grid_spec,
        compiler_params=pltpu.CompilerParams(dimension_semantics=("arbitrary",),
                                             vmem_limit_bytes=VMEM_LIMIT_BYTES, has_side_effects=True),
        name="dispatch",
    )(cnt, lo, gstart, fill, idx_t, rank_t, lo_col, h2)


def _experts_kernel(be_ref, nvalid_ref, nused_ref, xs_ref, wgu_ref, bgu_ref, wd_ref, bd_ref, yb_ref, *, d_ff):
    del be_ref
    i = pl.program_id(0)

    @pl.when(i < nused_ref[0])
    def _():
        x = jnp.concatenate([xs_ref[:, c, :] for c in range(SUBLANES)], axis=1)
        row = lax.broadcasted_iota(I32, (x.shape[0], 1), 0)
        xb = jnp.where(row < nvalid_ref[i], x, 0.0).astype(BF16)
        acc = None
        for j in range(d_ff // FF_CHUNK):
            lo, hi = j * FF_CHUNK, (j + 1) * FF_CHUNK
            g = jnp.dot(xb, wgu_ref[:, lo:hi], preferred_element_type=F32) + bgu_ref[:, lo:hi]
            u = jnp.dot(xb, wgu_ref[:, d_ff + lo:d_ff + hi], preferred_element_type=F32) + bgu_ref[:, d_ff + lo:d_ff + hi]
            g = jnp.minimum(g, SWIGLU_LIMIT)
            u = jnp.clip(u, -SWIGLU_LIMIT, SWIGLU_LIMIT)
            act = g * (1.0 / (1.0 + jnp.exp(-SWIGLU_ALPHA * g))) * (u + 1.0)
            part = jnp.dot(act.astype(BF16), wd_ref[lo:hi, :], preferred_element_type=F32)
            acc = part if acc is None else acc + part
        y = acc + bd_ref[...]
        for c in range(SUBLANES):
            yb_ref[:, c, :] = y[:, c * LANES:(c + 1) * LANES]

    @pl.when(i >= nused_ref[0])
    def _():
        yb_ref[...] = jnp.zeros_like(yb_ref)


def _experts(block_expert, n_valid, n_used, xs, wgu, bgu, wd, bd):
    n_slots = xs.shape[0]
    d_ff, d = wd.shape[1], wd.shape[2]
    m = EXPERT_BLOCK
    blk = lambda i, be, nv, nu: (jnp.minimum(i, nu[0] - 1), 0, 0)
    per_expert = lambda i, be, nv, nu: (be[i], 0, 0)
    grid_spec = pltpu.PrefetchScalarGridSpec(
        num_scalar_prefetch=3,
        grid=(n_slots // m,),
        in_specs=[pl.BlockSpec((m, SUBLANES, LANES), blk),
                  pl.BlockSpec((None, d, 2 * d_ff), per_expert),
                  pl.BlockSpec((None, 1, 2 * d_ff), per_expert),
                  pl.BlockSpec((None, d_ff, d), per_expert),
                  pl.BlockSpec((None, 1, d), per_expert)],
        out_specs=pl.BlockSpec((m, SUBLANES, LANES), lambda i, be, nv, nu: (i, 0, 0)),
    )
    return pl.pallas_call(
        functools.partial(_experts_kernel, d_ff=d_ff),
        out_shape=jax.ShapeDtypeStruct(xs.shape, F32),
        grid_spec=grid_spec,
        compiler_params=_params("arbitrary"),
        name="experts",
    )(block_expert, n_valid, n_used, xs, wgu, bgu, wd, bd)


def _combine_kernel(cnt_ref, lo_ref, gs_ref, idx_ref, rank_ref, gates_ref, lo_row_ref, x_ref, gate2_ref, yb_ref,
                    xo_ref, stage_ref, sems, *, tt, n_stage):
    t = pl.program_id(0)
    slot = t % 2
    u = COMBINE_CHUNK

    def chunk_copy(tile, buf, e, i):
        src = gs_ref[tile * N_EXPERTS + e] + i * u
        dst = lo_ref[tile * N_EXPERTS + e] + i * u
        return pltpu.make_async_copy(yb_ref.at[pl.ds(src, u)], stage_ref.at[buf, pl.ds(dst, u)], sems.at[buf])

    @pl.when(t == 0)
    def _():
        stage_ref[...] = jnp.zeros_like(stage_ref)
        _chunk_loops(cnt_ref, 0, u, lambda e, i: chunk_copy(0, 0, e, i).start())

    @pl.when(t + 1 < pl.num_programs(0))
    def _():
        _chunk_loops(cnt_ref, t + 1, u, lambda e, i: chunk_copy(t + 1, 1 - slot, e, i).start())

    _chunk_loops(cnt_ref, t, u, lambda e, i: chunk_copy(t, slot, e, i).wait())

    lane = lax.broadcasted_iota(I32, (tt, N_EXPERTS), 1)
    cols = lax.broadcasted_iota(I32, (tt, n_stage), 1)
    weights = jnp.zeros((tt, n_stage), F32)
    for k in range(TOP_K):
        onehot = lane == idx_ref[:, k:k + 1]
        base = jnp.sum(jnp.where(onehot, lo_row_ref[...], 0.0), axis=1, keepdims=True)
        col = base.astype(I32) + rank_ref[:, k:k + 1]
        weights = weights + jnp.where(cols == col, gates_ref[:, k:k + 1], 0.0)
    staged = jnp.concatenate([stage_ref[slot, :, c, :] for c in range(SUBLANES)], axis=1)
    y = jnp.dot(weights.astype(BF16), staged.astype(BF16), preferred_element_type=F32)
    xo_ref[...] = x_ref[...] + gate2_ref[...] * y


def _combine(cnt, lo8, gstart, idx, rank, gates, lo_row, x2, gate2, yb, *, seq):
    n_tok, d = x2.shape
    tt = ROUTE_TILE
    spt = seq // tt
    n_stage = TOP_K * tt + N_EXPERTS * COMBINE_CHUNK
    grid_spec = pltpu.PrefetchScalarGridSpec(
        num_scalar_prefetch=3,
        grid=(n_tok // tt,),
        in_specs=[pl.BlockSpec((tt, TOP_K), lambda i, *_: (i, 0)),
                  pl.BlockSpec((tt, TOP_K), lambda i, *_: (i, 0)),
                  pl.BlockSpec((tt, TOP_K), lambda i, *_: (i, 0)),
                  pl.BlockSpec((None, 1, N_EXPERTS), lambda i, *_: (i, 0, 0)),
                  pl.BlockSpec((tt, d), lambda i, *_: (i, 0)),
                  pl.BlockSpec((None, 1, d), lambda i, *_: (i // spt, 0, 0)),
                  pl.BlockSpec(memory_space=pl.ANY)],
        out_specs=pl.BlockSpec((tt, d), lambda i, *_: (i, 0)),
        scratch_shapes=[pltpu.VMEM((2, n_stage, SUBLANES, LANES), F32),
                        pltpu.SemaphoreType.DMA((2,))],
    )
    return pl.pallas_call(
        functools.partial(_combine_kernel, tt=tt, n_stage=n_stage),
        out_shape=jax.ShapeDtypeStruct((n_tok, d), F32),
        grid_spec=grid_spec,
        compiler_params=_params("arbitrary"),
        name="combine",
    )(cnt, lo8, gstart, idx, rank, gates, lo_row, x2, gate2, yb)


def _exclusive_cumsum(a, axis):
    return jnp.cumsum(a, axis=axis) - a


def _routing_tables(tile_cnt):
    m = EXPERT_BLOCK
    total = jnp.sum(tile_cnt, axis=0)
    padded = (total + (DISPATCH_CHUNK - 1) + m - 1) // m * m
    pends = jnp.cumsum(padded)
    pstart = pends - padded
    gstart = pstart[None, :] + _exclusive_cumsum(tile_cnt, 0)
    lo = _exclusive_cumsum(tile_cnt, 1)
    cu = COMBINE_CHUNK
    lo_c = _exclusive_cumsum((tile_cnt + cu - 1) // cu * cu, 1)
    return total, pstart, pends, gstart, lo, lo_c


def kernel(x, c, positions, ada_w, ada_b, norm1_g, w_in, q_norm_g, k_norm_g, attn_sinks, pool_w, pool_b,
           pool_scale, w_out, norm2_g, router_w, router_b, expert_w_gu, expert_b_gu, expert_w_down,
           expert_b_down):
    batch, seq, d = x.shape
    n_layers = ada_w.shape[0]
    n_tok = batch * seq
    d_ff = expert_w_down.shape[2]
    assert d == SUBLANES * LANES
    assert seq % WINDOW == 0 and seq % min(TOKEN_TILE, seq) == 0 and seq % ROUTE_TILE == 0

    mod = _ada(c, ada_w, ada_b).reshape(n_layers, batch, 6, 1, d)
    rope = _rope_tables(positions)
    sel, unrot = _rope_select_matrix()
    bdq, bdk = _head_mean_matrix(ATTN_WIDTH), _head_mean_matrix(KV_WIDTH)
    tri = jnp.asarray(np.tril(np.ones((ROUTE_TILE, ROUTE_TILE), np.float32), -1), BF16)

    m = EXPERT_BLOCK
    n_blocks = -(-n_tok * TOP_K // m) + N_EXPERTS + 1
    n_slots = n_blocks * m
    n_tiles = n_tok // ROUTE_TILE

    x2 = x.reshape(n_tok, d)
    for l in range(n_layers):
        shift1, scale1, gate1, shift2, scale2, gate2 = (mod[l, :, j] for j in range(6))
        q, k, v, pool = _inproj(
            x2, shift1, scale1, norm1_g[l].reshape(1, d), w_in[l].astype(BF16),
            jnp.tile(q_norm_g[l], N_Q_HEADS).reshape(1, ATTN_WIDTH),
            jnp.tile(k_norm_g[l], N_KV_HEADS).reshape(1, KV_WIDTH),
            rope, sel, unrot, bdq, bdk, pool_w[l].astype(BF16), pool_b[l].reshape(1, POOL_WIDTH),
            pool_scale[l].reshape(1, POOL_WIDTH), batch=batch, seq=seq)
        attn = _attention(q, k, v, attn_sinks[l], seq=seq)
        w_out_b = w_out[l].astype(BF16)
        x2, h2, idx, rank, gates, tile_cnt = _outproj(
            attn, pool, x2, gate1, shift2, scale2, norm2_g[l].reshape(1, d),
            w_out_b[:ATTN_WIDTH], w_out_b[ATTN_WIDTH:], router_w[l], router_b[l].reshape(1, N_EXPERTS),
            tri, seq=seq)

        tile_cnt = tile_cnt.reshape(n_tiles, N_EXPERTS).astype(I32)
        total, pstart, pends, gstart, lo, lo_c = _routing_tables(tile_cnt)
        n_used = (pends[-1] // m).astype(I32).reshape(1)
        block_row = jnp.minimum(jnp.arange(n_blocks, dtype=I32), n_used[0] - 1) * m
        block_expert = jnp.minimum(jnp.sum(block_row[:, None] >= pends[None, :], axis=1), N_EXPERTS - 1).astype(I32)
        n_valid = jnp.clip((pstart + total)[block_expert] - block_row, 0, m).astype(I32)

        flat = lambda a: a.reshape(-1).astype(I32)
        fill = jnp.concatenate([pends, pends - pstart - total]).astype(I32)
        xs = _dispatch(flat(tile_cnt), flat(lo), flat(gstart), fill, idx.T, rank.T,
                       lo.astype(F32).reshape(n_tiles, N_EXPERTS, 1), h2, n_slots)
        yb = _experts(block_expert, n_valid, n_used, xs, expert_w_gu[l].astype(BF16),
                      expert_b_gu[l].reshape(N_EXPERTS, 1, 2 * d_ff), expert_w_down[l].astype(BF16),
                      expert_b_down[l].reshape(N_EXPERTS, 1, d))
        x2 = _combine(flat(tile_cnt), flat(lo_c), flat(gstart), idx, rank, gates,
                      lo_c.astype(F32).reshape(n_tiles, 1, N_EXPERTS), x2, gate2, yb, seq=seq)
    return x2.reshape(batch, seq, d)
```

```python
import functools

import numpy as np
import jax
import jax.numpy as jnp
from jax import lax
from jax.experimental import pallas as pl
from jax.experimental.pallas import tpu as pltpu

F32 = jnp.float32
BF16 = jnp.bfloat16
I32 = jnp.int32

HEAD_DIM = 64
N_Q_HEADS = 8
N_KV_HEADS = 2
Q_PER_KV = N_Q_HEADS // N_KV_HEADS
ATTN_WIDTH = N_Q_HEADS * HEAD_DIM
KV_WIDTH = N_KV_HEADS * HEAD_DIM
WINDOW = 128
ROT_DIM = HEAD_DIM // 4
ROT_HALF = ROT_DIM // 2
ROPE_THETA = 500000.0
POOL_WINDOWS = (2, 4, 8, 16)
N_POOL_GROUPS = len(POOL_WINDOWS)
POOL_GROUP_DIM = 128
POOL_WIDTH = N_POOL_GROUPS * POOL_GROUP_DIM
MAX_POOL_WINDOW = max(POOL_WINDOWS)
N_EXPERTS = 32
TOP_K = 4
SWIGLU_LIMIT = 7.0
SWIGLU_ALPHA = 1.702
EPS = 1e-6

LANES = 128
SUBLANES = 8
VMEM_LIMIT_BYTES = 56 * 1024 * 1024

TOKEN_TILE = 512
ROUTE_TILE = 256
EXPERT_BLOCK = 512
FF_CHUNK = 512
DISPATCH_CHUNK = 16
COMBINE_CHUNK = 8


def _params(*sem):
    return pltpu.CompilerParams(dimension_semantics=sem, vmem_limit_bytes=VMEM_LIMIT_BYTES)


def _tile_rows(start, n):
    return pl.ds(pl.multiple_of(start * SUBLANES, SUBLANES), n * SUBLANES)


def _feature_chunk(c, n):
    return pl.ds(c, n, stride=SUBLANES)


def _ada_kernel(c_ref, w_ref, b_ref, o_ref):
    c = c_ref[...]
    c_act = c * (1.0 / (1.0 + jnp.exp(-c)))
    o_ref[0] = jnp.dot(c_act, w_ref[0], preferred_element_type=F32,
                       precision=lax.Precision.HIGHEST) + b_ref[0]


def _ada(c, ada_w, ada_b):
    n_layers, d, d6 = ada_w.shape
    b = c.shape[0]
    tn = 1536
    return pl.pallas_call(
        _ada_kernel,
        out_shape=jax.ShapeDtypeStruct((n_layers, b, d6), F32),
        grid=(n_layers, d6 // tn),
        in_specs=[pl.BlockSpec((b, d), lambda l, j: (0, 0)),
                  pl.BlockSpec((1, d, tn), lambda l, j: (l, 0, j)),
                  pl.BlockSpec((1, 1, tn), lambda l, j: (l, 0, j))],
        out_specs=pl.BlockSpec((1, b, tn), lambda l, j: (l, 0, j)),
        compiler_params=_params("arbitrary", "arbitrary"),
        name="ada",
    )(c, ada_w, ada_b.reshape(n_layers, 1, d6))


def _rope_kernel(pos_ref, freq_ref, ch_ref, cl_ref, sh_ref, sl_ref):
    ang = pos_ref[...].astype(F32) * freq_ref[...]
    for fn, hi_ref, lo_ref in ((jnp.cos, ch_ref, cl_ref), (jnp.sin, sh_ref, sl_ref)):
        val = fn(ang)
        hi = val.astype(BF16)
        hi_ref[...] = hi
        lo_ref[...] = (val - hi.astype(F32)).astype(BF16)


def _rope_tables(positions):
    n_tok = positions.size
    inv_freq = ROPE_THETA ** (-jnp.arange(0, ROT_DIM, 2, dtype=F32) / ROT_DIM)
    rows = n_tok * ROT_HALF // LANES
    pos_rep = jnp.repeat(positions.reshape(-1), ROT_HALF).reshape(rows, LANES)
    freq = jnp.tile(inv_freq, LANES // ROT_HALF).reshape(1, LANES)
    tr = min(rows, 512)
    spec = pl.BlockSpec((tr, LANES), lambda i: (i, 0))
    outs = pl.pallas_call(
        _rope_kernel,
        out_shape=[jax.ShapeDtypeStruct((rows, LANES), BF16)] * 4,
        grid=(rows // tr,),
        in_specs=[spec, pl.BlockSpec((1, LANES), lambda i: (0, 0))],
        out_specs=[spec] * 4,
        compiler_params=_params("arbitrary"),
        name="rope",
    )(pos_rep, freq)
    return jnp.concatenate([o.reshape(n_tok, ROT_HALF) for o in outs], axis=1)


def _rope_select_matrix():
    sel = np.zeros((4 * ROT_HALF, 3 * LANES), np.float32)
    for d in range(LANES):
        dd = d % HEAD_DIM
        if dd < ROT_DIM:
            f = dd % ROT_HALF
            sel[f, d] = 1.0
            sel[ROT_HALF + f, d] = 1.0
        if dd < ROT_HALF:
            sel[2 * ROT_HALF + dd, LANES + d] = -1.0
            sel[3 * ROT_HALF + dd, LANES + d] = -1.0
        elif dd < ROT_DIM:
            sel[2 * ROT_HALF + dd - ROT_HALF, 2 * LANES + d] = 1.0
            sel[3 * ROT_HALF + dd - ROT_HALF, 2 * LANES + d] = 1.0
    unrotated = np.array([[1.0 if (d % HEAD_DIM) >= ROT_DIM else 0.0 for d in range(LANES)]], np.float32)
    return jnp.asarray(sel, BF16), jnp.asarray(unrotated, F32)


def _head_mean_matrix(width):
    m = np.zeros((width, width), np.float32)
    for h in range(width // HEAD_DIM):
        m[h * HEAD_DIM:(h + 1) * HEAD_DIM, h * HEAD_DIM:(h + 1) * HEAD_DIM] = 1.0 / HEAD_DIM
    return jnp.asarray(m, BF16)


def _inproj_kernel(x_ref, shift_ref, scale_ref, g1_ref, w_ref, qg_ref, kg_ref, rope_ref, sel_ref,
                   unrot_ref, bdq_ref, bdk_ref, wp_ref, bp_ref, ps_ref,
                   q_ref, k_ref, v_ref, pool_ref, carry_ref, *, ts):
    s_idx = pl.program_id(1)

    @pl.when(s_idx == 0)
    def _():
        carry_ref[...] = jnp.zeros_like(carry_ref)

    x = x_ref[...]
    ms = jnp.mean(x * x, axis=-1, keepdims=True)
    h = x * lax.rsqrt(ms + EPS) * (g1_ref[...] * (1.0 + scale_ref[...])) + shift_ref[...]
    hb = h.astype(BF16)

    tab = jnp.dot(rope_ref[...], sel_ref[...], preferred_element_type=F32)
    cos_t = tab[:, :LANES] + unrot_ref[...]
    sin_a = tab[:, LANES:2 * LANES]
    sin_b = tab[:, 2 * LANES:]

    def norm_rope(t, bd_ref, g_ref, out_scale):
        msq = jnp.dot((t * t).astype(BF16), bd_ref[...], preferred_element_type=F32)
        tn = t * lax.rsqrt(msq + EPS) * g_ref[...]
        chunks = []
        for c in range(t.shape[1] // LANES):
            tc = tn[:, c * LANES:(c + 1) * LANES]
            rot = (tc * cos_t + pltpu.roll(tc, LANES - ROT_HALF, 1) * sin_a
                   + pltpu.roll(tc, ROT_HALF, 1) * sin_b)
            chunks.append(rot * out_scale)
        return chunks[0] if len(chunks) == 1 else jnp.concatenate(chunks, axis=1)

    q = jnp.dot(hb, w_ref[:, :ATTN_WIDTH], preferred_element_type=F32)
    q_ref[...] = norm_rope(q, bdq_ref, qg_ref, HEAD_DIM ** -0.5).astype(BF16)
    k = jnp.dot(hb, w_ref[:, ATTN_WIDTH:ATTN_WIDTH + KV_WIDTH], preferred_element_type=F32)
    k_ref[...] = norm_rope(k, bdk_ref, kg_ref, 1.0).astype(BF16)
    v = jnp.dot(hb, w_ref[:, ATTN_WIDTH + KV_WIDTH:ATTN_WIDTH + 2 * KV_WIDTH], preferred_element_type=F32)
    v_ref[...] = v.astype(BF16)

    u = jnp.dot(hb, w_ref[:, ATTN_WIDTH + 2 * KV_WIDTH:], preferred_element_type=F32)
    ext = jnp.concatenate([carry_ref[...], u], axis=0)
    carry_ref[...] = u[ts - MAX_POOL_WINDOW:, :]
    count = (lax.broadcasted_iota(I32, (ts, POOL_GROUP_DIM), 0) + (s_idx * ts + 1)).astype(F32)
    outs = []
    for g, w in enumerate(POOL_WINDOWS):
        win = ext[:, g * POOL_GROUP_DIM:(g + 1) * POOL_GROUP_DIM]
        span = 1
        while span < w:
            win = win + pltpu.roll(win, span, 0)
            span *= 2
        mean = win[MAX_POOL_WINDOW:, :] * (1.0 / jnp.minimum(count, float(w)))
        pooled = mean - u[:, g * POOL_GROUP_DIM:(g + 1) * POOL_GROUP_DIM]
        outs.append(jnp.dot(pooled.astype(BF16), wp_ref[g], preferred_element_type=F32))
    y = (jnp.concatenate(outs, axis=1) + bp_ref[...]) * ps_ref[...]
    pool_ref[...] = y.astype(BF16)


def _inproj(x2, shift1, scale1, g1, w_in, qg, kg, rope, sel, unrot, bdq, bdk, wp, bp, ps, *, batch, seq):
    n_tok, d = x2.shape
    ts = min(TOKEN_TILE, seq)
    spt = seq // ts
    row = lambda width: pl.BlockSpec((ts, width), lambda b, s: (b * spt + s, 0))
    per_batch = pl.BlockSpec((None, 1, d), lambda b, s: (b, 0, 0))
    full = lambda a: pl.BlockSpec(a.shape, lambda b, s: (0,) * a.ndim)
    return pl.pallas_call(
        functools.partial(_inproj_kernel, ts=ts),
        out_shape=[jax.ShapeDtypeStruct((n_tok, ATTN_WIDTH), BF16),
                   jax.ShapeDtypeStruct((n_tok, KV_WIDTH), BF16),
                   jax.ShapeDtypeStruct((n_tok, KV_WIDTH), BF16),
                   jax.ShapeDtypeStruct((n_tok, POOL_WIDTH), BF16)],
        grid=(batch, spt),
        in_specs=[row(d), per_batch, per_batch, full(g1), full(w_in), full(qg), full(kg),
                  row(rope.shape[1]), full(sel), full(unrot), full(bdq), full(bdk),
                  full(wp), full(bp), full(ps)],
        out_specs=[row(ATTN_WIDTH), row(KV_WIDTH), row(KV_WIDTH), row(POOL_WIDTH)],
        scratch_shapes=[pltpu.VMEM((MAX_POOL_WINDOW, POOL_WIDTH), F32)],
        compiler_params=_params("arbitrary", "arbitrary"),
        name="inproj",
    )(x2, shift1, scale1, g1, w_in, qg, kg, rope, sel, unrot, bdq, bdk, wp, bp, ps)


def _attn_kernel(sink_ref, q_ref, kc_ref, vc_ref, kp_ref, vp_ref, o_ref, *, tq, tiles_per_seq):
    first = (pl.program_id(0) % tiles_per_seq) == 0
    kfull = jnp.concatenate([kp_ref[...], kc_ref[...]], axis=0)
    vfull = jnp.concatenate([vp_ref[...], vc_ref[...]], axis=0)
    rows = Q_PER_KV * WINDOW
    qi = lax.broadcasted_iota(I32, (rows, 2 * WINDOW), 0) % WINDOW
    kj = lax.broadcasted_iota(I32, (rows, 2 * WINDOW), 1)
    allowed = (kj > qi) & (kj <= qi + WINDOW)
    allowed_first = allowed & (jnp.logical_not(first) | (kj >= WINDOW))
    row_head = lax.broadcasted_iota(I32, (rows, 1), 0) // WINDOW
    for j in range(tq // WINDOW):
        kb = kfull[j * WINDOW:(j + 2) * WINDOW, :]
        vb = vfull[j * WINDOW:(j + 2) * WINDOW, :]
        mask = allowed_first if j == 0 else allowed
        for h in range(N_KV_HEADS):
            qs = jnp.concatenate(
                [q_ref[j * WINDOW:(j + 1) * WINDOW, (h * Q_PER_KV + g) * HEAD_DIM:(h * Q_PER_KV + g + 1) * HEAD_DIM]
                 for g in range(Q_PER_KV)], axis=0)
            s = lax.dot_general(qs, kb[:, h * HEAD_DIM:(h + 1) * HEAD_DIM], (((1,), (1,)), ((), ())),
                                preferred_element_type=F32)
            s = jnp.where(mask, s, -jnp.inf)
            sink = jnp.zeros((rows, 1), F32)
            for g in range(Q_PER_KV):
                sink = jnp.where(row_head == g, sink_ref[h * Q_PER_KV + g], sink)
            m = jnp.maximum(jnp.max(s, axis=-1, keepdims=True), sink)
            p = jnp.exp(s - m)
            denom = jnp.sum(p, axis=-1, keepdims=True) + jnp.exp(sink - m)
            o = jnp.dot(p.astype(BF16), vb[:, h * HEAD_DIM:(h + 1) * HEAD_DIM], preferred_element_type=F32)
            o = o * (1.0 / denom)
            o_ref[j * WINDOW:(j + 1) * WINDOW, h * Q_PER_KV * HEAD_DIM:(h + 1) * Q_PER_KV * HEAD_DIM] = (
                jnp.concatenate([o[g * WINDOW:(g + 1) * WINDOW, :] for g in range(Q_PER_KV)], axis=1).astype(BF16))


def _attention(q, k, v, sinks, *, seq):
    n_tok = q.shape[0]
    tq = min(TOKEN_TILE, seq)
    per = tq // WINDOW
    cur = lambda width: pl.BlockSpec((tq, width), lambda i: (i, 0))
    prev = pl.BlockSpec((WINDOW, KV_WIDTH), lambda i: (jnp.maximum(i * per - 1, 0), 0))
    return pl.pallas_call(
        functools.partial(_attn_kernel, tq=tq, tiles_per_seq=seq // tq),
        out_shape=jax.ShapeDtypeStruct((n_tok, ATTN_WIDTH), BF16),
        grid=(n_tok // tq,),
        in_specs=[pl.BlockSpec(memory_space=pltpu.SMEM), cur(ATTN_WIDTH), cur(KV_WIDTH), cur(KV_WIDTH), prev, prev],
        out_specs=cur(ATTN_WIDTH),
        compiler_params=_params("arbitrary"),
        name="attn",
    )(sinks, q, k, v, k, v)


def _outproj_kernel(attn_ref, pool_ref, x_ref, gate1_ref, shift2_ref, scale2_ref, g2_ref, wa_ref, wpo_ref,
                    rw_ref, rb_ref, tri_ref,
                    xo_ref, h2_ref, idx_ref, rank_ref, gates_ref, cnt_ref):
    mixed = (jnp.dot(attn_ref[...], wa_ref[...], preferred_element_type=F32)
             + jnp.dot(pool_ref[...], wpo_ref[...], preferred_element_type=F32))
    x = x_ref[...] + gate1_ref[...] * mixed
    xo_ref[...] = x
    ms = jnp.mean(x * x, axis=-1, keepdims=True)
    h2 = x * lax.rsqrt(ms + EPS) * (g2_ref[...] * (1.0 + scale2_ref[...])) + shift2_ref[...]
    h2b = h2.astype(BF16)
    h2_ref[...] = h2b

    logits = jnp.dot(h2b, rw_ref[...], preferred_element_type=F32) + rb_ref[...]
    lane = lax.broadcasted_iota(I32, logits.shape, 1)
    work = logits
    vals, picks, chosen = [], [], []
    for _ in range(TOP_K):
        m = jnp.max(work, axis=-1, keepdims=True)
        pick = jnp.min(jnp.where(work == m, lane, N_EXPERTS), axis=-1, keepdims=True)
        sel = lane == pick
        vals.append(m)
        picks.append(pick)
        chosen.append(sel)
        work = jnp.where(sel, -jnp.inf, work)
    exps = [jnp.exp(v - vals[0]) for v in vals]
    inv_total = 1.0 / (exps[0] + exps[1] + exps[2] + exps[3])

    multi = jnp.zeros(logits.shape, F32)
    for sel in chosen:
        multi = multi + sel.astype(F32)
    before = jnp.dot(tri_ref[...], multi.astype(BF16), preferred_element_type=F32)
    for kk in range(TOP_K):
        idx_ref[:, kk:kk + 1] = picks[kk]
        gates_ref[:, kk:kk + 1] = exps[kk] * inv_total
        rank_ref[:, kk:kk + 1] = jnp.sum(jnp.where(chosen[kk], before, 0.0), axis=-1, keepdims=True).astype(I32)
    cnt_ref[...] = jnp.sum(multi, axis=0, keepdims=True)


def _outproj(attn, pool, x2, gate1, shift2, scale2, g2, wa, wpo, rw, rb, tri, *, seq):
    n_tok, d = x2.shape
    ts = tri.shape[0]
    spt = seq // ts
    row = lambda width: pl.BlockSpec((ts, width), lambda i: (i, 0))
    per_batch = pl.BlockSpec((None, 1, d), lambda i: (i // spt, 0, 0))
    full = lambda a: pl.BlockSpec(a.shape, lambda i: (0,) * a.ndim)
    return pl.pallas_call(
        _outproj_kernel,
        out_shape=[jax.ShapeDtypeStruct((n_tok, d), F32),
                   jax.ShapeDtypeStruct((n_tok, d), BF16),
                   jax.ShapeDtypeStruct((n_tok, TOP_K), I32),
                   jax.ShapeDtypeStruct((n_tok, TOP_K), I32),
                   jax.ShapeDtypeStruct((n_tok, TOP_K), F32),
                   jax.ShapeDtypeStruct((n_tok // ts, 1, N_EXPERTS), F32)],
        grid=(n_tok // ts,),
        in_specs=[row(ATTN_WIDTH), row(POOL_WIDTH), row(d), per_batch, per_batch, per_batch, full(g2),
                  full(wa), full(wpo), full(rw), full(rb), full(tri)],
        out_specs=[row(d), row(d), row(TOP_K), row(TOP_K), row(TOP_K),
                   pl.BlockSpec((None, 1, N_EXPERTS), lambda i: (i, 0, 0))],
        compiler_params=_params("arbitrary"),
        name="outproj",
    )(attn, pool, x2, gate1, shift2, scale2, g2, wa, wpo, rw, rb, tri)


def _chunk_loops(cnt_ref, tile, chunk, body):
    def per_expert(e, carry):
        n_chunks = (cnt_ref[tile * N_EXPERTS + e] + chunk - 1) // chunk

        def per_chunk(i, c):
            body(e, i)
            return c

        return lax.fori_loop(0, n_chunks, per_chunk, carry)

    lax.fori_loop(0, N_EXPERTS, per_expert, 0)


def _dispatch_kernel(cnt_ref, lo_ref, gs_ref, fill_ref, idx_t_ref, rank_t_ref, lo_col_ref, h_ref, xs_ref,
                     stage_ref, sem, *, tt):
    t = pl.program_id(0)
    slot = t % 2
    n_rows = TOP_K * tt
    u = DISPATCH_CHUNK
    m = EXPERT_BLOCK

    @pl.when(t == 0)
    def _():
        stage_ref[...] = jnp.zeros_like(stage_ref)

        def pad_copy(e, i):
            return pltpu.make_async_copy(stage_ref.at[1, _tile_rows(0, u)],
                                         xs_ref.at[_tile_rows(fill_ref[e] - (i + 1) * u, u)], sem)

        def tail_copy(i):
            return pltpu.make_async_copy(stage_ref.at[1, _tile_rows(0, m)],
                                         xs_ref.at[_tile_rows(fill_ref[N_EXPERTS - 1] + i * m, m)], sem)

        n_tail = (xs_ref.shape[0] // SUBLANES - fill_ref[N_EXPERTS - 1]) // m
        for act in ("start", "wait"):
            _chunk_loops(fill_ref, 1, u, lambda e, i: getattr(pad_copy(e, i), act)())
            lax.fori_loop(0, n_tail, lambda i, c: (getattr(tail_copy(i), act)(), c)[1], 0)

    expert = lax.broadcasted_iota(I32, (N_EXPERTS, tt), 0)
    rows = lax.broadcasted_iota(I32, (n_rows, tt), 0)
    hit = None
    for k in range(TOP_K):
        onehot = expert == idx_t_ref[k:k + 1, :]
        base = jnp.sum(jnp.where(onehot, lo_col_ref[...], 0.0), axis=0, keepdims=True)
        slot_row = base.astype(I32) + rank_t_ref[k:k + 1, :]
        match = rows == slot_row
        hit = match if hit is None else (hit | match)
    perm = jnp.where(hit, 1.0, 0.0).astype(BF16)
    grouped = jnp.dot(perm, h_ref[...], preferred_element_type=F32)
    for c in range(SUBLANES):
        stage_ref[slot, _feature_chunk(c, n_rows), :] = grouped[:, c * LANES:(c + 1) * LANES]

    def chunk_copy(tile, buf, e, i):
        src = lo_ref[tile * N_EXPERTS + e] + i * u
        dst = gs_ref[tile * N_EXPERTS + e] + i * u
        return pltpu.make_async_copy(stage_ref.at[buf, _tile_rows(src, u)], xs_ref.at[_tile_rows(dst, u)], sem)

    @pl.when(t > 0)
    def _():
        _chunk_loops(cnt_ref, t - 1, u, lambda e, i: chunk_copy(t - 1, 1 - slot, e, i).wait())

    _chunk_loops(cnt_ref, t, u, lambda e, i: chunk_copy(t, slot, e, i).start())

    @pl.when(t == pl.num_programs(0) - 1)
    def _():
        _chunk_loops(cnt_ref, t, u, lambda e, i: chunk_copy(t, slot, e, i).wait())


def _dispatch(cnt, lo, gstart, fill, idx_t, rank_t, lo_col, h2, n_slots):
    n_tok, d = h2.shape
    tt = ROUTE_TILE
    grid_spec = pltpu.PrefetchScalarGridSpec(
        num_scalar_prefetch=4,
        grid=(n_tok // tt,),
        in_specs=[pl.BlockSpec((TOP_K, tt), lambda i, *_: (0, i)),
                  pl.BlockSpec((TOP_K, tt), lambda i, *_: (0, i)),
                  pl.BlockSpec((None, N_EXPERTS, 1), lambda i, *_: (i, 0, 0)),
                  pl.BlockSpec((tt, d), lambda i, *_: (i, 0))],
        out_specs=pl.BlockSpec(memory_space=pl.ANY),
        scratch_shapes=[pltpu.VMEM((2, (TOP_K * tt + DISPATCH_CHUNK) * SUBLANES, LANES), F32),
                        pltpu.SemaphoreType.DMA],
    )
    return pl.pallas_call(
        functools.partial(_dispatch_kernel, tt=tt),
        out_shape=jax.ShapeDtypeStruct((n_slots * SUBLANES, d // SUBLANES), F32),
        grid_spec=grid_spec,
        compiler_params=pltpu.CompilerParams(dimension_semantics=("arbitrary",),
                                             vmem_limit_bytes=VMEM_LIMIT_BYTES, has_side_effects=True),
        name="dispatch",
    )(cnt, lo, gstart, fill, idx_t, rank_t, lo_col, h2)


def _experts_kernel(be_ref, nvalid_ref, nused_ref, xs_ref, wgu_ref, bgu_ref, wd_ref, bd_ref, yb_ref, *, d_ff):
    del be_ref
    i = pl.program_id(0)

    @pl.when(i < nused_ref[0])
    def _():
        m = xs_ref.shape[0] // SUBLANES
        x = jnp.concatenate([xs_ref[_feature_chunk(c, m), :] for c in range(SUBLANES)], axis=1)
        row = lax.broadcasted_iota(I32, (x.shape[0], 1), 0)
        xb = jnp.where(row < nvalid_ref[i], x, 0.0).astype(BF16)
        acc = None
        for j in range(d_ff // FF_CHUNK):
            lo, hi = j * FF_CHUNK, (j + 1) * FF_CHUNK
            g = jnp.dot(xb, wgu_ref[:, lo:hi], preferred_element_type=F32) + bgu_ref[:, lo:hi]
            u = jnp.dot(xb, wgu_ref[:, d_ff + lo:d_ff + hi], preferred_element_type=F32) + bgu_ref[:, d_ff + lo:d_ff + hi]
            g = jnp.minimum(g, SWIGLU_LIMIT)
            u = jnp.clip(u, -SWIGLU_LIMIT, SWIGLU_LIMIT)
            act = g * (1.0 / (1.0 + jnp.exp(-SWIGLU_ALPHA * g))) * (u + 1.0)
            part = jnp.dot(act.astype(BF16), wd_ref[lo:hi, :], preferred_element_type=F32)
            acc = part if acc is None else acc + part
        y = acc + bd_ref[...]
        for c in range(SUBLANES):
            yb_ref[_feature_chunk(c, m), :] = y[:, c * LANES:(c + 1) * LANES]

    @pl.when(i >= nused_ref[0])
    def _():
        yb_ref[...] = jnp.zeros_like(yb_ref)


def _experts(block_expert, n_valid, n_used, xs, wgu, bgu, wd, bd):
    n_slots = xs.shape[0] // SUBLANES
    d_ff, d = wd.shape[1], wd.shape[2]
    m = EXPERT_BLOCK
    blk = lambda i, be, nv, nu: (jnp.minimum(i, nu[0] - 1), 0)
    per_expert = lambda i, be, nv, nu: (be[i], 0, 0)
    grid_spec = pltpu.PrefetchScalarGridSpec(
        num_scalar_prefetch=3,
        grid=(n_slots // m,),
        in_specs=[pl.BlockSpec((m * SUBLANES, LANES), blk),
                  pl.BlockSpec((None, d, 2 * d_ff), per_expert),
                  pl.BlockSpec((None, 1, 2 * d_ff), per_expert),
                  pl.BlockSpec((None, d_ff, d), per_expert),
                  pl.BlockSpec((None, 1, d), per_expert)],
        out_specs=pl.BlockSpec((m * SUBLANES, LANES), lambda i, be, nv, nu: (i, 0)),
    )
    return pl.pallas_call(
        functools.partial(_experts_kernel, d_ff=d_ff),
        out_shape=jax.ShapeDtypeStruct(xs.shape, F32),
        grid_spec=grid_spec,
        compiler_params=_params("arbitrary"),
        name="experts",
    )(block_expert, n_valid, n_used, xs, wgu, bgu, wd, bd)


def _combine_kernel(cnt_ref, lo_ref, gs_ref, idx_ref, rank_ref, gates_ref, lo_row_ref, x_ref, gate2_ref, yb_ref,
                    xo_ref, stage_ref, sems, *, tt, n_stage):
    t = pl.program_id(0)
    slot = t % 2
    u = COMBINE_CHUNK

    def chunk_copy(tile, buf, e, i):
        src = gs_ref[tile * N_EXPERTS + e] + i * u
        dst = lo_ref[tile * N_EXPERTS + e] + i * u
        return pltpu.make_async_copy(yb_ref.at[_tile_rows(src, u)], stage_ref.at[buf, _tile_rows(dst, u)],
                                     sems.at[buf])

    @pl.when(t == 0)
    def _():
        stage_ref[...] = jnp.zeros_like(stage_ref)
        _chunk_loops(cnt_ref, 0, u, lambda e, i: chunk_copy(0, 0, e, i).start())

    @pl.when(t + 1 < pl.num_programs(0))
    def _():
        _chunk_loops(cnt_ref, t + 1, u, lambda e, i: chunk_copy(t + 1, 1 - slot, e, i).start())

    _chunk_loops(cnt_ref, t, u, lambda e, i: chunk_copy(t, slot, e, i).wait())

    lane = lax.broadcasted_iota(I32, (tt, N_EXPERTS), 1)
    cols = lax.broadcasted_iota(I32, (tt, n_stage), 1)
    weights = jnp.zeros((tt, n_stage), F32)
    for k in range(TOP_K):
        onehot = lane == idx_ref[:, k:k + 1]
        base = jnp.sum(jnp.where(onehot, lo_row_ref[...], 0.0), axis=1, keepdims=True)
        col = base.astype(I32) + rank_ref[:, k:k + 1]
        weights = weights + jnp.where(cols == col, gates_ref[:, k:k + 1], 0.0)
    staged = jnp.concatenate([stage_ref[slot, _feature_chunk(c, n_stage), :] for c in range(SUBLANES)],
                             axis=1)
    y = jnp.dot(weights.astype(BF16), staged.astype(BF16), preferred_element_type=F32)
    xo_ref[...] = x_ref[...] + gate2_ref[...] * y


def _combine(cnt, lo8, gstart, idx, rank, gates, lo_row, x2, gate2, yb, *, seq):
    n_tok, d = x2.shape
    tt = ROUTE_TILE
    spt = seq // tt
    n_stage = TOP_K * tt + N_EXPERTS * COMBINE_CHUNK
    grid_spec = pltpu.PrefetchScalarGridSpec(
        num_scalar_prefetch=3,
        grid=(n_tok // tt,),
        in_specs=[pl.BlockSpec((tt, TOP_K), lambda i, *_: (i, 0)),
                  pl.BlockSpec((tt, TOP_K), lambda i, *_: (i, 0)),
                  pl.BlockSpec((tt, TOP_K), lambda i, *_: (i, 0)),
                  pl.BlockSpec((None, 1, N_EXPERTS), lambda i, *_: (i, 0, 0)),
                  pl.BlockSpec((tt, d), lambda i, *_: (i, 0)),
                  pl.BlockSpec((None, 1, d), lambda i, *_: (i // spt, 0, 0)),
                  pl.BlockSpec(memory_space=pl.ANY)],
        out_specs=pl.BlockSpec((tt, d), lambda i, *_: (i, 0)),
        scratch_shapes=[pltpu.VMEM((2, n_stage * SUBLANES, LANES), F32),
                        pltpu.SemaphoreType.DMA((2,))],
    )
    return pl.pallas_call(
        functools.partial(_combine_kernel, tt=tt, n_stage=n_stage),
        out_shape=jax.ShapeDtypeStruct((n_tok, d), F32),
        grid_spec=grid_spec,
        compiler_params=_params("arbitrary"),
        name="combine",
    )(cnt, lo8, gstart, idx, rank, gates, lo_row, x2, gate2, yb)


def _exclusive_cumsum(a, axis):
    return jnp.cumsum(a, axis=axis) - a


def _routing_tables(tile_cnt):
    m = EXPERT_BLOCK
    total = jnp.sum(tile_cnt, axis=0)
    padded = (total + (DISPATCH_CHUNK - 1) + m - 1) // m * m
    pends = jnp.cumsum(padded)
    pstart = pends - padded
    gstart = pstart[None, :] + _exclusive_cumsum(tile_cnt, 0)
    lo = _exclusive_cumsum(tile_cnt, 1)
    cu = COMBINE_CHUNK
    lo_c = _exclusive_cumsum((tile_cnt + cu - 1) // cu * cu, 1)
    return total, pstart, pends, gstart, lo, lo_c


def kernel(x, c, positions, ada_w, ada_b, norm1_g, w_in, q_norm_g, k_norm_g, attn_sinks, pool_w, pool_b,
           pool_scale, w_out, norm2_g, router_w, router_b, expert_w_gu, expert_b_gu, expert_w_down,
           expert_b_down):
    batch, seq, d = x.shape
    n_layers = ada_w.shape[0]
    n_tok = batch * seq
    d_ff = expert_w_down.shape[2]
    assert d == SUBLANES * LANES
    assert seq % WINDOW == 0 and seq % min(TOKEN_TILE, seq) == 0 and seq % ROUTE_TILE == 0

    mod = _ada(c, ada_w, ada_b).reshape(n_layers, batch, 6, 1, d)
    rope = _rope_tables(positions)
    sel, unrot = _rope_select_matrix()
    bdq, bdk = _head_mean_matrix(ATTN_WIDTH), _head_mean_matrix(KV_WIDTH)
    tri = jnp.asarray(np.tril(np.ones((ROUTE_TILE, ROUTE_TILE), np.float32), -1), BF16)

    m = EXPERT_BLOCK
    n_blocks = -(-n_tok * TOP_K // m) + N_EXPERTS + 1
    n_slots = n_blocks * m
    n_tiles = n_tok // ROUTE_TILE

    x2 = x.reshape(n_tok, d)
    for l in range(n_layers):
        shift1, scale1, gate1, shift2, scale2, gate2 = (mod[l, :, j] for j in range(6))
        q, k, v, pool = _inproj(
            x2, shift1, scale1, norm1_g[l].reshape(1, d), w_in[l].astype(BF16),
            jnp.tile(q_norm_g[l], N_Q_HEADS).reshape(1, ATTN_WIDTH),
            jnp.tile(k_norm_g[l], N_KV_HEADS).reshape(1, KV_WIDTH),
            rope, sel, unrot, bdq, bdk, pool_w[l].astype(BF16), pool_b[l].reshape(1, POOL_WIDTH),
            pool_scale[l].reshape(1, POOL_WIDTH), batch=batch, seq=seq)
        attn = _attention(q, k, v, attn_sinks[l], seq=seq)
        w_out_b = w_out[l].astype(BF16)
        x2, h2, idx, rank, gates, tile_cnt = _outproj(
            attn, pool, x2, gate1, shift2, scale2, norm2_g[l].reshape(1, d),
            w_out_b[:ATTN_WIDTH], w_out_b[ATTN_WIDTH:], router_w[l].astype(BF16),
            router_b[l].reshape(1, N_EXPERTS),
            tri, seq=seq)

        tile_cnt = tile_cnt.reshape(n_tiles, N_EXPERTS).astype(I32)
        total, pstart, pends, gstart, lo, lo_c = _routing_tables(tile_cnt)
        n_used = (pends[-1] // m).astype(I32).reshape(1)
        block_row = jnp.minimum(jnp.arange(n_blocks, dtype=I32), n_used[0] - 1) * m
        block_expert = jnp.minimum(jnp.sum(block_row[:, None] >= pends[None, :], axis=1), N_EXPERTS - 1).astype(I32)
        n_valid = jnp.clip((pstart + total)[block_expert] - block_row, 0, m).astype(I32)

        flat = lambda a: a.reshape(-1).astype(I32)
        fill = jnp.concatenate([pends, pends - pstart - total]).astype(I32)
        xs = _dispatch(flat(tile_cnt), flat(lo), flat(gstart), fill, idx.T, rank.T,
                       lo.astype(F32).reshape(n_tiles, N_EXPERTS, 1), h2, n_slots)
        yb = _experts(block_expert, n_valid, n_used, xs, expert_w_gu[l].astype(BF16),
                      expert_b_gu[l].reshape(N_EXPERTS, 1, 2 * d_ff), expert_w_down[l].astype(BF16),
                      expert_b_down[l].reshape(N_EXPERTS, 1, d))
        x2 = _combine(flat(tile_cnt), flat(lo_c), flat(gstart), idx, rank, gates,
                      lo_c.astype(F32).reshape(n_tiles, 1, N_EXPERTS), x2, gate2, yb, seq=seq)
    return x2.reshape(batch, seq, d)
```

```python
import functools

import numpy as np
import jax
import jax.numpy as jnp
from jax import lax
from jax.experimental import pallas as pl
from jax.experimental.pallas import tpu as pltpu

F32 = jnp.float32
BF16 = jnp.bfloat16
I32 = jnp.int32

HEAD_DIM = 64
N_Q_HEADS = 8
N_KV_HEADS = 2
Q_PER_KV = N_Q_HEADS // N_KV_HEADS
ATTN_WIDTH = N_Q_HEADS * HEAD_DIM
KV_WIDTH = N_KV_HEADS * HEAD_DIM
WINDOW = 128
ROT_DIM = HEAD_DIM // 4
ROT_HALF = ROT_DIM // 2
ROPE_THETA = 500000.0
POOL_WINDOWS = (2, 4, 8, 16)
N_POOL_GROUPS = len(POOL_WINDOWS)
POOL_GROUP_DIM = 128
POOL_WIDTH = N_POOL_GROUPS * POOL_GROUP_DIM
MAX_POOL_WINDOW = max(POOL_WINDOWS)
N_EXPERTS = 32
TOP_K = 4
SWIGLU_LIMIT = 7.0
SWIGLU_ALPHA = 1.702
EPS = 1e-6

LANES = 128
SUBLANES = 8
VMEM_LIMIT_BYTES = 56 * 1024 * 1024

TOKEN_TILE = 512
ROUTE_TILE = 256
SORT_TILE = 128
EXPERT_BLOCK = 512
FF_CHUNK = 512
RUN_CHUNK = 32
FILL_CHUNK = 16


def _params(*sem):
    return pltpu.CompilerParams(dimension_semantics=sem, vmem_limit_bytes=VMEM_LIMIT_BYTES)


def _tile_rows(start, n):
    return pl.ds(pl.multiple_of(start * SUBLANES, SUBLANES), n * SUBLANES)


def _feature_chunk(c, n):
    return pl.ds(c, n, stride=SUBLANES)


def _ada_kernel(c_ref, w_ref, b_ref, o_ref):
    c = c_ref[...]
    c_act = c * (1.0 / (1.0 + jnp.exp(-c)))
    o_ref[0] = jnp.dot(c_act, w_ref[0], preferred_element_type=F32,
                       precision=lax.Precision.HIGHEST) + b_ref[0]


def _ada(c, ada_w, ada_b):
    n_layers, d, d6 = ada_w.shape
    b = c.shape[0]
    tn = 1536
    return pl.pallas_call(
        _ada_kernel,
        out_shape=jax.ShapeDtypeStruct((n_layers, b, d6), F32),
        grid=(n_layers, d6 // tn),
        in_specs=[pl.BlockSpec((b, d), lambda l, j: (0, 0)),
                  pl.BlockSpec((1, d, tn), lambda l, j: (l, 0, j)),
                  pl.BlockSpec((1, 1, tn), lambda l, j: (l, 0, j))],
        out_specs=pl.BlockSpec((1, b, tn), lambda l, j: (l, 0, j)),
        compiler_params=_params("arbitrary", "arbitrary"),
        name="ada",
    )(c, ada_w, ada_b.reshape(n_layers, 1, d6))


def _rope_kernel(pos_ref, freq_ref, ch_ref, cl_ref, sh_ref, sl_ref):
    ang = pos_ref[...].astype(F32) * freq_ref[...]
    for fn, hi_ref, lo_ref in ((jnp.cos, ch_ref, cl_ref), (jnp.sin, sh_ref, sl_ref)):
        val = fn(ang)
        hi = val.astype(BF16)
        hi_ref[...] = hi
        lo_ref[...] = (val - hi.astype(F32)).astype(BF16)


def _rope_tables(positions):
    n_tok = positions.size
    inv_freq = ROPE_THETA ** (-jnp.arange(0, ROT_DIM, 2, dtype=F32) / ROT_DIM)
    rows = n_tok * ROT_HALF // LANES
    pos_rep = jnp.repeat(positions.reshape(-1), ROT_HALF).reshape(rows, LANES)
    freq = jnp.tile(inv_freq, LANES // ROT_HALF).reshape(1, LANES)
    tr = min(rows, 512)
    spec = pl.BlockSpec((tr, LANES), lambda i: (i, 0))
    outs = pl.pallas_call(
        _rope_kernel,
        out_shape=[jax.ShapeDtypeStruct((rows, LANES), BF16)] * 4,
        grid=(rows // tr,),
        in_specs=[spec, pl.BlockSpec((1, LANES), lambda i: (0, 0))],
        out_specs=[spec] * 4,
        compiler_params=_params("arbitrary"),
        name="rope",
    )(pos_rep, freq)
    return jnp.concatenate([o.reshape(n_tok, ROT_HALF) for o in outs], axis=1)


def _rope_select_matrix():
    sel = np.zeros((4 * ROT_HALF, 3 * LANES), np.float32)
    for d in range(LANES):
        dd = d % HEAD_DIM
        if dd < ROT_DIM:
            f = dd % ROT_HALF
            sel[f, d] = 1.0
            sel[ROT_HALF + f, d] = 1.0
        if dd < ROT_HALF:
            sel[2 * ROT_HALF + dd, LANES + d] = -1.0
            sel[3 * ROT_HALF + dd, LANES + d] = -1.0
        elif dd < ROT_DIM:
            sel[2 * ROT_HALF + dd - ROT_HALF, 2 * LANES + d] = 1.0
            sel[3 * ROT_HALF + dd - ROT_HALF, 2 * LANES + d] = 1.0
    unrotated = np.array([[1.0 if (d % HEAD_DIM) >= ROT_DIM else 0.0 for d in range(LANES)]], np.float32)
    return jnp.asarray(sel, BF16), jnp.asarray(unrotated, F32)


def _head_mean_matrix(width):
    m = np.zeros((width, width), np.float32)
    for h in range(width // HEAD_DIM):
        m[h * HEAD_DIM:(h + 1) * HEAD_DIM, h * HEAD_DIM:(h + 1) * HEAD_DIM] = 1.0 / HEAD_DIM
    return jnp.asarray(m, BF16)


def _inproj_kernel(x_ref, shift_ref, scale_ref, g1_ref, w_ref, qg_ref, kg_ref, rope_ref, sel_ref,
                   unrot_ref, bdq_ref, bdk_ref, wp_ref, bp_ref, ps_ref,
                   q_ref, k_ref, v_ref, pool_ref, carry_ref, *, ts):
    s_idx = pl.program_id(1)

    @pl.when(s_idx == 0)
    def _():
        carry_ref[...] = jnp.zeros_like(carry_ref)

    x = x_ref[...]
    ms = jnp.mean(x * x, axis=-1, keepdims=True)
    h = x * lax.rsqrt(ms + EPS) * (g1_ref[...] * (1.0 + scale_ref[...])) + shift_ref[...]
    hb = h.astype(BF16)

    tab = jnp.dot(rope_ref[...], sel_ref[...], preferred_element_type=F32)
    cos_t = tab[:, :LANES] + unrot_ref[...]
    sin_a = tab[:, LANES:2 * LANES]
    sin_b = tab[:, 2 * LANES:]

    def norm_rope(t, bd_ref, g_ref, out_scale):
        msq = jnp.dot((t * t).astype(BF16), bd_ref[...], preferred_element_type=F32)
        tn = t * lax.rsqrt(msq + EPS) * g_ref[...]
        chunks = []
        for c in range(t.shape[1] // LANES):
            tc = tn[:, c * LANES:(c + 1) * LANES]
            rot = (tc * cos_t + pltpu.roll(tc, LANES - ROT_HALF, 1) * sin_a
                   + pltpu.roll(tc, ROT_HALF, 1) * sin_b)
            chunks.append(rot * out_scale)
        return chunks[0] if len(chunks) == 1 else jnp.concatenate(chunks, axis=1)

    q = jnp.dot(hb, w_ref[:, :ATTN_WIDTH], preferred_element_type=F32)
    q_ref[...] = norm_rope(q, bdq_ref, qg_ref, HEAD_DIM ** -0.5).astype(BF16)
    k = jnp.dot(hb, w_ref[:, ATTN_WIDTH:ATTN_WIDTH + KV_WIDTH], preferred_element_type=F32)
    k_ref[...] = norm_rope(k, bdk_ref, kg_ref, 1.0).astype(BF16)
    v = jnp.dot(hb, w_ref[:, ATTN_WIDTH + KV_WIDTH:ATTN_WIDTH + 2 * KV_WIDTH], preferred_element_type=F32)
    v_ref[...] = v.astype(BF16)

    u = jnp.dot(hb, w_ref[:, ATTN_WIDTH + 2 * KV_WIDTH:], preferred_element_type=F32)
    ext = jnp.concatenate([carry_ref[...], u], axis=0)
    carry_ref[...] = u[ts - MAX_POOL_WINDOW:, :]
    count = (lax.broadcasted_iota(I32, (ts, POOL_GROUP_DIM), 0) + (s_idx * ts + 1)).astype(F32)
    outs = []
    for g, w in enumerate(POOL_WINDOWS):
        win = ext[:, g * POOL_GROUP_DIM:(g + 1) * POOL_GROUP_DIM]
        span = 1
        while span < w:
            win = win + pltpu.roll(win, span, 0)
            span *= 2
        mean = win[MAX_POOL_WINDOW:, :] * (1.0 / jnp.minimum(count, float(w)))
        pooled = mean - u[:, g * POOL_GROUP_DIM:(g + 1) * POOL_GROUP_DIM]
        outs.append(jnp.dot(pooled.astype(BF16), wp_ref[g], preferred_element_type=F32))
    y = (jnp.concatenate(outs, axis=1) + bp_ref[...]) * ps_ref[...]
    pool_ref[...] = y.astype(BF16)


def _inproj(x2, shift1, scale1, g1, w_in, qg, kg, rope, sel, unrot, bdq, bdk, wp, bp, ps, *, batch, seq):
    n_tok, d = x2.shape
    ts = min(TOKEN_TILE, seq)
    spt = seq // ts
    row = lambda width: pl.BlockSpec((ts, width), lambda b, s: (b * spt + s, 0))
    per_batch = pl.BlockSpec((None, 1, d), lambda b, s: (b, 0, 0))
    full = lambda a: pl.BlockSpec(a.shape, lambda b, s: (0,) * a.ndim)
    return pl.pallas_call(
        functools.partial(_inproj_kernel, ts=ts),
        out_shape=[jax.ShapeDtypeStruct((n_tok, ATTN_WIDTH), BF16),
                   jax.ShapeDtypeStruct((n_tok, KV_WIDTH), BF16),
                   jax.ShapeDtypeStruct((n_tok, KV_WIDTH), BF16),
                   jax.ShapeDtypeStruct((n_tok, POOL_WIDTH), BF16)],
        grid=(batch, spt),
        in_specs=[row(d), per_batch, per_batch, full(g1), full(w_in), full(qg), full(kg),
                  row(rope.shape[1]), full(sel), full(unrot), full(bdq), full(bdk),
                  full(wp), full(bp), full(ps)],
        out_specs=[row(ATTN_WIDTH), row(KV_WIDTH), row(KV_WIDTH), row(POOL_WIDTH)],
        scratch_shapes=[pltpu.VMEM((MAX_POOL_WINDOW, POOL_WIDTH), F32)],
        compiler_params=_params("arbitrary", "arbitrary"),
        name="inproj",
    )(x2, shift1, scale1, g1, w_in, qg, kg, rope, sel, unrot, bdq, bdk, wp, bp, ps)


def _attn_kernel(sink_ref, q_ref, kc_ref, vc_ref, kp_ref, vp_ref, o_ref, *, tq, tiles_per_seq):
    first = (pl.program_id(0) % tiles_per_seq) == 0
    kfull = jnp.concatenate([kp_ref[...], kc_ref[...]], axis=0)
    vfull = jnp.concatenate([vp_ref[...], vc_ref[...]], axis=0)
    rows = Q_PER_KV * WINDOW
    qi = lax.broadcasted_iota(I32, (rows, 2 * WINDOW), 0) % WINDOW
    kj = lax.broadcasted_iota(I32, (rows, 2 * WINDOW), 1)
    allowed = (kj > qi) & (kj <= qi + WINDOW)
    allowed_first = allowed & (jnp.logical_not(first) | (kj >= WINDOW))
    row_head = lax.broadcasted_iota(I32, (rows, 1), 0) // WINDOW
    for j in range(tq // WINDOW):
        kb = kfull[j * WINDOW:(j + 2) * WINDOW, :]
        vb = vfull[j * WINDOW:(j + 2) * WINDOW, :]
        mask = allowed_first if j == 0 else allowed
        for h in range(N_KV_HEADS):
            qs = jnp.concatenate(
                [q_ref[j * WINDOW:(j + 1) * WINDOW, (h * Q_PER_KV + g) * HEAD_DIM:(h * Q_PER_KV + g + 1) * HEAD_DIM]
                 for g in range(Q_PER_KV)], axis=0)
            s = lax.dot_general(qs, kb[:, h * HEAD_DIM:(h + 1) * HEAD_DIM], (((1,), (1,)), ((), ())),
                                preferred_element_type=F32)
            s = jnp.where(mask, s, -jnp.inf)
            sink = jnp.zeros((rows, 1), F32)
            for g in range(Q_PER_KV):
                sink = jnp.where(row_head == g, sink_ref[h * Q_PER_KV + g], sink)
            m = jnp.maximum(jnp.max(s, axis=-1, keepdims=True), sink)
            p = jnp.exp(s - m)
            denom = jnp.sum(p, axis=-1, keepdims=True) + jnp.exp(sink - m)
            o = jnp.dot(p.astype(BF16), vb[:, h * HEAD_DIM:(h + 1) * HEAD_DIM], preferred_element_type=F32)
            o = o * (1.0 / denom)
            o_ref[j * WINDOW:(j + 1) * WINDOW, h * Q_PER_KV * HEAD_DIM:(h + 1) * Q_PER_KV * HEAD_DIM] = (
                jnp.concatenate([o[g * WINDOW:(g + 1) * WINDOW, :] for g in range(Q_PER_KV)], axis=1).astype(BF16))


def _attention(q, k, v, sinks, *, seq):
    n_tok = q.shape[0]
    tq = min(TOKEN_TILE, seq)
    per = tq // WINDOW
    cur = lambda width: pl.BlockSpec((tq, width), lambda i: (i, 0))
    prev = pl.BlockSpec((WINDOW, KV_WIDTH), lambda i: (jnp.maximum(i * per - 1, 0), 0))
    return pl.pallas_call(
        functools.partial(_attn_kernel, tq=tq, tiles_per_seq=seq // tq),
        out_shape=jax.ShapeDtypeStruct((n_tok, ATTN_WIDTH), BF16),
        grid=(n_tok // tq,),
        in_specs=[pl.BlockSpec(memory_space=pltpu.SMEM), cur(ATTN_WIDTH), cur(KV_WIDTH), cur(KV_WIDTH), prev, prev],
        out_specs=cur(ATTN_WIDTH),
        compiler_params=_params("arbitrary"),
        name="attn",
    )(sinks, q, k, v, k, v)


def _outproj_kernel(attn_ref, pool_ref, x_ref, gate1_ref, shift2_ref, scale2_ref, g2_ref, wa_ref, wpo_ref,
                    rw_ref, rb_ref, tri_ref,
                    xo_ref, h2_ref, idx_ref, rank_ref, gates_ref, cnt_ref):
    mixed = (jnp.dot(attn_ref[...], wa_ref[...], preferred_element_type=F32)
             + jnp.dot(pool_ref[...], wpo_ref[...], preferred_element_type=F32))
    x = x_ref[...] + gate1_ref[...] * mixed
    xo_ref[...] = x
    ms = jnp.mean(x * x, axis=-1, keepdims=True)
    h2 = x * lax.rsqrt(ms + EPS) * (g2_ref[...] * (1.0 + scale2_ref[...])) + shift2_ref[...]
    h2b = h2.astype(BF16)
    h2_ref[...] = h2b

    logits = jnp.dot(h2b, rw_ref[...], preferred_element_type=F32) + rb_ref[...]
    lane = lax.broadcasted_iota(I32, logits.shape, 1)
    work = logits
    vals, picks, chosen = [], [], []
    for _ in range(TOP_K):
        m = jnp.max(work, axis=-1, keepdims=True)
        pick = jnp.min(jnp.where(work == m, lane, N_EXPERTS), axis=-1, keepdims=True)
        sel = lane == pick
        vals.append(m)
        picks.append(pick)
        chosen.append(sel)
        work = jnp.where(sel, -jnp.inf, work)
    exps = [jnp.exp(v - vals[0]) for v in vals]
    inv_total = 1.0 / (exps[0] + exps[1] + exps[2] + exps[3])

    multi = jnp.zeros(logits.shape, F32)
    for sel in chosen:
        multi = multi + sel.astype(F32)
    before = jnp.dot(tri_ref[...], multi.astype(BF16), preferred_element_type=F32)
    for kk in range(TOP_K):
        idx_ref[:, kk:kk + 1] = picks[kk]
        gates_ref[:, kk:kk + 1] = exps[kk] * inv_total
        rank_ref[:, kk:kk + 1] = jnp.sum(jnp.where(chosen[kk], before, 0.0), axis=-1, keepdims=True).astype(I32)
    for s in range(multi.shape[0] // SORT_TILE):
        cnt_ref[s] = jnp.sum(multi[s * SORT_TILE:(s + 1) * SORT_TILE], axis=0, keepdims=True)


def _outproj(attn, pool, x2, gate1, shift2, scale2, g2, wa, wpo, rw, rb, tri, *, seq):
    n_tok, d = x2.shape
    ts = tri.shape[0]
    spt = seq // ts
    row = lambda width: pl.BlockSpec((ts, width), lambda i: (i, 0))
    per_batch = pl.BlockSpec((None, 1, d), lambda i: (i // spt, 0, 0))
    full = lambda a: pl.BlockSpec(a.shape, lambda i: (0,) * a.ndim)
    return pl.pallas_call(
        _outproj_kernel,
        out_shape=[jax.ShapeDtypeStruct((n_tok, d), F32),
                   jax.ShapeDtypeStruct((n_tok, d), BF16),
                   jax.ShapeDtypeStruct((n_tok, TOP_K), I32),
                   jax.ShapeDtypeStruct((n_tok, TOP_K), I32),
                   jax.ShapeDtypeStruct((n_tok, TOP_K), F32),
                   jax.ShapeDtypeStruct((n_tok // SORT_TILE, 1, N_EXPERTS), F32)],
        grid=(n_tok // ts,),
        in_specs=[row(ATTN_WIDTH), row(POOL_WIDTH), row(d), per_batch, per_batch, per_batch, full(g2),
                  full(wa), full(wpo), full(rw), full(rb), full(tri)],
        out_specs=[row(d), row(d), row(TOP_K), row(TOP_K), row(TOP_K),
                   pl.BlockSpec((ts // SORT_TILE, 1, N_EXPERTS), lambda i: (i, 0, 0))],
        compiler_params=_params("arbitrary"),
        name="outproj",
    )(attn, pool, x2, gate1, shift2, scale2, g2, wa, wpo, rw, rb, tri)


def _chunk_loops(cnt_ref, tile, chunk, body):
    def per_expert(e, carry):
        n_chunks = (cnt_ref[tile * N_EXPERTS + e] + chunk - 1) // chunk

        def per_chunk(i, c):
            body(e, i)
            return c

        return lax.fori_loop(0, n_chunks, per_chunk, carry)

    lax.fori_loop(0, N_EXPERTS, per_expert, 0)


def _run_copies(cnt_ref, extra_ref, tile, make_copy, act):
    for e in range(N_EXPERTS):
        getattr(make_copy(tile, e, 0), act)()

    @pl.when(extra_ref[tile] > 0)
    def _():
        def per_expert(e, carry):
            n_chunks = (cnt_ref[tile * N_EXPERTS + e] + RUN_CHUNK - 1) // RUN_CHUNK

            def per_chunk(i, c):
                getattr(make_copy(tile, e, i), act)()
                return c

            return lax.fori_loop(1, n_chunks, per_chunk, carry)

        lax.fori_loop(0, N_EXPERTS, per_expert, 0)


def _dispatch_kernel(cnt_ref, extra_ref, lo_ref, gs_ref, fill_ref, idx_t_ref, rank_t_ref, lo_col_ref, h_ref,
                     xs_ref, stage_ref, sem, *, tt):
    t = pl.program_id(0)
    slot = t % 2
    n_rows = TOP_K * tt
    u = FILL_CHUNK
    m = EXPERT_BLOCK

    @pl.when(t == 0)
    def _():
        stage_ref[...] = jnp.zeros_like(stage_ref)

        def pad_copy(e, i):
            return pltpu.make_async_copy(stage_ref.at[1, _tile_rows(0, u)],
                                         xs_ref.at[_tile_rows(fill_ref[e] - (i + 1) * u, u)], sem)

        def tail_copy(i):
            return pltpu.make_async_copy(stage_ref.at[1, _tile_rows(0, m)],
                                         xs_ref.at[_tile_rows(fill_ref[N_EXPERTS - 1] + i * m, m)], sem)

        n_tail = (xs_ref.shape[0] // SUBLANES - fill_ref[N_EXPERTS - 1]) // m
        for act in ("start", "wait"):
            _chunk_loops(fill_ref, 1, u, lambda e, i: getattr(pad_copy(e, i), act)())
            lax.fori_loop(0, n_tail, lambda i, c: (getattr(tail_copy(i), act)(), c)[1], 0)

    expert = lax.broadcasted_iota(I32, (N_EXPERTS, tt), 0)
    rows = lax.broadcasted_iota(I32, (n_rows, tt), 0)
    hit = None
    for k in range(TOP_K):
        onehot = expert == idx_t_ref[k:k + 1, :]
        base = jnp.sum(jnp.where(onehot, lo_col_ref[...], 0.0), axis=0, keepdims=True)
        slot_row = base.astype(I32) + rank_t_ref[k:k + 1, :]
        match = rows == slot_row
        hit = match if hit is None else (hit | match)
    perm = jnp.where(hit, 1.0, 0.0).astype(BF16)
    grouped = jnp.dot(perm, h_ref[...], preferred_element_type=F32)
    for c in range(SUBLANES):
        stage_ref[slot, _feature_chunk(c, n_rows), :] = grouped[:, c * LANES:(c + 1) * LANES]

    def chunk_copy(tile, e, i):
        src = lo_ref[tile * N_EXPERTS + e] + i * RUN_CHUNK
        dst = gs_ref[tile * N_EXPERTS + e] + i * RUN_CHUNK
        return pltpu.make_async_copy(stage_ref.at[tile % 2, _tile_rows(src, RUN_CHUNK)],
                                     xs_ref.at[_tile_rows(dst, RUN_CHUNK)], sem)

    @pl.when(t > 0)
    def _():
        _run_copies(cnt_ref, extra_ref, t - 1, chunk_copy, "wait")

    _run_copies(cnt_ref, extra_ref, t, chunk_copy, "start")

    @pl.when(t == pl.num_programs(0) - 1)
    def _():
        _run_copies(cnt_ref, extra_ref, t, chunk_copy, "wait")


def _dispatch(cnt, extra, lo, gstart, fill, idx_t, rank_t, lo_col, h2, n_slots):
    n_tok, d = h2.shape
    tt = SORT_TILE
    grid_spec = pltpu.PrefetchScalarGridSpec(
        num_scalar_prefetch=5,
        grid=(n_tok // tt,),
        in_specs=[pl.BlockSpec((TOP_K, tt), lambda i, *_: (0, i)),
                  pl.BlockSpec((TOP_K, tt), lambda i, *_: (0, i)),
                  pl.BlockSpec((None, N_EXPERTS, 1), lambda i, *_: (i, 0, 0)),
                  pl.BlockSpec((tt, d), lambda i, *_: (i, 0))],
        out_specs=pl.BlockSpec(memory_space=pl.ANY),
        scratch_shapes=[pltpu.VMEM((2, (TOP_K * tt + RUN_CHUNK) * SUBLANES, LANES), F32),
                        pltpu.SemaphoreType.DMA],
    )
    return pl.pallas_call(
        functools.partial(_dispatch_kernel, tt=tt),
        out_shape=jax.ShapeDtypeStruct((n_slots * SUBLANES, d // SUBLANES), F32),
        grid_spec=grid_spec,
        compiler_params=pltpu.CompilerParams(dimension_semantics=("arbitrary",),
                                             vmem_limit_bytes=VMEM_LIMIT_BYTES, has_side_effects=True),
        name="dispatch",
    )(cnt, extra, lo, gstart, fill, idx_t, rank_t, lo_col, h2)


def _experts_kernel(be_ref, nvalid_ref, nused_ref, xs_ref, wgu_ref, bgu_ref, wd_ref, bd_ref, yb_ref, *, d_ff):
    del be_ref
    i = pl.program_id(0)

    @pl.when(i < nused_ref[0])
    def _():
        m = xs_ref.shape[0] // SUBLANES
        x = jnp.concatenate([xs_ref[_feature_chunk(c, m), :] for c in range(SUBLANES)], axis=1)
        row = lax.broadcasted_iota(I32, (x.shape[0], 1), 0)
        xb = jnp.where(row < nvalid_ref[i], x, 0.0).astype(BF16)
        acc = None
        for j in range(d_ff // FF_CHUNK):
            lo, hi = j * FF_CHUNK, (j + 1) * FF_CHUNK
            g = jnp.dot(xb, wgu_ref[:, lo:hi], preferred_element_type=F32) + bgu_ref[:, lo:hi]
            u = jnp.dot(xb, wgu_ref[:, d_ff + lo:d_ff + hi], preferred_element_type=F32) + bgu_ref[:, d_ff + lo:d_ff + hi]
            g = jnp.minimum(g, SWIGLU_LIMIT)
            u = jnp.clip(u, -SWIGLU_LIMIT, SWIGLU_LIMIT)
            act = g * (1.0 / (1.0 + jnp.exp(-SWIGLU_ALPHA * g))) * (u + 1.0)
            part = jnp.dot(act.astype(BF16), wd_ref[lo:hi, :], preferred_element_type=F32)
            acc = part if acc is None else acc + part
        y = acc + bd_ref[...]
        for c in range(SUBLANES):
            yb_ref[_feature_chunk(c, m), :] = y[:, c * LANES:(c + 1) * LANES]

    @pl.when(i >= nused_ref[0])
    def _():
        yb_ref[...] = jnp.zeros_like(yb_ref)


def _experts(block_expert, n_valid, n_used, xs, wgu, bgu, wd, bd):
    n_slots = xs.shape[0] // SUBLANES
    d_ff, d = wd.shape[1], wd.shape[2]
    m = EXPERT_BLOCK
    blk = lambda i, be, nv, nu: (jnp.minimum(i, nu[0] - 1), 0)
    per_expert = lambda i, be, nv, nu: (be[i], 0, 0)
    grid_spec = pltpu.PrefetchScalarGridSpec(
        num_scalar_prefetch=3,
        grid=(n_slots // m,),
        in_specs=[pl.BlockSpec((m * SUBLANES, LANES), blk),
                  pl.BlockSpec((None, d, 2 * d_ff), per_expert),
                  pl.BlockSpec((None, 1, 2 * d_ff), per_expert),
                  pl.BlockSpec((None, d_ff, d), per_expert),
                  pl.BlockSpec((None, 1, d), per_expert)],
        out_specs=pl.BlockSpec((m * SUBLANES, LANES), lambda i, be, nv, nu: (i, 0)),
    )
    return pl.pallas_call(
        functools.partial(_experts_kernel, d_ff=d_ff),
        out_shape=jax.ShapeDtypeStruct(xs.shape, F32),
        grid_spec=grid_spec,
        compiler_params=_params("arbitrary"),
        name="experts",
    )(block_expert, n_valid, n_used, xs, wgu, bgu, wd, bd)


def _combine_kernel(cnt_ref, extra_ref, lo_ref, gs_ref, idx_ref, rank_ref, gates_ref, lo_row_ref, x_ref, gate2_ref,
                    yb_ref, xo_ref, stage_ref, sems, *, tt, n_stage):
    t = pl.program_id(0)
    slot = t % 2

    def chunk_copy(tile, e, i):
        src = gs_ref[tile * N_EXPERTS + e] + i * RUN_CHUNK
        dst = lo_ref[tile * N_EXPERTS + e] + i * RUN_CHUNK
        return pltpu.make_async_copy(yb_ref.at[_tile_rows(src, RUN_CHUNK)],
                                     stage_ref.at[tile % 2, _tile_rows(dst, RUN_CHUNK)], sems.at[tile % 2])

    @pl.when(t == 0)
    def _():
        stage_ref[...] = jnp.zeros_like(stage_ref)
        _run_copies(cnt_ref, extra_ref, 0, chunk_copy, "start")

    @pl.when(t + 1 < pl.num_programs(0))
    def _():
        _run_copies(cnt_ref, extra_ref, t + 1, chunk_copy, "start")

    _run_copies(cnt_ref, extra_ref, t, chunk_copy, "wait")

    lane = lax.broadcasted_iota(I32, (tt, N_EXPERTS), 1)
    cols = lax.broadcasted_iota(I32, (tt, n_stage), 1)
    weights = jnp.zeros((tt, n_stage), F32)
    for k in range(TOP_K):
        onehot = lane == idx_ref[:, k:k + 1]
        base = jnp.sum(jnp.where(onehot, lo_row_ref[...], 0.0), axis=1, keepdims=True)
        col = base.astype(I32) + rank_ref[:, k:k + 1]
        weights = weights + jnp.where(cols == col, gates_ref[:, k:k + 1], 0.0)
    staged = jnp.concatenate([stage_ref[slot, _feature_chunk(c, n_stage), :] for c in range(SUBLANES)],
                             axis=1)
    y = jnp.dot(weights.astype(BF16), staged.astype(BF16), preferred_element_type=F32)
    xo_ref[...] = x_ref[...] + gate2_ref[...] * y


def _combine(cnt, extra, lo_c, gstart, idx, rank, gates, lo_row, x2, gate2, yb, *, seq):
    n_tok, d = x2.shape
    tt = SORT_TILE
    spt = seq // tt
    n_stage = TOP_K * tt + N_EXPERTS * RUN_CHUNK
    grid_spec = pltpu.PrefetchScalarGridSpec(
        num_scalar_prefetch=4,
        grid=(n_tok // tt,),
        in_specs=[pl.BlockSpec((tt, TOP_K), lambda i, *_: (i, 0)),
                  pl.BlockSpec((tt, TOP_K), lambda i, *_: (i, 0)),
                  pl.BlockSpec((tt, TOP_K), lambda i, *_: (i, 0)),
                  pl.BlockSpec((None, 1, N_EXPERTS), lambda i, *_: (i, 0, 0)),
                  pl.BlockSpec((tt, d), lambda i, *_: (i, 0)),
                  pl.BlockSpec((None, 1, d), lambda i, *_: (i // spt, 0, 0)),
                  pl.BlockSpec(memory_space=pl.ANY)],
        out_specs=pl.BlockSpec((tt, d), lambda i, *_: (i, 0)),
        scratch_shapes=[pltpu.VMEM((2, n_stage * SUBLANES, LANES), F32),
                        pltpu.SemaphoreType.DMA((2,))],
    )
    return pl.pallas_call(
        functools.partial(_combine_kernel, tt=tt, n_stage=n_stage),
        out_shape=jax.ShapeDtypeStruct((n_tok, d), F32),
        grid_spec=grid_spec,
        compiler_params=_params("arbitrary"),
        name="combine",
    )(cnt, extra, lo_c, gstart, idx, rank, gates, lo_row, x2, gate2, yb)


def _exclusive_cumsum(a, axis):
    return jnp.cumsum(a, axis=axis) - a


def _routing_tables(tile_cnt):
    m = EXPERT_BLOCK
    total = jnp.sum(tile_cnt, axis=0)
    padded = (total + RUN_CHUNK + m - 1) // m * m
    pends = jnp.cumsum(padded)
    pstart = pends - padded
    gstart = pstart[None, :] + _exclusive_cumsum(tile_cnt, 0)
    lo = _exclusive_cumsum(tile_cnt, 1)
    n_chunks = jnp.maximum((tile_cnt + RUN_CHUNK - 1) // RUN_CHUNK, 1)
    lo_c = _exclusive_cumsum(n_chunks * RUN_CHUNK, 1)
    extra = jnp.sum(n_chunks - 1, axis=1)
    return total, pstart, pends, gstart, lo, lo_c, extra


def kernel(x, c, positions, ada_w, ada_b, norm1_g, w_in, q_norm_g, k_norm_g, attn_sinks, pool_w, pool_b,
           pool_scale, w_out, norm2_g, router_w, router_b, expert_w_gu, expert_b_gu, expert_w_down,
           expert_b_down):
    batch, seq, d = x.shape
    n_layers = ada_w.shape[0]
    n_tok = batch * seq
    d_ff = expert_w_down.shape[2]
    assert d == SUBLANES * LANES
    assert seq % WINDOW == 0 and seq % min(TOKEN_TILE, seq) == 0 and seq % ROUTE_TILE == 0

    mod = _ada(c, ada_w, ada_b).reshape(n_layers, batch, 6, 1, d)
    rope = _rope_tables(positions)
    sel, unrot = _rope_select_matrix()
    bdq, bdk = _head_mean_matrix(ATTN_WIDTH), _head_mean_matrix(KV_WIDTH)
    tri = jnp.asarray(np.kron(np.eye(ROUTE_TILE // SORT_TILE, dtype=np.float32),
                              np.tril(np.ones((SORT_TILE, SORT_TILE), np.float32), -1)), BF16)

    m = EXPERT_BLOCK
    n_blocks = -(-(n_tok * TOP_K + N_EXPERTS * (RUN_CHUNK + m - 1)) // m)
    n_slots = n_blocks * m
    n_tiles = n_tok // SORT_TILE

    x2 = x.reshape(n_tok, d)
    for l in range(n_layers):
        shift1, scale1, gate1, shift2, scale2, gate2 = (mod[l, :, j] for j in range(6))
        q, k, v, pool = _inproj(
            x2, shift1, scale1, norm1_g[l].reshape(1, d), w_in[l].astype(BF16),
            jnp.tile(q_norm_g[l], N_Q_HEADS).reshape(1, ATTN_WIDTH),
            jnp.tile(k_norm_g[l], N_KV_HEADS).reshape(1, KV_WIDTH),
            rope, sel, unrot, bdq, bdk, pool_w[l].astype(BF16), pool_b[l].reshape(1, POOL_WIDTH),
            pool_scale[l].reshape(1, POOL_WIDTH), batch=batch, seq=seq)
        attn = _attention(q, k, v, attn_sinks[l], seq=seq)
        w_out_b = w_out[l].astype(BF16)
        x2, h2, idx, rank, gates, tile_cnt = _outproj(
            attn, pool, x2, gate1, shift2, scale2, norm2_g[l].reshape(1, d),
            w_out_b[:ATTN_WIDTH], w_out_b[ATTN_WIDTH:], router_w[l].astype(BF16),
            router_b[l].reshape(1, N_EXPERTS),
            tri, seq=seq)

        tile_cnt = tile_cnt.reshape(n_tiles, N_EXPERTS).astype(I32)
        total, pstart, pends, gstart, lo, lo_c, extra = _routing_tables(tile_cnt)
        n_used = (pends[-1] // m).astype(I32).reshape(1)
        block_row = jnp.minimum(jnp.arange(n_blocks, dtype=I32), n_used[0] - 1) * m
        block_expert = jnp.minimum(jnp.sum(block_row[:, None] >= pends[None, :], axis=1), N_EXPERTS - 1).astype(I32)
        n_valid = jnp.clip((pstart + total)[block_expert] - block_row, 0, m).astype(I32)

        flat = lambda a: a.reshape(-1).astype(I32)
        fill = jnp.concatenate([pends, pends - pstart - total]).astype(I32)
        xs = _dispatch(flat(tile_cnt), flat(extra), flat(lo), flat(gstart), fill, idx.T, rank.T,
                       lo.astype(F32).reshape(n_tiles, N_EXPERTS, 1), h2, n_slots)
        yb = _experts(block_expert, n_valid, n_used, xs, expert_w_gu[l].astype(BF16),
                      expert_b_gu[l].reshape(N_EXPERTS, 1, 2 * d_ff), expert_w_down[l].astype(BF16),
                      expert_b_down[l].reshape(N_EXPERTS, 1, d))
        x2 = _combine(flat(tile_cnt), flat(extra), flat(lo_c), flat(gstart), idx, rank, gates,
                      lo_c.astype(F32).reshape(n_tiles, 1, N_EXPERTS), x2, gate2, yb, seq=seq)
    return x2.reshape(batch, seq, d)
```

```python
import functools

import numpy as np
import jax
import jax.numpy as jnp
from jax import lax
from jax.experimental import pallas as pl
from jax.experimental.pallas import tpu as pltpu

F32 = jnp.float32
BF16 = jnp.bfloat16
I32 = jnp.int32
U32 = jnp.uint32

HEAD_DIM = 64
N_Q_HEADS = 8
N_KV_HEADS = 2
Q_PER_KV = N_Q_HEADS // N_KV_HEADS
ATTN_WIDTH = N_Q_HEADS * HEAD_DIM
KV_WIDTH = N_KV_HEADS * HEAD_DIM
WINDOW = 128
ROT_DIM = HEAD_DIM // 4
ROT_HALF = ROT_DIM // 2
ROPE_THETA = 500000.0
POOL_WINDOWS = (2, 4, 8, 16)
N_POOL_GROUPS = len(POOL_WINDOWS)
POOL_GROUP_DIM = 128
POOL_WIDTH = N_POOL_GROUPS * POOL_GROUP_DIM
MAX_POOL_WINDOW = max(POOL_WINDOWS)
N_EXPERTS = 32
TOP_K = 4
SWIGLU_LIMIT = 7.0
SWIGLU_ALPHA = 1.702
EPS = 1e-6

LANES = 128
SUBLANES = 8
VMEM_LIMIT_BYTES = 56 * 1024 * 1024

TOKEN_TILE = 512
ROUTE_TILE = 256
SORT_TILE = 256
EXPERT_BLOCK = 512
FF_CHUNK = 512
RUN_CHUNK = 32
FILL_CHUNK = 16
MAX_EXTRA_CHUNKS = TOP_K * SORT_TILE // RUN_CHUNK


def _params(*sem):
    return pltpu.CompilerParams(dimension_semantics=sem, vmem_limit_bytes=VMEM_LIMIT_BYTES)


ROW_SUBLANES = 4
HALF_WIDTH = ROW_SUBLANES * LANES
HIGH_HALF_MASK = 0xFFFF0000


def _tile_rows(start, n):
    return pl.ds(pl.multiple_of(start * ROW_SUBLANES, ROW_SUBLANES), n * ROW_SUBLANES)


def _word_chunk(c, n):
    return pl.ds(c, n, stride=ROW_SUBLANES)


def _pack_rows(vals):
    lo = lax.bitcast_convert_type(vals[:, :HALF_WIDTH], U32)
    hi = lax.bitcast_convert_type(vals[:, HALF_WIDTH:], U32)
    return lax.shift_right_logical(lo, jnp.uint32(16)) | hi


def _store_rows(ref, n, words):
    for c in range(ROW_SUBLANES):
        ref[_word_chunk(c, n), :] = words[:, c * LANES:(c + 1) * LANES]


def _load_rows(ref, n, keep=None):
    lo, hi = [], []
    for c in range(ROW_SUBLANES):
        w = ref[_word_chunk(c, n), :]
        if keep is not None:
            w = jnp.where(keep, w, jnp.uint32(0))
        lo.append(lax.bitcast_convert_type(lax.shift_left(w, jnp.uint32(16)), F32))
        hi.append(lax.bitcast_convert_type(w & jnp.uint32(HIGH_HALF_MASK), F32))
    return jnp.concatenate(lo + hi, axis=1).astype(BF16)


def _ada_kernel(c_ref, w_ref, b_ref, o_ref):
    c = c_ref[...]
    c_act = c * (1.0 / (1.0 + jnp.exp(-c)))
    o_ref[0] = jnp.dot(c_act, w_ref[0], preferred_element_type=F32,
                       precision=lax.Precision.HIGHEST) + b_ref[0]


def _ada(c, ada_w, ada_b):
    n_layers, d, d6 = ada_w.shape
    b = c.shape[0]
    tn = 1536
    return pl.pallas_call(
        _ada_kernel,
        out_shape=jax.ShapeDtypeStruct((n_layers, b, d6), F32),
        grid=(n_layers, d6 // tn),
        in_specs=[pl.BlockSpec((b, d), lambda l, j: (0, 0)),
                  pl.BlockSpec((1, d, tn), lambda l, j: (l, 0, j)),
                  pl.BlockSpec((1, 1, tn), lambda l, j: (l, 0, j))],
        out_specs=pl.BlockSpec((1, b, tn), lambda l, j: (l, 0, j)),
        compiler_params=_params("arbitrary", "arbitrary"),
        name="ada",
    )(c, ada_w, ada_b.reshape(n_layers, 1, d6))


def _rope_kernel(pos_ref, freq_ref, ch_ref, cl_ref, sh_ref, sl_ref):
    ang = pos_ref[...].astype(F32) * freq_ref[...]
    for fn, hi_ref, lo_ref in ((jnp.cos, ch_ref, cl_ref), (jnp.sin, sh_ref, sl_ref)):
        val = fn(ang)
        hi = val.astype(BF16)
        hi_ref[...] = hi
        lo_ref[...] = (val - hi.astype(F32)).astype(BF16)


def _rope_tables(positions):
    n_tok = positions.size
    inv_freq = ROPE_THETA ** (-jnp.arange(0, ROT_DIM, 2, dtype=F32) / ROT_DIM)
    rows = n_tok * ROT_HALF // LANES
    pos_rep = jnp.repeat(positions.reshape(-1), ROT_HALF).reshape(rows, LANES)
    freq = jnp.tile(inv_freq, LANES // ROT_HALF).reshape(1, LANES)
    tr = min(rows, 512)
    spec = pl.BlockSpec((tr, LANES), lambda i: (i, 0))
    outs = pl.pallas_call(
        _rope_kernel,
        out_shape=[jax.ShapeDtypeStruct((rows, LANES), BF16)] * 4,
        grid=(rows // tr,),
        in_specs=[spec, pl.BlockSpec((1, LANES), lambda i: (0, 0))],
        out_specs=[spec] * 4,
        compiler_params=_params("arbitrary"),
        name="rope",
    )(pos_rep, freq)
    return jnp.concatenate([o.reshape(n_tok, ROT_HALF) for o in outs], axis=1)


def _rope_select_matrix():
    sel = np.zeros((4 * ROT_HALF, 3 * LANES), np.float32)
    for d in range(LANES):
        dd = d % HEAD_DIM
        if dd < ROT_DIM:
            f = dd % ROT_HALF
            sel[f, d] = 1.0
            sel[ROT_HALF + f, d] = 1.0
        if dd < ROT_HALF:
            sel[2 * ROT_HALF + dd, LANES + d] = -1.0
            sel[3 * ROT_HALF + dd, LANES + d] = -1.0
        elif dd < ROT_DIM:
            sel[2 * ROT_HALF + dd - ROT_HALF, 2 * LANES + d] = 1.0
            sel[3 * ROT_HALF + dd - ROT_HALF, 2 * LANES + d] = 1.0
    unrotated = np.array([[1.0 if (d % HEAD_DIM) >= ROT_DIM else 0.0 for d in range(LANES)]], np.float32)
    return jnp.asarray(sel, BF16), jnp.asarray(unrotated, F32)


def _head_mean_matrix(width):
    m = np.zeros((width, width), np.float32)
    for h in range(width // HEAD_DIM):
        m[h * HEAD_DIM:(h + 1) * HEAD_DIM, h * HEAD_DIM:(h + 1) * HEAD_DIM] = 1.0 / HEAD_DIM
    return jnp.asarray(m, BF16)


def _inproj_kernel(x_ref, shift_ref, scale_ref, g1_ref, w_ref, qg_ref, kg_ref, rope_ref, sel_ref,
                   unrot_ref, bdq_ref, bdk_ref, wp_ref, bp_ref, ps_ref,
                   q_ref, k_ref, v_ref, pool_ref, carry_ref, *, ts):
    s_idx = pl.program_id(1)

    @pl.when(s_idx == 0)
    def _():
        carry_ref[...] = jnp.zeros_like(carry_ref)

    x = x_ref[...]
    ms = jnp.mean(x * x, axis=-1, keepdims=True)
    h = x * lax.rsqrt(ms + EPS) * (g1_ref[...] * (1.0 + scale_ref[...])) + shift_ref[...]
    hb = h.astype(BF16)

    tab = jnp.dot(rope_ref[...], sel_ref[...], preferred_element_type=F32)
    cos_t = tab[:, :LANES] + unrot_ref[...]
    sin_a = tab[:, LANES:2 * LANES]
    sin_b = tab[:, 2 * LANES:]

    def norm_rope(t, bd_ref, g_ref, out_scale):
        msq = jnp.dot((t * t).astype(BF16), bd_ref[...], preferred_element_type=F32)
        tn = t * lax.rsqrt(msq + EPS) * g_ref[...]
        chunks = []
        for c in range(t.shape[1] // LANES):
            tc = tn[:, c * LANES:(c + 1) * LANES]
            rot = (tc * cos_t + pltpu.roll(tc, LANES - ROT_HALF, 1) * sin_a
                   + pltpu.roll(tc, ROT_HALF, 1) * sin_b)
            chunks.append(rot * out_scale)
        return chunks[0] if len(chunks) == 1 else jnp.concatenate(chunks, axis=1)

    q = jnp.dot(hb, w_ref[:, :ATTN_WIDTH], preferred_element_type=F32)
    q_ref[...] = norm_rope(q, bdq_ref, qg_ref, HEAD_DIM ** -0.5).astype(BF16)
    k = jnp.dot(hb, w_ref[:, ATTN_WIDTH:ATTN_WIDTH + KV_WIDTH], preferred_element_type=F32)
    k_ref[...] = norm_rope(k, bdk_ref, kg_ref, 1.0).astype(BF16)
    v = jnp.dot(hb, w_ref[:, ATTN_WIDTH + KV_WIDTH:ATTN_WIDTH + 2 * KV_WIDTH], preferred_element_type=F32)
    v_ref[...] = v.astype(BF16)

    u = jnp.dot(hb, w_ref[:, ATTN_WIDTH + 2 * KV_WIDTH:], preferred_element_type=F32)
    ext = jnp.concatenate([carry_ref[...], u], axis=0)
    carry_ref[...] = u[ts - MAX_POOL_WINDOW:, :]
    count = (lax.broadcasted_iota(I32, (ts, POOL_GROUP_DIM), 0) + (s_idx * ts + 1)).astype(F32)
    outs = []
    for g, w in enumerate(POOL_WINDOWS):
        win = ext[:, g * POOL_GROUP_DIM:(g + 1) * POOL_GROUP_DIM]
        span = 1
        while span < w:
            win = win + pltpu.roll(win, span, 0)
            span *= 2
        mean = win[MAX_POOL_WINDOW:, :] * (1.0 / jnp.minimum(count, float(w)))
        pooled = mean - u[:, g * POOL_GROUP_DIM:(g + 1) * POOL_GROUP_DIM]
        outs.append(jnp.dot(pooled.astype(BF16), wp_ref[g], preferred_element_type=F32))
    y = (jnp.concatenate(outs, axis=1) + bp_ref[...]) * ps_ref[...]
    pool_ref[...] = y.astype(BF16)


def _inproj(x2, shift1, scale1, g1, w_in, qg, kg, rope, sel, unrot, bdq, bdk, wp, bp, ps, *, batch, seq):
    n_tok, d = x2.shape
    ts = min(TOKEN_TILE, seq)
    spt = seq // ts
    row = lambda width: pl.BlockSpec((ts, width), lambda b, s: (b * spt + s, 0))
    per_batch = pl.BlockSpec((None, 1, d), lambda b, s: (b, 0, 0))
    full = lambda a: pl.BlockSpec(a.shape, lambda b, s: (0,) * a.ndim)
    return pl.pallas_call(
        functools.partial(_inproj_kernel, ts=ts),
        out_shape=[jax.ShapeDtypeStruct((n_tok, ATTN_WIDTH), BF16),
                   jax.ShapeDtypeStruct((n_tok, KV_WIDTH), BF16),
                   jax.ShapeDtypeStruct((n_tok, KV_WIDTH), BF16),
                   jax.ShapeDtypeStruct((n_tok, POOL_WIDTH), BF16)],
        grid=(batch, spt),
        in_specs=[row(d), per_batch, per_batch, full(g1), full(w_in), full(qg), full(kg),
                  row(rope.shape[1]), full(sel), full(unrot), full(bdq), full(bdk),
                  full(wp), full(bp), full(ps)],
        out_specs=[row(ATTN_WIDTH), row(KV_WIDTH), row(KV_WIDTH), row(POOL_WIDTH)],
        scratch_shapes=[pltpu.VMEM((MAX_POOL_WINDOW, POOL_WIDTH), F32)],
        compiler_params=_params("arbitrary", "arbitrary"),
        name="inproj",
    )(x2, shift1, scale1, g1, w_in, qg, kg, rope, sel, unrot, bdq, bdk, wp, bp, ps)


def _attn_kernel(sink_ref, q_ref, kc_ref, vc_ref, kp_ref, vp_ref, o_ref, *, tq, tiles_per_seq):
    first = (pl.program_id(0) % tiles_per_seq) == 0
    kfull = jnp.concatenate([kp_ref[...], kc_ref[...]], axis=0)
    vfull = jnp.concatenate([vp_ref[...], vc_ref[...]], axis=0)
    rows = Q_PER_KV * WINDOW
    qi = lax.broadcasted_iota(I32, (rows, 2 * WINDOW), 0) % WINDOW
    kj = lax.broadcasted_iota(I32, (rows, 2 * WINDOW), 1)
    allowed = (kj > qi) & (kj <= qi + WINDOW)
    allowed_first = allowed & (jnp.logical_not(first) | (kj >= WINDOW))
    row_head = lax.broadcasted_iota(I32, (rows, 1), 0) // WINDOW
    for j in range(tq // WINDOW):
        kb = kfull[j * WINDOW:(j + 2) * WINDOW, :]
        vb = vfull[j * WINDOW:(j + 2) * WINDOW, :]
        mask = allowed_first if j == 0 else allowed
        for h in range(N_KV_HEADS):
            qs = jnp.concatenate(
                [q_ref[j * WINDOW:(j + 1) * WINDOW, (h * Q_PER_KV + g) * HEAD_DIM:(h * Q_PER_KV + g + 1) * HEAD_DIM]
                 for g in range(Q_PER_KV)], axis=0)
            s = lax.dot_general(qs, kb[:, h * HEAD_DIM:(h + 1) * HEAD_DIM], (((1,), (1,)), ((), ())),
                                preferred_element_type=F32)
            s = jnp.where(mask, s, -jnp.inf)
            sink = jnp.zeros((rows, 1), F32)
            for g in range(Q_PER_KV):
                sink = jnp.where(row_head == g, sink_ref[h * Q_PER_KV + g], sink)
            m = jnp.maximum(jnp.max(s, axis=-1, keepdims=True), sink)
            p = jnp.exp(s - m)
            denom = jnp.sum(p, axis=-1, keepdims=True) + jnp.exp(sink - m)
            o = jnp.dot(p.astype(BF16), vb[:, h * HEAD_DIM:(h + 1) * HEAD_DIM], preferred_element_type=F32)
            o = o * (1.0 / denom)
            o_ref[j * WINDOW:(j + 1) * WINDOW, h * Q_PER_KV * HEAD_DIM:(h + 1) * Q_PER_KV * HEAD_DIM] = (
                jnp.concatenate([o[g * WINDOW:(g + 1) * WINDOW, :] for g in range(Q_PER_KV)], axis=1).astype(BF16))


def _attention(q, k, v, sinks, *, seq):
    n_tok = q.shape[0]
    tq = min(TOKEN_TILE, seq)
    per = tq // WINDOW
    cur = lambda width: pl.BlockSpec((tq, width), lambda i: (i, 0))
    prev = pl.BlockSpec((WINDOW, KV_WIDTH), lambda i: (jnp.maximum(i * per - 1, 0), 0))
    return pl.pallas_call(
        functools.partial(_attn_kernel, tq=tq, tiles_per_seq=seq // tq),
        out_shape=jax.ShapeDtypeStruct((n_tok, ATTN_WIDTH), BF16),
        grid=(n_tok // tq,),
        in_specs=[pl.BlockSpec(memory_space=pltpu.SMEM), cur(ATTN_WIDTH), cur(KV_WIDTH), cur(KV_WIDTH), prev, prev],
        out_specs=cur(ATTN_WIDTH),
        compiler_params=_params("arbitrary"),
        name="attn",
    )(sinks, q, k, v, k, v)


def _outproj_kernel(attn_ref, pool_ref, x_ref, gate1_ref, shift2_ref, scale2_ref, g2_ref, wa_ref, wpo_ref,
                    rw_ref, rb_ref, tri_ref,
                    xo_ref, h2_ref, idx_ref, rank_ref, gates_ref, cnt_ref):
    mixed = (jnp.dot(attn_ref[...], wa_ref[...], preferred_element_type=F32)
             + jnp.dot(pool_ref[...], wpo_ref[...], preferred_element_type=F32))
    x = x_ref[...] + gate1_ref[...] * mixed
    xo_ref[...] = x
    ms = jnp.mean(x * x, axis=-1, keepdims=True)
    h2 = x * lax.rsqrt(ms + EPS) * (g2_ref[...] * (1.0 + scale2_ref[...])) + shift2_ref[...]
    h2b = h2.astype(BF16)
    h2_ref[...] = h2b

    logits = jnp.dot(h2b, rw_ref[...], preferred_element_type=F32) + rb_ref[...]
    lane = lax.broadcasted_iota(I32, logits.shape, 1)
    work = logits
    vals, picks, chosen = [], [], []
    for _ in range(TOP_K):
        m = jnp.max(work, axis=-1, keepdims=True)
        pick = jnp.min(jnp.where(work == m, lane, N_EXPERTS), axis=-1, keepdims=True)
        sel = lane == pick
        vals.append(m)
        picks.append(pick)
        chosen.append(sel)
        work = jnp.where(sel, -jnp.inf, work)
    exps = [jnp.exp(v - vals[0]) for v in vals]
    inv_total = 1.0 / (exps[0] + exps[1] + exps[2] + exps[3])

    multi = jnp.zeros(logits.shape, F32)
    for sel in chosen:
        multi = multi + sel.astype(F32)
    before = jnp.dot(tri_ref[...], multi.astype(BF16), preferred_element_type=F32)
    for kk in range(TOP_K):
        idx_ref[:, kk:kk + 1] = picks[kk]
        gates_ref[:, kk:kk + 1] = exps[kk] * inv_total
        rank_ref[:, kk:kk + 1] = jnp.sum(jnp.where(chosen[kk], before, 0.0), axis=-1, keepdims=True).astype(I32)
    for s in range(multi.shape[0] // SORT_TILE):
        cnt_ref[s] = jnp.sum(multi[s * SORT_TILE:(s + 1) * SORT_TILE], axis=0, keepdims=True)


def _outproj(attn, pool, x2, gate1, shift2, scale2, g2, wa, wpo, rw, rb, tri, *, seq):
    n_tok, d = x2.shape
    ts = tri.shape[0]
    spt = seq // ts
    row = lambda width: pl.BlockSpec((ts, width), lambda i: (i, 0))
    per_batch = pl.BlockSpec((None, 1, d), lambda i: (i // spt, 0, 0))
    full = lambda a: pl.BlockSpec(a.shape, lambda i: (0,) * a.ndim)
    return pl.pallas_call(
        _outproj_kernel,
        out_shape=[jax.ShapeDtypeStruct((n_tok, d), F32),
                   jax.ShapeDtypeStruct((n_tok, d), BF16),
                   jax.ShapeDtypeStruct((n_tok, TOP_K), I32),
                   jax.ShapeDtypeStruct((n_tok, TOP_K), I32),
                   jax.ShapeDtypeStruct((n_tok, TOP_K), F32),
                   jax.ShapeDtypeStruct((n_tok // SORT_TILE, 1, N_EXPERTS), F32)],
        grid=(n_tok // ts,),
        in_specs=[row(ATTN_WIDTH), row(POOL_WIDTH), row(d), per_batch, per_batch, per_batch, full(g2),
                  full(wa), full(wpo), full(rw), full(rb), full(tri)],
        out_specs=[row(d), row(d), row(TOP_K), row(TOP_K), row(TOP_K),
                   pl.BlockSpec((ts // SORT_TILE, 1, N_EXPERTS), lambda i: (i, 0, 0))],
        compiler_params=_params("arbitrary"),
        name="outproj",
    )(attn, pool, x2, gate1, shift2, scale2, g2, wa, wpo, rw, rb, tri)


def _chunk_loops(cnt_ref, tile, chunk, body):
    def per_expert(e, carry):
        n_chunks = (cnt_ref[tile * N_EXPERTS + e] + chunk - 1) // chunk

        def per_chunk(i, c):
            body(e, i)
            return c

        return lax.fori_loop(0, n_chunks, per_chunk, carry)

    lax.fori_loop(0, N_EXPERTS, per_expert, 0)


def _run_copies(extra_ref, tile, make_copy, act):
    for e in range(N_EXPERTS):
        getattr(make_copy(tile * N_EXPERTS + e, True), act)()

    def per_extra(p, carry):
        getattr(make_copy(tile * MAX_EXTRA_CHUNKS + p, False), act)()
        return carry

    lax.fori_loop(0, extra_ref[tile], per_extra, 0)


def _dispatch_kernel(extra_ref, src_ref, dst_ref, xsrc_ref, xdst_ref, fill_ref, idx_t_ref, rank_t_ref, lo_col_ref,
                     h_ref, xs_ref, stage_ref, sem, *, tt):
    t = pl.program_id(0)
    slot = t % 2
    n_rows = TOP_K * tt
    u = FILL_CHUNK
    m = EXPERT_BLOCK

    @pl.when(t == 0)
    def _():
        stage_ref[...] = jnp.zeros_like(stage_ref)

        def pad_copy(e, i):
            return pltpu.make_async_copy(stage_ref.at[1, _tile_rows(0, u)],
                                         xs_ref.at[_tile_rows(fill_ref[e] - (i + 1) * u, u)], sem)

        def tail_copy(i):
            return pltpu.make_async_copy(stage_ref.at[1, _tile_rows(0, m)],
                                         xs_ref.at[_tile_rows(fill_ref[N_EXPERTS - 1] + i * m, m)], sem)

        n_tail = (xs_ref.shape[0] // ROW_SUBLANES - fill_ref[N_EXPERTS - 1]) // m
        for act in ("start", "wait"):
            _chunk_loops(fill_ref, 1, u, lambda e, i: getattr(pad_copy(e, i), act)())
            lax.fori_loop(0, n_tail, lambda i, c: (getattr(tail_copy(i), act)(), c)[1], 0)

    expert = lax.broadcasted_iota(I32, (N_EXPERTS, tt), 0)
    rows = lax.broadcasted_iota(I32, (n_rows, tt), 0)
    hit = None
    for k in range(TOP_K):
        onehot = expert == idx_t_ref[k:k + 1, :]
        base = jnp.sum(jnp.where(onehot, lo_col_ref[...], 0.0), axis=0, keepdims=True)
        slot_row = base.astype(I32) + rank_t_ref[k:k + 1, :]
        match = rows == slot_row
        hit = match if hit is None else (hit | match)
    perm = jnp.where(hit, 1.0, 0.0).astype(BF16)
    grouped = jnp.dot(perm, h_ref[...], preferred_element_type=F32)
    _store_rows(stage_ref.at[slot], n_rows, _pack_rows(grouped))

    def copies_of(tile):
        def make_copy(j, first):
            src = src_ref[j] if first else xsrc_ref[j]
            dst = dst_ref[j] if first else xdst_ref[j]
            return pltpu.make_async_copy(stage_ref.at[tile % 2, _tile_rows(src, RUN_CHUNK)],
                                         xs_ref.at[_tile_rows(dst, RUN_CHUNK)], sem)
        return make_copy

    @pl.when(t > 0)
    def _():
        _run_copies(extra_ref, t - 1, copies_of(t - 1), "wait")

    _run_copies(extra_ref, t, copies_of(t), "start")

    @pl.when(t == pl.num_programs(0) - 1)
    def _():
        _run_copies(extra_ref, t, copies_of(t), "wait")


def _dispatch(extra, src, dst, xsrc, xdst, fill, idx_t, rank_t, lo_col, h2, n_slots):
    n_tok, d = h2.shape
    tt = SORT_TILE
    grid_spec = pltpu.PrefetchScalarGridSpec(
        num_scalar_prefetch=6,
        grid=(n_tok // tt,),
        in_specs=[pl.BlockSpec((TOP_K, tt), lambda i, *_: (0, i)),
                  pl.BlockSpec((TOP_K, tt), lambda i, *_: (0, i)),
                  pl.BlockSpec((None, N_EXPERTS, 1), lambda i, *_: (i, 0, 0)),
                  pl.BlockSpec((tt, d), lambda i, *_: (i, 0))],
        out_specs=pl.BlockSpec(memory_space=pl.ANY),
        scratch_shapes=[pltpu.VMEM((2, (TOP_K * tt + RUN_CHUNK) * ROW_SUBLANES, LANES), U32),
                        pltpu.SemaphoreType.DMA],
    )
    return pl.pallas_call(
        functools.partial(_dispatch_kernel, tt=tt),
        out_shape=jax.ShapeDtypeStruct((n_slots * ROW_SUBLANES, LANES), U32),
        grid_spec=grid_spec,
        compiler_params=pltpu.CompilerParams(dimension_semantics=("arbitrary",),
                                             vmem_limit_bytes=VMEM_LIMIT_BYTES, has_side_effects=True),
        name="dispatch",
    )(extra, src, dst, xsrc, xdst, fill, idx_t, rank_t, lo_col, h2)


def _experts_kernel(be_ref, nvalid_ref, nused_ref, xs_ref, wgu_ref, bgu_ref, wd_ref, bd_ref, yb_ref, *, d_ff):
    del be_ref
    i = pl.program_id(0)

    @pl.when(i < nused_ref[0])
    def _():
        m = xs_ref.shape[0] // ROW_SUBLANES
        row = lax.broadcasted_iota(I32, (m, 1), 0)
        xb = _load_rows(xs_ref, m, keep=row < nvalid_ref[i])
        acc = None
        for j in range(d_ff // FF_CHUNK):
            lo, hi = j * FF_CHUNK, (j + 1) * FF_CHUNK
            g = jnp.dot(xb, wgu_ref[:, lo:hi], preferred_element_type=F32) + bgu_ref[:, lo:hi]
            u = jnp.dot(xb, wgu_ref[:, d_ff + lo:d_ff + hi], preferred_element_type=F32) + bgu_ref[:, d_ff + lo:d_ff + hi]
            g = jnp.minimum(g, SWIGLU_LIMIT)
            u = jnp.clip(u, -SWIGLU_LIMIT, SWIGLU_LIMIT)
            act = g * (1.0 / (1.0 + jnp.exp(-SWIGLU_ALPHA * g))) * (u + 1.0)
            part = jnp.dot(act.astype(BF16), wd_ref[lo:hi, :], preferred_element_type=F32)
            acc = part if acc is None else acc + part
        y = (acc + bd_ref[...]).astype(BF16).astype(F32)
        _store_rows(yb_ref, m, _pack_rows(y))

    @pl.when(i >= nused_ref[0])
    def _():
        yb_ref[...] = jnp.zeros_like(yb_ref)


def _experts(block_expert, n_valid, n_used, xs, wgu, bgu, wd, bd):
    n_slots = xs.shape[0] // ROW_SUBLANES
    d_ff, d = wd.shape[1], wd.shape[2]
    m = EXPERT_BLOCK
    blk = lambda i, be, nv, nu: (jnp.minimum(i, nu[0] - 1), 0)
    per_expert = lambda i, be, nv, nu: (be[i], 0, 0)
    grid_spec = pltpu.PrefetchScalarGridSpec(
        num_scalar_prefetch=3,
        grid=(n_slots // m,),
        in_specs=[pl.BlockSpec((m * ROW_SUBLANES, LANES), blk),
                  pl.BlockSpec((None, d, 2 * d_ff), per_expert),
                  pl.BlockSpec((None, 1, 2 * d_ff), per_expert),
                  pl.BlockSpec((None, d_ff, d), per_expert),
                  pl.BlockSpec((None, 1, d), per_expert)],
        out_specs=pl.BlockSpec((m * ROW_SUBLANES, LANES), lambda i, be, nv, nu: (i, 0)),
    )
    return pl.pallas_call(
        functools.partial(_experts_kernel, d_ff=d_ff),
        out_shape=jax.ShapeDtypeStruct(xs.shape, U32),
        grid_spec=grid_spec,
        compiler_params=_params("arbitrary"),
        name="experts",
    )(block_expert, n_valid, n_used, xs, wgu, bgu, wd, bd)


def _combine_kernel(extra_ref, src_ref, dst_ref, xsrc_ref, xdst_ref, idx_ref, rank_ref, gates_ref, lo_row_ref,
                    x_ref, gate2_ref, yb_ref, xo_ref, stage_ref, sems, *, tt, n_stage):
    t = pl.program_id(0)
    slot = t % 2

    def copies_of(tile):
        def make_copy(j, first):
            src = src_ref[j] if first else xsrc_ref[j]
            dst = dst_ref[j] if first else xdst_ref[j]
            return pltpu.make_async_copy(yb_ref.at[_tile_rows(src, RUN_CHUNK)],
                                         stage_ref.at[tile % 2, _tile_rows(dst, RUN_CHUNK)], sems.at[tile % 2])
        return make_copy

    @pl.when(t == 0)
    def _():
        stage_ref[...] = jnp.zeros_like(stage_ref)
        _run_copies(extra_ref, 0, copies_of(0), "start")

    @pl.when(t + 1 < pl.num_programs(0))
    def _():
        _run_copies(extra_ref, t + 1, copies_of(t + 1), "start")

    _run_copies(extra_ref, t, copies_of(t), "wait")

    lane = lax.broadcasted_iota(I32, (tt, N_EXPERTS), 1)
    cols = lax.broadcasted_iota(I32, (tt, n_stage), 1)
    weights = jnp.zeros((tt, n_stage), F32)
    for k in range(TOP_K):
        onehot = lane == idx_ref[:, k:k + 1]
        base = jnp.sum(jnp.where(onehot, lo_row_ref[...], 0.0), axis=1, keepdims=True)
        col = base.astype(I32) + rank_ref[:, k:k + 1]
        weights = weights + jnp.where(cols == col, gates_ref[:, k:k + 1], 0.0)
    staged = _load_rows(stage_ref.at[slot], n_stage)
    y = jnp.dot(weights.astype(BF16), staged, preferred_element_type=F32)
    xo_ref[...] = x_ref[...] + gate2_ref[...] * y


def _combine(extra, src, dst, xsrc, xdst, idx, rank, gates, lo_row, x2, gate2, yb, *, seq):
    n_tok, d = x2.shape
    tt = SORT_TILE
    spt = seq // tt
    n_stage = TOP_K * tt + N_EXPERTS * RUN_CHUNK
    grid_spec = pltpu.PrefetchScalarGridSpec(
        num_scalar_prefetch=5,
        grid=(n_tok // tt,),
        in_specs=[pl.BlockSpec((tt, TOP_K), lambda i, *_: (i, 0)),
                  pl.BlockSpec((tt, TOP_K), lambda i, *_: (i, 0)),
                  pl.BlockSpec((tt, TOP_K), lambda i, *_: (i, 0)),
                  pl.BlockSpec((None, 1, N_EXPERTS), lambda i, *_: (i, 0, 0)),
                  pl.BlockSpec((tt, d), lambda i, *_: (i, 0)),
                  pl.BlockSpec((None, 1, d), lambda i, *_: (i // spt, 0, 0)),
                  pl.BlockSpec(memory_space=pl.ANY)],
        out_specs=pl.BlockSpec((tt, d), lambda i, *_: (i, 0)),
        scratch_shapes=[pltpu.VMEM((2, n_stage * ROW_SUBLANES, LANES), U32),
                        pltpu.SemaphoreType.DMA((2,))],
    )
    return pl.pallas_call(
        functools.partial(_combine_kernel, tt=tt, n_stage=n_stage),
        out_shape=jax.ShapeDtypeStruct((n_tok, d), F32),
        grid_spec=grid_spec,
        compiler_params=_params("arbitrary"),
        name="combine",
    )(extra, src, dst, xsrc, xdst, idx, rank, gates, lo_row, x2, gate2, yb)


def _exclusive_cumsum(a, axis):
    return jnp.cumsum(a, axis=axis) - a


def _routing_tables(tile_cnt):
    m = EXPERT_BLOCK
    total = jnp.sum(tile_cnt, axis=0)
    padded = (total + RUN_CHUNK + m - 1) // m * m
    pends = jnp.cumsum(padded)
    pstart = pends - padded
    gstart = pstart[None, :] + _exclusive_cumsum(tile_cnt, 0)
    lo = _exclusive_cumsum(tile_cnt, 1)
    n_chunks = jnp.maximum((tile_cnt + RUN_CHUNK - 1) // RUN_CHUNK, 1)
    lo_c = _exclusive_cumsum(n_chunks * RUN_CHUNK, 1)
    n_extra = n_chunks - 1
    ends = jnp.cumsum(n_extra, axis=1)
    extra = ends[:, -1]
    p = jnp.arange(MAX_EXTRA_CHUNKS, dtype=I32)
    x_expert = jnp.minimum(jnp.sum(p[None, :, None] >= ends[:, None, :], axis=2), N_EXPERTS - 1)
    x_offset = (p[None, :] - jnp.take_along_axis(ends - n_extra, x_expert, axis=1) + 1) * RUN_CHUNK
    return total, pstart, pends, gstart, lo, lo_c, extra, x_expert, x_offset


def kernel(x, c, positions, ada_w, ada_b, norm1_g, w_in, q_norm_g, k_norm_g, attn_sinks, pool_w, pool_b,
           pool_scale, w_out, norm2_g, router_w, router_b, expert_w_gu, expert_b_gu, expert_w_down,
           expert_b_down):
    batch, seq, d = x.shape
    n_layers = ada_w.shape[0]
    n_tok = batch * seq
    d_ff = expert_w_down.shape[2]
    assert d == SUBLANES * LANES
    assert seq % WINDOW == 0 and seq % min(TOKEN_TILE, seq) == 0 and seq % ROUTE_TILE == 0

    mod = _ada(c, ada_w, ada_b).reshape(n_layers, batch, 6, 1, d)
    rope = _rope_tables(positions)
    sel, unrot = _rope_select_matrix()
    bdq, bdk = _head_mean_matrix(ATTN_WIDTH), _head_mean_matrix(KV_WIDTH)
    tri = jnp.asarray(np.kron(np.eye(ROUTE_TILE // SORT_TILE, dtype=np.float32),
                              np.tril(np.ones((SORT_TILE, SORT_TILE), np.float32), -1)), BF16)

    m = EXPERT_BLOCK
    n_blocks = -(-(n_tok * TOP_K + N_EXPERTS * (RUN_CHUNK + m - 1)) // m)
    n_slots = n_blocks * m
    n_tiles = n_tok // SORT_TILE

    x2 = x.reshape(n_tok, d)
    for l in range(n_layers):
        shift1, scale1, gate1, shift2, scale2, gate2 = (mod[l, :, j] for j in range(6))
        q, k, v, pool = _inproj(
            x2, shift1, scale1, norm1_g[l].reshape(1, d), w_in[l].astype(BF16),
            jnp.tile(q_norm_g[l], N_Q_HEADS).reshape(1, ATTN_WIDTH),
            jnp.tile(k_norm_g[l], N_KV_HEADS).reshape(1, KV_WIDTH),
            rope, sel, unrot, bdq, bdk, pool_w[l].astype(BF16), pool_b[l].reshape(1, POOL_WIDTH),
            pool_scale[l].reshape(1, POOL_WIDTH), batch=batch, seq=seq)
        attn = _attention(q, k, v, attn_sinks[l], seq=seq)
        w_out_b = w_out[l].astype(BF16)
        x2, h2, idx, rank, gates, tile_cnt = _outproj(
            attn, pool, x2, gate1, shift2, scale2, norm2_g[l].reshape(1, d),
            w_out_b[:ATTN_WIDTH], w_out_b[ATTN_WIDTH:], router_w[l].astype(BF16),
            router_b[l].reshape(1, N_EXPERTS),
            tri, seq=seq)

        tile_cnt = tile_cnt.reshape(n_tiles, N_EXPERTS).astype(I32)
        total, pstart, pends, gstart, lo, lo_c, extra, x_expert, x_offset = _routing_tables(tile_cnt)
        flat = lambda a: a.reshape(-1).astype(I32)
        extra_rows = lambda table: flat(jnp.take_along_axis(table, x_expert, axis=1) + x_offset)
        n_used = (pends[-1] // m).astype(I32).reshape(1)
        block_row = jnp.minimum(jnp.arange(n_blocks, dtype=I32), n_used[0] - 1) * m
        block_expert = jnp.minimum(jnp.sum(block_row[:, None] >= pends[None, :], axis=1), N_EXPERTS - 1).astype(I32)
        n_valid = jnp.clip((pstart + total)[block_expert] - block_row, 0, m).astype(I32)

        fill = jnp.concatenate([pends, pends - pstart - total]).astype(I32)
        xs = _dispatch(flat(extra), flat(lo), flat(gstart), extra_rows(lo), extra_rows(gstart), fill,
                       idx.T, rank.T, lo.astype(F32).reshape(n_tiles, N_EXPERTS, 1), h2, n_slots)
        yb = _experts(block_expert, n_valid, n_used, xs, expert_w_gu[l].astype(BF16),
                      expert_b_gu[l].reshape(N_EXPERTS, 1, 2 * d_ff), expert_w_down[l].astype(BF16),
                      expert_b_down[l].reshape(N_EXPERTS, 1, d))
        x2 = _combine(flat(extra), flat(gstart), flat(lo_c), extra_rows(gstart), extra_rows(lo_c),
                      idx, rank, gates, lo_c.astype(F32).reshape(n_tiles, 1, N_EXPERTS), x2, gate2, yb, seq=seq)
    return x2.reshape(batch, seq, d)
```

```python
import functools

import numpy as np
import jax
import jax.numpy as jnp
from jax import lax
from jax.experimental import pallas as pl
from jax.experimental.pallas import tpu as pltpu

F32 = jnp.float32
BF16 = jnp.bfloat16
I32 = jnp.int32
U32 = jnp.uint32

HEAD_DIM = 64
N_Q_HEADS = 8
N_KV_HEADS = 2
Q_PER_KV = N_Q_HEADS // N_KV_HEADS
ATTN_WIDTH = N_Q_HEADS * HEAD_DIM
KV_WIDTH = N_KV_HEADS * HEAD_DIM
WINDOW = 128
ROT_DIM = HEAD_DIM // 4
ROT_HALF = ROT_DIM // 2
ROPE_THETA = 500000.0
POOL_WINDOWS = (2, 4, 8, 16)
N_POOL_GROUPS = len(POOL_WINDOWS)
POOL_GROUP_DIM = 128
POOL_WIDTH = N_POOL_GROUPS * POOL_GROUP_DIM
MAX_POOL_WINDOW = max(POOL_WINDOWS)
N_EXPERTS = 32
TOP_K = 4
SWIGLU_LIMIT = 7.0
SWIGLU_ALPHA = 1.702
EPS = 1e-6

LANES = 128
SUBLANES = 8
VMEM_LIMIT_BYTES = 56 * 1024 * 1024

TOKEN_TILE = 512
ROUTE_TILE = 256
SORT_TILE = 256
EXPERT_BLOCK = 512
FF_CHUNK = 512
RUN_CHUNK = 32
FILL_CHUNK = 16
MAX_EXTRA_CHUNKS = TOP_K * SORT_TILE // RUN_CHUNK


def _params(*sem):
    return pltpu.CompilerParams(dimension_semantics=sem, vmem_limit_bytes=VMEM_LIMIT_BYTES)


ROW_SUBLANES = 4
HALF_WIDTH = ROW_SUBLANES * LANES
HIGH_HALF_MASK = 0xFFFF0000


def _tile_rows(start, n):
    return pl.ds(pl.multiple_of(start * ROW_SUBLANES, ROW_SUBLANES), n * ROW_SUBLANES)


def _word_chunk(c, n):
    return pl.ds(c, n, stride=ROW_SUBLANES)


def _pack_rows(vals):
    lo = lax.bitcast_convert_type(vals[:, :HALF_WIDTH], U32)
    hi = lax.bitcast_convert_type(vals[:, HALF_WIDTH:], U32)
    return lax.shift_right_logical(lo, jnp.uint32(16)) | hi


def _store_rows(ref, n, words):
    for c in range(ROW_SUBLANES):
        ref[_word_chunk(c, n), :] = words[:, c * LANES:(c + 1) * LANES]


def _load_rows(ref, n, keep=None):
    lo, hi = [], []
    for c in range(ROW_SUBLANES):
        w = ref[_word_chunk(c, n), :]
        if keep is not None:
            w = jnp.where(keep, w, jnp.uint32(0))
        lo.append(lax.bitcast_convert_type(lax.shift_left(w, jnp.uint32(16)), F32))
        hi.append(lax.bitcast_convert_type(w & jnp.uint32(HIGH_HALF_MASK), F32))
    return jnp.concatenate(lo + hi, axis=1).astype(BF16)


def _ada_kernel(c_ref, w_ref, b_ref, o_ref):
    c = c_ref[...]
    c_act = c * (1.0 / (1.0 + jnp.exp(-c)))
    o_ref[0] = jnp.dot(c_act, w_ref[0], preferred_element_type=F32,
                       precision=lax.Precision.HIGHEST) + b_ref[0]


def _ada(c, ada_w, ada_b):
    n_layers, d, d6 = ada_w.shape
    b = c.shape[0]
    tn = 1536
    return pl.pallas_call(
        _ada_kernel,
        out_shape=jax.ShapeDtypeStruct((n_layers, b, d6), F32),
        grid=(n_layers, d6 // tn),
        in_specs=[pl.BlockSpec((b, d), lambda l, j: (0, 0)),
                  pl.BlockSpec((1, d, tn), lambda l, j: (l, 0, j)),
                  pl.BlockSpec((1, 1, tn), lambda l, j: (l, 0, j))],
        out_specs=pl.BlockSpec((1, b, tn), lambda l, j: (l, 0, j)),
        compiler_params=_params("arbitrary", "arbitrary"),
        name="ada",
    )(c, ada_w, ada_b.reshape(n_layers, 1, d6))


def _rope_kernel(pos_ref, freq_ref, ch_ref, cl_ref, sh_ref, sl_ref):
    ang = pos_ref[...].astype(F32) * freq_ref[...]
    for fn, hi_ref, lo_ref in ((jnp.cos, ch_ref, cl_ref), (jnp.sin, sh_ref, sl_ref)):
        val = fn(ang)
        hi = val.astype(BF16)
        hi_ref[...] = hi
        lo_ref[...] = (val - hi.astype(F32)).astype(BF16)


def _rope_tables(positions):
    n_tok = positions.size
    inv_freq = ROPE_THETA ** (-jnp.arange(0, ROT_DIM, 2, dtype=F32) / ROT_DIM)
    rows = n_tok * ROT_HALF // LANES
    pos_rep = jnp.repeat(positions.reshape(-1), ROT_HALF).reshape(rows, LANES)
    freq = jnp.tile(inv_freq, LANES // ROT_HALF).reshape(1, LANES)
    tr = min(rows, 512)
    spec = pl.BlockSpec((tr, LANES), lambda i: (i, 0))
    outs = pl.pallas_call(
        _rope_kernel,
        out_shape=[jax.ShapeDtypeStruct((rows, LANES), BF16)] * 4,
        grid=(rows // tr,),
        in_specs=[spec, pl.BlockSpec((1, LANES), lambda i: (0, 0))],
        out_specs=[spec] * 4,
        compiler_params=_params("arbitrary"),
        name="rope",
    )(pos_rep, freq)
    return jnp.concatenate([o.reshape(n_tok, ROT_HALF) for o in outs], axis=1)


def _rope_select_matrix():
    sel = np.zeros((4 * ROT_HALF, 3 * LANES), np.float32)
    for d in range(LANES):
        dd = d % HEAD_DIM
        if dd < ROT_DIM:
            f = dd % ROT_HALF
            sel[f, d] = 1.0
            sel[ROT_HALF + f, d] = 1.0
        if dd < ROT_HALF:
            sel[2 * ROT_HALF + dd, LANES + d] = -1.0
            sel[3 * ROT_HALF + dd, LANES + d] = -1.0
        elif dd < ROT_DIM:
            sel[2 * ROT_HALF + dd - ROT_HALF, 2 * LANES + d] = 1.0
            sel[3 * ROT_HALF + dd - ROT_HALF, 2 * LANES + d] = 1.0
    unrotated = np.array([[1.0 if (d % HEAD_DIM) >= ROT_DIM else 0.0 for d in range(LANES)]], np.float32)
    return jnp.asarray(sel, BF16), jnp.asarray(unrotated, F32)


def _head_mean_matrix(width):
    m = np.zeros((width, width), np.float32)
    for h in range(width // HEAD_DIM):
        m[h * HEAD_DIM:(h + 1) * HEAD_DIM, h * HEAD_DIM:(h + 1) * HEAD_DIM] = 1.0 / HEAD_DIM
    return jnp.asarray(m, BF16)


def _inproj_kernel(x_ref, shift_ref, scale_ref, g1_ref, w_ref, qg_ref, kg_ref, rope_ref, sel_ref,
                   unrot_ref, bdq_ref, bdk_ref, wp_ref, bp_ref, ps_ref,
                   q_ref, k_ref, v_ref, pool_ref, carry_ref, *, ts):
    s_idx = pl.program_id(1)

    @pl.when(s_idx == 0)
    def _():
        carry_ref[...] = jnp.zeros_like(carry_ref)

    x = x_ref[...]
    ms = jnp.mean(x * x, axis=-1, keepdims=True)
    h = x * lax.rsqrt(ms + EPS) * (g1_ref[...] * (1.0 + scale_ref[...])) + shift_ref[...]
    hb = h.astype(BF16)

    tab = jnp.dot(rope_ref[...], sel_ref[...], preferred_element_type=F32)
    cos_t = tab[:, :LANES] + unrot_ref[...]
    sin_a = tab[:, LANES:2 * LANES]
    sin_b = tab[:, 2 * LANES:]

    def norm_rope(t, bd_ref, g_ref, out_scale):
        msq = jnp.dot((t * t).astype(BF16), bd_ref[...], preferred_element_type=F32)
        tn = t * lax.rsqrt(msq + EPS) * g_ref[...]
        chunks = []
        for c in range(t.shape[1] // LANES):
            tc = tn[:, c * LANES:(c + 1) * LANES]
            rot = (tc * cos_t + pltpu.roll(tc, LANES - ROT_HALF, 1) * sin_a
                   + pltpu.roll(tc, ROT_HALF, 1) * sin_b)
            chunks.append(rot * out_scale)
        return chunks[0] if len(chunks) == 1 else jnp.concatenate(chunks, axis=1)

    q = jnp.dot(hb, w_ref[:, :ATTN_WIDTH], preferred_element_type=F32)
    q_ref[...] = norm_rope(q, bdq_ref, qg_ref, HEAD_DIM ** -0.5).astype(BF16)
    k = jnp.dot(hb, w_ref[:, ATTN_WIDTH:ATTN_WIDTH + KV_WIDTH], preferred_element_type=F32)
    k_ref[...] = norm_rope(k, bdk_ref, kg_ref, 1.0).astype(BF16)
    v = jnp.dot(hb, w_ref[:, ATTN_WIDTH + KV_WIDTH:ATTN_WIDTH + 2 * KV_WIDTH], preferred_element_type=F32)
    v_ref[...] = v.astype(BF16)

    u = jnp.dot(hb, w_ref[:, ATTN_WIDTH + 2 * KV_WIDTH:], preferred_element_type=F32)
    ext = jnp.concatenate([carry_ref[...], u], axis=0)
    carry_ref[...] = u[ts - MAX_POOL_WINDOW:, :]
    count = (lax.broadcasted_iota(I32, (ts, POOL_GROUP_DIM), 0) + (s_idx * ts + 1)).astype(F32)
    outs = []
    for g, w in enumerate(POOL_WINDOWS):
        win = ext[:, g * POOL_GROUP_DIM:(g + 1) * POOL_GROUP_DIM]
        span = 1
        while span < w:
            win = win + pltpu.roll(win, span, 0)
            span *= 2
        mean = win[MAX_POOL_WINDOW:, :] * (1.0 / jnp.minimum(count, float(w)))
        pooled = mean - u[:, g * POOL_GROUP_DIM:(g + 1) * POOL_GROUP_DIM]
        outs.append(jnp.dot(pooled.astype(BF16), wp_ref[g], preferred_element_type=F32))
    y = (jnp.concatenate(outs, axis=1) + bp_ref[...]) * ps_ref[...]
    pool_ref[...] = y.astype(BF16)


def _inproj(x2, shift1, scale1, g1, w_in, qg, kg, rope, sel, unrot, bdq, bdk, wp, bp, ps, *, batch, seq):
    n_tok, d = x2.shape
    ts = min(TOKEN_TILE, seq)
    spt = seq // ts
    row = lambda width: pl.BlockSpec((ts, width), lambda b, s: (b * spt + s, 0))
    per_batch = pl.BlockSpec((None, 1, d), lambda b, s: (b, 0, 0))
    full = lambda a: pl.BlockSpec(a.shape, lambda b, s: (0,) * a.ndim)
    return pl.pallas_call(
        functools.partial(_inproj_kernel, ts=ts),
        out_shape=[jax.ShapeDtypeStruct((n_tok, ATTN_WIDTH), BF16),
                   jax.ShapeDtypeStruct((n_tok, KV_WIDTH), BF16),
                   jax.ShapeDtypeStruct((n_tok, KV_WIDTH), BF16),
                   jax.ShapeDtypeStruct((n_tok, POOL_WIDTH), BF16)],
        grid=(batch, spt),
        in_specs=[row(d), per_batch, per_batch, full(g1), full(w_in), full(qg), full(kg),
                  row(rope.shape[1]), full(sel), full(unrot), full(bdq), full(bdk),
                  full(wp), full(bp), full(ps)],
        out_specs=[row(ATTN_WIDTH), row(KV_WIDTH), row(KV_WIDTH), row(POOL_WIDTH)],
        scratch_shapes=[pltpu.VMEM((MAX_POOL_WINDOW, POOL_WIDTH), F32)],
        compiler_params=_params("arbitrary", "arbitrary"),
        name="inproj",
    )(x2, shift1, scale1, g1, w_in, qg, kg, rope, sel, unrot, bdq, bdk, wp, bp, ps)


def _attn_kernel(sink_ref, q_ref, kc_ref, vc_ref, kp_ref, vp_ref, o_ref, *, tq, tiles_per_seq):
    first = (pl.program_id(0) % tiles_per_seq) == 0
    kfull = jnp.concatenate([kp_ref[...], kc_ref[...]], axis=0)
    vfull = jnp.concatenate([vp_ref[...], vc_ref[...]], axis=0)
    rows = Q_PER_KV * WINDOW
    qi = lax.broadcasted_iota(I32, (rows, 2 * WINDOW), 0) % WINDOW
    kj = lax.broadcasted_iota(I32, (rows, 2 * WINDOW), 1)
    allowed = (kj > qi) & (kj <= qi + WINDOW)
    allowed_first = allowed & (jnp.logical_not(first) | (kj >= WINDOW))
    row_head = lax.broadcasted_iota(I32, (rows, 1), 0) // WINDOW
    for j in range(tq // WINDOW):
        kb = kfull[j * WINDOW:(j + 2) * WINDOW, :]
        vb = vfull[j * WINDOW:(j + 2) * WINDOW, :]
        mask = allowed_first if j == 0 else allowed
        for h in range(N_KV_HEADS):
            qs = jnp.concatenate(
                [q_ref[j * WINDOW:(j + 1) * WINDOW, (h * Q_PER_KV + g) * HEAD_DIM:(h * Q_PER_KV + g + 1) * HEAD_DIM]
                 for g in range(Q_PER_KV)], axis=0)
            s = lax.dot_general(qs, kb[:, h * HEAD_DIM:(h + 1) * HEAD_DIM], (((1,), (1,)), ((), ())),
                                preferred_element_type=F32)
            s = jnp.where(mask, s, -jnp.inf)
            sink = jnp.zeros((rows, 1), F32)
            for g in range(Q_PER_KV):
                sink = jnp.where(row_head == g, sink_ref[h * Q_PER_KV + g], sink)
            m = jnp.maximum(jnp.max(s, axis=-1, keepdims=True), sink)
            p = jnp.exp(s - m)
            denom = jnp.sum(p, axis=-1, keepdims=True) + jnp.exp(sink - m)
            o = jnp.dot(p.astype(BF16), vb[:, h * HEAD_DIM:(h + 1) * HEAD_DIM], preferred_element_type=F32)
            o = o * (1.0 / denom)
            o_ref[j * WINDOW:(j + 1) * WINDOW, h * Q_PER_KV * HEAD_DIM:(h + 1) * Q_PER_KV * HEAD_DIM] = (
                jnp.concatenate([o[g * WINDOW:(g + 1) * WINDOW, :] for g in range(Q_PER_KV)], axis=1).astype(BF16))


def _attention(q, k, v, sinks, *, seq):
    n_tok = q.shape[0]
    tq = min(TOKEN_TILE, seq)
    per = tq // WINDOW
    cur = lambda width: pl.BlockSpec((tq, width), lambda i: (i, 0))
    prev = pl.BlockSpec((WINDOW, KV_WIDTH), lambda i: (jnp.maximum(i * per - 1, 0), 0))
    return pl.pallas_call(
        functools.partial(_attn_kernel, tq=tq, tiles_per_seq=seq // tq),
        out_shape=jax.ShapeDtypeStruct((n_tok, ATTN_WIDTH), BF16),
        grid=(n_tok // tq,),
        in_specs=[pl.BlockSpec(memory_space=pltpu.SMEM), cur(ATTN_WIDTH), cur(KV_WIDTH), cur(KV_WIDTH), prev, prev],
        out_specs=cur(ATTN_WIDTH),
        compiler_params=_params("arbitrary"),
        name="attn",
    )(sinks, q, k, v, k, v)


def _outproj_kernel(attn_ref, pool_ref, x_ref, gate1_ref, shift2_ref, scale2_ref, g2_ref, wa_ref, wpo_ref,
                    rw_ref, rb_ref, tri_ref,
                    xo_ref, h2_ref, idx_ref, rank_ref, gates_ref, cnt_ref):
    mixed = (jnp.dot(attn_ref[...], wa_ref[...], preferred_element_type=F32)
             + jnp.dot(pool_ref[...], wpo_ref[...], preferred_element_type=F32))
    x = x_ref[...] + gate1_ref[...] * mixed
    xo_ref[...] = x
    ms = jnp.mean(x * x, axis=-1, keepdims=True)
    h2 = x * lax.rsqrt(ms + EPS) * (g2_ref[...] * (1.0 + scale2_ref[...])) + shift2_ref[...]
    h2b = h2.astype(BF16)
    h2_ref[...] = h2b

    contract_last = (((1,), (1,)), ((), ()))
    logits = lax.dot_general(rw_ref[...], h2b, contract_last, preferred_element_type=F32) + rb_ref[...]
    expert = lax.broadcasted_iota(I32, logits.shape, 0)
    work = logits
    vals, picks, chosen = [], [], []
    for _ in range(TOP_K):
        m = jnp.max(work, axis=0, keepdims=True)
        pick = jnp.min(jnp.where(work == m, expert, N_EXPERTS), axis=0, keepdims=True)
        sel = expert == pick
        vals.append(m)
        picks.append(pick)
        chosen.append(sel)
        work = jnp.where(sel, -jnp.inf, work)
    exps = [jnp.exp(v - vals[0]) for v in vals]
    inv_total = 1.0 / (exps[0] + exps[1] + exps[2] + exps[3])

    multi = jnp.zeros(logits.shape, F32)
    for sel in chosen:
        multi = multi + sel.astype(F32)
    multi_b = multi.astype(BF16)
    before = jnp.dot(multi_b, tri_ref[...], preferred_element_type=F32)
    for kk in range(TOP_K):
        idx_ref[kk:kk + 1, :] = picks[kk]
        gates_ref[kk:kk + 1, :] = exps[kk] * inv_total
        rank_ref[kk:kk + 1, :] = jnp.sum(jnp.where(chosen[kk], before, 0.0), axis=0, keepdims=True).astype(I32)
    ones = jnp.ones((SUBLANES, SORT_TILE), BF16)
    for s in range(multi.shape[1] // SORT_TILE):
        per_expert = lax.dot_general(ones, multi_b[:, s * SORT_TILE:(s + 1) * SORT_TILE], contract_last,
                                     preferred_element_type=F32)
        cnt_ref[s] = per_expert[0:1, :]


def _outproj(attn, pool, x2, gate1, shift2, scale2, g2, wa, wpo, rw, rb, tri, *, seq):
    n_tok, d = x2.shape
    ts = tri.shape[0]
    spt = seq // ts
    row = lambda width: pl.BlockSpec((ts, width), lambda i: (i, 0))
    per_token = pl.BlockSpec((TOP_K, ts), lambda i: (0, i))
    per_batch = pl.BlockSpec((None, 1, d), lambda i: (i // spt, 0, 0))
    full = lambda a: pl.BlockSpec(a.shape, lambda i: (0,) * a.ndim)
    return pl.pallas_call(
        _outproj_kernel,
        out_shape=[jax.ShapeDtypeStruct((n_tok, d), F32),
                   jax.ShapeDtypeStruct((n_tok, d), BF16),
                   jax.ShapeDtypeStruct((TOP_K, n_tok), I32),
                   jax.ShapeDtypeStruct((TOP_K, n_tok), I32),
                   jax.ShapeDtypeStruct((TOP_K, n_tok), F32),
                   jax.ShapeDtypeStruct((n_tok // SORT_TILE, 1, N_EXPERTS), F32)],
        grid=(n_tok // ts,),
        in_specs=[row(ATTN_WIDTH), row(POOL_WIDTH), row(d), per_batch, per_batch, per_batch, full(g2),
                  full(wa), full(wpo), full(rw), full(rb), full(tri)],
        out_specs=[row(d), row(d), per_token, per_token, per_token,
                   pl.BlockSpec((ts // SORT_TILE, 1, N_EXPERTS), lambda i: (i, 0, 0))],
        compiler_params=_params("arbitrary"),
        name="outproj",
    )(attn, pool, x2, gate1, shift2, scale2, g2, wa, wpo, rw, rb, tri)


def _chunk_loops(cnt_ref, tile, chunk, body):
    def per_expert(e, carry):
        n_chunks = (cnt_ref[tile * N_EXPERTS + e] + chunk - 1) // chunk

        def per_chunk(i, c):
            body(e, i)
            return c

        return lax.fori_loop(0, n_chunks, per_chunk, carry)

    lax.fori_loop(0, N_EXPERTS, per_expert, 0)


def _run_copies(extra_ref, tile, make_copy, act):
    for e in range(N_EXPERTS):
        getattr(make_copy(tile * N_EXPERTS + e, True), act)()

    def per_extra(p, carry):
        getattr(make_copy(tile * MAX_EXTRA_CHUNKS + p, False), act)()
        return carry

    lax.fori_loop(0, extra_ref[tile], per_extra, 0)


def _dispatch_kernel(extra_ref, src_ref, dst_ref, xsrc_ref, xdst_ref, fill_ref, idx_t_ref, rank_t_ref, lo_col_ref,
                     h_ref, xs_ref, stage_ref, sem, *, tt):
    t = pl.program_id(0)
    slot = t % 2
    n_rows = TOP_K * tt
    u = FILL_CHUNK
    m = EXPERT_BLOCK

    @pl.when(t == 0)
    def _():
        stage_ref[...] = jnp.zeros_like(stage_ref)

        def pad_copy(e, i):
            return pltpu.make_async_copy(stage_ref.at[1, _tile_rows(0, u)],
                                         xs_ref.at[_tile_rows(fill_ref[e] - (i + 1) * u, u)], sem)

        def tail_copy(i):
            return pltpu.make_async_copy(stage_ref.at[1, _tile_rows(0, m)],
                                         xs_ref.at[_tile_rows(fill_ref[N_EXPERTS - 1] + i * m, m)], sem)

        n_tail = (xs_ref.shape[0] // ROW_SUBLANES - fill_ref[N_EXPERTS - 1]) // m
        for act in ("start", "wait"):
            _chunk_loops(fill_ref, 1, u, lambda e, i: getattr(pad_copy(e, i), act)())
            lax.fori_loop(0, n_tail, lambda i, c: (getattr(tail_copy(i), act)(), c)[1], 0)

    expert = lax.broadcasted_iota(I32, (N_EXPERTS, tt), 0)
    rows = lax.broadcasted_iota(I32, (n_rows, tt), 0)
    hit = None
    for k in range(TOP_K):
        onehot = expert == idx_t_ref[k:k + 1, :]
        base = jnp.sum(jnp.where(onehot, lo_col_ref[...], 0.0), axis=0, keepdims=True)
        slot_row = base.astype(I32) + rank_t_ref[k:k + 1, :]
        match = rows == slot_row
        hit = match if hit is None else (hit | match)
    perm = jnp.where(hit, 1.0, 0.0).astype(BF16)
    grouped = jnp.dot(perm, h_ref[...], preferred_element_type=F32)
    _store_rows(stage_ref.at[slot], n_rows, _pack_rows(grouped))

    def copies_of(tile):
        def make_copy(j, first):
            src = src_ref[j] if first else xsrc_ref[j]
            dst = dst_ref[j] if first else xdst_ref[j]
            return pltpu.make_async_copy(stage_ref.at[tile % 2, _tile_rows(src, RUN_CHUNK)],
                                         xs_ref.at[_tile_rows(dst, RUN_CHUNK)], sem)
        return make_copy

    @pl.when(t > 0)
    def _():
        _run_copies(extra_ref, t - 1, copies_of(t - 1), "wait")

    _run_copies(extra_ref, t, copies_of(t), "start")

    @pl.when(t == pl.num_programs(0) - 1)
    def _():
        _run_copies(extra_ref, t, copies_of(t), "wait")


def _dispatch(extra, src, dst, xsrc, xdst, fill, idx_t, rank_t, lo_col, h2, n_slots):
    n_tok, d = h2.shape
    tt = SORT_TILE
    grid_spec = pltpu.PrefetchScalarGridSpec(
        num_scalar_prefetch=6,
        grid=(n_tok // tt,),
        in_specs=[pl.BlockSpec((TOP_K, tt), lambda i, *_: (0, i)),
                  pl.BlockSpec((TOP_K, tt), lambda i, *_: (0, i)),
                  pl.BlockSpec((None, N_EXPERTS, 1), lambda i, *_: (i, 0, 0)),
                  pl.BlockSpec((tt, d), lambda i, *_: (i, 0))],
        out_specs=pl.BlockSpec(memory_space=pl.ANY),
        scratch_shapes=[pltpu.VMEM((2, (TOP_K * tt + RUN_CHUNK) * ROW_SUBLANES, LANES), U32),
                        pltpu.SemaphoreType.DMA],
    )
    return pl.pallas_call(
        functools.partial(_dispatch_kernel, tt=tt),
        out_shape=jax.ShapeDtypeStruct((n_slots * ROW_SUBLANES, LANES), U32),
        grid_spec=grid_spec,
        compiler_params=pltpu.CompilerParams(dimension_semantics=("arbitrary",),
                                             vmem_limit_bytes=VMEM_LIMIT_BYTES, has_side_effects=True),
        name="dispatch",
    )(extra, src, dst, xsrc, xdst, fill, idx_t, rank_t, lo_col, h2)


def _experts_kernel(be_ref, nvalid_ref, nused_ref, xs_ref, wgu32_ref, bgu_ref, wd32_ref, bd_ref, yb_ref,
                    wgu_ref, wd_ref, *, d_ff):
    i = pl.program_id(0)

    @pl.when((i == 0) | (be_ref[i] != be_ref[jnp.maximum(i - 1, 0)]))
    def _():
        wgu_ref[...] = wgu32_ref[...].astype(BF16)
        wd_ref[...] = wd32_ref[...].astype(BF16)

    @pl.when(i < nused_ref[0])
    def _():
        m = xs_ref.shape[0] // ROW_SUBLANES
        row = lax.broadcasted_iota(I32, (m, 1), 0)
        xb = _load_rows(xs_ref, m, keep=row < nvalid_ref[i])
        acc = None
        for j in range(d_ff // FF_CHUNK):
            lo, hi = j * FF_CHUNK, (j + 1) * FF_CHUNK
            g = jnp.dot(xb, wgu_ref[:, lo:hi], preferred_element_type=F32) + bgu_ref[:, lo:hi]
            u = jnp.dot(xb, wgu_ref[:, d_ff + lo:d_ff + hi], preferred_element_type=F32) + bgu_ref[:, d_ff + lo:d_ff + hi]
            g = jnp.minimum(g, SWIGLU_LIMIT)
            u = jnp.clip(u, -SWIGLU_LIMIT, SWIGLU_LIMIT)
            act = g * (1.0 / (1.0 + jnp.exp(-SWIGLU_ALPHA * g))) * (u + 1.0)
            part = jnp.dot(act.astype(BF16), wd_ref[lo:hi, :], preferred_element_type=F32)
            acc = part if acc is None else acc + part
        y = (acc + bd_ref[...]).astype(BF16).astype(F32)
        _store_rows(yb_ref, m, _pack_rows(y))

    @pl.when(i >= nused_ref[0])
    def _():
        yb_ref[...] = jnp.zeros_like(yb_ref)


def _experts(block_expert, n_valid, n_used, xs, wgu, bgu, wd, bd, *, layer):
    n_slots = xs.shape[0] // ROW_SUBLANES
    d_ff, d = wd.shape[2], wd.shape[3]
    m = EXPERT_BLOCK
    blk = lambda i, be, nv, nu: (jnp.minimum(i, nu[0] - 1), 0)
    per_expert = lambda i, be, nv, nu: (layer, be[i], 0, 0)
    grid_spec = pltpu.PrefetchScalarGridSpec(
        num_scalar_prefetch=3,
        grid=(n_slots // m,),
        in_specs=[pl.BlockSpec((m * ROW_SUBLANES, LANES), blk),
                  pl.BlockSpec((None, None, d, 2 * d_ff), per_expert),
                  pl.BlockSpec((None, None, 1, 2 * d_ff), per_expert),
                  pl.BlockSpec((None, None, d_ff, d), per_expert),
                  pl.BlockSpec((None, None, 1, d), per_expert)],
        out_specs=pl.BlockSpec((m * ROW_SUBLANES, LANES), lambda i, be, nv, nu: (i, 0)),
        scratch_shapes=[pltpu.VMEM((d, 2 * d_ff), BF16), pltpu.VMEM((d_ff, d), BF16)],
    )
    return pl.pallas_call(
        functools.partial(_experts_kernel, d_ff=d_ff),
        out_shape=jax.ShapeDtypeStruct(xs.shape, U32),
        grid_spec=grid_spec,
        compiler_params=_params("arbitrary"),
        name="experts",
    )(block_expert, n_valid, n_used, xs, wgu, bgu, wd, bd)


def _combine_kernel(extra_ref, src_ref, dst_ref, xsrc_ref, xdst_ref, idx_ref, rank_ref, gates_ref, lo_row_ref,
                    x_ref, gate2_ref, yb_ref, xo_ref, stage_ref, sems, *, tt, n_stage):
    t = pl.program_id(0)
    slot = t % 2

    def copies_of(tile):
        def make_copy(j, first):
            src = src_ref[j] if first else xsrc_ref[j]
            dst = dst_ref[j] if first else xdst_ref[j]
            return pltpu.make_async_copy(yb_ref.at[_tile_rows(src, RUN_CHUNK)],
                                         stage_ref.at[tile % 2, _tile_rows(dst, RUN_CHUNK)], sems.at[tile % 2])
        return make_copy

    @pl.when(t == 0)
    def _():
        stage_ref[...] = jnp.zeros_like(stage_ref)
        _run_copies(extra_ref, 0, copies_of(0), "start")

    @pl.when(t + 1 < pl.num_programs(0))
    def _():
        _run_copies(extra_ref, t + 1, copies_of(t + 1), "start")

    _run_copies(extra_ref, t, copies_of(t), "wait")

    lane = lax.broadcasted_iota(I32, (tt, N_EXPERTS), 1)
    cols = lax.broadcasted_iota(I32, (tt, n_stage), 1)
    weights = jnp.zeros((tt, n_stage), F32)
    for k in range(TOP_K):
        onehot = lane == idx_ref[:, k:k + 1]
        base = jnp.sum(jnp.where(onehot, lo_row_ref[...], 0.0), axis=1, keepdims=True)
        col = base.astype(I32) + rank_ref[:, k:k + 1]
        weights = jnp.where(cols == col, gates_ref[:, k:k + 1], weights)
    staged = _load_rows(stage_ref.at[slot], n_stage)
    y = jnp.dot(weights.astype(BF16), staged, preferred_element_type=F32)
    xo_ref[...] = x_ref[...] + gate2_ref[...] * y


def _combine(extra, src, dst, xsrc, xdst, idx, rank, gates, lo_row, x2, gate2, yb, *, seq):
    n_tok, d = x2.shape
    tt = SORT_TILE
    spt = seq // tt
    n_stage = TOP_K * tt + N_EXPERTS * RUN_CHUNK
    grid_spec = pltpu.PrefetchScalarGridSpec(
        num_scalar_prefetch=5,
        grid=(n_tok // tt,),
        in_specs=[pl.BlockSpec((tt, TOP_K), lambda i, *_: (i, 0)),
                  pl.BlockSpec((tt, TOP_K), lambda i, *_: (i, 0)),
                  pl.BlockSpec((tt, TOP_K), lambda i, *_: (i, 0)),
                  pl.BlockSpec((None, 1, N_EXPERTS), lambda i, *_: (i, 0, 0)),
                  pl.BlockSpec((tt, d), lambda i, *_: (i, 0)),
                  pl.BlockSpec((None, 1, d), lambda i, *_: (i // spt, 0, 0)),
                  pl.BlockSpec(memory_space=pl.ANY)],
        out_specs=pl.BlockSpec((tt, d), lambda i, *_: (i, 0)),
        scratch_shapes=[pltpu.VMEM((2, n_stage * ROW_SUBLANES, LANES), U32),
                        pltpu.SemaphoreType.DMA((2,))],
    )
    return pl.pallas_call(
        functools.partial(_combine_kernel, tt=tt, n_stage=n_stage),
        out_shape=jax.ShapeDtypeStruct((n_tok, d), F32),
        grid_spec=grid_spec,
        compiler_params=_params("arbitrary"),
        name="combine",
    )(extra, src, dst, xsrc, xdst, idx, rank, gates, lo_row, x2, gate2, yb)


def _exclusive_cumsum(a, axis):
    return jnp.cumsum(a, axis=axis) - a


def _routing_tables(tile_cnt):
    m = EXPERT_BLOCK
    total = jnp.sum(tile_cnt, axis=0)
    padded = (total + RUN_CHUNK + m - 1) // m * m
    pends = jnp.cumsum(padded)
    pstart = pends - padded
    gstart = pstart[None, :] + _exclusive_cumsum(tile_cnt, 0)
    lo = _exclusive_cumsum(tile_cnt, 1)
    n_chunks = jnp.maximum((tile_cnt + RUN_CHUNK - 1) // RUN_CHUNK, 1)
    lo_c = _exclusive_cumsum(n_chunks * RUN_CHUNK, 1)
    n_extra = n_chunks - 1
    ends = jnp.cumsum(n_extra, axis=1)
    extra = ends[:, -1]
    p = jnp.arange(MAX_EXTRA_CHUNKS, dtype=I32)
    x_expert = jnp.minimum(jnp.sum(p[None, :, None] >= ends[:, None, :], axis=2), N_EXPERTS - 1)
    x_offset = (p[None, :] - jnp.take_along_axis(ends - n_extra, x_expert, axis=1) + 1) * RUN_CHUNK
    return total, pstart, pends, gstart, lo, lo_c, extra, x_expert, x_offset


def kernel(x, c, positions, ada_w, ada_b, norm1_g, w_in, q_norm_g, k_norm_g, attn_sinks, pool_w, pool_b,
           pool_scale, w_out, norm2_g, router_w, router_b, expert_w_gu, expert_b_gu, expert_w_down,
           expert_b_down):
    batch, seq, d = x.shape
    n_layers = ada_w.shape[0]
    n_tok = batch * seq
    d_ff = expert_w_down.shape[2]
    assert d == SUBLANES * LANES
    assert seq % WINDOW == 0 and seq % min(TOKEN_TILE, seq) == 0 and seq % ROUTE_TILE == 0

    mod = _ada(c, ada_w, ada_b).reshape(n_layers, batch, 6, 1, d)
    rope = _rope_tables(positions)
    sel, unrot = _rope_select_matrix()
    bdq, bdk = _head_mean_matrix(ATTN_WIDTH), _head_mean_matrix(KV_WIDTH)
    tri = jnp.asarray(np.kron(np.eye(ROUTE_TILE // SORT_TILE, dtype=np.float32),
                              np.triu(np.ones((SORT_TILE, SORT_TILE), np.float32), 1)), BF16)

    m = EXPERT_BLOCK
    n_blocks = -(-(n_tok * TOP_K + N_EXPERTS * (RUN_CHUNK + m - 1)) // m)
    n_slots = n_blocks * m
    n_tiles = n_tok // SORT_TILE

    x2 = x.reshape(n_tok, d)
    for l in range(n_layers):
        shift1, scale1, gate1, shift2, scale2, gate2 = (mod[l, :, j] for j in range(6))
        q, k, v, pool = _inproj(
            x2, shift1, scale1, norm1_g[l].reshape(1, d), w_in[l].astype(BF16),
            jnp.tile(q_norm_g[l], N_Q_HEADS).reshape(1, ATTN_WIDTH),
            jnp.tile(k_norm_g[l], N_KV_HEADS).reshape(1, KV_WIDTH),
            rope, sel, unrot, bdq, bdk, pool_w[l].astype(BF16), pool_b[l].reshape(1, POOL_WIDTH),
            pool_scale[l].reshape(1, POOL_WIDTH), batch=batch, seq=seq)
        attn = _attention(q, k, v, attn_sinks[l], seq=seq)
        w_out_b = w_out[l].astype(BF16)
        x2, h2, idx_t, rank_t, gates_t, tile_cnt = _outproj(
            attn, pool, x2, gate1, shift2, scale2, norm2_g[l].reshape(1, d),
            w_out_b[:ATTN_WIDTH], w_out_b[ATTN_WIDTH:], router_w[l].T.astype(BF16),
            router_b[l].reshape(N_EXPERTS, 1),
            tri, seq=seq)

        tile_cnt = tile_cnt.reshape(n_tiles, N_EXPERTS).astype(I32)
        total, pstart, pends, gstart, lo, lo_c, extra, x_expert, x_offset = _routing_tables(tile_cnt)
        flat = lambda a: a.reshape(-1).astype(I32)
        extra_rows = lambda table: flat(jnp.take_along_axis(table, x_expert, axis=1) + x_offset)
        n_used = (pends[-1] // m).astype(I32).reshape(1)
        block_row = jnp.minimum(jnp.arange(n_blocks, dtype=I32), n_used[0] - 1) * m
        block_expert = jnp.minimum(jnp.sum(block_row[:, None] >= pends[None, :], axis=1), N_EXPERTS - 1).astype(I32)
        n_valid = jnp.clip((pstart + total)[block_expert] - block_row, 0, m).astype(I32)

        fill = jnp.concatenate([pends, pends - pstart - total]).astype(I32)
        xs = _dispatch(flat(extra), flat(lo), flat(gstart), extra_rows(lo), extra_rows(gstart), fill,
                       idx_t, rank_t, lo.astype(F32).reshape(n_tiles, N_EXPERTS, 1), h2, n_slots)
        yb = _experts(block_expert, n_valid, n_used, xs, expert_w_gu,
                      expert_b_gu.reshape(n_layers, N_EXPERTS, 1, 2 * d_ff), expert_w_down,
                      expert_b_down.reshape(n_layers, N_EXPERTS, 1, d), layer=l)
        x2 = _combine(flat(extra), flat(gstart), flat(lo_c), extra_rows(gstart), extra_rows(lo_c),
                      idx_t.T, rank_t.T, gates_t.T, lo_c.astype(F32).reshape(n_tiles, 1, N_EXPERTS), x2, gate2, yb,
                      seq=seq)
    return x2.reshape(batch, seq, d)
```

```python
import functools

import numpy as np
import jax
import jax.numpy as jnp
from jax import lax
from jax.experimental import pallas as pl
from jax.experimental.pallas import tpu as pltpu

F32 = jnp.float32
BF16 = jnp.bfloat16
I32 = jnp.int32
U32 = jnp.uint32

HEAD_DIM = 64
N_Q_HEADS = 8
N_KV_HEADS = 2
Q_PER_KV = N_Q_HEADS // N_KV_HEADS
ATTN_WIDTH = N_Q_HEADS * HEAD_DIM
KV_WIDTH = N_KV_HEADS * HEAD_DIM
WINDOW = 128
ROT_DIM = HEAD_DIM // 4
ROT_HALF = ROT_DIM // 2
ROPE_THETA = 500000.0
POOL_WINDOWS = (2, 4, 8, 16)
N_POOL_GROUPS = len(POOL_WINDOWS)
POOL_GROUP_DIM = 128
POOL_WIDTH = N_POOL_GROUPS * POOL_GROUP_DIM
MAX_POOL_WINDOW = max(POOL_WINDOWS)
N_EXPERTS = 32
TOP_K = 4
SWIGLU_LIMIT = 7.0
SWIGLU_ALPHA = 1.702
EPS = 1e-6

LANES = 128
SUBLANES = 8
VMEM_LIMIT_BYTES = 56 * 1024 * 1024

TOKEN_TILE = 512
ROUTE_TILE = 256
SORT_TILE = 128
DISPATCH_TILE = 2 * SORT_TILE
EXPERT_BLOCK = 512
FF_CHUNK = 512
DISPATCH_CHUNK = 32
COMBINE_CHUNK = 16
FILL_CHUNK = 16
MAX_EXTRA_CHUNKS = TOP_K * DISPATCH_TILE // DISPATCH_CHUNK
assert MAX_EXTRA_CHUNKS == TOP_K * SORT_TILE // COMBINE_CHUNK


def _params(*sem):
    return pltpu.CompilerParams(dimension_semantics=sem, vmem_limit_bytes=VMEM_LIMIT_BYTES)


ROW_SUBLANES = 4
HALF_WIDTH = ROW_SUBLANES * LANES
HIGH_HALF_MASK = 0xFFFF0000


def _tile_rows(start, n):
    return pl.ds(pl.multiple_of(start * ROW_SUBLANES, ROW_SUBLANES), n * ROW_SUBLANES)


def _word_chunk(c, n):
    return pl.ds(c, n, stride=ROW_SUBLANES)


def _pack_rows(vals):
    lo = lax.bitcast_convert_type(vals[:, :HALF_WIDTH], U32)
    hi = lax.bitcast_convert_type(vals[:, HALF_WIDTH:], U32)
    return lax.shift_right_logical(lo, jnp.uint32(16)) | hi


def _store_rows(ref, n, words):
    for c in range(ROW_SUBLANES):
        ref[_word_chunk(c, n), :] = words[:, c * LANES:(c + 1) * LANES]


def _load_rows(ref, n, keep=None):
    lo, hi = [], []
    for c in range(ROW_SUBLANES):
        w = ref[_word_chunk(c, n), :]
        if keep is not None:
            w = jnp.where(keep, w, jnp.uint32(0))
        lo.append(lax.bitcast_convert_type(lax.shift_left(w, jnp.uint32(16)), F32))
        hi.append(lax.bitcast_convert_type(w & jnp.uint32(HIGH_HALF_MASK), F32))
    return jnp.concatenate(lo + hi, axis=1).astype(BF16)


def _ada_kernel(c_ref, w_ref, b_ref, o_ref):
    c = c_ref[...]
    c_act = c * (1.0 / (1.0 + jnp.exp(-c)))
    o_ref[0] = jnp.dot(c_act, w_ref[0], preferred_element_type=F32,
                       precision=lax.Precision.HIGHEST) + b_ref[0]


def _ada(c, ada_w, ada_b):
    n_layers, d, d6 = ada_w.shape
    b = c.shape[0]
    tn = 1536
    return pl.pallas_call(
        _ada_kernel,
        out_shape=jax.ShapeDtypeStruct((n_layers, b, d6), F32),
        grid=(n_layers, d6 // tn),
        in_specs=[pl.BlockSpec((b, d), lambda l, j: (0, 0)),
                  pl.BlockSpec((1, d, tn), lambda l, j: (l, 0, j)),
                  pl.BlockSpec((1, 1, tn), lambda l, j: (l, 0, j))],
        out_specs=pl.BlockSpec((1, b, tn), lambda l, j: (l, 0, j)),
        compiler_params=_params("arbitrary", "arbitrary"),
        name="ada",
    )(c, ada_w, ada_b.reshape(n_layers, 1, d6))


def _rope_kernel(pos_ref, freq_ref, ch_ref, cl_ref, sh_ref, sl_ref):
    ang = pos_ref[...].astype(F32) * freq_ref[...]
    for fn, hi_ref, lo_ref in ((jnp.cos, ch_ref, cl_ref), (jnp.sin, sh_ref, sl_ref)):
        val = fn(ang)
        hi = val.astype(BF16)
        hi_ref[...] = hi
        lo_ref[...] = (val - hi.astype(F32)).astype(BF16)


def _rope_tables(positions):
    n_tok = positions.size
    inv_freq = ROPE_THETA ** (-jnp.arange(0, ROT_DIM, 2, dtype=F32) / ROT_DIM)
    rows = n_tok * ROT_HALF // LANES
    pos_rep = jnp.repeat(positions.reshape(-1), ROT_HALF).reshape(rows, LANES)
    freq = jnp.tile(inv_freq, LANES // ROT_HALF).reshape(1, LANES)
    tr = min(rows, 512)
    spec = pl.BlockSpec((tr, LANES), lambda i: (i, 0))
    outs = pl.pallas_call(
        _rope_kernel,
        out_shape=[jax.ShapeDtypeStruct((rows, LANES), BF16)] * 4,
        grid=(rows // tr,),
        in_specs=[spec, pl.BlockSpec((1, LANES), lambda i: (0, 0))],
        out_specs=[spec] * 4,
        compiler_params=_params("arbitrary"),
        name="rope",
    )(pos_rep, freq)
    return jnp.concatenate([o.reshape(n_tok, ROT_HALF) for o in outs], axis=1)


def _rope_select_matrix():
    sel = np.zeros((4 * ROT_HALF, 3 * LANES), np.float32)
    for d in range(LANES):
        dd = d % HEAD_DIM
        if dd < ROT_DIM:
            f = dd % ROT_HALF
            sel[f, d] = 1.0
            sel[ROT_HALF + f, d] = 1.0
        if dd < ROT_HALF:
            sel[2 * ROT_HALF + dd, LANES + d] = -1.0
            sel[3 * ROT_HALF + dd, LANES + d] = -1.0
        elif dd < ROT_DIM:
            sel[2 * ROT_HALF + dd - ROT_HALF, 2 * LANES + d] = 1.0
            sel[3 * ROT_HALF + dd - ROT_HALF, 2 * LANES + d] = 1.0
    unrotated = np.array([[1.0 if (d % HEAD_DIM) >= ROT_DIM else 0.0 for d in range(LANES)]], np.float32)
    return jnp.asarray(sel, BF16), jnp.asarray(unrotated, F32)


def _head_mean_matrix(width):
    m = np.zeros((width, width), np.float32)
    for h in range(width // HEAD_DIM):
        m[h * HEAD_DIM:(h + 1) * HEAD_DIM, h * HEAD_DIM:(h + 1) * HEAD_DIM] = 1.0 / HEAD_DIM
    return jnp.asarray(m, BF16)


def _inproj_kernel(x_ref, shift_ref, scale_ref, g1_ref, w_ref, qg_ref, kg_ref, rope_ref, sel_ref,
                   unrot_ref, bdq_ref, bdk_ref, wp_ref, bp_ref, ps_ref,
                   q_ref, k_ref, v_ref, pool_ref, carry_ref, *, ts):
    s_idx = pl.program_id(1)

    @pl.when(s_idx == 0)
    def _():
        carry_ref[...] = jnp.zeros_like(carry_ref)

    x = x_ref[...]
    ms = jnp.mean(x * x, axis=-1, keepdims=True)
    h = x * lax.rsqrt(ms + EPS) * (g1_ref[...] * (1.0 + scale_ref[...])) + shift_ref[...]
    hb = h.astype(BF16)

    tab = jnp.dot(rope_ref[...], sel_ref[...], preferred_element_type=F32)
    cos_t = tab[:, :LANES] + unrot_ref[...]
    sin_a = tab[:, LANES:2 * LANES]
    sin_b = tab[:, 2 * LANES:]

    def norm_rope(t, bd_ref, g_ref, out_scale):
        msq = jnp.dot((t * t).astype(BF16), bd_ref[...], preferred_element_type=F32)
        tn = t * lax.rsqrt(msq + EPS) * g_ref[...]
        chunks = []
        for c in range(t.shape[1] // LANES):
            tc = tn[:, c * LANES:(c + 1) * LANES]
            rot = (tc * cos_t + pltpu.roll(tc, LANES - ROT_HALF, 1) * sin_a
                   + pltpu.roll(tc, ROT_HALF, 1) * sin_b)
            chunks.append(rot * out_scale)
        return chunks[0] if len(chunks) == 1 else jnp.concatenate(chunks, axis=1)

    q = jnp.dot(hb, w_ref[:, :ATTN_WIDTH], preferred_element_type=F32)
    q_ref[...] = norm_rope(q, bdq_ref, qg_ref, HEAD_DIM ** -0.5).astype(BF16)
    k = jnp.dot(hb, w_ref[:, ATTN_WIDTH:ATTN_WIDTH + KV_WIDTH], preferred_element_type=F32)
    k_ref[...] = norm_rope(k, bdk_ref, kg_ref, 1.0).astype(BF16)
    v = jnp.dot(hb, w_ref[:, ATTN_WIDTH + KV_WIDTH:ATTN_WIDTH + 2 * KV_WIDTH], preferred_element_type=F32)
    v_ref[...] = v.astype(BF16)

    u = jnp.dot(hb, w_ref[:, ATTN_WIDTH + 2 * KV_WIDTH:], preferred_element_type=F32)
    ext = jnp.concatenate([carry_ref[...], u], axis=0)
    carry_ref[...] = u[ts - MAX_POOL_WINDOW:, :]
    count = (lax.broadcasted_iota(I32, (ts, POOL_GROUP_DIM), 0) + (s_idx * ts + 1)).astype(F32)
    outs = []
    for g, w in enumerate(POOL_WINDOWS):
        win = ext[:, g * POOL_GROUP_DIM:(g + 1) * POOL_GROUP_DIM]
        span = 1
        while span < w:
            win = win + pltpu.roll(win, span, 0)
            span *= 2
        mean = win[MAX_POOL_WINDOW:, :] * (1.0 / jnp.minimum(count, float(w)))
        pooled = mean - u[:, g * POOL_GROUP_DIM:(g + 1) * POOL_GROUP_DIM]
        outs.append(jnp.dot(pooled.astype(BF16), wp_ref[g], preferred_element_type=F32))
    y = (jnp.concatenate(outs, axis=1) + bp_ref[...]) * ps_ref[...]
    pool_ref[...] = y.astype(BF16)


def _inproj(x2, shift1, scale1, g1, w_in, qg, kg, rope, sel, unrot, bdq, bdk, wp, bp, ps, *, batch, seq):
    n_tok, d = x2.shape
    ts = min(TOKEN_TILE, seq)
    spt = seq // ts
    row = lambda width: pl.BlockSpec((ts, width), lambda b, s: (b * spt + s, 0))
    per_batch = pl.BlockSpec((None, 1, d), lambda b, s: (b, 0, 0))
    full = lambda a: pl.BlockSpec(a.shape, lambda b, s: (0,) * a.ndim)
    return pl.pallas_call(
        functools.partial(_inproj_kernel, ts=ts),
        out_shape=[jax.ShapeDtypeStruct((n_tok, ATTN_WIDTH), BF16),
                   jax.ShapeDtypeStruct((n_tok, KV_WIDTH), BF16),
                   jax.ShapeDtypeStruct((n_tok, KV_WIDTH), BF16),
                   jax.ShapeDtypeStruct((n_tok, POOL_WIDTH), BF16)],
        grid=(batch, spt),
        in_specs=[row(d), per_batch, per_batch, full(g1), full(w_in), full(qg), full(kg),
                  row(rope.shape[1]), full(sel), full(unrot), full(bdq), full(bdk),
                  full(wp), full(bp), full(ps)],
        out_specs=[row(ATTN_WIDTH), row(KV_WIDTH), row(KV_WIDTH), row(POOL_WIDTH)],
        scratch_shapes=[pltpu.VMEM((MAX_POOL_WINDOW, POOL_WIDTH), F32)],
        compiler_params=_params("arbitrary", "arbitrary"),
        name="inproj",
    )(x2, shift1, scale1, g1, w_in, qg, kg, rope, sel, unrot, bdq, bdk, wp, bp, ps)


def _attn_kernel(sink_ref, q_ref, kc_ref, vc_ref, kp_ref, vp_ref, o_ref, *, tq, tiles_per_seq):
    first = (pl.program_id(0) % tiles_per_seq) == 0
    kfull = jnp.concatenate([kp_ref[...], kc_ref[...]], axis=0)
    vfull = jnp.concatenate([vp_ref[...], vc_ref[...]], axis=0)
    rows = Q_PER_KV * WINDOW
    qi = lax.broadcasted_iota(I32, (rows, WINDOW), 0) % WINDOW
    from_prev = lax.broadcasted_iota(I32, (rows, WINDOW), 1) > qi
    row_head = lax.broadcasted_iota(I32, (rows, 1), 0) // WINDOW
    sinks = []
    for h in range(N_KV_HEADS):
        sink = jnp.zeros((rows, 1), F32)
        for g in range(Q_PER_KV):
            sink = jnp.where(row_head == g, sink_ref[h * Q_PER_KV + g], sink)
        sinks.append(sink)

    problems = [(j, h) for j in range(tq // WINDOW) for h in range(N_KV_HEADS)]
    scores, maxes = [], []
    for j, h in problems:
        qs = jnp.concatenate(
            [q_ref[j * WINDOW:(j + 1) * WINDOW, (h * Q_PER_KV + g) * HEAD_DIM:(h * Q_PER_KV + g + 1) * HEAD_DIM]
             for g in range(Q_PER_KV)], axis=0)
        kb = kfull[j * WINDOW:(j + 2) * WINDOW, h * HEAD_DIM:(h + 1) * HEAD_DIM]
        s = lax.dot_general(qs, kb, (((1,), (1,)), ((), ())), preferred_element_type=F32)
        s_prev = s[:, :WINDOW]
        if j == 0:
            s_prev = jnp.where(first, -jnp.inf, s_prev)
        s = jnp.where(from_prev, s_prev, s[:, WINDOW:])
        scores.append(s)
        maxes.append(jnp.maximum(jnp.max(s, axis=-1, keepdims=True), sinks[h]))
    probs, denoms = [], []
    for (j, h), s, m in zip(problems, scores, maxes):
        p = jnp.exp(s - m)
        denoms.append(jnp.sum(p, axis=-1, keepdims=True) + jnp.exp(sinks[h] - m))
        p = p.astype(BF16)
        zero = jnp.zeros_like(p)
        probs.append(jnp.concatenate([jnp.where(from_prev, p, zero), jnp.where(from_prev, zero, p)], axis=1))
    for (j, h), p, denom in zip(problems, probs, denoms):
        vb = vfull[j * WINDOW:(j + 2) * WINDOW, h * HEAD_DIM:(h + 1) * HEAD_DIM]
        o = jnp.dot(p, vb, preferred_element_type=F32) * (1.0 / denom)
        o_ref[j * WINDOW:(j + 1) * WINDOW, h * Q_PER_KV * HEAD_DIM:(h + 1) * Q_PER_KV * HEAD_DIM] = (
            jnp.concatenate([o[g * WINDOW:(g + 1) * WINDOW, :] for g in range(Q_PER_KV)], axis=1).astype(BF16))


def _attention(q, k, v, sinks, *, seq):
    n_tok = q.shape[0]
    tq = min(TOKEN_TILE, seq)
    per = tq // WINDOW
    cur = lambda width: pl.BlockSpec((tq, width), lambda i: (i, 0))
    prev = pl.BlockSpec((WINDOW, KV_WIDTH), lambda i: (jnp.maximum(i * per - 1, 0), 0))
    return pl.pallas_call(
        functools.partial(_attn_kernel, tq=tq, tiles_per_seq=seq // tq),
        out_shape=jax.ShapeDtypeStruct((n_tok, ATTN_WIDTH), BF16),
        grid=(n_tok // tq,),
        in_specs=[pl.BlockSpec(memory_space=pltpu.SMEM), cur(ATTN_WIDTH), cur(KV_WIDTH), cur(KV_WIDTH), prev, prev],
        out_specs=cur(ATTN_WIDTH),
        compiler_params=_params("arbitrary"),
        name="attn",
    )(sinks, q, k, v, k, v)


def _outproj_kernel(attn_ref, pool_ref, x_ref, gate1_ref, shift2_ref, scale2_ref, g2_ref, wa_ref, wpo_ref,
                    rw_ref, rb_ref, tri_ref,
                    xo_ref, h2_ref, idx_ref, rank_ref, gates_ref, cnt_ref):
    mixed = (jnp.dot(attn_ref[...], wa_ref[...], preferred_element_type=F32)
             + jnp.dot(pool_ref[...], wpo_ref[...], preferred_element_type=F32))
    x = x_ref[...] + gate1_ref[...] * mixed
    xo_ref[...] = x
    ms = jnp.mean(x * x, axis=-1, keepdims=True)
    h2 = x * lax.rsqrt(ms + EPS) * (g2_ref[...] * (1.0 + scale2_ref[...])) + shift2_ref[...]
    h2b = h2.astype(BF16)
    h2_ref[...] = h2b

    contract_last = (((1,), (1,)), ((), ()))
    logits = lax.dot_general(rw_ref[...], h2b, contract_last, preferred_element_type=F32) + rb_ref[...]
    expert = lax.broadcasted_iota(I32, logits.shape, 0)
    work = logits
    vals, picks, chosen = [], [], []
    for _ in range(TOP_K):
        m = jnp.max(work, axis=0, keepdims=True)
        pick = jnp.min(jnp.where(work == m, expert, N_EXPERTS), axis=0, keepdims=True)
        sel = expert == pick
        vals.append(m)
        picks.append(pick)
        chosen.append(sel)
        work = jnp.where(sel, -jnp.inf, work)
    exps = [jnp.exp(v - vals[0]) for v in vals]
    inv_total = 1.0 / (exps[0] + exps[1] + exps[2] + exps[3])

    multi = jnp.zeros(logits.shape, F32)
    for sel in chosen:
        multi = multi + sel.astype(F32)
    multi_b = multi.astype(BF16)
    before = jnp.dot(multi_b, tri_ref[...], preferred_element_type=F32)
    for kk in range(TOP_K):
        idx_ref[kk:kk + 1, :] = picks[kk]
        gates_ref[kk:kk + 1, :] = exps[kk] * inv_total
        rank_ref[kk:kk + 1, :] = jnp.sum(jnp.where(chosen[kk], before, 0.0), axis=0, keepdims=True).astype(I32)
    ones = jnp.ones((SUBLANES, SORT_TILE), BF16)
    for s in range(multi.shape[1] // SORT_TILE):
        per_expert = lax.dot_general(ones, multi_b[:, s * SORT_TILE:(s + 1) * SORT_TILE], contract_last,
                                     preferred_element_type=F32)
        cnt_ref[s] = per_expert[0:1, :]


def _outproj(attn, pool, x2, gate1, shift2, scale2, g2, wa, wpo, rw, rb, tri, *, seq):
    n_tok, d = x2.shape
    ts = tri.shape[0]
    spt = seq // ts
    row = lambda width: pl.BlockSpec((ts, width), lambda i: (i, 0))
    per_token = pl.BlockSpec((TOP_K, ts), lambda i: (0, i))
    per_batch = pl.BlockSpec((None, 1, d), lambda i: (i // spt, 0, 0))
    full = lambda a: pl.BlockSpec(a.shape, lambda i: (0,) * a.ndim)
    return pl.pallas_call(
        _outproj_kernel,
        out_shape=[jax.ShapeDtypeStruct((n_tok, d), F32),
                   jax.ShapeDtypeStruct((n_tok, d), BF16),
                   jax.ShapeDtypeStruct((TOP_K, n_tok), I32),
                   jax.ShapeDtypeStruct((TOP_K, n_tok), I32),
                   jax.ShapeDtypeStruct((TOP_K, n_tok), F32),
                   jax.ShapeDtypeStruct((n_tok // SORT_TILE, 1, N_EXPERTS), F32)],
        grid=(n_tok // ts,),
        in_specs=[row(ATTN_WIDTH), row(POOL_WIDTH), row(d), per_batch, per_batch, per_batch, full(g2),
                  full(wa), full(wpo), full(rw), full(rb), full(tri)],
        out_specs=[row(d), row(d), per_token, per_token, per_token,
                   pl.BlockSpec((ts // SORT_TILE, 1, N_EXPERTS), lambda i: (i, 0, 0))],
        compiler_params=_params("arbitrary"),
        name="outproj",
    )(attn, pool, x2, gate1, shift2, scale2, g2, wa, wpo, rw, rb, tri)


def _chunk_loops(cnt_ref, tile, chunk, body):
    def per_expert(e, carry):
        n_chunks = (cnt_ref[tile * N_EXPERTS + e] + chunk - 1) // chunk

        def per_chunk(i, c):
            body(e, i)
            return c

        return lax.fori_loop(0, n_chunks, per_chunk, carry)

    lax.fori_loop(0, N_EXPERTS, per_expert, 0)


def _run_copies(extra_ref, tile, make_copy, act):
    for e in range(N_EXPERTS):
        getattr(make_copy(tile * N_EXPERTS + e, True), act)()

    def per_extra(p, carry):
        getattr(make_copy(tile * MAX_EXTRA_CHUNKS + p, False), act)()
        return carry

    lax.fori_loop(0, extra_ref[tile], per_extra, 0)


def _dispatch_kernel(extra_ref, src_ref, dst_ref, xsrc_ref, xdst_ref, fill_ref, idx_t_ref, rank_t_ref, lo_col_ref,
                     h_ref, xs_ref, stage_ref, sem, *, tt):
    t = pl.program_id(0)
    slot = t % 2
    n_rows = TOP_K * tt
    u = FILL_CHUNK
    m = EXPERT_BLOCK

    @pl.when(t == 0)
    def _():
        stage_ref[...] = jnp.zeros_like(stage_ref)

        def pad_copy(e, i):
            return pltpu.make_async_copy(stage_ref.at[1, _tile_rows(0, u)],
                                         xs_ref.at[_tile_rows(fill_ref[e] - (i + 1) * u, u)], sem)

        def tail_copy(i):
            return pltpu.make_async_copy(stage_ref.at[1, _tile_rows(0, m)],
                                         xs_ref.at[_tile_rows(fill_ref[N_EXPERTS - 1] + i * m, m)], sem)

        n_tail = (xs_ref.shape[0] // ROW_SUBLANES - fill_ref[N_EXPERTS - 1]) // m
        for act in ("start", "wait"):
            _chunk_loops(fill_ref, 1, u, lambda e, i: getattr(pad_copy(e, i), act)())
            lax.fori_loop(0, n_tail, lambda i, c: (getattr(tail_copy(i), act)(), c)[1], 0)

    expert = lax.broadcasted_iota(I32, (N_EXPERTS, tt), 0)
    token = lax.broadcasted_iota(I32, (N_EXPERTS, tt), 1)
    rows = lax.broadcasted_iota(I32, (n_rows, tt), 0)
    run_base = jnp.broadcast_to(lo_col_ref[:, 0:1], (N_EXPERTS, tt))
    for s in range(1, tt // SORT_TILE):
        run_base = jnp.where(token >= s * SORT_TILE, lo_col_ref[:, s:s + 1], run_base)
    hit = None
    for k in range(TOP_K):
        onehot = expert == idx_t_ref[k:k + 1, :]
        base = jnp.sum(jnp.where(onehot, run_base, 0.0), axis=0, keepdims=True)
        slot_row = base.astype(I32) + rank_t_ref[k:k + 1, :]
        match = rows == slot_row
        hit = match if hit is None else (hit | match)
    perm = jnp.where(hit, 1.0, 0.0).astype(BF16)
    grouped = jnp.dot(perm, h_ref[...], preferred_element_type=F32)
    _store_rows(stage_ref.at[slot], n_rows, _pack_rows(grouped))

    def copies_of(tile):
        def make_copy(j, first):
            src = src_ref[j] if first else xsrc_ref[j]
            dst = dst_ref[j] if first else xdst_ref[j]
            return pltpu.make_async_copy(stage_ref.at[tile % 2, _tile_rows(src, DISPATCH_CHUNK)],
                                         xs_ref.at[_tile_rows(dst, DISPATCH_CHUNK)], sem)
        return make_copy

    @pl.when(t > 0)
    def _():
        _run_copies(extra_ref, t - 1, copies_of(t - 1), "wait")

    _run_copies(extra_ref, t, copies_of(t), "start")

    @pl.when(t == pl.num_programs(0) - 1)
    def _():
        _run_copies(extra_ref, t, copies_of(t), "wait")


def _dispatch(extra, src, dst, xsrc, xdst, fill, idx_t, rank_t, lo_col, h2, n_slots):
    n_tok, d = h2.shape
    tt = DISPATCH_TILE
    grid_spec = pltpu.PrefetchScalarGridSpec(
        num_scalar_prefetch=6,
        grid=(n_tok // tt,),
        in_specs=[pl.BlockSpec((TOP_K, tt), lambda i, *_: (0, i)),
                  pl.BlockSpec((TOP_K, tt), lambda i, *_: (0, i)),
                  pl.BlockSpec((None, N_EXPERTS, tt // SORT_TILE), lambda i, *_: (i, 0, 0)),
                  pl.BlockSpec((tt, d), lambda i, *_: (i, 0))],
        out_specs=pl.BlockSpec(memory_space=pl.ANY),
        scratch_shapes=[pltpu.VMEM((2, (TOP_K * tt + DISPATCH_CHUNK) * ROW_SUBLANES, LANES), U32),
                        pltpu.SemaphoreType.DMA],
    )
    return pl.pallas_call(
        functools.partial(_dispatch_kernel, tt=tt),
        out_shape=jax.ShapeDtypeStruct((n_slots * ROW_SUBLANES, LANES), U32),
        grid_spec=grid_spec,
        compiler_params=pltpu.CompilerParams(dimension_semantics=("arbitrary",),
                                             vmem_limit_bytes=VMEM_LIMIT_BYTES, has_side_effects=True),
        name="dispatch",
    )(extra, src, dst, xsrc, xdst, fill, idx_t, rank_t, lo_col, h2)


def _experts_kernel(be_ref, nvalid_ref, nused_ref, xs_ref, wgu32_ref, bgu_ref, wd32_ref, bd_ref, yb_ref,
                    wgu_ref, wd_ref, *, d_ff):
    i = pl.program_id(0)

    @pl.when((i == 0) | (be_ref[i] != be_ref[jnp.maximum(i - 1, 0)]))
    def _():
        wgu_ref[...] = wgu32_ref[...].astype(BF16)
        wd_ref[...] = wd32_ref[...].astype(BF16)

    @pl.when(i < nused_ref[0])
    def _():
        m = xs_ref.shape[0] // ROW_SUBLANES
        row = lax.broadcasted_iota(I32, (m, 1), 0)
        xb = _load_rows(xs_ref, m, keep=row < nvalid_ref[i])
        acc = None
        for j in range(d_ff // FF_CHUNK):
            lo, hi = j * FF_CHUNK, (j + 1) * FF_CHUNK
            g = jnp.dot(xb, wgu_ref[:, lo:hi], preferred_element_type=F32) + bgu_ref[:, lo:hi]
            u = jnp.dot(xb, wgu_ref[:, d_ff + lo:d_ff + hi], preferred_element_type=F32) + bgu_ref[:, d_ff + lo:d_ff + hi]
            g = jnp.minimum(g, SWIGLU_LIMIT)
            u = jnp.clip(u, -SWIGLU_LIMIT, SWIGLU_LIMIT)
            act = g * (1.0 / (1.0 + jnp.exp(-SWIGLU_ALPHA * g))) * (u + 1.0)
            part = jnp.dot(act.astype(BF16), wd_ref[lo:hi, :], preferred_element_type=F32)
            acc = part if acc is None else acc + part
        y = (acc + bd_ref[...]).astype(BF16).astype(F32)
        _store_rows(yb_ref, m, _pack_rows(y))

    @pl.when(i >= nused_ref[0])
    def _():
        yb_ref[...] = jnp.zeros_like(yb_ref)


def _experts(block_expert, n_valid, n_used, xs, wgu, bgu, wd, bd, *, layer):
    n_slots = xs.shape[0] // ROW_SUBLANES
    d_ff, d = wd.shape[2], wd.shape[3]
    m = EXPERT_BLOCK
    blk = lambda i, be, nv, nu: (jnp.minimum(i, nu[0] - 1), 0)
    per_expert = lambda i, be, nv, nu: (layer, be[i], 0, 0)
    grid_spec = pltpu.PrefetchScalarGridSpec(
        num_scalar_prefetch=3,
        grid=(n_slots // m,),
        in_specs=[pl.BlockSpec((m * ROW_SUBLANES, LANES), blk),
                  pl.BlockSpec((None, None, d, 2 * d_ff), per_expert),
                  pl.BlockSpec((None, None, 1, 2 * d_ff), per_expert),
                  pl.BlockSpec((None, None, d_ff, d), per_expert),
                  pl.BlockSpec((None, None, 1, d), per_expert)],
        out_specs=pl.BlockSpec((m * ROW_SUBLANES, LANES), lambda i, be, nv, nu: (i, 0)),
        scratch_shapes=[pltpu.VMEM((d, 2 * d_ff), BF16), pltpu.VMEM((d_ff, d), BF16)],
    )
    return pl.pallas_call(
        functools.partial(_experts_kernel, d_ff=d_ff),
        out_shape=jax.ShapeDtypeStruct(xs.shape, U32),
        grid_spec=grid_spec,
        compiler_params=_params("arbitrary"),
        name="experts",
    )(block_expert, n_valid, n_used, xs, wgu, bgu, wd, bd)


def _combine_kernel(extra_ref, src_ref, dst_ref, xsrc_ref, xdst_ref, idx_ref, rank_ref, gates_ref, lo_row_ref,
                    x_ref, gate2_ref, yb_ref, xo_ref, stage_ref, sems, *, tt, n_stage):
    t = pl.program_id(0)
    slot = t % 2

    def copies_of(tile):
        def make_copy(j, first):
            src = src_ref[j] if first else xsrc_ref[j]
            dst = dst_ref[j] if first else xdst_ref[j]
            return pltpu.make_async_copy(yb_ref.at[_tile_rows(src, COMBINE_CHUNK)],
                                         stage_ref.at[tile % 2, _tile_rows(dst, COMBINE_CHUNK)], sems.at[tile % 2])
        return make_copy

    @pl.when(t == 0)
    def _():
        stage_ref[...] = jnp.zeros_like(stage_ref)
        _run_copies(extra_ref, 0, copies_of(0), "start")

    @pl.when(t + 1 < pl.num_programs(0))
    def _():
        _run_copies(extra_ref, t + 1, copies_of(t + 1), "start")

    _run_copies(extra_ref, t, copies_of(t), "wait")

    lane = lax.broadcasted_iota(I32, (tt, N_EXPERTS), 1)
    cols = lax.broadcasted_iota(I32, (tt, n_stage), 1)
    weights = jnp.zeros((tt, n_stage), F32)
    for k in range(TOP_K):
        onehot = lane == idx_ref[:, k:k + 1]
        base = jnp.sum(jnp.where(onehot, lo_row_ref[...], 0.0), axis=1, keepdims=True)
        col = base.astype(I32) + rank_ref[:, k:k + 1]
        weights = jnp.where(cols == col, gates_ref[:, k:k + 1], weights)
    staged = _load_rows(stage_ref.at[slot], n_stage)
    y = jnp.dot(weights.astype(BF16), staged, preferred_element_type=F32)
    xo_ref[...] = x_ref[...] + gate2_ref[...] * y


def _combine(extra, src, dst, xsrc, xdst, idx, rank, gates, lo_row, x2, gate2, yb, *, seq):
    n_tok, d = x2.shape
    tt = SORT_TILE
    spt = seq // tt
    n_stage = TOP_K * tt + N_EXPERTS * COMBINE_CHUNK
    grid_spec = pltpu.PrefetchScalarGridSpec(
        num_scalar_prefetch=5,
        grid=(n_tok // tt,),
        in_specs=[pl.BlockSpec((tt, TOP_K), lambda i, *_: (i, 0)),
                  pl.BlockSpec((tt, TOP_K), lambda i, *_: (i, 0)),
                  pl.BlockSpec((tt, TOP_K), lambda i, *_: (i, 0)),
                  pl.BlockSpec((None, 1, N_EXPERTS), lambda i, *_: (i, 0, 0)),
                  pl.BlockSpec((tt, d), lambda i, *_: (i, 0)),
                  pl.BlockSpec((None, 1, d), lambda i, *_: (i // spt, 0, 0)),
                  pl.BlockSpec(memory_space=pl.ANY)],
        out_specs=pl.BlockSpec((tt, d), lambda i, *_: (i, 0)),
        scratch_shapes=[pltpu.VMEM((2, n_stage * ROW_SUBLANES, LANES), U32),
                        pltpu.SemaphoreType.DMA((2,))],
    )
    return pl.pallas_call(
        functools.partial(_combine_kernel, tt=tt, n_stage=n_stage),
        out_shape=jax.ShapeDtypeStruct((n_tok, d), F32),
        grid_spec=grid_spec,
        compiler_params=_params("arbitrary"),
        name="combine",
    )(extra, src, dst, xsrc, xdst, idx, rank, gates, lo_row, x2, gate2, yb)


def _exclusive_cumsum(a, axis):
    return jnp.cumsum(a, axis=axis) - a


def _region_tables(sort_cnt):
    m = EXPERT_BLOCK
    total = jnp.sum(sort_cnt, axis=0)
    padded = (total + max(DISPATCH_CHUNK, COMBINE_CHUNK) + m - 1) // m * m
    pends = jnp.cumsum(padded)
    return total, pends - padded, pends


def _chunk_tables(step_cnt, chunk):
    n_chunks = jnp.maximum((step_cnt + chunk - 1) // chunk, 1)
    lo_chunked = _exclusive_cumsum(n_chunks * chunk, 1)
    n_extra = n_chunks - 1
    ends = jnp.cumsum(n_extra, axis=1)
    p = jnp.arange(MAX_EXTRA_CHUNKS, dtype=I32)
    x_expert = jnp.minimum(jnp.sum(p[None, :, None] >= ends[:, None, :], axis=2), N_EXPERTS - 1)
    x_offset = (p[None, :] - jnp.take_along_axis(ends - n_extra, x_expert, axis=1) + 1) * chunk
    return lo_chunked, ends[:, -1], x_expert, x_offset


def kernel(x, c, positions, ada_w, ada_b, norm1_g, w_in, q_norm_g, k_norm_g, attn_sinks, pool_w, pool_b,
           pool_scale, w_out, norm2_g, router_w, router_b, expert_w_gu, expert_b_gu, expert_w_down,
           expert_b_down):
    batch, seq, d = x.shape
    n_layers = ada_w.shape[0]
    n_tok = batch * seq
    d_ff = expert_w_down.shape[2]
    assert d == SUBLANES * LANES
    assert seq % WINDOW == 0 and seq % min(TOKEN_TILE, seq) == 0 and seq % ROUTE_TILE == 0

    mod = _ada(c, ada_w, ada_b).reshape(n_layers, batch, 6, 1, d)
    rope = _rope_tables(positions)
    sel, unrot = _rope_select_matrix()
    bdq, bdk = _head_mean_matrix(ATTN_WIDTH), _head_mean_matrix(KV_WIDTH)
    tri = jnp.asarray(np.kron(np.eye(ROUTE_TILE // SORT_TILE, dtype=np.float32),
                              np.triu(np.ones((SORT_TILE, SORT_TILE), np.float32), 1)), BF16)

    m = EXPERT_BLOCK
    slack = max(DISPATCH_CHUNK, COMBINE_CHUNK)
    n_blocks = -(-(n_tok * TOP_K + N_EXPERTS * (slack + m - 1)) // m)
    n_slots = n_blocks * m
    n_tiles = n_tok // SORT_TILE

    x2 = x.reshape(n_tok, d)
    for l in range(n_layers):
        shift1, scale1, gate1, shift2, scale2, gate2 = (mod[l, :, j] for j in range(6))
        q, k, v, pool = _inproj(
            x2, shift1, scale1, norm1_g[l].reshape(1, d), w_in[l].astype(BF16),
            jnp.tile(q_norm_g[l], N_Q_HEADS).reshape(1, ATTN_WIDTH),
            jnp.tile(k_norm_g[l], N_KV_HEADS).reshape(1, KV_WIDTH),
            rope, sel, unrot, bdq, bdk, pool_w[l].astype(BF16), pool_b[l].reshape(1, POOL_WIDTH),
            pool_scale[l].reshape(1, POOL_WIDTH), batch=batch, seq=seq)
        attn = _attention(q, k, v, attn_sinks[l], seq=seq)
        w_out_b = w_out[l].astype(BF16)
        x2, h2, idx_t, rank_t, gates_t, tile_cnt = _outproj(
            attn, pool, x2, gate1, shift2, scale2, norm2_g[l].reshape(1, d),
            w_out_b[:ATTN_WIDTH], w_out_b[ATTN_WIDTH:], router_w[l].T.astype(BF16),
            router_b[l].reshape(N_EXPERTS, 1),
            tri, seq=seq)

        sort_cnt = tile_cnt.reshape(n_tiles, N_EXPERTS).astype(I32)
        total, pstart, pends = _region_tables(sort_cnt)
        flat = lambda a: a.reshape(-1).astype(I32)
        n_used = (pends[-1] // m).astype(I32).reshape(1)
        block_row = jnp.minimum(jnp.arange(n_blocks, dtype=I32), n_used[0] - 1) * m
        block_expert = jnp.minimum(jnp.sum(block_row[:, None] >= pends[None, :], axis=1), N_EXPERTS - 1).astype(I32)
        n_valid = jnp.clip((pstart + total)[block_expert] - block_row, 0, m).astype(I32)
        gstart = pstart[None, :] + _exclusive_cumsum(sort_cnt, 0)

        per = DISPATCH_TILE // SORT_TILE
        pair_cnt = sort_cnt.reshape(n_tiles // per, per, N_EXPERTS)
        step_cnt = jnp.sum(pair_cnt, axis=1)
        d_lo = _exclusive_cumsum(step_cnt, 1)
        d_gs = gstart[::per]
        _, d_extra, d_xe, d_xo = _chunk_tables(step_cnt, DISPATCH_CHUNK)
        d_rows = lambda table: flat(jnp.take_along_axis(table, d_xe, axis=1) + d_xo)
        run_base = (d_lo[:, None, :] + _exclusive_cumsum(pair_cnt, 1)).transpose(0, 2, 1).astype(F32)
        fill = jnp.concatenate([pends, pends - pstart - total]).astype(I32)
        xs = _dispatch(flat(d_extra), flat(d_lo), flat(d_gs), d_rows(d_lo), d_rows(d_gs), fill,
                       idx_t, rank_t, run_base, h2, n_slots)
        yb = _experts(block_expert, n_valid, n_used, xs, expert_w_gu,
                      expert_b_gu.reshape(n_layers, N_EXPERTS, 1, 2 * d_ff), expert_w_down,
                      expert_b_down.reshape(n_layers, N_EXPERTS, 1, d), layer=l)
        c_lo, c_extra, c_xe, c_xo = _chunk_tables(sort_cnt, COMBINE_CHUNK)
        c_rows = lambda table: flat(jnp.take_along_axis(table, c_xe, axis=1) + c_xo)
        x2 = _combine(flat(c_extra), flat(gstart), flat(c_lo), c_rows(gstart), c_rows(c_lo),
                      idx_t.T, rank_t.T, gates_t.T, c_lo.astype(F32).reshape(n_tiles, 1, N_EXPERTS), x2, gate2, yb,
                      seq=seq)
    return x2.reshape(batch, seq, d)
```

```python
import functools

import numpy as np
import jax
import jax.numpy as jnp
from jax import lax
from jax.experimental import pallas as pl
from jax.experimental.pallas import tpu as pltpu

F32 = jnp.float32
BF16 = jnp.bfloat16
I32 = jnp.int32
U32 = jnp.uint32

HEAD_DIM = 64
N_Q_HEADS = 8
N_KV_HEADS = 2
Q_PER_KV = N_Q_HEADS // N_KV_HEADS
ATTN_WIDTH = N_Q_HEADS * HEAD_DIM
KV_WIDTH = N_KV_HEADS * HEAD_DIM
WINDOW = 128
ROT_DIM = HEAD_DIM // 4
ROT_HALF = ROT_DIM // 2
ROPE_THETA = 500000.0
POOL_WINDOWS = (2, 4, 8, 16)
N_POOL_GROUPS = len(POOL_WINDOWS)
POOL_GROUP_DIM = 128
POOL_WIDTH = N_POOL_GROUPS * POOL_GROUP_DIM
MAX_POOL_WINDOW = max(POOL_WINDOWS)
N_EXPERTS = 32
TOP_K = 4
SWIGLU_LIMIT = 7.0
SWIGLU_ALPHA = 1.702
EPS = 1e-6

LANES = 128
SUBLANES = 8
VMEM_LIMIT_BYTES = 56 * 1024 * 1024

TOKEN_TILE = 1024
ATTN_TILE = 512
ROUTE_TILE = 512
SORT_TILE = 256
DISPATCH_TILE = SORT_TILE
EXPERT_BLOCK = 512
FF_CHUNK = 512
DISPATCH_CHUNK = 32
COMBINE_CHUNK = 32
FILL_CHUNK = 16
MAX_EXTRA_CHUNKS = TOP_K * DISPATCH_TILE // DISPATCH_CHUNK
assert MAX_EXTRA_CHUNKS == TOP_K * SORT_TILE // COMBINE_CHUNK


def _params(*sem):
    return pltpu.CompilerParams(dimension_semantics=sem, vmem_limit_bytes=VMEM_LIMIT_BYTES)


ROW_SUBLANES = 4
HALF_WIDTH = ROW_SUBLANES * LANES
HIGH_HALF_MASK = 0xFFFF0000


def _tile_rows(start, n):
    return pl.ds(pl.multiple_of(start * ROW_SUBLANES, ROW_SUBLANES), n * ROW_SUBLANES)


def _word_chunk(c, n):
    return pl.ds(c, n, stride=ROW_SUBLANES)


def _pack_rows(vals):
    lo = lax.bitcast_convert_type(vals[:, :HALF_WIDTH], U32)
    hi = lax.bitcast_convert_type(vals[:, HALF_WIDTH:], U32)
    return lax.shift_right_logical(lo, jnp.uint32(16)) | hi


def _store_rows(ref, n, words):
    for c in range(ROW_SUBLANES):
        ref[_word_chunk(c, n), :] = words[:, c * LANES:(c + 1) * LANES]


def _load_rows(ref, n, keep=None):
    lo, hi = [], []
    for c in range(ROW_SUBLANES):
        w = ref[_word_chunk(c, n), :]
        if keep is not None:
            w = jnp.where(keep, w, jnp.uint32(0))
        lo.append(lax.bitcast_convert_type(lax.shift_left(w, jnp.uint32(16)), F32))
        hi.append(lax.bitcast_convert_type(w & jnp.uint32(HIGH_HALF_MASK), F32))
    return jnp.concatenate(lo + hi, axis=1).astype(BF16)


def _ada_kernel(c_ref, w_ref, b_ref, o_ref):
    c = c_ref[...]
    c_act = c * (1.0 / (1.0 + jnp.exp(-c)))
    o_ref[0] = jnp.dot(c_act, w_ref[0], preferred_element_type=F32,
                       precision=lax.Precision.HIGHEST) + b_ref[0]


def _ada(c, ada_w, ada_b):
    n_layers, d, d6 = ada_w.shape
    b = c.shape[0]
    tn = 1536
    return pl.pallas_call(
        _ada_kernel,
        out_shape=jax.ShapeDtypeStruct((n_layers, b, d6), F32),
        grid=(n_layers, d6 // tn),
        in_specs=[pl.BlockSpec((b, d), lambda l, j: (0, 0)),
                  pl.BlockSpec((1, d, tn), lambda l, j: (l, 0, j)),
                  pl.BlockSpec((1, 1, tn), lambda l, j: (l, 0, j))],
        out_specs=pl.BlockSpec((1, b, tn), lambda l, j: (l, 0, j)),
        compiler_params=_params("arbitrary", "arbitrary"),
        name="ada",
    )(c, ada_w, ada_b.reshape(n_layers, 1, d6))


def _rope_kernel(pos_ref, freq_ref, ch_ref, cl_ref, sh_ref, sl_ref):
    ang = pos_ref[...].astype(F32) * freq_ref[...]
    for fn, hi_ref, lo_ref in ((jnp.cos, ch_ref, cl_ref), (jnp.sin, sh_ref, sl_ref)):
        val = fn(ang)
        hi = val.astype(BF16)
        hi_ref[...] = hi
        lo_ref[...] = (val - hi.astype(F32)).astype(BF16)


def _rope_tables(positions):
    n_tok = positions.size
    inv_freq = ROPE_THETA ** (-jnp.arange(0, ROT_DIM, 2, dtype=F32) / ROT_DIM)
    rows = n_tok * ROT_HALF // LANES
    pos_rep = jnp.repeat(positions.reshape(-1), ROT_HALF).reshape(rows, LANES)
    freq = jnp.tile(inv_freq, LANES // ROT_HALF).reshape(1, LANES)
    tr = min(rows, 512)
    spec = pl.BlockSpec((tr, LANES), lambda i: (i, 0))
    outs = pl.pallas_call(
        _rope_kernel,
        out_shape=[jax.ShapeDtypeStruct((rows, LANES), BF16)] * 4,
        grid=(rows // tr,),
        in_specs=[spec, pl.BlockSpec((1, LANES), lambda i: (0, 0))],
        out_specs=[spec] * 4,
        compiler_params=_params("arbitrary"),
        name="rope",
    )(pos_rep, freq)
    return jnp.concatenate([o.reshape(n_tok, ROT_HALF) for o in outs], axis=1)


def _rope_select_matrix():
    sel = np.zeros((4 * ROT_HALF, 3 * LANES), np.float32)
    for d in range(LANES):
        dd = d % HEAD_DIM
        if dd < ROT_DIM:
            f = dd % ROT_HALF
            sel[f, d] = 1.0
            sel[ROT_HALF + f, d] = 1.0
        if dd < ROT_HALF:
            sel[2 * ROT_HALF + dd, LANES + d] = -1.0
            sel[3 * ROT_HALF + dd, LANES + d] = -1.0
        elif dd < ROT_DIM:
            sel[2 * ROT_HALF + dd - ROT_HALF, 2 * LANES + d] = 1.0
            sel[3 * ROT_HALF + dd - ROT_HALF, 2 * LANES + d] = 1.0
    unrotated = np.array([[1.0 if (d % HEAD_DIM) >= ROT_DIM else 0.0 for d in range(LANES)]], np.float32)
    return jnp.asarray(sel, BF16), jnp.asarray(unrotated, F32)


def _head_mean_matrix(width):
    m = np.zeros((width, width), np.float32)
    for h in range(width // HEAD_DIM):
        m[h * HEAD_DIM:(h + 1) * HEAD_DIM, h * HEAD_DIM:(h + 1) * HEAD_DIM] = 1.0 / HEAD_DIM
    return jnp.asarray(m, BF16)


def _inproj_kernel(x_ref, shift_ref, scale_ref, g1_ref, w_ref, qg_ref, kg_ref, rope_ref, sel_ref,
                   unrot_ref, bdq_ref, bdk_ref, wp_ref, bp_ref, ps_ref,
                   q_ref, k_ref, v_ref, pool_ref, carry_ref, *, ts):
    s_idx = pl.program_id(1)

    @pl.when(s_idx == 0)
    def _():
        carry_ref[...] = jnp.zeros_like(carry_ref)

    x = x_ref[...]
    ms = jnp.mean(x * x, axis=-1, keepdims=True)
    h = x * lax.rsqrt(ms + EPS) * (g1_ref[...] * (1.0 + scale_ref[...])) + shift_ref[...]
    hb = h.astype(BF16)

    tab = jnp.dot(rope_ref[...], sel_ref[...], preferred_element_type=F32)
    cos_t = tab[:, :LANES] + unrot_ref[...]
    sin_a = tab[:, LANES:2 * LANES]
    sin_b = tab[:, 2 * LANES:]

    def norm_rope(t, bd_ref, g_ref, out_scale):
        msq = jnp.dot((t * t).astype(BF16), bd_ref[...], preferred_element_type=F32)
        tn = t * lax.rsqrt(msq + EPS) * g_ref[...]
        chunks = []
        for c in range(t.shape[1] // LANES):
            tc = tn[:, c * LANES:(c + 1) * LANES]
            rot = (tc * cos_t + pltpu.roll(tc, LANES - ROT_HALF, 1) * sin_a
                   + pltpu.roll(tc, ROT_HALF, 1) * sin_b)
            chunks.append(rot * out_scale)
        return chunks[0] if len(chunks) == 1 else jnp.concatenate(chunks, axis=1)

    q = jnp.dot(hb, w_ref[:, :ATTN_WIDTH], preferred_element_type=F32)
    q_ref[...] = norm_rope(q, bdq_ref, qg_ref, HEAD_DIM ** -0.5).astype(BF16)
    k = jnp.dot(hb, w_ref[:, ATTN_WIDTH:ATTN_WIDTH + KV_WIDTH], preferred_element_type=F32)
    k_ref[...] = norm_rope(k, bdk_ref, kg_ref, 1.0).astype(BF16)
    v = jnp.dot(hb, w_ref[:, ATTN_WIDTH + KV_WIDTH:ATTN_WIDTH + 2 * KV_WIDTH], preferred_element_type=F32)
    v_ref[...] = v.astype(BF16)

    u = jnp.dot(hb, w_ref[:, ATTN_WIDTH + 2 * KV_WIDTH:], preferred_element_type=F32)
    ext = jnp.concatenate([carry_ref[...], u], axis=0)
    carry_ref[...] = u[ts - MAX_POOL_WINDOW:, :]
    count = (lax.broadcasted_iota(I32, (ts, POOL_GROUP_DIM), 0) + (s_idx * ts + 1)).astype(F32)
    outs = []
    for g, w in enumerate(POOL_WINDOWS):
        win = ext[:, g * POOL_GROUP_DIM:(g + 1) * POOL_GROUP_DIM]
        span = 1
        while span < w:
            win = win + pltpu.roll(win, span, 0)
            span *= 2
        mean = win[MAX_POOL_WINDOW:, :] * (1.0 / jnp.minimum(count, float(w)))
        pooled = mean - u[:, g * POOL_GROUP_DIM:(g + 1) * POOL_GROUP_DIM]
        outs.append(jnp.dot(pooled.astype(BF16), wp_ref[g], preferred_element_type=F32))
    y = (jnp.concatenate(outs, axis=1) + bp_ref[...]) * ps_ref[...]
    pool_ref[...] = y.astype(BF16)


def _inproj(x2, shift1, scale1, g1, w_in, qg, kg, rope, sel, unrot, bdq, bdk, wp, bp, ps, *, batch, seq):
    n_tok, d = x2.shape
    ts = min(TOKEN_TILE, seq)
    spt = seq // ts
    row = lambda width: pl.BlockSpec((ts, width), lambda b, s: (b * spt + s, 0))
    per_batch = pl.BlockSpec((None, 1, d), lambda b, s: (b, 0, 0))
    full = lambda a: pl.BlockSpec(a.shape, lambda b, s: (0,) * a.ndim)
    return pl.pallas_call(
        functools.partial(_inproj_kernel, ts=ts),
        out_shape=[jax.ShapeDtypeStruct((n_tok, ATTN_WIDTH), BF16),
                   jax.ShapeDtypeStruct((n_tok, KV_WIDTH), BF16),
                   jax.ShapeDtypeStruct((n_tok, KV_WIDTH), BF16),
                   jax.ShapeDtypeStruct((n_tok, POOL_WIDTH), BF16)],
        grid=(batch, spt),
        in_specs=[row(d), per_batch, per_batch, full(g1), full(w_in), full(qg), full(kg),
                  row(rope.shape[1]), full(sel), full(unrot), full(bdq), full(bdk),
                  full(wp), full(bp), full(ps)],
        out_specs=[row(ATTN_WIDTH), row(KV_WIDTH), row(KV_WIDTH), row(POOL_WIDTH)],
        scratch_shapes=[pltpu.VMEM((MAX_POOL_WINDOW, POOL_WIDTH), F32)],
        compiler_params=_params("arbitrary", "arbitrary"),
        name="inproj",
    )(x2, shift1, scale1, g1, w_in, qg, kg, rope, sel, unrot, bdq, bdk, wp, bp, ps)


def _attn_kernel(sink_ref, q_ref, kc_ref, vc_ref, kp_ref, vp_ref, o_ref, *, tq, tiles_per_seq):
    first = (pl.program_id(0) % tiles_per_seq) == 0
    kfull = jnp.concatenate([kp_ref[...], kc_ref[...]], axis=0)
    vfull = jnp.concatenate([vp_ref[...], vc_ref[...]], axis=0)
    rows = Q_PER_KV * WINDOW
    qi = lax.broadcasted_iota(I32, (rows, WINDOW), 0) % WINDOW
    from_prev = lax.broadcasted_iota(I32, (rows, WINDOW), 1) > qi
    row_head = lax.broadcasted_iota(I32, (rows, 1), 0) // WINDOW
    sinks = []
    for h in range(N_KV_HEADS):
        sink = jnp.zeros((rows, 1), F32)
        for g in range(Q_PER_KV):
            sink = jnp.where(row_head == g, sink_ref[h * Q_PER_KV + g], sink)
        sinks.append(sink)

    problems = [(j, h) for j in range(tq // WINDOW) for h in range(N_KV_HEADS)]
    scores, maxes = [], []
    for j, h in problems:
        qs = jnp.concatenate(
            [q_ref[j * WINDOW:(j + 1) * WINDOW, (h * Q_PER_KV + g) * HEAD_DIM:(h * Q_PER_KV + g + 1) * HEAD_DIM]
             for g in range(Q_PER_KV)], axis=0)
        kb = kfull[j * WINDOW:(j + 2) * WINDOW, h * HEAD_DIM:(h + 1) * HEAD_DIM]
        s = lax.dot_general(qs, kb, (((1,), (1,)), ((), ())), preferred_element_type=F32)
        s_prev = s[:, :WINDOW]
        if j == 0:
            s_prev = jnp.where(first, -jnp.inf, s_prev)
        s = jnp.where(from_prev, s_prev, s[:, WINDOW:])
        scores.append(s)
        maxes.append(jnp.maximum(jnp.max(s, axis=-1, keepdims=True), sinks[h]))
    probs, denoms = [], []
    for (j, h), s, m in zip(problems, scores, maxes):
        p = jnp.exp(s - m)
        denoms.append(jnp.sum(p, axis=-1, keepdims=True) + jnp.exp(sinks[h] - m))
        p = p.astype(BF16)
        zero = jnp.zeros_like(p)
        probs.append(jnp.concatenate([jnp.where(from_prev, p, zero), jnp.where(from_prev, zero, p)], axis=1))
    for (j, h), p, denom in zip(problems, probs, denoms):
        vb = vfull[j * WINDOW:(j + 2) * WINDOW, h * HEAD_DIM:(h + 1) * HEAD_DIM]
        o = jnp.dot(p, vb, preferred_element_type=F32) * (1.0 / denom)
        o_ref[j * WINDOW:(j + 1) * WINDOW, h * Q_PER_KV * HEAD_DIM:(h + 1) * Q_PER_KV * HEAD_DIM] = (
            jnp.concatenate([o[g * WINDOW:(g + 1) * WINDOW, :] for g in range(Q_PER_KV)], axis=1).astype(BF16))


def _attention(q, k, v, sinks, *, seq):
    n_tok = q.shape[0]
    tq = min(ATTN_TILE, seq)
    per = tq // WINDOW
    cur = lambda width: pl.BlockSpec((tq, width), lambda i: (i, 0))
    prev = pl.BlockSpec((WINDOW, KV_WIDTH), lambda i: (jnp.maximum(i * per - 1, 0), 0))
    return pl.pallas_call(
        functools.partial(_attn_kernel, tq=tq, tiles_per_seq=seq // tq),
        out_shape=jax.ShapeDtypeStruct((n_tok, ATTN_WIDTH), BF16),
        grid=(n_tok // tq,),
        in_specs=[pl.BlockSpec(memory_space=pltpu.SMEM), cur(ATTN_WIDTH), cur(KV_WIDTH), cur(KV_WIDTH), prev, prev],
        out_specs=cur(ATTN_WIDTH),
        compiler_params=_params("arbitrary"),
        name="attn",
    )(sinks, q, k, v, k, v)


def _outproj_kernel(attn_ref, pool_ref, x_ref, gate1_ref, shift2_ref, scale2_ref, g2_ref, wa_ref, wpo_ref,
                    rw_ref, rb_ref, tri_ref,
                    xo_ref, h2_ref, idx_ref, rank_ref, gates_ref, cnt_ref):
    mixed = (jnp.dot(attn_ref[...], wa_ref[...], preferred_element_type=F32)
             + jnp.dot(pool_ref[...], wpo_ref[...], preferred_element_type=F32))
    x = x_ref[...] + gate1_ref[...] * mixed
    xo_ref[...] = x
    ms = jnp.mean(x * x, axis=-1, keepdims=True)
    h2 = x * lax.rsqrt(ms + EPS) * (g2_ref[...] * (1.0 + scale2_ref[...])) + shift2_ref[...]
    h2b = h2.astype(BF16)
    h2_ref[...] = h2b

    contract_last = (((1,), (1,)), ((), ()))
    logits = lax.dot_general(rw_ref[...], h2b, contract_last, preferred_element_type=F32) + rb_ref[...]
    expert = lax.broadcasted_iota(I32, logits.shape, 0)
    work = logits
    vals, picks, chosen = [], [], []
    for _ in range(TOP_K):
        m = jnp.max(work, axis=0, keepdims=True)
        pick = jnp.min(jnp.where(work == m, expert, N_EXPERTS), axis=0, keepdims=True)
        sel = expert == pick
        vals.append(m)
        picks.append(pick)
        chosen.append(sel)
        work = jnp.where(sel, -jnp.inf, work)
    exps = [jnp.exp(v - vals[0]) for v in vals]
    inv_total = 1.0 / (exps[0] + exps[1] + exps[2] + exps[3])

    multi = jnp.zeros(logits.shape, F32)
    for sel in chosen:
        multi = multi + sel.astype(F32)
    multi_b = multi.astype(BF16)
    before = jnp.dot(multi_b, tri_ref[...], preferred_element_type=F32)
    for kk in range(TOP_K):
        idx_ref[kk:kk + 1, :] = picks[kk]
        gates_ref[kk:kk + 1, :] = exps[kk] * inv_total
        rank_ref[kk:kk + 1, :] = jnp.sum(jnp.where(chosen[kk], before, 0.0), axis=0, keepdims=True).astype(I32)
    ones = jnp.ones((SUBLANES, SORT_TILE), BF16)
    for s in range(multi.shape[1] // SORT_TILE):
        per_expert = lax.dot_general(ones, multi_b[:, s * SORT_TILE:(s + 1) * SORT_TILE], contract_last,
                                     preferred_element_type=F32)
        cnt_ref[s] = per_expert[0:1, :]


def _outproj(attn, pool, x2, gate1, shift2, scale2, g2, wa, wpo, rw, rb, tri, *, seq):
    n_tok, d = x2.shape
    ts = tri.shape[0]
    spt = seq // ts
    row = lambda width: pl.BlockSpec((ts, width), lambda i: (i, 0))
    per_token = pl.BlockSpec((TOP_K, ts), lambda i: (0, i))
    per_batch = pl.BlockSpec((None, 1, d), lambda i: (i // spt, 0, 0))
    full = lambda a: pl.BlockSpec(a.shape, lambda i: (0,) * a.ndim)
    return pl.pallas_call(
        _outproj_kernel,
        out_shape=[jax.ShapeDtypeStruct((n_tok, d), F32),
                   jax.ShapeDtypeStruct((n_tok, d), BF16),
                   jax.ShapeDtypeStruct((TOP_K, n_tok), I32),
                   jax.ShapeDtypeStruct((TOP_K, n_tok), I32),
                   jax.ShapeDtypeStruct((TOP_K, n_tok), F32),
                   jax.ShapeDtypeStruct((n_tok // SORT_TILE, 1, N_EXPERTS), F32)],
        grid=(n_tok // ts,),
        in_specs=[row(ATTN_WIDTH), row(POOL_WIDTH), row(d), per_batch, per_batch, per_batch, full(g2),
                  full(wa), full(wpo), full(rw), full(rb), full(tri)],
        out_specs=[row(d), row(d), per_token, per_token, per_token,
                   pl.BlockSpec((ts // SORT_TILE, 1, N_EXPERTS), lambda i: (i, 0, 0))],
        compiler_params=_params("arbitrary"),
        name="outproj",
    )(attn, pool, x2, gate1, shift2, scale2, g2, wa, wpo, rw, rb, tri)


def _chunk_loops(cnt_ref, tile, chunk, body):
    def per_expert(e, carry):
        n_chunks = (cnt_ref[tile * N_EXPERTS + e] + chunk - 1) // chunk

        def per_chunk(i, c):
            body(e, i)
            return c

        return lax.fori_loop(0, n_chunks, per_chunk, carry)

    lax.fori_loop(0, N_EXPERTS, per_expert, 0)


def _run_copies(extra_ref, tile, make_copy, act):
    for e in range(N_EXPERTS):
        getattr(make_copy(tile * N_EXPERTS + e, True), act)()

    def per_extra(p, carry):
        getattr(make_copy(tile * MAX_EXTRA_CHUNKS + p, False), act)()
        return carry

    lax.fori_loop(0, extra_ref[tile], per_extra, 0)


def _dispatch_kernel(extra_ref, src_ref, dst_ref, xsrc_ref, xdst_ref, fill_ref, idx_t_ref, rank_t_ref, lo_col_ref,
                     h_ref, xs_ref, stage_ref, sem, *, tt):
    t = pl.program_id(0)
    slot = t % 2
    n_rows = TOP_K * tt
    u = FILL_CHUNK
    m = EXPERT_BLOCK

    @pl.when(t == 0)
    def _():
        stage_ref[...] = jnp.zeros_like(stage_ref)

        def pad_copy(e, i):
            return pltpu.make_async_copy(stage_ref.at[1, _tile_rows(0, u)],
                                         xs_ref.at[_tile_rows(fill_ref[e] - (i + 1) * u, u)], sem)

        def tail_copy(i):
            return pltpu.make_async_copy(stage_ref.at[1, _tile_rows(0, m)],
                                         xs_ref.at[_tile_rows(fill_ref[N_EXPERTS - 1] + i * m, m)], sem)

        n_tail = (xs_ref.shape[0] // ROW_SUBLANES - fill_ref[N_EXPERTS - 1]) // m
        for act in ("start", "wait"):
            _chunk_loops(fill_ref, 1, u, lambda e, i: getattr(pad_copy(e, i), act)())
            lax.fori_loop(0, n_tail, lambda i, c: (getattr(tail_copy(i), act)(), c)[1], 0)

    expert = lax.broadcasted_iota(I32, (N_EXPERTS, tt), 0)
    token = lax.broadcasted_iota(I32, (N_EXPERTS, tt), 1)
    rows = lax.broadcasted_iota(I32, (n_rows, tt), 0)
    run_base = jnp.broadcast_to(lo_col_ref[:, 0:1], (N_EXPERTS, tt))
    for s in range(1, tt // SORT_TILE):
        run_base = jnp.where(token >= s * SORT_TILE, lo_col_ref[:, s:s + 1], run_base)
    hit = None
    for k in range(TOP_K):
        onehot = expert == idx_t_ref[k:k + 1, :]
        base = jnp.sum(jnp.where(onehot, run_base, 0.0), axis=0, keepdims=True)
        slot_row = base.astype(I32) + rank_t_ref[k:k + 1, :]
        match = rows == slot_row
        hit = match if hit is None else (hit | match)
    perm = jnp.where(hit, 1.0, 0.0).astype(BF16)
    grouped = jnp.dot(perm, h_ref[...], preferred_element_type=F32)
    _store_rows(stage_ref.at[slot], n_rows, _pack_rows(grouped))

    def copies_of(tile):
        def make_copy(j, first):
            src = src_ref[j] if first else xsrc_ref[j]
            dst = dst_ref[j] if first else xdst_ref[j]
            return pltpu.make_async_copy(stage_ref.at[tile % 2, _tile_rows(src, DISPATCH_CHUNK)],
                                         xs_ref.at[_tile_rows(dst, DISPATCH_CHUNK)], sem)
        return make_copy

    @pl.when(t > 0)
    def _():
        _run_copies(extra_ref, t - 1, copies_of(t - 1), "wait")

    _run_copies(extra_ref, t, copies_of(t), "start")

    @pl.when(t == pl.num_programs(0) - 1)
    def _():
        _run_copies(extra_ref, t, copies_of(t), "wait")


def _dispatch(extra, src, dst, xsrc, xdst, fill, idx_t, rank_t, lo_col, h2, n_slots):
    n_tok, d = h2.shape
    tt = DISPATCH_TILE
    grid_spec = pltpu.PrefetchScalarGridSpec(
        num_scalar_prefetch=6,
        grid=(n_tok // tt,),
        in_specs=[pl.BlockSpec((TOP_K, tt), lambda i, *_: (0, i)),
                  pl.BlockSpec((TOP_K, tt), lambda i, *_: (0, i)),
                  pl.BlockSpec((None, N_EXPERTS, tt // SORT_TILE), lambda i, *_: (i, 0, 0)),
                  pl.BlockSpec((tt, d), lambda i, *_: (i, 0))],
        out_specs=pl.BlockSpec(memory_space=pl.ANY),
        scratch_shapes=[pltpu.VMEM((2, (TOP_K * tt + DISPATCH_CHUNK) * ROW_SUBLANES, LANES), U32),
                        pltpu.SemaphoreType.DMA],
    )
    return pl.pallas_call(
        functools.partial(_dispatch_kernel, tt=tt),
        out_shape=jax.ShapeDtypeStruct((n_slots * ROW_SUBLANES, LANES), U32),
        grid_spec=grid_spec,
        compiler_params=pltpu.CompilerParams(dimension_semantics=("arbitrary",),
                                             vmem_limit_bytes=VMEM_LIMIT_BYTES, has_side_effects=True),
        name="dispatch",
    )(extra, src, dst, xsrc, xdst, fill, idx_t, rank_t, lo_col, h2)


def _experts_kernel(be_ref, nvalid_ref, nused_ref, xs_ref, wgu32_ref, bgu_ref, wd32_ref, bd_ref, yb_ref,
                    wgu_ref, wd_ref, *, d_ff):
    i = pl.program_id(0)

    @pl.when((i == 0) | (be_ref[i] != be_ref[jnp.maximum(i - 1, 0)]))
    def _():
        wgu_ref[...] = wgu32_ref[...].astype(BF16)
        wd_ref[...] = wd32_ref[...].astype(BF16)

    @pl.when(i < nused_ref[0])
    def _():
        m = xs_ref.shape[0] // ROW_SUBLANES
        row = lax.broadcasted_iota(I32, (m, 1), 0)
        xb = _load_rows(xs_ref, m, keep=row < nvalid_ref[i])
        acc = None
        for j in range(d_ff // FF_CHUNK):
            lo, hi = j * FF_CHUNK, (j + 1) * FF_CHUNK
            g = jnp.dot(xb, wgu_ref[:, lo:hi], preferred_element_type=F32) + bgu_ref[:, lo:hi]
            u = jnp.dot(xb, wgu_ref[:, d_ff + lo:d_ff + hi], preferred_element_type=F32) + bgu_ref[:, d_ff + lo:d_ff + hi]
            g = jnp.minimum(g, SWIGLU_LIMIT)
            u = jnp.clip(u, -SWIGLU_LIMIT, SWIGLU_LIMIT)
            act = g * (1.0 / (1.0 + jnp.exp(-SWIGLU_ALPHA * g))) * (u + 1.0)
            part = jnp.dot(act.astype(BF16), wd_ref[lo:hi, :], preferred_element_type=F32)
            acc = part if acc is None else acc + part
        y = (acc + bd_ref[...]).astype(BF16).astype(F32)
        _store_rows(yb_ref, m, _pack_rows(y))

    @pl.when(i >= nused_ref[0])
    def _():
        yb_ref[...] = jnp.zeros_like(yb_ref)


def _experts(block_expert, n_valid, n_used, xs, wgu, bgu, wd, bd, *, layer):
    n_slots = xs.shape[0] // ROW_SUBLANES
    d_ff, d = wd.shape[2], wd.shape[3]
    m = EXPERT_BLOCK
    blk = lambda i, be, nv, nu: (jnp.minimum(i, nu[0] - 1), 0)
    per_expert = lambda i, be, nv, nu: (layer, be[i], 0, 0)
    grid_spec = pltpu.PrefetchScalarGridSpec(
        num_scalar_prefetch=3,
        grid=(n_slots // m,),
        in_specs=[pl.BlockSpec((m * ROW_SUBLANES, LANES), blk),
                  pl.BlockSpec((None, None, d, 2 * d_ff), per_expert),
                  pl.BlockSpec((None, None, 1, 2 * d_ff), per_expert),
                  pl.BlockSpec((None, None, d_ff, d), per_expert),
                  pl.BlockSpec((None, None, 1, d), per_expert)],
        out_specs=pl.BlockSpec((m * ROW_SUBLANES, LANES), lambda i, be, nv, nu: (i, 0)),
        scratch_shapes=[pltpu.VMEM((d, 2 * d_ff), BF16), pltpu.VMEM((d_ff, d), BF16)],
    )
    return pl.pallas_call(
        functools.partial(_experts_kernel, d_ff=d_ff),
        out_shape=jax.ShapeDtypeStruct(xs.shape, U32),
        grid_spec=grid_spec,
        compiler_params=_params("arbitrary"),
        name="experts",
    )(block_expert, n_valid, n_used, xs, wgu, bgu, wd, bd)


def _combine_kernel(extra_ref, src_ref, dst_ref, xsrc_ref, xdst_ref, idx_ref, rank_ref, gates_ref, lo_row_ref,
                    x_ref, gate2_ref, yb_ref, xo_ref, stage_ref, weights_ref, sems, *, tt, n_stage):
    t = pl.program_id(0)
    slot = t % 2

    def copies_of(tile):
        def make_copy(j, first):
            src = src_ref[j] if first else xsrc_ref[j]
            dst = dst_ref[j] if first else xdst_ref[j]
            return pltpu.make_async_copy(yb_ref.at[_tile_rows(src, COMBINE_CHUNK)],
                                         stage_ref.at[tile % 2, _tile_rows(dst, COMBINE_CHUNK)], sems.at[tile % 2])
        return make_copy

    @pl.when(t == 0)
    def _():
        stage_ref[...] = jnp.zeros_like(stage_ref)
        _run_copies(extra_ref, 0, copies_of(0), "start")

    @pl.when(t + 1 < pl.num_programs(0))
    def _():
        _run_copies(extra_ref, t + 1, copies_of(t + 1), "start")

    lane = lax.broadcasted_iota(I32, (tt, N_EXPERTS), 1)
    cols = lax.broadcasted_iota(I32, (tt, n_stage), 1)
    weights = jnp.zeros((tt, n_stage), F32)
    for k in range(TOP_K):
        onehot = lane == idx_ref[:, k:k + 1]
        base = jnp.sum(jnp.where(onehot, lo_row_ref[...], 0.0), axis=1, keepdims=True)
        col = base.astype(I32) + rank_ref[:, k:k + 1]
        weights = jnp.where(cols == col, gates_ref[:, k:k + 1], weights)
    weights_ref[...] = weights.astype(BF16)

    _run_copies(extra_ref, t, copies_of(t), "wait")

    staged = _load_rows(stage_ref.at[slot], n_stage)
    y = jnp.dot(weights_ref[...], staged, preferred_element_type=F32)
    xo_ref[...] = x_ref[...] + gate2_ref[...] * y


def _combine(extra, src, dst, xsrc, xdst, idx, rank, gates, lo_row, x2, gate2, yb, *, seq):
    n_tok, d = x2.shape
    tt = SORT_TILE
    spt = seq // tt
    n_stage = TOP_K * tt + N_EXPERTS * COMBINE_CHUNK
    grid_spec = pltpu.PrefetchScalarGridSpec(
        num_scalar_prefetch=5,
        grid=(n_tok // tt,),
        in_specs=[pl.BlockSpec((tt, TOP_K), lambda i, *_: (i, 0)),
                  pl.BlockSpec((tt, TOP_K), lambda i, *_: (i, 0)),
                  pl.BlockSpec((tt, TOP_K), lambda i, *_: (i, 0)),
                  pl.BlockSpec((None, 1, N_EXPERTS), lambda i, *_: (i, 0, 0)),
                  pl.BlockSpec((tt, d), lambda i, *_: (i, 0)),
                  pl.BlockSpec((None, 1, d), lambda i, *_: (i // spt, 0, 0)),
                  pl.BlockSpec(memory_space=pl.ANY)],
        out_specs=pl.BlockSpec((tt, d), lambda i, *_: (i, 0)),
        scratch_shapes=[pltpu.VMEM((2, n_stage * ROW_SUBLANES, LANES), U32),
                        pltpu.VMEM((tt, n_stage), BF16),
                        pltpu.SemaphoreType.DMA((2,))],
    )
    return pl.pallas_call(
        functools.partial(_combine_kernel, tt=tt, n_stage=n_stage),
        out_shape=jax.ShapeDtypeStruct((n_tok, d), F32),
        grid_spec=grid_spec,
        compiler_params=_params("arbitrary"),
        name="combine",
    )(extra, src, dst, xsrc, xdst, idx, rank, gates, lo_row, x2, gate2, yb)


def _exclusive_cumsum(a, axis):
    return jnp.cumsum(a, axis=axis) - a


def _region_tables(sort_cnt):
    m = EXPERT_BLOCK
    total = jnp.sum(sort_cnt, axis=0)
    padded = (total + max(DISPATCH_CHUNK, COMBINE_CHUNK) + m - 1) // m * m
    pends = jnp.cumsum(padded)
    return total, pends - padded, pends


def _chunk_tables(step_cnt, chunk):
    n_chunks = jnp.maximum((step_cnt + chunk - 1) // chunk, 1)
    lo_chunked = _exclusive_cumsum(n_chunks * chunk, 1)
    n_extra = n_chunks - 1
    ends = jnp.cumsum(n_extra, axis=1)
    p = jnp.arange(MAX_EXTRA_CHUNKS, dtype=I32)
    x_expert = jnp.minimum(jnp.sum(p[None, :, None] >= ends[:, None, :], axis=2), N_EXPERTS - 1)
    x_offset = (p[None, :] - jnp.take_along_axis(ends - n_extra, x_expert, axis=1) + 1) * chunk
    return lo_chunked, ends[:, -1], x_expert, x_offset


def kernel(x, c, positions, ada_w, ada_b, norm1_g, w_in, q_norm_g, k_norm_g, attn_sinks, pool_w, pool_b,
           pool_scale, w_out, norm2_g, router_w, router_b, expert_w_gu, expert_b_gu, expert_w_down,
           expert_b_down):
    batch, seq, d = x.shape
    n_layers = ada_w.shape[0]
    n_tok = batch * seq
    d_ff = expert_w_down.shape[2]
    assert d == SUBLANES * LANES
    assert seq % min(TOKEN_TILE, seq) == 0 and seq % min(ATTN_TILE, seq) == 0 and ATTN_TILE % WINDOW == 0
    assert seq % ROUTE_TILE == 0 and ROUTE_TILE % SORT_TILE == 0 and DISPATCH_TILE % SORT_TILE == 0

    mod = _ada(c, ada_w, ada_b).reshape(n_layers, batch, 6, 1, d)
    rope = _rope_tables(positions)
    sel, unrot = _rope_select_matrix()
    bdq, bdk = _head_mean_matrix(ATTN_WIDTH), _head_mean_matrix(KV_WIDTH)
    tri = jnp.asarray(np.kron(np.eye(ROUTE_TILE // SORT_TILE, dtype=np.float32),
                              np.triu(np.ones((SORT_TILE, SORT_TILE), np.float32), 1)), BF16)

    m = EXPERT_BLOCK
    slack = max(DISPATCH_CHUNK, COMBINE_CHUNK)
    n_blocks = -(-(n_tok * TOP_K + N_EXPERTS * (slack + m - 1)) // m)
    n_slots = n_blocks * m
    n_tiles = n_tok // SORT_TILE

    x2 = x.reshape(n_tok, d)
    for l in range(n_layers):
        shift1, scale1, gate1, shift2, scale2, gate2 = (mod[l, :, j] for j in range(6))
        q, k, v, pool = _inproj(
            x2, shift1, scale1, norm1_g[l].reshape(1, d), w_in[l].astype(BF16),
            jnp.tile(q_norm_g[l], N_Q_HEADS).reshape(1, ATTN_WIDTH),
            jnp.tile(k_norm_g[l], N_KV_HEADS).reshape(1, KV_WIDTH),
            rope, sel, unrot, bdq, bdk, pool_w[l].astype(BF16), pool_b[l].reshape(1, POOL_WIDTH),
            pool_scale[l].reshape(1, POOL_WIDTH), batch=batch, seq=seq)
        attn = _attention(q, k, v, attn_sinks[l], seq=seq)
        w_out_b = w_out[l].astype(BF16)
        x2, h2, idx_t, rank_t, gates_t, tile_cnt = _outproj(
            attn, pool, x2, gate1, shift2, scale2, norm2_g[l].reshape(1, d),
            w_out_b[:ATTN_WIDTH], w_out_b[ATTN_WIDTH:], router_w[l].T.astype(BF16),
            router_b[l].reshape(N_EXPERTS, 1),
            tri, seq=seq)

        sort_cnt = tile_cnt.reshape(n_tiles, N_EXPERTS).astype(I32)
        total, pstart, pends = _region_tables(sort_cnt)
        flat = lambda a: a.reshape(-1).astype(I32)
        n_used = (pends[-1] // m).astype(I32).reshape(1)
        block_row = jnp.minimum(jnp.arange(n_blocks, dtype=I32), n_used[0] - 1) * m
        block_expert = jnp.minimum(jnp.sum(block_row[:, None] >= pends[None, :], axis=1), N_EXPERTS - 1).astype(I32)
        n_valid = jnp.clip((pstart + total)[block_expert] - block_row, 0, m).astype(I32)
        gstart = pstart[None, :] + _exclusive_cumsum(sort_cnt, 0)

        per = DISPATCH_TILE // SORT_TILE
        pair_cnt = sort_cnt.reshape(n_tiles // per, per, N_EXPERTS)
        step_cnt = jnp.sum(pair_cnt, axis=1)
        d_lo = _exclusive_cumsum(step_cnt, 1)
        d_gs = gstart[::per]
        _, d_extra, d_xe, d_xo = _chunk_tables(step_cnt, DISPATCH_CHUNK)
        d_rows = lambda table: flat(jnp.take_along_axis(table, d_xe, axis=1) + d_xo)
        run_base = (d_lo[:, None, :] + _exclusive_cumsum(pair_cnt, 1)).transpose(0, 2, 1).astype(F32)
        fill = jnp.concatenate([pends, pends - pstart - total]).astype(I32)
        xs = _dispatch(flat(d_extra), flat(d_lo), flat(d_gs), d_rows(d_lo), d_rows(d_gs), fill,
                       idx_t, rank_t, run_base, h2, n_slots)
        yb = _experts(block_expert, n_valid, n_used, xs, expert_w_gu,
                      expert_b_gu.reshape(n_layers, N_EXPERTS, 1, 2 * d_ff), expert_w_down,
                      expert_b_down.reshape(n_layers, N_EXPERTS, 1, d), layer=l)
        c_lo, c_extra, c_xe, c_xo = _chunk_tables(sort_cnt, COMBINE_CHUNK)
        c_rows = lambda table: flat(jnp.take_along_axis(table, c_xe, axis=1) + c_xo)
        x2 = _combine(flat(c_extra), flat(gstart), flat(c_lo), c_rows(gstart), c_rows(c_lo),
                      idx_t.T, rank_t.T, gates_t.T, c_lo.astype(F32).reshape(n_tiles, 1, N_EXPERTS), x2, gate2, yb,
                      seq=seq)
    return x2.reshape(batch, seq, d)
```

```python
import functools

import numpy as np
import jax
import jax.numpy as jnp
from jax import lax
from jax.experimental import pallas as pl
from jax.experimental.pallas import tpu as pltpu

F32 = jnp.float32
BF16 = jnp.bfloat16
I32 = jnp.int32
U32 = jnp.uint32

HEAD_DIM = 64
N_Q_HEADS = 8
N_KV_HEADS = 2
Q_PER_KV = N_Q_HEADS // N_KV_HEADS
ATTN_WIDTH = N_Q_HEADS * HEAD_DIM
KV_WIDTH = N_KV_HEADS * HEAD_DIM
WINDOW = 128
ROT_DIM = HEAD_DIM // 4
ROT_HALF = ROT_DIM // 2
ROPE_THETA = 500000.0
POOL_WINDOWS = (2, 4, 8, 16)
N_POOL_GROUPS = len(POOL_WINDOWS)
POOL_GROUP_DIM = 128
POOL_WIDTH = N_POOL_GROUPS * POOL_GROUP_DIM
MAX_POOL_WINDOW = max(POOL_WINDOWS)
N_EXPERTS = 32
TOP_K = 4
SWIGLU_LIMIT = 7.0
SWIGLU_ALPHA = 1.702
EPS = 1e-6

LANES = 128
SUBLANES = 8
VMEM_LIMIT_BYTES = 56 * 1024 * 1024

TOKEN_TILE = 1024
ATTN_TILE = 512
ROUTE_TILE = 512
SORT_TILE = 256
DISPATCH_TILE = SORT_TILE
EXPERT_BLOCK = 512
FF_CHUNK = 512
DISPATCH_CHUNK = 32
COMBINE_CHUNK = 32
COMBINE_SLAB = 512
FILL_CHUNK = 16
MAX_EXTRA_CHUNKS = TOP_K * DISPATCH_TILE // DISPATCH_CHUNK
assert MAX_EXTRA_CHUNKS == TOP_K * SORT_TILE // COMBINE_CHUNK


def _params(*sem):
    return pltpu.CompilerParams(dimension_semantics=sem, vmem_limit_bytes=VMEM_LIMIT_BYTES)


ROW_SUBLANES = 4
HALF_WIDTH = ROW_SUBLANES * LANES
HIGH_HALF_MASK = 0xFFFF0000


def _tile_rows(start, n):
    return pl.ds(pl.multiple_of(start * ROW_SUBLANES, ROW_SUBLANES), n * ROW_SUBLANES)


def _word_chunk(c, n, start=0):
    return pl.ds(start * ROW_SUBLANES + c, n, stride=ROW_SUBLANES)


def _pack_rows(vals):
    lo = lax.bitcast_convert_type(vals[:, :HALF_WIDTH], U32)
    hi = lax.bitcast_convert_type(vals[:, HALF_WIDTH:], U32)
    return lax.shift_right_logical(lo, jnp.uint32(16)) | hi


def _store_rows(ref, n, words):
    for c in range(ROW_SUBLANES):
        ref[_word_chunk(c, n), :] = words[:, c * LANES:(c + 1) * LANES]


def _load_rows(ref, n, keep=None, start=0):
    lo, hi = [], []
    for c in range(ROW_SUBLANES):
        w = ref[_word_chunk(c, n, start), :]
        if keep is not None:
            w = jnp.where(keep, w, jnp.uint32(0))
        lo.append(lax.bitcast_convert_type(lax.shift_left(w, jnp.uint32(16)), F32))
        hi.append(lax.bitcast_convert_type(w & jnp.uint32(HIGH_HALF_MASK), F32))
    return jnp.concatenate(lo + hi, axis=1).astype(BF16)


def _ada_kernel(c_ref, w_ref, b_ref, o_ref):
    c = c_ref[...]
    c_act = c * (1.0 / (1.0 + jnp.exp(-c)))
    o_ref[0] = jnp.dot(c_act, w_ref[0], preferred_element_type=F32,
                       precision=lax.Precision.HIGHEST) + b_ref[0]


def _ada(c, ada_w, ada_b):
    n_layers, d, d6 = ada_w.shape
    b = c.shape[0]
    tn = 1536
    return pl.pallas_call(
        _ada_kernel,
        out_shape=jax.ShapeDtypeStruct((n_layers, b, d6), F32),
        grid=(n_layers, d6 // tn),
        in_specs=[pl.BlockSpec((b, d), lambda l, j: (0, 0)),
                  pl.BlockSpec((1, d, tn), lambda l, j: (l, 0, j)),
                  pl.BlockSpec((1, 1, tn), lambda l, j: (l, 0, j))],
        out_specs=pl.BlockSpec((1, b, tn), lambda l, j: (l, 0, j)),
        compiler_params=_params("arbitrary", "arbitrary"),
        name="ada",
    )(c, ada_w, ada_b.reshape(n_layers, 1, d6))


def _rope_kernel(pos_ref, freq_ref, ch_ref, cl_ref, sh_ref, sl_ref):
    ang = pos_ref[...].astype(F32) * freq_ref[...]
    for fn, hi_ref, lo_ref in ((jnp.cos, ch_ref, cl_ref), (jnp.sin, sh_ref, sl_ref)):
        val = fn(ang)
        hi = val.astype(BF16)
        hi_ref[...] = hi
        lo_ref[...] = (val - hi.astype(F32)).astype(BF16)


def _rope_tables(positions):
    n_tok = positions.size
    inv_freq = ROPE_THETA ** (-jnp.arange(0, ROT_DIM, 2, dtype=F32) / ROT_DIM)
    rows = n_tok * ROT_HALF // LANES
    pos_rep = jnp.repeat(positions.reshape(-1), ROT_HALF).reshape(rows, LANES)
    freq = jnp.tile(inv_freq, LANES // ROT_HALF).reshape(1, LANES)
    tr = min(rows, 512)
    spec = pl.BlockSpec((tr, LANES), lambda i: (i, 0))
    outs = pl.pallas_call(
        _rope_kernel,
        out_shape=[jax.ShapeDtypeStruct((rows, LANES), BF16)] * 4,
        grid=(rows // tr,),
        in_specs=[spec, pl.BlockSpec((1, LANES), lambda i: (0, 0))],
        out_specs=[spec] * 4,
        compiler_params=_params("arbitrary"),
        name="rope",
    )(pos_rep, freq)
    return jnp.concatenate([o.reshape(n_tok, ROT_HALF) for o in outs], axis=1)


def _rope_select_matrix():
    sel = np.zeros((4 * ROT_HALF, 3 * LANES), np.float32)
    for d in range(LANES):
        dd = d % HEAD_DIM
        if dd < ROT_DIM:
            f = dd % ROT_HALF
            sel[f, d] = 1.0
            sel[ROT_HALF + f, d] = 1.0
        if dd < ROT_HALF:
            sel[2 * ROT_HALF + dd, LANES + d] = -1.0
            sel[3 * ROT_HALF + dd, LANES + d] = -1.0
        elif dd < ROT_DIM:
            sel[2 * ROT_HALF + dd - ROT_HALF, 2 * LANES + d] = 1.0
            sel[3 * ROT_HALF + dd - ROT_HALF, 2 * LANES + d] = 1.0
    unrotated = np.array([[1.0 if (d % HEAD_DIM) >= ROT_DIM else 0.0 for d in range(LANES)]], np.float32)
    return jnp.asarray(sel, BF16), jnp.asarray(unrotated, F32)


def _head_mean_matrix(width):
    m = np.zeros((width, width), np.float32)
    for h in range(width // HEAD_DIM):
        m[h * HEAD_DIM:(h + 1) * HEAD_DIM, h * HEAD_DIM:(h + 1) * HEAD_DIM] = 1.0 / HEAD_DIM
    return jnp.asarray(m, BF16)


def _inproj_kernel(x_ref, shift_ref, scale_ref, g1_ref, w_ref, qg_ref, kg_ref, rope_ref, sel_ref,
                   unrot_ref, bdq_ref, bdk_ref, wp_ref, bp_ref, ps_ref,
                   q_ref, k_ref, v_ref, pool_ref, carry_ref, *, ts):
    s_idx = pl.program_id(1)

    @pl.when(s_idx == 0)
    def _():
        carry_ref[...] = jnp.zeros_like(carry_ref)

    x = x_ref[...]
    ms = jnp.mean(x * x, axis=-1, keepdims=True)
    h = x * lax.rsqrt(ms + EPS) * (g1_ref[...] * (1.0 + scale_ref[...])) + shift_ref[...]
    hb = h.astype(BF16)

    tab = jnp.dot(rope_ref[...], sel_ref[...], preferred_element_type=F32)
    cos_t = tab[:, :LANES] + unrot_ref[...]
    sin_a = tab[:, LANES:2 * LANES]
    sin_b = tab[:, 2 * LANES:]

    def norm_rope(t, bd_ref, g_ref, out_scale):
        msq = jnp.dot((t * t).astype(BF16), bd_ref[...], preferred_element_type=F32)
        tn = t * lax.rsqrt(msq + EPS) * g_ref[...]
        chunks = []
        for c in range(t.shape[1] // LANES):
            tc = tn[:, c * LANES:(c + 1) * LANES]
            rot = (tc * cos_t + pltpu.roll(tc, LANES - ROT_HALF, 1) * sin_a
                   + pltpu.roll(tc, ROT_HALF, 1) * sin_b)
            chunks.append(rot * out_scale)
        return chunks[0] if len(chunks) == 1 else jnp.concatenate(chunks, axis=1)

    q = jnp.dot(hb, w_ref[:, :ATTN_WIDTH], preferred_element_type=F32)
    q_ref[...] = norm_rope(q, bdq_ref, qg_ref, HEAD_DIM ** -0.5).astype(BF16)
    k = jnp.dot(hb, w_ref[:, ATTN_WIDTH:ATTN_WIDTH + KV_WIDTH], preferred_element_type=F32)
    k_ref[...] = norm_rope(k, bdk_ref, kg_ref, 1.0).astype(BF16)
    v = jnp.dot(hb, w_ref[:, ATTN_WIDTH + KV_WIDTH:ATTN_WIDTH + 2 * KV_WIDTH], preferred_element_type=F32)
    v_ref[...] = v.astype(BF16)

    u = jnp.dot(hb, w_ref[:, ATTN_WIDTH + 2 * KV_WIDTH:], preferred_element_type=F32)
    ext = jnp.concatenate([carry_ref[...], u], axis=0)
    carry_ref[...] = u[ts - MAX_POOL_WINDOW:, :]
    count = (lax.broadcasted_iota(I32, (ts, POOL_GROUP_DIM), 0) + (s_idx * ts + 1)).astype(F32)
    outs = []
    for g, w in enumerate(POOL_WINDOWS):
        win = ext[:, g * POOL_GROUP_DIM:(g + 1) * POOL_GROUP_DIM]
        span = 1
        while span < w:
            win = win + pltpu.roll(win, span, 0)
            span *= 2
        mean = win[MAX_POOL_WINDOW:, :] * (1.0 / jnp.minimum(count, float(w)))
        pooled = mean - u[:, g * POOL_GROUP_DIM:(g + 1) * POOL_GROUP_DIM]
        outs.append(jnp.dot(pooled.astype(BF16), wp_ref[g], preferred_element_type=F32))
    y = (jnp.concatenate(outs, axis=1) + bp_ref[...]) * ps_ref[...]
    pool_ref[...] = y.astype(BF16)


def _inproj(x2, shift1, scale1, g1, w_in, qg, kg, rope, sel, unrot, bdq, bdk, wp, bp, ps, *, batch, seq):
    n_tok, d = x2.shape
    ts = min(TOKEN_TILE, seq)
    spt = seq // ts
    row = lambda width: pl.BlockSpec((ts, width), lambda b, s: (b * spt + s, 0))
    per_batch = pl.BlockSpec((None, 1, d), lambda b, s: (b, 0, 0))
    full = lambda a: pl.BlockSpec(a.shape, lambda b, s: (0,) * a.ndim)
    return pl.pallas_call(
        functools.partial(_inproj_kernel, ts=ts),
        out_shape=[jax.ShapeDtypeStruct((n_tok, ATTN_WIDTH), BF16),
                   jax.ShapeDtypeStruct((n_tok, KV_WIDTH), BF16),
                   jax.ShapeDtypeStruct((n_tok, KV_WIDTH), BF16),
                   jax.ShapeDtypeStruct((n_tok, POOL_WIDTH), BF16)],
        grid=(batch, spt),
        in_specs=[row(d), per_batch, per_batch, full(g1), full(w_in), full(qg), full(kg),
                  row(rope.shape[1]), full(sel), full(unrot), full(bdq), full(bdk),
                  full(wp), full(bp), full(ps)],
        out_specs=[row(ATTN_WIDTH), row(KV_WIDTH), row(KV_WIDTH), row(POOL_WIDTH)],
        scratch_shapes=[pltpu.VMEM((MAX_POOL_WINDOW, POOL_WIDTH), F32)],
        compiler_params=_params("arbitrary", "arbitrary"),
        name="inproj",
    )(x2, shift1, scale1, g1, w_in, qg, kg, rope, sel, unrot, bdq, bdk, wp, bp, ps)


def _attn_kernel(sink_ref, q_ref, kc_ref, vc_ref, kp_ref, vp_ref, o_ref, *, tq, tiles_per_seq):
    first = (pl.program_id(0) % tiles_per_seq) == 0
    kfull = jnp.concatenate([kp_ref[...], kc_ref[...]], axis=0)
    vfull = jnp.concatenate([vp_ref[...], vc_ref[...]], axis=0)
    rows = Q_PER_KV * WINDOW
    qi = lax.broadcasted_iota(I32, (rows, WINDOW), 0) % WINDOW
    from_prev = lax.broadcasted_iota(I32, (rows, WINDOW), 1) > qi
    row_head = lax.broadcasted_iota(I32, (rows, 1), 0) // WINDOW
    sinks = []
    for h in range(N_KV_HEADS):
        sink = jnp.zeros((rows, 1), F32)
        for g in range(Q_PER_KV):
            sink = jnp.where(row_head == g, sink_ref[h * Q_PER_KV + g], sink)
        sinks.append(sink)

    problems = [(j, h) for j in range(tq // WINDOW) for h in range(N_KV_HEADS)]
    scores, maxes = [], []
    for j, h in problems:
        qs = jnp.concatenate(
            [q_ref[j * WINDOW:(j + 1) * WINDOW, (h * Q_PER_KV + g) * HEAD_DIM:(h * Q_PER_KV + g + 1) * HEAD_DIM]
             for g in range(Q_PER_KV)], axis=0)
        kb = kfull[j * WINDOW:(j + 2) * WINDOW, h * HEAD_DIM:(h + 1) * HEAD_DIM]
        s = lax.dot_general(qs, kb, (((1,), (1,)), ((), ())), preferred_element_type=F32)
        s_prev = s[:, :WINDOW]
        if j == 0:
            s_prev = jnp.where(first, -jnp.inf, s_prev)
        s = jnp.where(from_prev, s_prev, s[:, WINDOW:])
        scores.append(s)
        maxes.append(jnp.maximum(jnp.max(s, axis=-1, keepdims=True), sinks[h]))
    probs, denoms = [], []
    for (j, h), s, m in zip(problems, scores, maxes):
        p = jnp.exp(s - m)
        denoms.append(jnp.sum(p, axis=-1, keepdims=True) + jnp.exp(sinks[h] - m))
        p = p.astype(BF16)
        zero = jnp.zeros_like(p)
        probs.append(jnp.concatenate([jnp.where(from_prev, p, zero), jnp.where(from_prev, zero, p)], axis=1))
    for (j, h), p, denom in zip(problems, probs, denoms):
        vb = vfull[j * WINDOW:(j + 2) * WINDOW, h * HEAD_DIM:(h + 1) * HEAD_DIM]
        o = jnp.dot(p, vb, preferred_element_type=F32) * (1.0 / denom)
        o_ref[j * WINDOW:(j + 1) * WINDOW, h * Q_PER_KV * HEAD_DIM:(h + 1) * Q_PER_KV * HEAD_DIM] = (
            jnp.concatenate([o[g * WINDOW:(g + 1) * WINDOW, :] for g in range(Q_PER_KV)], axis=1).astype(BF16))


def _attention(q, k, v, sinks, *, seq):
    n_tok = q.shape[0]
    tq = min(ATTN_TILE, seq)
    per = tq // WINDOW
    cur = lambda width: pl.BlockSpec((tq, width), lambda i: (i, 0))
    prev = pl.BlockSpec((WINDOW, KV_WIDTH), lambda i: (jnp.maximum(i * per - 1, 0), 0))
    return pl.pallas_call(
        functools.partial(_attn_kernel, tq=tq, tiles_per_seq=seq // tq),
        out_shape=jax.ShapeDtypeStruct((n_tok, ATTN_WIDTH), BF16),
        grid=(n_tok // tq,),
        in_specs=[pl.BlockSpec(memory_space=pltpu.SMEM), cur(ATTN_WIDTH), cur(KV_WIDTH), cur(KV_WIDTH), prev, prev],
        out_specs=cur(ATTN_WIDTH),
        compiler_params=_params("arbitrary"),
        name="attn",
    )(sinks, q, k, v, k, v)


def _outproj_kernel(attn_ref, pool_ref, x_ref, gate1_ref, shift2_ref, scale2_ref, g2_ref, wa_ref, wpo_ref,
                    rw_ref, rb_ref, tri_ref,
                    xo_ref, h2_ref, idx_ref, rank_ref, gates_ref, cnt_ref):
    mixed = (jnp.dot(attn_ref[...], wa_ref[...], preferred_element_type=F32)
             + jnp.dot(pool_ref[...], wpo_ref[...], preferred_element_type=F32))
    x = x_ref[...] + gate1_ref[...] * mixed
    xo_ref[...] = x
    ms = jnp.mean(x * x, axis=-1, keepdims=True)
    h2 = x * lax.rsqrt(ms + EPS) * (g2_ref[...] * (1.0 + scale2_ref[...])) + shift2_ref[...]
    h2b = h2.astype(BF16)
    h2_ref[...] = h2b

    contract_last = (((1,), (1,)), ((), ()))
    logits = lax.dot_general(rw_ref[...], h2b, contract_last, preferred_element_type=F32) + rb_ref[...]
    expert = lax.broadcasted_iota(I32, logits.shape, 0)
    work = logits
    vals, picks, chosen = [], [], []
    for _ in range(TOP_K):
        m = jnp.max(work, axis=0, keepdims=True)
        pick = jnp.min(jnp.where(work == m, expert, N_EXPERTS), axis=0, keepdims=True)
        sel = expert == pick
        vals.append(m)
        picks.append(pick)
        chosen.append(sel)
        work = jnp.where(sel, -jnp.inf, work)
    exps = [jnp.exp(v - vals[0]) for v in vals]
    inv_total = 1.0 / (exps[0] + exps[1] + exps[2] + exps[3])

    multi = jnp.zeros(logits.shape, F32)
    for sel in chosen:
        multi = multi + sel.astype(F32)
    multi_b = multi.astype(BF16)
    before = jnp.dot(multi_b, tri_ref[...], preferred_element_type=F32)
    for kk in range(TOP_K):
        idx_ref[kk:kk + 1, :] = picks[kk]
        gates_ref[kk:kk + 1, :] = exps[kk] * inv_total
        rank_ref[kk:kk + 1, :] = jnp.sum(jnp.where(chosen[kk], before, 0.0), axis=0, keepdims=True).astype(I32)
    ones = jnp.ones((SUBLANES, SORT_TILE), BF16)
    for s in range(multi.shape[1] // SORT_TILE):
        per_expert = lax.dot_general(ones, multi_b[:, s * SORT_TILE:(s + 1) * SORT_TILE], contract_last,
                                     preferred_element_type=F32)
        cnt_ref[s] = per_expert[0:1, :]


def _outproj(attn, pool, x2, gate1, shift2, scale2, g2, wa, wpo, rw, rb, tri, *, seq):
    n_tok, d = x2.shape
    ts = tri.shape[0]
    spt = seq // ts
    row = lambda width: pl.BlockSpec((ts, width), lambda i: (i, 0))
    per_token = pl.BlockSpec((TOP_K, ts), lambda i: (0, i))
    per_batch = pl.BlockSpec((None, 1, d), lambda i: (i // spt, 0, 0))
    full = lambda a: pl.BlockSpec(a.shape, lambda i: (0,) * a.ndim)
    return pl.pallas_call(
        _outproj_kernel,
        out_shape=[jax.ShapeDtypeStruct((n_tok, d), F32),
                   jax.ShapeDtypeStruct((n_tok, d), BF16),
                   jax.ShapeDtypeStruct((TOP_K, n_tok), I32),
                   jax.ShapeDtypeStruct((TOP_K, n_tok), I32),
                   jax.ShapeDtypeStruct((TOP_K, n_tok), F32),
                   jax.ShapeDtypeStruct((n_tok // SORT_TILE, 1, N_EXPERTS), F32)],
        grid=(n_tok // ts,),
        in_specs=[row(ATTN_WIDTH), row(POOL_WIDTH), row(d), per_batch, per_batch, per_batch, full(g2),
                  full(wa), full(wpo), full(rw), full(rb), full(tri)],
        out_specs=[row(d), row(d), per_token, per_token, per_token,
                   pl.BlockSpec((ts // SORT_TILE, 1, N_EXPERTS), lambda i: (i, 0, 0))],
        compiler_params=_params("arbitrary"),
        name="outproj",
    )(attn, pool, x2, gate1, shift2, scale2, g2, wa, wpo, rw, rb, tri)


def _chunk_loops(cnt_ref, tile, chunk, body):
    def per_expert(e, carry):
        n_chunks = (cnt_ref[tile * N_EXPERTS + e] + chunk - 1) // chunk

        def per_chunk(i, c):
            body(e, i)
            return c

        return lax.fori_loop(0, n_chunks, per_chunk, carry)

    lax.fori_loop(0, N_EXPERTS, per_expert, 0)


def _run_copies(extra_ref, tile, make_copy, act):
    for e in range(N_EXPERTS):
        getattr(make_copy(tile * N_EXPERTS + e, True), act)()

    def per_extra(p, carry):
        getattr(make_copy(tile * MAX_EXTRA_CHUNKS + p, False), act)()
        return carry

    lax.fori_loop(0, extra_ref[tile], per_extra, 0)


def _dispatch_kernel(extra_ref, src_ref, dst_ref, xsrc_ref, xdst_ref, fill_ref, idx_t_ref, rank_t_ref, lo_col_ref,
                     h_ref, xs_ref, stage_ref, sem, *, tt):
    t = pl.program_id(0)
    slot = t % 2
    n_rows = TOP_K * tt
    u = FILL_CHUNK
    m = EXPERT_BLOCK

    @pl.when(t == 0)
    def _():
        stage_ref[...] = jnp.zeros_like(stage_ref)

        def pad_copy(e, i):
            return pltpu.make_async_copy(stage_ref.at[1, _tile_rows(0, u)],
                                         xs_ref.at[_tile_rows(fill_ref[e] - (i + 1) * u, u)], sem)

        def tail_copy(i):
            return pltpu.make_async_copy(stage_ref.at[1, _tile_rows(0, m)],
                                         xs_ref.at[_tile_rows(fill_ref[N_EXPERTS - 1] + i * m, m)], sem)

        n_tail = (xs_ref.shape[0] // ROW_SUBLANES - fill_ref[N_EXPERTS - 1]) // m
        for act in ("start", "wait"):
            _chunk_loops(fill_ref, 1, u, lambda e, i: getattr(pad_copy(e, i), act)())
            lax.fori_loop(0, n_tail, lambda i, c: (getattr(tail_copy(i), act)(), c)[1], 0)

    expert = lax.broadcasted_iota(I32, (N_EXPERTS, tt), 0)
    token = lax.broadcasted_iota(I32, (N_EXPERTS, tt), 1)
    rows = lax.broadcasted_iota(I32, (n_rows, tt), 0)
    run_base = jnp.broadcast_to(lo_col_ref[:, 0:1], (N_EXPERTS, tt))
    for s in range(1, tt // SORT_TILE):
        run_base = jnp.where(token >= s * SORT_TILE, lo_col_ref[:, s:s + 1], run_base)
    hit = None
    for k in range(TOP_K):
        onehot = expert == idx_t_ref[k:k + 1, :]
        base = jnp.sum(jnp.where(onehot, run_base, 0.0), axis=0, keepdims=True)
        slot_row = base.astype(I32) + rank_t_ref[k:k + 1, :]
        match = rows == slot_row
        hit = match if hit is None else (hit | match)
    perm = jnp.where(hit, 1.0, 0.0).astype(BF16)
    grouped = jnp.dot(perm, h_ref[...], preferred_element_type=F32)
    _store_rows(stage_ref.at[slot], n_rows, _pack_rows(grouped))

    def copies_of(tile):
        def make_copy(j, first):
            src = src_ref[j] if first else xsrc_ref[j]
            dst = dst_ref[j] if first else xdst_ref[j]
            return pltpu.make_async_copy(stage_ref.at[tile % 2, _tile_rows(src, DISPATCH_CHUNK)],
                                         xs_ref.at[_tile_rows(dst, DISPATCH_CHUNK)], sem)
        return make_copy

    @pl.when(t > 0)
    def _():
        _run_copies(extra_ref, t - 1, copies_of(t - 1), "wait")

    _run_copies(extra_ref, t, copies_of(t), "start")

    @pl.when(t == pl.num_programs(0) - 1)
    def _():
        _run_copies(extra_ref, t, copies_of(t), "wait")


def _dispatch(extra, src, dst, xsrc, xdst, fill, idx_t, rank_t, lo_col, h2, n_slots):
    n_tok, d = h2.shape
    tt = DISPATCH_TILE
    grid_spec = pltpu.PrefetchScalarGridSpec(
        num_scalar_prefetch=6,
        grid=(n_tok // tt,),
        in_specs=[pl.BlockSpec((TOP_K, tt), lambda i, *_: (0, i)),
                  pl.BlockSpec((TOP_K, tt), lambda i, *_: (0, i)),
                  pl.BlockSpec((None, N_EXPERTS, tt // SORT_TILE), lambda i, *_: (i, 0, 0)),
                  pl.BlockSpec((tt, d), lambda i, *_: (i, 0))],
        out_specs=pl.BlockSpec(memory_space=pl.ANY),
        scratch_shapes=[pltpu.VMEM((2, (TOP_K * tt + DISPATCH_CHUNK) * ROW_SUBLANES, LANES), U32),
                        pltpu.SemaphoreType.DMA],
    )
    return pl.pallas_call(
        functools.partial(_dispatch_kernel, tt=tt),
        out_shape=jax.ShapeDtypeStruct((n_slots * ROW_SUBLANES, LANES), U32),
        grid_spec=grid_spec,
        compiler_params=pltpu.CompilerParams(dimension_semantics=("arbitrary",),
                                             vmem_limit_bytes=VMEM_LIMIT_BYTES, has_side_effects=True),
        name="dispatch",
    )(extra, src, dst, xsrc, xdst, fill, idx_t, rank_t, lo_col, h2)


def _experts_kernel(be_ref, nvalid_ref, nused_ref, xs_ref, wgu32_ref, bgu_ref, wd32_ref, bd_ref, yb_ref,
                    wgu_ref, wd_ref, *, d_ff):
    i = pl.program_id(0)

    @pl.when((i == 0) | (be_ref[i] != be_ref[jnp.maximum(i - 1, 0)]))
    def _():
        wgu_ref[...] = wgu32_ref[...].astype(BF16)
        wd_ref[...] = wd32_ref[...].astype(BF16)

    @pl.when(i < nused_ref[0])
    def _():
        m = xs_ref.shape[0] // ROW_SUBLANES
        row = lax.broadcasted_iota(I32, (m, 1), 0)
        xb = _load_rows(xs_ref, m, keep=row < nvalid_ref[i])
        acc = None
        for j in range(d_ff // FF_CHUNK):
            lo, hi = j * FF_CHUNK, (j + 1) * FF_CHUNK
            g = jnp.dot(xb, wgu_ref[:, lo:hi], preferred_element_type=F32) + bgu_ref[:, lo:hi]
            u = jnp.dot(xb, wgu_ref[:, d_ff + lo:d_ff + hi], preferred_element_type=F32) + bgu_ref[:, d_ff + lo:d_ff + hi]
            g = jnp.minimum(g, SWIGLU_LIMIT)
            u = jnp.clip(u, -SWIGLU_LIMIT, SWIGLU_LIMIT)
            act = g * (1.0 / (1.0 + jnp.exp(-SWIGLU_ALPHA * g))) * (u + 1.0)
            part = jnp.dot(act.astype(BF16), wd_ref[lo:hi, :], preferred_element_type=F32)
            acc = part if acc is None else acc + part
        y = (acc + bd_ref[...]).astype(BF16).astype(F32)
        _store_rows(yb_ref, m, _pack_rows(y))

    @pl.when(i >= nused_ref[0])
    def _():
        yb_ref[...] = jnp.zeros_like(yb_ref)


def _experts(block_expert, n_valid, n_used, xs, wgu, bgu, wd, bd, *, layer):
    n_slots = xs.shape[0] // ROW_SUBLANES
    d_ff, d = wd.shape[2], wd.shape[3]
    m = EXPERT_BLOCK
    blk = lambda i, be, nv, nu: (jnp.minimum(i, nu[0] - 1), 0)
    per_expert = lambda i, be, nv, nu: (layer, be[i], 0, 0)
    grid_spec = pltpu.PrefetchScalarGridSpec(
        num_scalar_prefetch=3,
        grid=(n_slots // m,),
        in_specs=[pl.BlockSpec((m * ROW_SUBLANES, LANES), blk),
                  pl.BlockSpec((None, None, d, 2 * d_ff), per_expert),
                  pl.BlockSpec((None, None, 1, 2 * d_ff), per_expert),
                  pl.BlockSpec((None, None, d_ff, d), per_expert),
                  pl.BlockSpec((None, None, 1, d), per_expert)],
        out_specs=pl.BlockSpec((m * ROW_SUBLANES, LANES), lambda i, be, nv, nu: (i, 0)),
        scratch_shapes=[pltpu.VMEM((d, 2 * d_ff), BF16), pltpu.VMEM((d_ff, d), BF16)],
    )
    return pl.pallas_call(
        functools.partial(_experts_kernel, d_ff=d_ff),
        out_shape=jax.ShapeDtypeStruct(xs.shape, U32),
        grid_spec=grid_spec,
        compiler_params=_params("arbitrary"),
        name="experts",
    )(block_expert, n_valid, n_used, xs, wgu, bgu, wd, bd)


def _combine_kernel(extra_ref, src_ref, dst_ref, xsrc_ref, xdst_ref, idx_ref, rank_ref, gates_ref, lo_row_ref,
                    x_ref, gate2_ref, yb_ref, xo_ref, stage_ref, sems, *, tt, n_stage):
    t = pl.program_id(0)
    slot = t % 2

    def copies_of(tile):
        def make_copy(j, first):
            src = src_ref[j] if first else xsrc_ref[j]
            dst = dst_ref[j] if first else xdst_ref[j]
            return pltpu.make_async_copy(yb_ref.at[_tile_rows(src, COMBINE_CHUNK)],
                                         stage_ref.at[tile % 2, _tile_rows(dst, COMBINE_CHUNK)], sems.at[tile % 2])
        return make_copy

    @pl.when(t == 0)
    def _():
        stage_ref[...] = jnp.zeros_like(stage_ref)
        _run_copies(extra_ref, 0, copies_of(0), "start")

    @pl.when(t + 1 < pl.num_programs(0))
    def _():
        _run_copies(extra_ref, t + 1, copies_of(t + 1), "start")

    lane = lax.broadcasted_iota(I32, (tt, N_EXPERTS), 1)
    picked = []
    for k in range(TOP_K):
        onehot = lane == idx_ref[:, k:k + 1]
        base = jnp.sum(jnp.where(onehot, lo_row_ref[...], 0.0), axis=1, keepdims=True)
        picked.append((base.astype(I32) + rank_ref[:, k:k + 1], gates_ref[:, k:k + 1]))

    _run_copies(extra_ref, t, copies_of(t), "wait")

    y = None
    for s in range(n_stage // COMBINE_SLAB):
        cols = lax.broadcasted_iota(I32, (tt, COMBINE_SLAB), 1) + s * COMBINE_SLAB
        weights = jnp.zeros((tt, COMBINE_SLAB), F32)
        for col, gate in picked:
            weights = jnp.where(cols == col, gate, weights)
        staged = _load_rows(stage_ref.at[slot], COMBINE_SLAB, start=s * COMBINE_SLAB)
        part = jnp.dot(weights.astype(BF16), staged, preferred_element_type=F32)
        y = part if y is None else y + part
    xo_ref[...] = x_ref[...] + gate2_ref[...] * y


def _combine(extra, src, dst, xsrc, xdst, idx, rank, gates, lo_row, x2, gate2, yb, *, seq):
    n_tok, d = x2.shape
    tt = SORT_TILE
    spt = seq // tt
    n_stage = TOP_K * tt + N_EXPERTS * COMBINE_CHUNK
    grid_spec = pltpu.PrefetchScalarGridSpec(
        num_scalar_prefetch=5,
        grid=(n_tok // tt,),
        in_specs=[pl.BlockSpec((tt, TOP_K), lambda i, *_: (i, 0)),
                  pl.BlockSpec((tt, TOP_K), lambda i, *_: (i, 0)),
                  pl.BlockSpec((tt, TOP_K), lambda i, *_: (i, 0)),
                  pl.BlockSpec((None, 1, N_EXPERTS), lambda i, *_: (i, 0, 0)),
                  pl.BlockSpec((tt, d), lambda i, *_: (i, 0)),
                  pl.BlockSpec((None, 1, d), lambda i, *_: (i // spt, 0, 0)),
                  pl.BlockSpec(memory_space=pl.ANY)],
        out_specs=pl.BlockSpec((tt, d), lambda i, *_: (i, 0)),
        scratch_shapes=[pltpu.VMEM((2, n_stage * ROW_SUBLANES, LANES), U32),
                        pltpu.SemaphoreType.DMA((2,))],
    )
    return pl.pallas_call(
        functools.partial(_combine_kernel, tt=tt, n_stage=n_stage),
        out_shape=jax.ShapeDtypeStruct((n_tok, d), F32),
        grid_spec=grid_spec,
        compiler_params=_params("arbitrary"),
        name="combine",
    )(extra, src, dst, xsrc, xdst, idx, rank, gates, lo_row, x2, gate2, yb)


def _exclusive_cumsum(a, axis):
    return jnp.cumsum(a, axis=axis) - a


def _region_tables(sort_cnt):
    m = EXPERT_BLOCK
    total = jnp.sum(sort_cnt, axis=0)
    padded = (total + max(DISPATCH_CHUNK, COMBINE_CHUNK) + m - 1) // m * m
    pends = jnp.cumsum(padded)
    return total, pends - padded, pends


def _chunk_tables(step_cnt, chunk):
    n_chunks = jnp.maximum((step_cnt + chunk - 1) // chunk, 1)
    lo_chunked = _exclusive_cumsum(n_chunks * chunk, 1)
    n_extra = n_chunks - 1
    ends = jnp.cumsum(n_extra, axis=1)
    p = jnp.arange(MAX_EXTRA_CHUNKS, dtype=I32)
    x_expert = jnp.minimum(jnp.sum(p[None, :, None] >= ends[:, None, :], axis=2), N_EXPERTS - 1)
    x_offset = (p[None, :] - jnp.take_along_axis(ends - n_extra, x_expert, axis=1) + 1) * chunk
    return lo_chunked, ends[:, -1], x_expert, x_offset


def kernel(x, c, positions, ada_w, ada_b, norm1_g, w_in, q_norm_g, k_norm_g, attn_sinks, pool_w, pool_b,
           pool_scale, w_out, norm2_g, router_w, router_b, expert_w_gu, expert_b_gu, expert_w_down,
           expert_b_down):
    batch, seq, d = x.shape
    n_layers = ada_w.shape[0]
    n_tok = batch * seq
    d_ff = expert_w_down.shape[2]
    assert d == SUBLANES * LANES
    assert seq % min(TOKEN_TILE, seq) == 0 and seq % min(ATTN_TILE, seq) == 0 and ATTN_TILE % WINDOW == 0
    assert seq % ROUTE_TILE == 0 and ROUTE_TILE % SORT_TILE == 0 and DISPATCH_TILE % SORT_TILE == 0

    mod = _ada(c, ada_w, ada_b).reshape(n_layers, batch, 6, 1, d)
    rope = _rope_tables(positions)
    sel, unrot = _rope_select_matrix()
    bdq, bdk = _head_mean_matrix(ATTN_WIDTH), _head_mean_matrix(KV_WIDTH)
    tri = jnp.asarray(np.kron(np.eye(ROUTE_TILE // SORT_TILE, dtype=np.float32),
                              np.triu(np.ones((SORT_TILE, SORT_TILE), np.float32), 1)), BF16)

    m = EXPERT_BLOCK
    slack = max(DISPATCH_CHUNK, COMBINE_CHUNK)
    n_blocks = -(-(n_tok * TOP_K + N_EXPERTS * (slack + m - 1)) // m)
    n_slots = n_blocks * m
    n_tiles = n_tok // SORT_TILE

    x2 = x.reshape(n_tok, d)
    for l in range(n_layers):
        shift1, scale1, gate1, shift2, scale2, gate2 = (mod[l, :, j] for j in range(6))
        q, k, v, pool = _inproj(
            x2, shift1, scale1, norm1_g[l].reshape(1, d), w_in[l].astype(BF16),
            jnp.tile(q_norm_g[l], N_Q_HEADS).reshape(1, ATTN_WIDTH),
            jnp.tile(k_norm_g[l], N_KV_HEADS).reshape(1, KV_WIDTH),
            rope, sel, unrot, bdq, bdk, pool_w[l].astype(BF16), pool_b[l].reshape(1, POOL_WIDTH),
            pool_scale[l].reshape(1, POOL_WIDTH), batch=batch, seq=seq)
        attn = _attention(q, k, v, attn_sinks[l], seq=seq)
        w_out_b = w_out[l].astype(BF16)
        x2, h2, idx_t, rank_t, gates_t, tile_cnt = _outproj(
            attn, pool, x2, gate1, shift2, scale2, norm2_g[l].reshape(1, d),
            w_out_b[:ATTN_WIDTH], w_out_b[ATTN_WIDTH:], router_w[l].T.astype(BF16),
            router_b[l].reshape(N_EXPERTS, 1),
            tri, seq=seq)

        sort_cnt = tile_cnt.reshape(n_tiles, N_EXPERTS).astype(I32)
        total, pstart, pends = _region_tables(sort_cnt)
        flat = lambda a: a.reshape(-1).astype(I32)
        n_used = (pends[-1] // m).astype(I32).reshape(1)
        block_row = jnp.minimum(jnp.arange(n_blocks, dtype=I32), n_used[0] - 1) * m
        block_expert = jnp.minimum(jnp.sum(block_row[:, None] >= pends[None, :], axis=1), N_EXPERTS - 1).astype(I32)
        n_valid = jnp.clip((pstart + total)[block_expert] - block_row, 0, m).astype(I32)
        gstart = pstart[None, :] + _exclusive_cumsum(sort_cnt, 0)

        per = DISPATCH_TILE // SORT_TILE
        pair_cnt = sort_cnt.reshape(n_tiles // per, per, N_EXPERTS)
        step_cnt = jnp.sum(pair_cnt, axis=1)
        d_lo = _exclusive_cumsum(step_cnt, 1)
        d_gs = gstart[::per]
        _, d_extra, d_xe, d_xo = _chunk_tables(step_cnt, DISPATCH_CHUNK)
        d_rows = lambda table: flat(jnp.take_along_axis(table, d_xe, axis=1) + d_xo)
        run_base = (d_lo[:, None, :] + _exclusive_cumsum(pair_cnt, 1)).transpose(0, 2, 1).astype(F32)
        fill = jnp.concatenate([pends, pends - pstart - total]).astype(I32)
        xs = _dispatch(flat(d_extra), flat(d_lo), flat(d_gs), d_rows(d_lo), d_rows(d_gs), fill,
                       idx_t, rank_t, run_base, h2, n_slots)
        yb = _experts(block_expert, n_valid, n_used, xs, expert_w_gu,
                      expert_b_gu.reshape(n_layers, N_EXPERTS, 1, 2 * d_ff), expert_w_down,
                      expert_b_down.reshape(n_layers, N_EXPERTS, 1, d), layer=l)
        c_lo, c_extra, c_xe, c_xo = _chunk_tables(sort_cnt, COMBINE_CHUNK)
        c_rows = lambda table: flat(jnp.take_along_axis(table, c_xe, axis=1) + c_xo)
        x2 = _combine(flat(c_extra), flat(gstart), flat(c_lo), c_rows(gstart), c_rows(c_lo),
                      idx_t.T, rank_t.T, gates_t.T, c_lo.astype(F32).reshape(n_tiles, 1, N_EXPERTS), x2, gate2, yb,
                      seq=seq)
    return x2.reshape(batch, seq, d)
```

```python
import functools

import numpy as np
import jax
import jax.numpy as jnp
from jax import lax
from jax.experimental import pallas as pl
from jax.experimental.pallas import tpu as pltpu

F32 = jnp.float32
BF16 = jnp.bfloat16
I32 = jnp.int32
U32 = jnp.uint32

HEAD_DIM = 64
N_Q_HEADS = 8
N_KV_HEADS = 2
Q_PER_KV = N_Q_HEADS // N_KV_HEADS
ATTN_WIDTH = N_Q_HEADS * HEAD_DIM
KV_WIDTH = N_KV_HEADS * HEAD_DIM
WINDOW = 128
ROT_DIM = HEAD_DIM // 4
ROT_HALF = ROT_DIM // 2
ROPE_THETA = 500000.0
POOL_WINDOWS = (2, 4, 8, 16)
N_POOL_GROUPS = len(POOL_WINDOWS)
POOL_GROUP_DIM = 128
POOL_WIDTH = N_POOL_GROUPS * POOL_GROUP_DIM
MAX_POOL_WINDOW = max(POOL_WINDOWS)
N_EXPERTS = 32
TOP_K = 4
SWIGLU_LIMIT = 7.0
SWIGLU_ALPHA = 1.702
EPS = 1e-6

LANES = 128
SUBLANES = 8
MXU_TILE = 256
VMEM_LIMIT_BYTES = 56 * 1024 * 1024

TOKEN_TILE = 1024
ATTN_TILE = 512
ROUTE_TILE = 512
SORT_TILE = 256
DISPATCH_TILE = 2 * SORT_TILE
EXPERT_BLOCK = 512
FF_CHUNK = 512
DISPATCH_CHUNK = 32
COMBINE_CHUNK = 32
COMBINE_SLAB = 512
FILL_CHUNK = 16


def _max_extra_chunks(step_tokens, chunk):
    return TOP_K * step_tokens // chunk


def _params(*sem):
    return pltpu.CompilerParams(dimension_semantics=sem, vmem_limit_bytes=VMEM_LIMIT_BYTES)


ROW_SUBLANES = 4
HALF_WIDTH = ROW_SUBLANES * LANES
HIGH_HALF_MASK = 0xFFFF0000


def _tile_rows(start, n):
    return pl.ds(pl.multiple_of(start * ROW_SUBLANES, ROW_SUBLANES), n * ROW_SUBLANES)


def _word_chunk(c, n, start=0):
    return pl.ds(start * ROW_SUBLANES + c, n, stride=ROW_SUBLANES)


def _pack_rows(vals):
    lo = lax.bitcast_convert_type(vals[:, :HALF_WIDTH], U32)
    hi = lax.bitcast_convert_type(vals[:, HALF_WIDTH:], U32)
    return lax.shift_right_logical(lo, jnp.uint32(16)) | hi


def _store_rows(ref, n, words):
    for c in range(ROW_SUBLANES):
        ref[_word_chunk(c, n), :] = words[:, c * LANES:(c + 1) * LANES]


def _load_rows(ref, n, keep=None, start=0):
    lo, hi = [], []
    for c in range(ROW_SUBLANES):
        w = ref[_word_chunk(c, n, start), :]
        if keep is not None:
            w = jnp.where(keep, w, jnp.uint32(0))
        lo.append(lax.bitcast_convert_type(lax.shift_left(w, jnp.uint32(16)), F32))
        hi.append(lax.bitcast_convert_type(w & jnp.uint32(HIGH_HALF_MASK), F32))
    return jnp.concatenate(lo + hi, axis=1).astype(BF16)


def _ada_kernel(c_ref, w_ref, b_ref, o_ref):
    c = c_ref[...]
    c_act = c * (1.0 / (1.0 + jnp.exp(-c)))
    o_ref[0] = jnp.dot(c_act, w_ref[0], preferred_element_type=F32,
                       precision=lax.Precision.HIGHEST) + b_ref[0]


def _ada(c, ada_w, ada_b):
    n_layers, d, d6 = ada_w.shape
    b = c.shape[0]
    tn = 1536
    return pl.pallas_call(
        _ada_kernel,
        out_shape=jax.ShapeDtypeStruct((n_layers, b, d6), F32),
        grid=(n_layers, d6 // tn),
        in_specs=[pl.BlockSpec((b, d), lambda l, j: (0, 0)),
                  pl.BlockSpec((1, d, tn), lambda l, j: (l, 0, j)),
                  pl.BlockSpec((1, 1, tn), lambda l, j: (l, 0, j))],
        out_specs=pl.BlockSpec((1, b, tn), lambda l, j: (l, 0, j)),
        compiler_params=_params("arbitrary", "arbitrary"),
        name="ada",
    )(c, ada_w, ada_b.reshape(n_layers, 1, d6))


def _rope_kernel(pos_ref, freq_ref, ch_ref, cl_ref, sh_ref, sl_ref):
    ang = pos_ref[...].astype(F32) * freq_ref[...]
    for fn, hi_ref, lo_ref in ((jnp.cos, ch_ref, cl_ref), (jnp.sin, sh_ref, sl_ref)):
        val = fn(ang)
        hi = val.astype(BF16)
        hi_ref[...] = hi
        lo_ref[...] = (val - hi.astype(F32)).astype(BF16)


def _rope_tables(positions):
    n_tok = positions.size
    inv_freq = ROPE_THETA ** (-jnp.arange(0, ROT_DIM, 2, dtype=F32) / ROT_DIM)
    rows = n_tok * ROT_HALF // LANES
    pos_rep = jnp.repeat(positions.reshape(-1), ROT_HALF).reshape(rows, LANES)
    freq = jnp.tile(inv_freq, LANES // ROT_HALF).reshape(1, LANES)
    tr = min(rows, 512)
    spec = pl.BlockSpec((tr, LANES), lambda i: (i, 0))
    outs = pl.pallas_call(
        _rope_kernel,
        out_shape=[jax.ShapeDtypeStruct((rows, LANES), BF16)] * 4,
        grid=(rows // tr,),
        in_specs=[spec, pl.BlockSpec((1, LANES), lambda i: (0, 0))],
        out_specs=[spec] * 4,
        compiler_params=_params("arbitrary"),
        name="rope",
    )(pos_rep, freq)
    return jnp.concatenate([o.reshape(n_tok, ROT_HALF) for o in outs], axis=1)


def _rope_select_matrix():
    sel = np.zeros((4 * ROT_HALF, 3 * LANES), np.float32)
    for d in range(LANES):
        dd = d % HEAD_DIM
        if dd < ROT_DIM:
            f = dd % ROT_HALF
            sel[f, d] = 1.0
            sel[ROT_HALF + f, d] = 1.0
        if dd < ROT_HALF:
            sel[2 * ROT_HALF + dd, LANES + d] = -1.0
            sel[3 * ROT_HALF + dd, LANES + d] = -1.0
        elif dd < ROT_DIM:
            sel[2 * ROT_HALF + dd - ROT_HALF, 2 * LANES + d] = 1.0
            sel[3 * ROT_HALF + dd - ROT_HALF, 2 * LANES + d] = 1.0
    unrotated = np.array([[1.0 if (d % HEAD_DIM) >= ROT_DIM else 0.0 for d in range(LANES)]], np.float32)
    return jnp.asarray(sel, BF16), jnp.asarray(unrotated, F32)


def _pair_block_diag(w):
    g, c, _ = w.shape
    zero = jnp.zeros((g // 2, c, c), w.dtype)
    top = jnp.concatenate([w[0::2], zero], axis=2)
    bottom = jnp.concatenate([zero, w[1::2]], axis=2)
    return jnp.concatenate([top, bottom], axis=1)


def _head_mean_matrix(width):
    m = np.zeros((width, width), np.float32)
    for h in range(width // HEAD_DIM):
        m[h * HEAD_DIM:(h + 1) * HEAD_DIM, h * HEAD_DIM:(h + 1) * HEAD_DIM] = 1.0 / HEAD_DIM
    return jnp.asarray(m, BF16)


def _inproj_kernel(x_ref, shift_ref, scale_ref, g1_ref, w_ref, qg_ref, kg_ref, rope_ref, sel_ref,
                   unrot_ref, bdq_ref, bdk_ref, wp_ref, bp_ref, ps_ref,
                   q_ref, k_ref, v_ref, pool_ref, carry_ref, *, ts):
    s_idx = pl.program_id(1)

    @pl.when(s_idx == 0)
    def _():
        carry_ref[...] = jnp.zeros_like(carry_ref)

    x = x_ref[...]
    ms = jnp.mean(x * x, axis=-1, keepdims=True)
    h = x * lax.rsqrt(ms + EPS) * (g1_ref[...] * (1.0 + scale_ref[...])) + shift_ref[...]
    hb = h.astype(BF16)

    tab = jnp.dot(rope_ref[...], sel_ref[...], preferred_element_type=F32)
    cos_t = tab[:, :LANES] + unrot_ref[...]
    sin_a = tab[:, LANES:2 * LANES]
    sin_b = tab[:, 2 * LANES:]

    def norm_rope(t, bd_ref, g_ref, out_scale):
        bw = bd_ref.shape[0]
        sq = (t * t).astype(BF16)
        msq = [jnp.dot(sq[:, i * bw:(i + 1) * bw], bd_ref[...], preferred_element_type=F32)
               for i in range(t.shape[1] // bw)]
        msq = msq[0] if len(msq) == 1 else jnp.concatenate(msq, axis=1)
        tn = t * lax.rsqrt(msq + EPS) * g_ref[...]
        chunks = []
        for c in range(t.shape[1] // LANES):
            tc = tn[:, c * LANES:(c + 1) * LANES]
            rot = (tc * cos_t + pltpu.roll(tc, LANES - ROT_HALF, 1) * sin_a
                   + pltpu.roll(tc, ROT_HALF, 1) * sin_b)
            chunks.append(rot * out_scale)
        return chunks[0] if len(chunks) == 1 else jnp.concatenate(chunks, axis=1)

    q = jnp.dot(hb, w_ref[:, :ATTN_WIDTH], preferred_element_type=F32)
    q_ref[...] = norm_rope(q, bdq_ref, qg_ref, HEAD_DIM ** -0.5).astype(BF16)
    k = jnp.dot(hb, w_ref[:, ATTN_WIDTH:ATTN_WIDTH + KV_WIDTH], preferred_element_type=F32)
    k_ref[...] = norm_rope(k, bdk_ref, kg_ref, 1.0).astype(BF16)
    v = jnp.dot(hb, w_ref[:, ATTN_WIDTH + KV_WIDTH:ATTN_WIDTH + 2 * KV_WIDTH], preferred_element_type=F32)
    v_ref[...] = v.astype(BF16)

    u = jnp.dot(hb, w_ref[:, ATTN_WIDTH + 2 * KV_WIDTH:], preferred_element_type=F32)
    ext = jnp.concatenate([carry_ref[...], u], axis=0)
    carry_ref[...] = u[ts - MAX_POOL_WINDOW:, :]
    count = (lax.broadcasted_iota(I32, (ts, POOL_GROUP_DIM), 0) + (s_idx * ts + 1)).astype(F32)
    outs = []
    for g, w in enumerate(POOL_WINDOWS):
        win = ext[:, g * POOL_GROUP_DIM:(g + 1) * POOL_GROUP_DIM]
        span = 1
        while span < w:
            win = win + pltpu.roll(win, span, 0)
            span *= 2
        mean = win[MAX_POOL_WINDOW:, :] * (1.0 / jnp.minimum(count, float(w)))
        outs.append((mean - u[:, g * POOL_GROUP_DIM:(g + 1) * POOL_GROUP_DIM]).astype(BF16))
    pooled = jnp.concatenate(outs, axis=1)
    pw = wp_ref.shape[1]
    mapped = [jnp.dot(pooled[:, i * pw:(i + 1) * pw], wp_ref[i], preferred_element_type=F32)
              for i in range(wp_ref.shape[0])]
    y = (jnp.concatenate(mapped, axis=1) + bp_ref[...]) * ps_ref[...]
    pool_ref[...] = y.astype(BF16)


def _inproj(x2, shift1, scale1, g1, w_in, qg, kg, rope, sel, unrot, bdq, bdk, wp, bp, ps, *, batch, seq):
    n_tok, d = x2.shape
    ts = min(TOKEN_TILE, seq)
    spt = seq // ts
    row = lambda width: pl.BlockSpec((ts, width), lambda b, s: (b * spt + s, 0))
    per_batch = pl.BlockSpec((None, 1, d), lambda b, s: (b, 0, 0))
    full = lambda a: pl.BlockSpec(a.shape, lambda b, s: (0,) * a.ndim)
    return pl.pallas_call(
        functools.partial(_inproj_kernel, ts=ts),
        out_shape=[jax.ShapeDtypeStruct((n_tok, ATTN_WIDTH), BF16),
                   jax.ShapeDtypeStruct((n_tok, KV_WIDTH), BF16),
                   jax.ShapeDtypeStruct((n_tok, KV_WIDTH), BF16),
                   jax.ShapeDtypeStruct((n_tok, POOL_WIDTH), BF16)],
        grid=(batch, spt),
        in_specs=[row(d), per_batch, per_batch, full(g1), full(w_in), full(qg), full(kg),
                  row(rope.shape[1]), full(sel), full(unrot), full(bdq), full(bdk),
                  full(wp), full(bp), full(ps)],
        out_specs=[row(ATTN_WIDTH), row(KV_WIDTH), row(KV_WIDTH), row(POOL_WIDTH)],
        scratch_shapes=[pltpu.VMEM((MAX_POOL_WINDOW, POOL_WIDTH), F32)],
        compiler_params=_params("arbitrary", "arbitrary"),
        name="inproj",
    )(x2, shift1, scale1, g1, w_in, qg, kg, rope, sel, unrot, bdq, bdk, wp, bp, ps)


def _attn_kernel(sink_ref, q_ref, kc_ref, vc_ref, kp_ref, vp_ref, o_ref, *, tq, tiles_per_seq):
    first = (pl.program_id(0) % tiles_per_seq) == 0
    kfull = jnp.concatenate([kp_ref[...], kc_ref[...]], axis=0)
    vfull = jnp.concatenate([vp_ref[...], vc_ref[...]], axis=0)
    rows = Q_PER_KV * WINDOW
    qi = lax.broadcasted_iota(I32, (rows, WINDOW), 0) % WINDOW
    from_prev = lax.broadcasted_iota(I32, (rows, WINDOW), 1) > qi
    row_head = lax.broadcasted_iota(I32, (rows, 1), 0) // WINDOW
    sinks = []
    for h in range(N_KV_HEADS):
        sink = jnp.zeros((rows, 1), F32)
        for g in range(Q_PER_KV):
            sink = jnp.where(row_head == g, sink_ref[h * Q_PER_KV + g], sink)
        sinks.append(sink)

    problems = [(j, h) for j in range(tq // WINDOW) for h in range(N_KV_HEADS)]
    scores, maxes = [], []
    for j, h in problems:
        qs = jnp.concatenate(
            [q_ref[j * WINDOW:(j + 1) * WINDOW, (h * Q_PER_KV + g) * HEAD_DIM:(h * Q_PER_KV + g + 1) * HEAD_DIM]
             for g in range(Q_PER_KV)], axis=0)
        kb = kfull[j * WINDOW:(j + 2) * WINDOW, h * HEAD_DIM:(h + 1) * HEAD_DIM]
        s = lax.dot_general(qs, kb, (((1,), (1,)), ((), ())), preferred_element_type=F32)
        s_prev = s[:, :WINDOW]
        if j == 0:
            s_prev = jnp.where(first, -jnp.inf, s_prev)
        s = jnp.where(from_prev, s_prev, s[:, WINDOW:])
        scores.append(s)
        maxes.append(jnp.maximum(jnp.max(s, axis=-1, keepdims=True), sinks[h]))
    probs, denoms = [], []
    for (j, h), s, m in zip(problems, scores, maxes):
        p = jnp.exp(s - m)
        denoms.append(jnp.sum(p, axis=-1, keepdims=True) + jnp.exp(sinks[h] - m))
        p = p.astype(BF16)
        zero = jnp.zeros_like(p)
        probs.append(jnp.concatenate([jnp.where(from_prev, p, zero), jnp.where(from_prev, zero, p)], axis=1))
    for (j, h), p, denom in zip(problems, probs, denoms):
        vb = vfull[j * WINDOW:(j + 2) * WINDOW, h * HEAD_DIM:(h + 1) * HEAD_DIM]
        o = jnp.dot(p, vb, preferred_element_type=F32) * (1.0 / denom)
        o_ref[j * WINDOW:(j + 1) * WINDOW, h * Q_PER_KV * HEAD_DIM:(h + 1) * Q_PER_KV * HEAD_DIM] = (
            jnp.concatenate([o[g * WINDOW:(g + 1) * WINDOW, :] for g in range(Q_PER_KV)], axis=1).astype(BF16))


def _attention(q, k, v, sinks, *, seq):
    n_tok = q.shape[0]
    tq = min(ATTN_TILE, seq)
    per = tq // WINDOW
    cur = lambda width: pl.BlockSpec((tq, width), lambda i: (i, 0))
    prev = pl.BlockSpec((WINDOW, KV_WIDTH), lambda i: (jnp.maximum(i * per - 1, 0), 0))
    return pl.pallas_call(
        functools.partial(_attn_kernel, tq=tq, tiles_per_seq=seq // tq),
        out_shape=jax.ShapeDtypeStruct((n_tok, ATTN_WIDTH), BF16),
        grid=(n_tok // tq,),
        in_specs=[pl.BlockSpec(memory_space=pltpu.SMEM), cur(ATTN_WIDTH), cur(KV_WIDTH), cur(KV_WIDTH), prev, prev],
        out_specs=cur(ATTN_WIDTH),
        compiler_params=_params("arbitrary"),
        name="attn",
    )(sinks, q, k, v, k, v)


def _outproj_kernel(attn_ref, pool_ref, x_ref, gate1_ref, shift2_ref, scale2_ref, g2_ref, wa_ref, wpo_ref,
                    rw_ref, rb_ref, tri_ref,
                    xo_ref, h2_ref, idx_ref, rank_ref, gates_ref, cnt_ref):
    contract_last = (((1,), (1,)), ((), ()))
    slabs = [pl.ds(r, SORT_TILE) for r in range(0, x_ref.shape[0], SORT_TILE)]
    mixed = [jnp.dot(attn_ref[rows, :], wa_ref[...], preferred_element_type=F32)
             + jnp.dot(pool_ref[rows, :], wpo_ref[...], preferred_element_type=F32) for rows in slabs]
    xs = [x_ref[rows, :] + gate1_ref[...] * m for rows, m in zip(slabs, mixed)]
    for rows, x in zip(slabs, xs):
        xo_ref[rows, :] = x
    norm_gain = g2_ref[...] * (1.0 + scale2_ref[...])
    h2bs = []
    for rows, x in zip(slabs, xs):
        ms = jnp.mean(x * x, axis=-1, keepdims=True)
        h2b = (x * lax.rsqrt(ms + EPS) * norm_gain + shift2_ref[...]).astype(BF16)
        h2_ref[rows, :] = h2b
        h2bs.append(h2b)

    logits = jnp.concatenate([lax.dot_general(rw_ref[...], h2b, contract_last, preferred_element_type=F32)
                              for h2b in h2bs], axis=1) + rb_ref[...]
    expert = lax.broadcasted_iota(I32, logits.shape, 0)
    work = logits
    vals, picks, chosen = [], [], []
    for _ in range(TOP_K):
        m = jnp.max(work, axis=0, keepdims=True)
        pick = jnp.min(jnp.where(work == m, expert, N_EXPERTS), axis=0, keepdims=True)
        sel = expert == pick
        vals.append(m)
        picks.append(pick)
        chosen.append(sel)
        work = jnp.where(sel, -jnp.inf, work)
    exps = [jnp.exp(v - vals[0]) for v in vals]
    inv_total = 1.0 / (exps[0] + exps[1] + exps[2] + exps[3])

    multi = jnp.zeros(logits.shape, F32)
    for sel in chosen:
        multi = multi + sel.astype(F32)
    multi_b = multi.astype(BF16)
    before = jnp.dot(multi_b, tri_ref[...], preferred_element_type=F32)
    for kk in range(TOP_K):
        idx_ref[kk:kk + 1, :] = picks[kk]
        gates_ref[kk:kk + 1, :] = exps[kk] * inv_total
        rank_ref[kk:kk + 1, :] = jnp.sum(jnp.where(chosen[kk], before, 0.0), axis=0, keepdims=True).astype(I32)
    ones = jnp.ones((SUBLANES, SORT_TILE), BF16)
    for s in range(multi.shape[1] // SORT_TILE):
        per_expert = lax.dot_general(ones, multi_b[:, s * SORT_TILE:(s + 1) * SORT_TILE], contract_last,
                                     preferred_element_type=F32)
        cnt_ref[s] = per_expert[0:1, :]


def _outproj(attn, pool, x2, gate1, shift2, scale2, g2, wa, wpo, rw, rb, tri, *, seq):
    n_tok, d = x2.shape
    ts = tri.shape[0]
    spt = seq // ts
    row = lambda width: pl.BlockSpec((ts, width), lambda i: (i, 0))
    per_token = pl.BlockSpec((TOP_K, ts), lambda i: (0, i))
    per_batch = pl.BlockSpec((None, 1, d), lambda i: (i // spt, 0, 0))
    full = lambda a: pl.BlockSpec(a.shape, lambda i: (0,) * a.ndim)
    return pl.pallas_call(
        _outproj_kernel,
        out_shape=[jax.ShapeDtypeStruct((n_tok, d), F32),
                   jax.ShapeDtypeStruct((n_tok, d), BF16),
                   jax.ShapeDtypeStruct((TOP_K, n_tok), I32),
                   jax.ShapeDtypeStruct((TOP_K, n_tok), I32),
                   jax.ShapeDtypeStruct((TOP_K, n_tok), F32),
                   jax.ShapeDtypeStruct((n_tok // SORT_TILE, 1, N_EXPERTS), F32)],
        grid=(n_tok // ts,),
        in_specs=[row(ATTN_WIDTH), row(POOL_WIDTH), row(d), per_batch, per_batch, per_batch, full(g2),
                  full(wa), full(wpo), full(rw), full(rb), full(tri)],
        out_specs=[row(d), row(d), per_token, per_token, per_token,
                   pl.BlockSpec((ts // SORT_TILE, 1, N_EXPERTS), lambda i: (i, 0, 0))],
        compiler_params=_params("arbitrary"),
        name="outproj",
    )(attn, pool, x2, gate1, shift2, scale2, g2, wa, wpo, rw, rb, tri)


def _chunk_loops(cnt_ref, tile, chunk, body):
    def per_expert(e, carry):
        n_chunks = (cnt_ref[tile * N_EXPERTS + e] + chunk - 1) // chunk

        def per_chunk(i, c):
            body(e, i)
            return c

        return lax.fori_loop(0, n_chunks, per_chunk, carry)

    lax.fori_loop(0, N_EXPERTS, per_expert, 0)


def _run_copies(extra_ref, tile, make_copy, act):
    for e in range(N_EXPERTS):
        getattr(make_copy(tile * N_EXPERTS + e, True), act)()

    def per_extra(p, carry):
        getattr(make_copy(tile * make_copy.max_extra + p, False), act)()
        return carry

    lax.fori_loop(0, extra_ref[tile], per_extra, 0)


def _dispatch_kernel(extra_ref, src_ref, dst_ref, xsrc_ref, xdst_ref, fill_ref, idx_t_ref, rank_t_ref, lo_col_ref,
                     h_ref, xs_ref, stage_ref, sem, *, tt):
    t = pl.program_id(0)
    slot = t % 2
    n_rows = TOP_K * tt
    u = FILL_CHUNK
    m = EXPERT_BLOCK

    @pl.when(t == 0)
    def _():
        stage_ref[...] = jnp.zeros_like(stage_ref)

        def pad_copy(e, i):
            return pltpu.make_async_copy(stage_ref.at[1, _tile_rows(0, u)],
                                         xs_ref.at[_tile_rows(fill_ref[e] - (i + 1) * u, u)], sem)

        def tail_copy(i):
            return pltpu.make_async_copy(stage_ref.at[1, _tile_rows(0, m)],
                                         xs_ref.at[_tile_rows(fill_ref[N_EXPERTS - 1] + i * m, m)], sem)

        n_tail = (xs_ref.shape[0] // ROW_SUBLANES - fill_ref[N_EXPERTS - 1]) // m
        for act in ("start", "wait"):
            _chunk_loops(fill_ref, 1, u, lambda e, i: getattr(pad_copy(e, i), act)())
            lax.fori_loop(0, n_tail, lambda i, c: (getattr(tail_copy(i), act)(), c)[1], 0)

    expert = lax.broadcasted_iota(I32, (N_EXPERTS, tt), 0)
    token = lax.broadcasted_iota(I32, (N_EXPERTS, tt), 1)
    rows = lax.broadcasted_iota(I32, (n_rows, tt), 0)
    run_base = jnp.broadcast_to(lo_col_ref[:, 0:1], (N_EXPERTS, tt))
    for s in range(1, tt // SORT_TILE):
        run_base = jnp.where(token >= s * SORT_TILE, lo_col_ref[:, s:s + 1], run_base)
    hit = None
    for k in range(TOP_K):
        onehot = expert == idx_t_ref[k:k + 1, :]
        base = jnp.sum(jnp.where(onehot, run_base, 0.0), axis=0, keepdims=True)
        slot_row = base.astype(I32) + rank_t_ref[k:k + 1, :]
        match = rows == slot_row
        hit = match if hit is None else (hit | match)
    perm = jnp.where(hit, 1.0, 0.0).astype(BF16)
    grouped = jnp.dot(perm, h_ref[...], preferred_element_type=F32)
    _store_rows(stage_ref.at[slot], n_rows, _pack_rows(grouped))

    def copies_of(tile):
        def make_copy(j, first):
            src = src_ref[j] if first else xsrc_ref[j]
            dst = dst_ref[j] if first else xdst_ref[j]
            return pltpu.make_async_copy(stage_ref.at[tile % 2, _tile_rows(src, DISPATCH_CHUNK)],
                                         xs_ref.at[_tile_rows(dst, DISPATCH_CHUNK)], sem)
        make_copy.max_extra = _max_extra_chunks(tt, DISPATCH_CHUNK)
        return make_copy

    @pl.when(t > 0)
    def _():
        _run_copies(extra_ref, t - 1, copies_of(t - 1), "wait")

    _run_copies(extra_ref, t, copies_of(t), "start")

    @pl.when(t == pl.num_programs(0) - 1)
    def _():
        _run_copies(extra_ref, t, copies_of(t), "wait")


def _dispatch(extra, src, dst, xsrc, xdst, fill, idx_t, rank_t, lo_col, h2, n_slots):
    n_tok, d = h2.shape
    tt = DISPATCH_TILE
    grid_spec = pltpu.PrefetchScalarGridSpec(
        num_scalar_prefetch=6,
        grid=(n_tok // tt,),
        in_specs=[pl.BlockSpec((TOP_K, tt), lambda i, *_: (0, i)),
                  pl.BlockSpec((TOP_K, tt), lambda i, *_: (0, i)),
                  pl.BlockSpec((None, N_EXPERTS, tt // SORT_TILE), lambda i, *_: (i, 0, 0)),
                  pl.BlockSpec((tt, d), lambda i, *_: (i, 0))],
        out_specs=pl.BlockSpec(memory_space=pl.ANY),
        scratch_shapes=[pltpu.VMEM((2, (TOP_K * tt + DISPATCH_CHUNK) * ROW_SUBLANES, LANES), U32),
                        pltpu.SemaphoreType.DMA],
    )
    return pl.pallas_call(
        functools.partial(_dispatch_kernel, tt=tt),
        out_shape=jax.ShapeDtypeStruct((n_slots * ROW_SUBLANES, LANES), U32),
        grid_spec=grid_spec,
        compiler_params=pltpu.CompilerParams(dimension_semantics=("arbitrary",),
                                             vmem_limit_bytes=VMEM_LIMIT_BYTES, has_side_effects=True),
        name="dispatch",
    )(extra, src, dst, xsrc, xdst, fill, idx_t, rank_t, lo_col, h2)


def _experts_kernel(be_ref, nvalid_ref, nused_ref, xs_ref, wgu32_ref, bgu_ref, wd32_ref, bd_ref, yb_ref,
                    wgu_ref, wd_ref, *, d_ff):
    i = pl.program_id(0)

    @pl.when((i == 0) | (be_ref[i] != be_ref[jnp.maximum(i - 1, 0)]))
    def _():
        wgu_ref[...] = wgu32_ref[...].astype(BF16)
        wd_ref[...] = wd32_ref[...].astype(BF16)

    @pl.when(i < nused_ref[0])
    def _():
        m = xs_ref.shape[0] // ROW_SUBLANES
        row = lax.broadcasted_iota(I32, (m, 1), 0)
        xb = _load_rows(xs_ref, m, keep=row < nvalid_ref[i])
        acc = None
        for j in range(d_ff // FF_CHUNK):
            lo, hi = j * FF_CHUNK, (j + 1) * FF_CHUNK
            g = jnp.dot(xb, wgu_ref[:, lo:hi], preferred_element_type=F32) + bgu_ref[:, lo:hi]
            u = jnp.dot(xb, wgu_ref[:, d_ff + lo:d_ff + hi], preferred_element_type=F32) + bgu_ref[:, d_ff + lo:d_ff + hi]
            g = jnp.minimum(g, SWIGLU_LIMIT)
            u = jnp.clip(u, -SWIGLU_LIMIT, SWIGLU_LIMIT)
            act = g * (1.0 / (1.0 + jnp.exp(-SWIGLU_ALPHA * g))) * (u + 1.0)
            part = jnp.dot(act.astype(BF16), wd_ref[lo:hi, :], preferred_element_type=F32)
            acc = part if acc is None else acc + part
        y = (acc + bd_ref[...]).astype(BF16).astype(F32)
        _store_rows(yb_ref, m, _pack_rows(y))

    @pl.when(i >= nused_ref[0])
    def _():
        yb_ref[...] = jnp.zeros_like(yb_ref)


def _experts(block_expert, n_valid, n_used, xs, wgu, bgu, wd, bd, *, layer):
    n_slots = xs.shape[0] // ROW_SUBLANES
    d_ff, d = wd.shape[2], wd.shape[3]
    m = EXPERT_BLOCK
    blk = lambda i, be, nv, nu: (jnp.minimum(i, nu[0] - 1), 0)
    per_expert = lambda i, be, nv, nu: (layer, be[i], 0, 0)
    grid_spec = pltpu.PrefetchScalarGridSpec(
        num_scalar_prefetch=3,
        grid=(n_slots // m,),
        in_specs=[pl.BlockSpec((m * ROW_SUBLANES, LANES), blk),
                  pl.BlockSpec((None, None, d, 2 * d_ff), per_expert),
                  pl.BlockSpec((None, None, 1, 2 * d_ff), per_expert),
                  pl.BlockSpec((None, None, d_ff, d), per_expert),
                  pl.BlockSpec((None, None, 1, d), per_expert)],
        out_specs=pl.BlockSpec((m * ROW_SUBLANES, LANES), lambda i, be, nv, nu: (i, 0)),
        scratch_shapes=[pltpu.VMEM((d, 2 * d_ff), BF16), pltpu.VMEM((d_ff, d), BF16)],
    )
    return pl.pallas_call(
        functools.partial(_experts_kernel, d_ff=d_ff),
        out_shape=jax.ShapeDtypeStruct(xs.shape, U32),
        grid_spec=grid_spec,
        compiler_params=_params("arbitrary"),
        name="experts",
    )(block_expert, n_valid, n_used, xs, wgu, bgu, wd, bd)


def _combine_kernel(extra_ref, src_ref, dst_ref, xsrc_ref, xdst_ref, idx_ref, rank_ref, gates_ref, lo_row_ref,
                    x_ref, gate2_ref, yb_ref, xo_ref, stage_ref, sems, *, tt, n_stage):
    t = pl.program_id(0)
    slot = t % 2

    def copies_of(tile):
        def make_copy(j, first):
            src = src_ref[j] if first else xsrc_ref[j]
            dst = dst_ref[j] if first else xdst_ref[j]
            return pltpu.make_async_copy(yb_ref.at[_tile_rows(src, COMBINE_CHUNK)],
                                         stage_ref.at[tile % 2, _tile_rows(dst, COMBINE_CHUNK)], sems.at[tile % 2])
        make_copy.max_extra = _max_extra_chunks(tt, COMBINE_CHUNK)
        return make_copy

    @pl.when(t == 0)
    def _():
        stage_ref[...] = jnp.zeros_like(stage_ref)
        _run_copies(extra_ref, 0, copies_of(0), "start")

    @pl.when(t + 1 < pl.num_programs(0))
    def _():
        _run_copies(extra_ref, t + 1, copies_of(t + 1), "start")

    lane = lax.broadcasted_iota(I32, (tt, N_EXPERTS), 1)
    picked = []
    for k in range(TOP_K):
        onehot = lane == idx_ref[:, k:k + 1]
        base = jnp.sum(jnp.where(onehot, lo_row_ref[...], 0.0), axis=1, keepdims=True)
        picked.append((base.astype(I32) + rank_ref[:, k:k + 1], gates_ref[:, k:k + 1]))

    _run_copies(extra_ref, t, copies_of(t), "wait")

    y = None
    for s in range(n_stage // COMBINE_SLAB):
        cols = lax.broadcasted_iota(I32, (tt, COMBINE_SLAB), 1) + s * COMBINE_SLAB
        weights = jnp.zeros((tt, COMBINE_SLAB), F32)
        for col, gate in picked:
            weights = jnp.where(cols == col, gate, weights)
        staged = _load_rows(stage_ref.at[slot], COMBINE_SLAB, start=s * COMBINE_SLAB)
        part = jnp.dot(weights.astype(BF16), staged, preferred_element_type=F32)
        y = part if y is None else y + part
    xo_ref[...] = x_ref[...] + gate2_ref[...] * y


def _combine(extra, src, dst, xsrc, xdst, idx, rank, gates, lo_row, x2, gate2, yb, *, seq):
    n_tok, d = x2.shape
    tt = SORT_TILE
    spt = seq // tt
    n_stage = TOP_K * tt + N_EXPERTS * COMBINE_CHUNK
    grid_spec = pltpu.PrefetchScalarGridSpec(
        num_scalar_prefetch=5,
        grid=(n_tok // tt,),
        in_specs=[pl.BlockSpec((tt, TOP_K), lambda i, *_: (i, 0)),
                  pl.BlockSpec((tt, TOP_K), lambda i, *_: (i, 0)),
                  pl.BlockSpec((tt, TOP_K), lambda i, *_: (i, 0)),
                  pl.BlockSpec((None, 1, N_EXPERTS), lambda i, *_: (i, 0, 0)),
                  pl.BlockSpec((tt, d), lambda i, *_: (i, 0)),
                  pl.BlockSpec((None, 1, d), lambda i, *_: (i // spt, 0, 0)),
                  pl.BlockSpec(memory_space=pl.ANY)],
        out_specs=pl.BlockSpec((tt, d), lambda i, *_: (i, 0)),
        scratch_shapes=[pltpu.VMEM((2, n_stage * ROW_SUBLANES, LANES), U32),
                        pltpu.SemaphoreType.DMA((2,))],
    )
    return pl.pallas_call(
        functools.partial(_combine_kernel, tt=tt, n_stage=n_stage),
        out_shape=jax.ShapeDtypeStruct((n_tok, d), F32),
        grid_spec=grid_spec,
        compiler_params=_params("arbitrary"),
        name="combine",
    )(extra, src, dst, xsrc, xdst, idx, rank, gates, lo_row, x2, gate2, yb)


def _exclusive_cumsum(a, axis):
    return jnp.cumsum(a, axis=axis) - a


def _region_tables(sort_cnt):
    m = EXPERT_BLOCK
    total = jnp.sum(sort_cnt, axis=0)
    padded = (total + max(DISPATCH_CHUNK, COMBINE_CHUNK) + m - 1) // m * m
    pends = jnp.cumsum(padded)
    return total, pends - padded, pends


def _chunk_tables(step_cnt, chunk, step_tokens):
    n_chunks = jnp.maximum((step_cnt + chunk - 1) // chunk, 1)
    lo_chunked = _exclusive_cumsum(n_chunks * chunk, 1)
    n_extra = n_chunks - 1
    ends = jnp.cumsum(n_extra, axis=1)
    p = jnp.arange(_max_extra_chunks(step_tokens, chunk), dtype=I32)
    x_expert = jnp.minimum(jnp.sum(p[None, :, None] >= ends[:, None, :], axis=2), N_EXPERTS - 1)
    x_offset = (p[None, :] - jnp.take_along_axis(ends - n_extra, x_expert, axis=1) + 1) * chunk
    return lo_chunked, ends[:, -1], x_expert, x_offset


def kernel(x, c, positions, ada_w, ada_b, norm1_g, w_in, q_norm_g, k_norm_g, attn_sinks, pool_w, pool_b,
           pool_scale, w_out, norm2_g, router_w, router_b, expert_w_gu, expert_b_gu, expert_w_down,
           expert_b_down):
    batch, seq, d = x.shape
    n_layers = ada_w.shape[0]
    n_tok = batch * seq
    d_ff = expert_w_down.shape[2]
    assert d == SUBLANES * LANES
    assert seq % min(TOKEN_TILE, seq) == 0 and seq % min(ATTN_TILE, seq) == 0 and ATTN_TILE % WINDOW == 0
    assert seq % ROUTE_TILE == 0 and ROUTE_TILE % SORT_TILE == 0 and DISPATCH_TILE % SORT_TILE == 0

    mod = _ada(c, ada_w, ada_b).reshape(n_layers, batch, 6, 1, d)
    rope = _rope_tables(positions)
    sel, unrot = _rope_select_matrix()
    bdq, bdk = _head_mean_matrix(MXU_TILE), _head_mean_matrix(KV_WIDTH)
    tri = jnp.asarray(np.kron(np.eye(ROUTE_TILE // SORT_TILE, dtype=np.float32),
                              np.triu(np.ones((SORT_TILE, SORT_TILE), np.float32), 1)), BF16)

    m = EXPERT_BLOCK
    slack = max(DISPATCH_CHUNK, COMBINE_CHUNK)
    n_blocks = -(-(n_tok * TOP_K + N_EXPERTS * (slack + m - 1)) // m)
    n_slots = n_blocks * m
    n_tiles = n_tok // SORT_TILE

    x2 = x.reshape(n_tok, d)
    for l in range(n_layers):
        shift1, scale1, gate1, shift2, scale2, gate2 = (mod[l, :, j] for j in range(6))
        q, k, v, pool = _inproj(
            x2, shift1, scale1, norm1_g[l].reshape(1, d), w_in[l].astype(BF16),
            jnp.tile(q_norm_g[l], N_Q_HEADS).reshape(1, ATTN_WIDTH),
            jnp.tile(k_norm_g[l], N_KV_HEADS).reshape(1, KV_WIDTH),
            rope, sel, unrot, bdq, bdk, _pair_block_diag(pool_w[l]).astype(BF16), pool_b[l].reshape(1, POOL_WIDTH),
            pool_scale[l].reshape(1, POOL_WIDTH), batch=batch, seq=seq)
        attn = _attention(q, k, v, attn_sinks[l], seq=seq)
        w_out_b = w_out[l].astype(BF16)
        x2, h2, idx_t, rank_t, gates_t, tile_cnt = _outproj(
            attn, pool, x2, gate1, shift2, scale2, norm2_g[l].reshape(1, d),
            w_out_b[:ATTN_WIDTH], w_out_b[ATTN_WIDTH:], router_w[l].T.astype(BF16),
            router_b[l].reshape(N_EXPERTS, 1),
            tri, seq=seq)

        sort_cnt = tile_cnt.reshape(n_tiles, N_EXPERTS).astype(I32)
        total, pstart, pends = _region_tables(sort_cnt)
        flat = lambda a: a.reshape(-1).astype(I32)
        n_used = (pends[-1] // m).astype(I32).reshape(1)
        block_row = jnp.minimum(jnp.arange(n_blocks, dtype=I32), n_used[0] - 1) * m
        block_expert = jnp.minimum(jnp.sum(block_row[:, None] >= pends[None, :], axis=1), N_EXPERTS - 1).astype(I32)
        n_valid = jnp.clip((pstart + total)[block_expert] - block_row, 0, m).astype(I32)
        gstart = pstart[None, :] + _exclusive_cumsum(sort_cnt, 0)

        per = DISPATCH_TILE // SORT_TILE
        pair_cnt = sort_cnt.reshape(n_tiles // per, per, N_EXPERTS)
        step_cnt = jnp.sum(pair_cnt, axis=1)
        d_lo = _exclusive_cumsum(step_cnt, 1)
        d_gs = gstart[::per]
        _, d_extra, d_xe, d_xo = _chunk_tables(step_cnt, DISPATCH_CHUNK, DISPATCH_TILE)
        d_rows = lambda table: flat(jnp.take_along_axis(table, d_xe, axis=1) + d_xo)
        run_base = (d_lo[:, None, :] + _exclusive_cumsum(pair_cnt, 1)).transpose(0, 2, 1).astype(F32)
        fill = jnp.concatenate([pends, pends - pstart - total]).astype(I32)
        xs = _dispatch(flat(d_extra), flat(d_lo), flat(d_gs), d_rows(d_lo), d_rows(d_gs), fill,
                       idx_t, rank_t, run_base, h2, n_slots)
        yb = _experts(block_expert, n_valid, n_used, xs, expert_w_gu,
                      expert_b_gu.reshape(n_layers, N_EXPERTS, 1, 2 * d_ff), expert_w_down,
                      expert_b_down.reshape(n_layers, N_EXPERTS, 1, d), layer=l)
        c_lo, c_extra, c_xe, c_xo = _chunk_tables(sort_cnt, COMBINE_CHUNK, SORT_TILE)
        c_rows = lambda table: flat(jnp.take_along_axis(table, c_xe, axis=1) + c_xo)
        x2 = _combine(flat(c_extra), flat(gstart), flat(c_lo), c_rows(gstart), c_rows(c_lo),
                      idx_t.T, rank_t.T, gates_t.T, c_lo.astype(F32).reshape(n_tiles, 1, N_EXPERTS), x2, gate2, yb,
                      seq=seq)
    return x2.reshape(batch, seq, d)
```

```python
import functools

import numpy as np
import jax
import jax.numpy as jnp
from jax import lax
from jax.experimental import pallas as pl
from jax.experimental.pallas import tpu as pltpu

F32 = jnp.float32
BF16 = jnp.bfloat16
I32 = jnp.int32
U32 = jnp.uint32

HEAD_DIM = 64
N_Q_HEADS = 8
N_KV_HEADS = 2
Q_PER_KV = N_Q_HEADS // N_KV_HEADS
ATTN_WIDTH = N_Q_HEADS * HEAD_DIM
KV_WIDTH = N_KV_HEADS * HEAD_DIM
WINDOW = 128
ROT_DIM = HEAD_DIM // 4
ROT_HALF = ROT_DIM // 2
ROPE_THETA = 500000.0
POOL_WINDOWS = (2, 4, 8, 16)
N_POOL_GROUPS = len(POOL_WINDOWS)
POOL_GROUP_DIM = 128
POOL_WIDTH = N_POOL_GROUPS * POOL_GROUP_DIM
MAX_POOL_WINDOW = max(POOL_WINDOWS)
N_EXPERTS = 32
TOP_K = 4
SWIGLU_LIMIT = 7.0
SWIGLU_ALPHA = 1.702
EPS = 1e-6

LANES = 128
SUBLANES = 8
MXU_TILE = 256
VMEM_LIMIT_BYTES = 56 * 1024 * 1024

TOKEN_TILE = 1024
ATTN_TILE = 512
ROUTE_TILE = 512
SORT_TILE = 256
DISPATCH_TILE = SORT_TILE
EXPERT_BLOCK = 512
FF_CHUNK = 512
DISPATCH_CHUNK = 32
COMBINE_CHUNK = 32
COMBINE_SLAB = 512
ROUTE_RECORD = 16
FILL_CHUNK = 16


def _max_extra_chunks(step_tokens, chunk):
    return TOP_K * step_tokens // chunk


def _params(*sem):
    return pltpu.CompilerParams(dimension_semantics=sem, vmem_limit_bytes=VMEM_LIMIT_BYTES)


ROW_SUBLANES = 4
HALF_WIDTH = ROW_SUBLANES * LANES
HIGH_HALF_MASK = 0xFFFF0000


def _tile_rows(start, n):
    return pl.ds(pl.multiple_of(start * ROW_SUBLANES, ROW_SUBLANES), n * ROW_SUBLANES)


def _word_chunk(c, n, start=0):
    return pl.ds(start * ROW_SUBLANES + c, n, stride=ROW_SUBLANES)


def _pack_rows(vals):
    lo = lax.bitcast_convert_type(vals[:, :HALF_WIDTH], U32)
    hi = lax.bitcast_convert_type(vals[:, HALF_WIDTH:], U32)
    return lax.shift_right_logical(lo, jnp.uint32(16)) | hi


def _store_rows(ref, n, words):
    for c in range(ROW_SUBLANES):
        ref[_word_chunk(c, n), :] = words[:, c * LANES:(c + 1) * LANES]


def _load_rows(ref, n, keep=None, start=0):
    lo, hi = [], []
    for c in range(ROW_SUBLANES):
        w = ref[_word_chunk(c, n, start), :]
        if keep is not None:
            w = jnp.where(keep, w, jnp.uint32(0))
        lo.append(lax.bitcast_convert_type(lax.shift_left(w, jnp.uint32(16)), F32))
        hi.append(lax.bitcast_convert_type(w & jnp.uint32(HIGH_HALF_MASK), F32))
    return jnp.concatenate(lo + hi, axis=1).astype(BF16)


def _ada_kernel(c_ref, w_ref, b_ref, o_ref):
    c = c_ref[...]
    c_act = c * (1.0 / (1.0 + jnp.exp(-c)))
    o_ref[0] = jnp.dot(c_act, w_ref[0], preferred_element_type=F32,
                       precision=lax.Precision.HIGHEST) + b_ref[0]


def _ada(c, ada_w, ada_b):
    n_layers, d, d6 = ada_w.shape
    b = c.shape[0]
    tn = 1536
    return pl.pallas_call(
        _ada_kernel,
        out_shape=jax.ShapeDtypeStruct((n_layers, b, d6), F32),
        grid=(n_layers, d6 // tn),
        in_specs=[pl.BlockSpec((b, d), lambda l, j: (0, 0)),
                  pl.BlockSpec((1, d, tn), lambda l, j: (l, 0, j)),
                  pl.BlockSpec((1, 1, tn), lambda l, j: (l, 0, j))],
        out_specs=pl.BlockSpec((1, b, tn), lambda l, j: (l, 0, j)),
        compiler_params=_params("arbitrary", "arbitrary"),
        name="ada",
    )(c, ada_w, ada_b.reshape(n_layers, 1, d6))


def _rope_kernel(pos_ref, freq_ref, ch_ref, cl_ref, sh_ref, sl_ref):
    ang = pos_ref[...].astype(F32) * freq_ref[...]
    for fn, hi_ref, lo_ref in ((jnp.cos, ch_ref, cl_ref), (jnp.sin, sh_ref, sl_ref)):
        val = fn(ang)
        hi = val.astype(BF16)
        hi_ref[...] = hi
        lo_ref[...] = (val - hi.astype(F32)).astype(BF16)


def _rope_tables(positions):
    n_tok = positions.size
    inv_freq = ROPE_THETA ** (-jnp.arange(0, ROT_DIM, 2, dtype=F32) / ROT_DIM)
    rows = n_tok * ROT_HALF // LANES
    pos_rep = jnp.repeat(positions.reshape(-1), ROT_HALF).reshape(rows, LANES)
    freq = jnp.tile(inv_freq, LANES // ROT_HALF).reshape(1, LANES)
    tr = min(rows, 512)
    spec = pl.BlockSpec((tr, LANES), lambda i: (i, 0))
    outs = pl.pallas_call(
        _rope_kernel,
        out_shape=[jax.ShapeDtypeStruct((rows, LANES), BF16)] * 4,
        grid=(rows // tr,),
        in_specs=[spec, pl.BlockSpec((1, LANES), lambda i: (0, 0))],
        out_specs=[spec] * 4,
        compiler_params=_params("arbitrary"),
        name="rope",
    )(pos_rep, freq)
    return jnp.concatenate([o.reshape(n_tok, ROT_HALF) for o in outs], axis=1)


def _rope_select_matrix():
    sel = np.zeros((4 * ROT_HALF, 3 * LANES), np.float32)
    for d in range(LANES):
        dd = d % HEAD_DIM
        if dd < ROT_DIM:
            f = dd % ROT_HALF
            sel[f, d] = 1.0
            sel[ROT_HALF + f, d] = 1.0
        if dd < ROT_HALF:
            sel[2 * ROT_HALF + dd, LANES + d] = -1.0
            sel[3 * ROT_HALF + dd, LANES + d] = -1.0
        elif dd < ROT_DIM:
            sel[2 * ROT_HALF + dd - ROT_HALF, 2 * LANES + d] = 1.0
            sel[3 * ROT_HALF + dd - ROT_HALF, 2 * LANES + d] = 1.0
    unrotated = np.array([[1.0 if (d % HEAD_DIM) >= ROT_DIM else 0.0 for d in range(LANES)]], np.float32)
    return jnp.asarray(sel, BF16), jnp.asarray(unrotated, F32)


def _pair_block_diag(w):
    g, c, _ = w.shape
    zero = jnp.zeros((g // 2, c, c), w.dtype)
    top = jnp.concatenate([w[0::2], zero], axis=2)
    bottom = jnp.concatenate([zero, w[1::2]], axis=2)
    return jnp.concatenate([top, bottom], axis=1)


def _head_mean_matrix(width):
    m = np.zeros((width, width), np.float32)
    for h in range(width // HEAD_DIM):
        m[h * HEAD_DIM:(h + 1) * HEAD_DIM, h * HEAD_DIM:(h + 1) * HEAD_DIM] = 1.0 / HEAD_DIM
    return jnp.asarray(m, BF16)


def _inproj_kernel(x_ref, shift_ref, scale_ref, g1_ref, w_ref, qg_ref, kg_ref, rope_ref, sel_ref,
                   unrot_ref, bdq_ref, bdk_ref, wp_ref, bp_ref, ps_ref,
                   q_ref, k_ref, v_ref, pool_ref, carry_ref, *, ts):
    s_idx = pl.program_id(1)

    @pl.when(s_idx == 0)
    def _():
        carry_ref[...] = jnp.zeros_like(carry_ref)

    x = x_ref[...]
    ms = jnp.mean(x * x, axis=-1, keepdims=True)
    h = x * lax.rsqrt(ms + EPS) * (g1_ref[...] * (1.0 + scale_ref[...])) + shift_ref[...]
    hb = h.astype(BF16)

    tab = jnp.dot(rope_ref[...], sel_ref[...], preferred_element_type=F32)
    cos_t = tab[:, :LANES] + unrot_ref[...]
    sin_a = tab[:, LANES:2 * LANES]
    sin_b = tab[:, 2 * LANES:]

    def norm_rope(t, bd_ref, g_ref, out_scale):
        bw = bd_ref.shape[0]
        sq = (t * t).astype(BF16)
        msq = [jnp.dot(sq[:, i * bw:(i + 1) * bw], bd_ref[...], preferred_element_type=F32)
               for i in range(t.shape[1] // bw)]
        msq = msq[0] if len(msq) == 1 else jnp.concatenate(msq, axis=1)
        tn = t * lax.rsqrt(msq + EPS) * g_ref[...]
        chunks = []
        for c in range(t.shape[1] // LANES):
            tc = tn[:, c * LANES:(c + 1) * LANES]
            rot = (tc * cos_t + pltpu.roll(tc, LANES - ROT_HALF, 1) * sin_a
                   + pltpu.roll(tc, ROT_HALF, 1) * sin_b)
            chunks.append(rot * out_scale)
        return chunks[0] if len(chunks) == 1 else jnp.concatenate(chunks, axis=1)

    q = jnp.dot(hb, w_ref[:, :ATTN_WIDTH], preferred_element_type=F32)
    q_ref[...] = norm_rope(q, bdq_ref, qg_ref, HEAD_DIM ** -0.5).astype(BF16)
    k = jnp.dot(hb, w_ref[:, ATTN_WIDTH:ATTN_WIDTH + KV_WIDTH], preferred_element_type=F32)
    k_ref[...] = norm_rope(k, bdk_ref, kg_ref, 1.0).astype(BF16)
    v = jnp.dot(hb, w_ref[:, ATTN_WIDTH + KV_WIDTH:ATTN_WIDTH + 2 * KV_WIDTH], preferred_element_type=F32)
    v_ref[...] = v.astype(BF16)

    u = jnp.dot(hb, w_ref[:, ATTN_WIDTH + 2 * KV_WIDTH:], preferred_element_type=F32)
    ext = jnp.concatenate([carry_ref[...], u], axis=0)
    carry_ref[...] = u[ts - MAX_POOL_WINDOW:, :]
    count = (lax.broadcasted_iota(I32, (ts, POOL_GROUP_DIM), 0) + (s_idx * ts + 1)).astype(F32)
    outs = []
    for g, w in enumerate(POOL_WINDOWS):
        win = ext[:, g * POOL_GROUP_DIM:(g + 1) * POOL_GROUP_DIM]
        span = 1
        while span < w:
            win = win + pltpu.roll(win, span, 0)
            span *= 2
        mean = win[MAX_POOL_WINDOW:, :] * (1.0 / jnp.minimum(count, float(w)))
        outs.append((mean - u[:, g * POOL_GROUP_DIM:(g + 1) * POOL_GROUP_DIM]).astype(BF16))
    pooled = jnp.concatenate(outs, axis=1)
    pw = wp_ref.shape[1]
    mapped = [jnp.dot(pooled[:, i * pw:(i + 1) * pw], wp_ref[i], preferred_element_type=F32)
              for i in range(wp_ref.shape[0])]
    y = (jnp.concatenate(mapped, axis=1) + bp_ref[...]) * ps_ref[...]
    pool_ref[...] = y.astype(BF16)


def _inproj(x2, shift1, scale1, g1, w_in, qg, kg, rope, sel, unrot, bdq, bdk, wp, bp, ps, *, batch, seq):
    n_tok, d = x2.shape
    ts = min(TOKEN_TILE, seq)
    spt = seq // ts
    row = lambda width: pl.BlockSpec((ts, width), lambda b, s: (b * spt + s, 0))
    per_batch = pl.BlockSpec((None, 1, d), lambda b, s: (b, 0, 0))
    full = lambda a: pl.BlockSpec(a.shape, lambda b, s: (0,) * a.ndim)
    return pl.pallas_call(
        functools.partial(_inproj_kernel, ts=ts),
        out_shape=[jax.ShapeDtypeStruct((n_tok, ATTN_WIDTH), BF16),
                   jax.ShapeDtypeStruct((n_tok, KV_WIDTH), BF16),
                   jax.ShapeDtypeStruct((n_tok, KV_WIDTH), BF16),
                   jax.ShapeDtypeStruct((n_tok, POOL_WIDTH), BF16)],
        grid=(batch, spt),
        in_specs=[row(d), per_batch, per_batch, full(g1), full(w_in), full(qg), full(kg),
                  row(rope.shape[1]), full(sel), full(unrot), full(bdq), full(bdk),
                  full(wp), full(bp), full(ps)],
        out_specs=[row(ATTN_WIDTH), row(KV_WIDTH), row(KV_WIDTH), row(POOL_WIDTH)],
        scratch_shapes=[pltpu.VMEM((MAX_POOL_WINDOW, POOL_WIDTH), F32)],
        compiler_params=_params("arbitrary", "arbitrary"),
        name="inproj",
    )(x2, shift1, scale1, g1, w_in, qg, kg, rope, sel, unrot, bdq, bdk, wp, bp, ps)


def _attn_kernel(sink_ref, q_ref, kc_ref, vc_ref, kp_ref, vp_ref, o_ref, *, tq, tiles_per_seq):
    first = (pl.program_id(0) % tiles_per_seq) == 0
    kfull = jnp.concatenate([kp_ref[...], kc_ref[...]], axis=0)
    vfull = jnp.concatenate([vp_ref[...], vc_ref[...]], axis=0)
    rows = Q_PER_KV * WINDOW
    qi = lax.broadcasted_iota(I32, (rows, WINDOW), 0) % WINDOW
    from_prev = lax.broadcasted_iota(I32, (rows, WINDOW), 1) > qi
    row_head = lax.broadcasted_iota(I32, (rows, 1), 0) // WINDOW
    sinks = []
    for h in range(N_KV_HEADS):
        sink = jnp.zeros((rows, 1), F32)
        for g in range(Q_PER_KV):
            sink = jnp.where(row_head == g, sink_ref[h * Q_PER_KV + g], sink)
        sinks.append(sink)

    problems = [(j, h) for j in range(tq // WINDOW) for h in range(N_KV_HEADS)]
    scores, maxes = [], []
    for j, h in problems:
        qs = jnp.concatenate(
            [q_ref[j * WINDOW:(j + 1) * WINDOW, (h * Q_PER_KV + g) * HEAD_DIM:(h * Q_PER_KV + g + 1) * HEAD_DIM]
             for g in range(Q_PER_KV)], axis=0)
        kb = kfull[j * WINDOW:(j + 2) * WINDOW, h * HEAD_DIM:(h + 1) * HEAD_DIM]
        s = lax.dot_general(qs, kb, (((1,), (1,)), ((), ())), preferred_element_type=F32)
        s_prev = s[:, :WINDOW]
        if j == 0:
            s_prev = jnp.where(first, -jnp.inf, s_prev)
        s = jnp.where(from_prev, s_prev, s[:, WINDOW:])
        scores.append(s)
        maxes.append(jnp.maximum(jnp.max(s, axis=-1, keepdims=True), sinks[h]))
    probs, denoms = [], []
    for (j, h), s, m in zip(problems, scores, maxes):
        p = jnp.exp(s - m)
        denoms.append(jnp.sum(p, axis=-1, keepdims=True) + jnp.exp(sinks[h] - m))
        p = p.astype(BF16)
        zero = jnp.zeros_like(p)
        probs.append(jnp.concatenate([jnp.where(from_prev, p, zero), jnp.where(from_prev, zero, p)], axis=1))
    for (j, h), p, denom in zip(problems, probs, denoms):
        vb = vfull[j * WINDOW:(j + 2) * WINDOW, h * HEAD_DIM:(h + 1) * HEAD_DIM]
        o = jnp.dot(p, vb, preferred_element_type=F32) * (1.0 / denom)
        o_ref[j * WINDOW:(j + 1) * WINDOW, h * Q_PER_KV * HEAD_DIM:(h + 1) * Q_PER_KV * HEAD_DIM] = (
            jnp.concatenate([o[g * WINDOW:(g + 1) * WINDOW, :] for g in range(Q_PER_KV)], axis=1).astype(BF16))


def _attention(q, k, v, sinks, *, seq):
    n_tok = q.shape[0]
    tq = min(ATTN_TILE, seq)
    per = tq // WINDOW
    cur = lambda width: pl.BlockSpec((tq, width), lambda i: (i, 0))
    prev = pl.BlockSpec((WINDOW, KV_WIDTH), lambda i: (jnp.maximum(i * per - 1, 0), 0))
    return pl.pallas_call(
        functools.partial(_attn_kernel, tq=tq, tiles_per_seq=seq // tq),
        out_shape=jax.ShapeDtypeStruct((n_tok, ATTN_WIDTH), BF16),
        grid=(n_tok // tq,),
        in_specs=[pl.BlockSpec(memory_space=pltpu.SMEM), cur(ATTN_WIDTH), cur(KV_WIDTH), cur(KV_WIDTH), prev, prev],
        out_specs=cur(ATTN_WIDTH),
        compiler_params=_params("arbitrary"),
        name="attn",
    )(sinks, q, k, v, k, v)


def _outproj_kernel(attn_ref, pool_ref, x_ref, gate1_ref, shift2_ref, scale2_ref, g2_ref, wa_ref, wpo_ref,
                    rw_ref, rb_ref, tri_ref, eye_ref,
                    xo_ref, h2_ref, idx_ref, rank_ref, route_ref, cnt_ref):
    contract_last = (((1,), (1,)), ((), ()))
    slabs = [pl.ds(r, SORT_TILE) for r in range(0, x_ref.shape[0], SORT_TILE)]
    mixed = [jnp.dot(attn_ref[rows, :], wa_ref[...], preferred_element_type=F32)
             + jnp.dot(pool_ref[rows, :], wpo_ref[...], preferred_element_type=F32) for rows in slabs]
    xs = [x_ref[rows, :] + gate1_ref[...] * m for rows, m in zip(slabs, mixed)]
    for rows, x in zip(slabs, xs):
        xo_ref[rows, :] = x
    norm_gain = g2_ref[...] * (1.0 + scale2_ref[...])
    h2bs = []
    for rows, x in zip(slabs, xs):
        ms = jnp.mean(x * x, axis=-1, keepdims=True)
        h2b = (x * lax.rsqrt(ms + EPS) * norm_gain + shift2_ref[...]).astype(BF16)
        h2_ref[rows, :] = h2b
        h2bs.append(h2b)

    logits = jnp.concatenate([lax.dot_general(rw_ref[...], h2b, contract_last, preferred_element_type=F32)
                              for h2b in h2bs], axis=1) + rb_ref[...]
    expert = lax.broadcasted_iota(I32, logits.shape, 0)
    work = logits
    vals, picks, chosen = [], [], []
    for _ in range(TOP_K):
        m = jnp.max(work, axis=0, keepdims=True)
        pick = jnp.min(jnp.where(work == m, expert, N_EXPERTS), axis=0, keepdims=True)
        sel = expert == pick
        vals.append(m)
        picks.append(pick)
        chosen.append(sel)
        work = jnp.where(sel, -jnp.inf, work)
    exps = [jnp.exp(v - vals[0]) for v in vals]
    inv_total = 1.0 / (exps[0] + exps[1] + exps[2] + exps[3])

    multi = jnp.zeros(logits.shape, F32)
    for sel in chosen:
        multi = multi + sel.astype(F32)
    multi_b = multi.astype(BF16)
    before = jnp.dot(multi_b, tri_ref[...], preferred_element_type=F32)
    ranks = [jnp.sum(jnp.where(sel, before, 0.0), axis=0, keepdims=True) for sel in chosen]
    for kk in range(TOP_K):
        idx_ref[kk:kk + 1, :] = picks[kk]
        rank_ref[kk:kk + 1, :] = ranks[kk].astype(I32)
    record = jnp.concatenate([p.astype(F32) for p in picks] + ranks + [e * inv_total for e in exps]
                             + [jnp.zeros((ROUTE_RECORD - 3 * TOP_K, logits.shape[1]), F32)], axis=0)
    route_ref[...] = lax.dot_general(eye_ref[...], record.astype(BF16), contract_last, preferred_element_type=F32)
    ones = jnp.ones((SUBLANES, SORT_TILE), BF16)
    for s in range(multi.shape[1] // SORT_TILE):
        per_expert = lax.dot_general(ones, multi_b[:, s * SORT_TILE:(s + 1) * SORT_TILE], contract_last,
                                     preferred_element_type=F32)
        cnt_ref[s] = per_expert[0:1, :]


def _outproj(attn, pool, x2, gate1, shift2, scale2, g2, wa, wpo, rw, rb, tri, eye, *, seq):
    n_tok, d = x2.shape
    ts = tri.shape[0]
    spt = seq // ts
    row = lambda width: pl.BlockSpec((ts, width), lambda i: (i, 0))
    per_token = pl.BlockSpec((TOP_K, ts), lambda i: (0, i))
    per_batch = pl.BlockSpec((None, 1, d), lambda i: (i // spt, 0, 0))
    full = lambda a: pl.BlockSpec(a.shape, lambda i: (0,) * a.ndim)
    return pl.pallas_call(
        _outproj_kernel,
        out_shape=[jax.ShapeDtypeStruct((n_tok, d), F32),
                   jax.ShapeDtypeStruct((n_tok, d), BF16),
                   jax.ShapeDtypeStruct((TOP_K, n_tok), I32),
                   jax.ShapeDtypeStruct((TOP_K, n_tok), I32),
                   jax.ShapeDtypeStruct((n_tok, ROUTE_RECORD), F32),
                   jax.ShapeDtypeStruct((n_tok // SORT_TILE, 1, N_EXPERTS), F32)],
        grid=(n_tok // ts,),
        in_specs=[row(ATTN_WIDTH), row(POOL_WIDTH), row(d), per_batch, per_batch, per_batch, full(g2),
                  full(wa), full(wpo), full(rw), full(rb), full(tri), full(eye)],
        out_specs=[row(d), row(d), per_token, per_token, row(ROUTE_RECORD),
                   pl.BlockSpec((ts // SORT_TILE, 1, N_EXPERTS), lambda i: (i, 0, 0))],
        compiler_params=_params("arbitrary"),
        name="outproj",
    )(attn, pool, x2, gate1, shift2, scale2, g2, wa, wpo, rw, rb, tri, eye)


def _chunk_loops(cnt_ref, tile, chunk, body):
    def per_expert(e, carry):
        n_chunks = (cnt_ref[tile * N_EXPERTS + e] + chunk - 1) // chunk

        def per_chunk(i, c):
            body(e, i)
            return c

        return lax.fori_loop(0, n_chunks, per_chunk, carry)

    lax.fori_loop(0, N_EXPERTS, per_expert, 0)


def _run_copies(extra_ref, tile, make_copy, act):
    for e in range(N_EXPERTS):
        getattr(make_copy(tile * N_EXPERTS + e, True), act)()

    def per_extra(p, carry):
        getattr(make_copy(tile * make_copy.max_extra + p, False), act)()
        return carry

    lax.fori_loop(0, extra_ref[tile], per_extra, 0)


def _dispatch_kernel(extra_ref, src_ref, dst_ref, xsrc_ref, xdst_ref, fill_ref, idx_t_ref, rank_t_ref, lo_col_ref,
                     h_ref, xs_ref, stage_ref, sem, *, tt):
    t = pl.program_id(0)
    slot = t % 2
    n_rows = TOP_K * tt
    u = FILL_CHUNK
    m = EXPERT_BLOCK

    @pl.when(t == 0)
    def _():
        stage_ref[...] = jnp.zeros_like(stage_ref)

        def pad_copy(e, i):
            return pltpu.make_async_copy(stage_ref.at[1, _tile_rows(0, u)],
                                         xs_ref.at[_tile_rows(fill_ref[e] - (i + 1) * u, u)], sem)

        def tail_copy(i):
            return pltpu.make_async_copy(stage_ref.at[1, _tile_rows(0, m)],
                                         xs_ref.at[_tile_rows(fill_ref[N_EXPERTS - 1] + i * m, m)], sem)

        n_tail = (xs_ref.shape[0] // ROW_SUBLANES - fill_ref[N_EXPERTS - 1]) // m
        for act in ("start", "wait"):
            _chunk_loops(fill_ref, 1, u, lambda e, i: getattr(pad_copy(e, i), act)())
            lax.fori_loop(0, n_tail, lambda i, c: (getattr(tail_copy(i), act)(), c)[1], 0)

    expert = lax.broadcasted_iota(I32, (N_EXPERTS, tt), 0)
    token = lax.broadcasted_iota(I32, (N_EXPERTS, tt), 1)
    rows = lax.broadcasted_iota(I32, (n_rows, tt), 0)
    run_base = jnp.broadcast_to(lo_col_ref[:, 0:1], (N_EXPERTS, tt))
    for s in range(1, tt // SORT_TILE):
        run_base = jnp.where(token >= s * SORT_TILE, lo_col_ref[:, s:s + 1], run_base)
    hit = None
    for k in range(TOP_K):
        onehot = expert == idx_t_ref[k:k + 1, :]
        base = jnp.sum(jnp.where(onehot, run_base, 0.0), axis=0, keepdims=True)
        slot_row = base.astype(I32) + rank_t_ref[k:k + 1, :]
        match = rows == slot_row
        hit = match if hit is None else (hit | match)
    perm = jnp.where(hit, 1.0, 0.0).astype(BF16)
    grouped = jnp.dot(perm, h_ref[...], preferred_element_type=F32)
    _store_rows(stage_ref.at[slot], n_rows, _pack_rows(grouped))

    def copies_of(tile):
        def make_copy(j, first):
            src = src_ref[j] if first else xsrc_ref[j]
            dst = dst_ref[j] if first else xdst_ref[j]
            return pltpu.make_async_copy(stage_ref.at[tile % 2, _tile_rows(src, DISPATCH_CHUNK)],
                                         xs_ref.at[_tile_rows(dst, DISPATCH_CHUNK)], sem)
        make_copy.max_extra = _max_extra_chunks(tt, DISPATCH_CHUNK)
        return make_copy

    @pl.when(t > 0)
    def _():
        _run_copies(extra_ref, t - 1, copies_of(t - 1), "wait")

    _run_copies(extra_ref, t, copies_of(t), "start")

    @pl.when(t == pl.num_programs(0) - 1)
    def _():
        _run_copies(extra_ref, t, copies_of(t), "wait")


def _dispatch(extra, src, dst, xsrc, xdst, fill, idx_t, rank_t, lo_col, h2, n_slots):
    n_tok, d = h2.shape
    tt = DISPATCH_TILE
    grid_spec = pltpu.PrefetchScalarGridSpec(
        num_scalar_prefetch=6,
        grid=(n_tok // tt,),
        in_specs=[pl.BlockSpec((TOP_K, tt), lambda i, *_: (0, i)),
                  pl.BlockSpec((TOP_K, tt), lambda i, *_: (0, i)),
                  pl.BlockSpec((None, N_EXPERTS, tt // SORT_TILE), lambda i, *_: (i, 0, 0)),
                  pl.BlockSpec((tt, d), lambda i, *_: (i, 0))],
        out_specs=pl.BlockSpec(memory_space=pl.ANY),
        scratch_shapes=[pltpu.VMEM((2, (TOP_K * tt + DISPATCH_CHUNK) * ROW_SUBLANES, LANES), U32),
                        pltpu.SemaphoreType.DMA],
    )
    return pl.pallas_call(
        functools.partial(_dispatch_kernel, tt=tt),
        out_shape=jax.ShapeDtypeStruct((n_slots * ROW_SUBLANES, LANES), U32),
        grid_spec=grid_spec,
        compiler_params=pltpu.CompilerParams(dimension_semantics=("arbitrary",),
                                             vmem_limit_bytes=VMEM_LIMIT_BYTES, has_side_effects=True),
        name="dispatch",
    )(extra, src, dst, xsrc, xdst, fill, idx_t, rank_t, lo_col, h2)


def _experts_kernel(be_ref, nvalid_ref, nused_ref, xs_ref, wgu32_ref, bgu_ref, wd32_ref, bd_ref, yb_ref,
                    wgu_ref, wd_ref, *, d_ff):
    i = pl.program_id(0)

    @pl.when((i == 0) | (be_ref[i] != be_ref[jnp.maximum(i - 1, 0)]))
    def _():
        wgu_ref[...] = wgu32_ref[...].astype(BF16)
        wd_ref[...] = wd32_ref[...].astype(BF16)

    @pl.when(i < nused_ref[0])
    def _():
        m = xs_ref.shape[0] // ROW_SUBLANES
        row = lax.broadcasted_iota(I32, (m, 1), 0)
        xb = _load_rows(xs_ref, m, keep=row < nvalid_ref[i])
        acc = None
        for j in range(d_ff // FF_CHUNK):
            lo, hi = j * FF_CHUNK, (j + 1) * FF_CHUNK
            g = jnp.dot(xb, wgu_ref[:, lo:hi], preferred_element_type=F32) + bgu_ref[:, lo:hi]
            u = jnp.dot(xb, wgu_ref[:, d_ff + lo:d_ff + hi], preferred_element_type=F32) + bgu_ref[:, d_ff + lo:d_ff + hi]
            g = jnp.minimum(g, SWIGLU_LIMIT)
            u = jnp.clip(u, -SWIGLU_LIMIT, SWIGLU_LIMIT)
            act = g * (1.0 / (1.0 + jnp.exp(-SWIGLU_ALPHA * g))) * (u + 1.0)
            part = jnp.dot(act.astype(BF16), wd_ref[lo:hi, :], preferred_element_type=F32)
            acc = part if acc is None else acc + part
        y = (acc + bd_ref[...]).astype(BF16).astype(F32)
        _store_rows(yb_ref, m, _pack_rows(y))

    @pl.when(i >= nused_ref[0])
    def _():
        yb_ref[...] = jnp.zeros_like(yb_ref)


def _experts(block_expert, n_valid, n_used, xs, wgu, bgu, wd, bd, *, layer):
    n_slots = xs.shape[0] // ROW_SUBLANES
    d_ff, d = wd.shape[2], wd.shape[3]
    m = EXPERT_BLOCK
    blk = lambda i, be, nv, nu: (jnp.minimum(i, nu[0] - 1), 0)
    per_expert = lambda i, be, nv, nu: (layer, be[i], 0, 0)
    grid_spec = pltpu.PrefetchScalarGridSpec(
        num_scalar_prefetch=3,
        grid=(n_slots // m,),
        in_specs=[pl.BlockSpec((m * ROW_SUBLANES, LANES), blk),
                  pl.BlockSpec((None, None, d, 2 * d_ff), per_expert),
                  pl.BlockSpec((None, None, 1, 2 * d_ff), per_expert),
                  pl.BlockSpec((None, None, d_ff, d), per_expert),
                  pl.BlockSpec((None, None, 1, d), per_expert)],
        out_specs=pl.BlockSpec((m * ROW_SUBLANES, LANES), lambda i, be, nv, nu: (i, 0)),
        scratch_shapes=[pltpu.VMEM((d, 2 * d_ff), BF16), pltpu.VMEM((d_ff, d), BF16)],
    )
    return pl.pallas_call(
        functools.partial(_experts_kernel, d_ff=d_ff),
        out_shape=jax.ShapeDtypeStruct(xs.shape, U32),
        grid_spec=grid_spec,
        compiler_params=_params("arbitrary"),
        name="experts",
    )(block_expert, n_valid, n_used, xs, wgu, bgu, wd, bd)


def _combine_kernel(extra_ref, used_ref, src_ref, dst_ref, xsrc_ref, xdst_ref, route_ref, lo_row_ref,
                    x_ref, gate2_ref, yb_ref, xo_ref, stage_ref, y_ref, sems, *, tt, n_stage):
    t = pl.program_id(0)
    slot = t % 2

    def copies_of(tile):
        def make_copy(j, first):
            src = src_ref[j] if first else xsrc_ref[j]
            dst = dst_ref[j] if first else xdst_ref[j]
            return pltpu.make_async_copy(yb_ref.at[_tile_rows(src, COMBINE_CHUNK)],
                                         stage_ref.at[tile % 2, _tile_rows(dst, COMBINE_CHUNK)], sems.at[tile % 2])
        make_copy.max_extra = _max_extra_chunks(tt, COMBINE_CHUNK)
        return make_copy

    @pl.when(t == 0)
    def _():
        stage_ref[...] = jnp.zeros_like(stage_ref)
        _run_copies(extra_ref, 0, copies_of(0), "start")

    @pl.when(t + 1 < pl.num_programs(0))
    def _():
        _run_copies(extra_ref, t + 1, copies_of(t + 1), "start")

    lane = lax.broadcasted_iota(I32, (tt, N_EXPERTS), 1)
    picked = []
    route = route_ref[...]
    for k in range(TOP_K):
        onehot = lane == route[:, k:k + 1].astype(I32)
        base = jnp.sum(jnp.where(onehot, lo_row_ref[...], 0.0), axis=1, keepdims=True)
        col = (base + route[:, TOP_K + k:TOP_K + k + 1]).astype(I32)
        picked.append((col, route[:, 2 * TOP_K + k:2 * TOP_K + k + 1]))

    _run_copies(extra_ref, t, copies_of(t), "wait")

    def slab_product(s):
        cols = lax.broadcasted_iota(I32, (tt, COMBINE_SLAB), 1) + s * COMBINE_SLAB
        weights = jnp.zeros((tt, COMBINE_SLAB), F32)
        for col, gate in picked:
            weights = jnp.where(cols == col, gate, weights)
        staged = _load_rows(stage_ref.at[slot], COMBINE_SLAB, start=s * COMBINE_SLAB)
        return jnp.dot(weights.astype(BF16), staged, preferred_element_type=F32)

    always = TOP_K * tt // COMBINE_SLAB
    y = slab_product(0)
    for s in range(1, always):
        y = y + slab_product(s)
    y_ref[...] = y
    for s in range(always, n_stage // COMBINE_SLAB):
        @pl.when(used_ref[t] > s * COMBINE_SLAB)
        def _():
            y_ref[...] += slab_product(s)
    xo_ref[...] = x_ref[...] + gate2_ref[...] * y_ref[...]


def _combine(extra, used, src, dst, xsrc, xdst, route, lo_row, x2, gate2, yb, *, seq):
    n_tok, d = x2.shape
    tt = SORT_TILE
    spt = seq // tt
    n_stage = TOP_K * tt + N_EXPERTS * COMBINE_CHUNK
    grid_spec = pltpu.PrefetchScalarGridSpec(
        num_scalar_prefetch=6,
        grid=(n_tok // tt,),
        in_specs=[pl.BlockSpec((tt, ROUTE_RECORD), lambda i, *_: (i, 0)),
                  pl.BlockSpec((None, 1, N_EXPERTS), lambda i, *_: (i, 0, 0)),
                  pl.BlockSpec((tt, d), lambda i, *_: (i, 0)),
                  pl.BlockSpec((None, 1, d), lambda i, *_: (i // spt, 0, 0)),
                  pl.BlockSpec(memory_space=pl.ANY)],
        out_specs=pl.BlockSpec((tt, d), lambda i, *_: (i, 0)),
        scratch_shapes=[pltpu.VMEM((2, n_stage * ROW_SUBLANES, LANES), U32),
                        pltpu.VMEM((tt, d), F32),
                        pltpu.SemaphoreType.DMA((2,))],
    )
    return pl.pallas_call(
        functools.partial(_combine_kernel, tt=tt, n_stage=n_stage),
        out_shape=jax.ShapeDtypeStruct((n_tok, d), F32),
        grid_spec=grid_spec,
        compiler_params=_params("arbitrary"),
        name="combine",
    )(extra, used, src, dst, xsrc, xdst, route, lo_row, x2, gate2, yb)


def _exclusive_cumsum(a, axis):
    return jnp.cumsum(a, axis=axis) - a


def _region_tables(sort_cnt):
    m = EXPERT_BLOCK
    total = jnp.sum(sort_cnt, axis=0)
    padded = (total + max(DISPATCH_CHUNK, COMBINE_CHUNK) + m - 1) // m * m
    pends = jnp.cumsum(padded)
    return total, pends - padded, pends


def _chunk_tables(step_cnt, chunk, step_tokens):
    n_chunks = jnp.maximum((step_cnt + chunk - 1) // chunk, 1)
    lo_chunked = _exclusive_cumsum(n_chunks * chunk, 1)
    n_extra = n_chunks - 1
    ends = jnp.cumsum(n_extra, axis=1)
    p = jnp.arange(_max_extra_chunks(step_tokens, chunk), dtype=I32)
    x_expert = jnp.minimum(jnp.sum(p[None, :, None] >= ends[:, None, :], axis=2), N_EXPERTS - 1)
    x_offset = (p[None, :] - jnp.take_along_axis(ends - n_extra, x_expert, axis=1) + 1) * chunk
    return lo_chunked, ends[:, -1], x_expert, x_offset, jnp.sum(n_chunks, axis=1) * chunk


def kernel(x, c, positions, ada_w, ada_b, norm1_g, w_in, q_norm_g, k_norm_g, attn_sinks, pool_w, pool_b,
           pool_scale, w_out, norm2_g, router_w, router_b, expert_w_gu, expert_b_gu, expert_w_down,
           expert_b_down):
    batch, seq, d = x.shape
    n_layers = ada_w.shape[0]
    n_tok = batch * seq
    d_ff = expert_w_down.shape[2]
    assert d == SUBLANES * LANES
    assert seq % min(TOKEN_TILE, seq) == 0 and seq % min(ATTN_TILE, seq) == 0 and ATTN_TILE % WINDOW == 0
    assert seq % ROUTE_TILE == 0 and ROUTE_TILE % SORT_TILE == 0 and DISPATCH_TILE % SORT_TILE == 0

    mod = _ada(c, ada_w, ada_b).reshape(n_layers, batch, 6, 1, d)
    rope = _rope_tables(positions)
    sel, unrot = _rope_select_matrix()
    bdq, bdk = _head_mean_matrix(MXU_TILE), _head_mean_matrix(KV_WIDTH)
    tri = jnp.asarray(np.kron(np.eye(ROUTE_TILE // SORT_TILE, dtype=np.float32),
                              np.triu(np.ones((SORT_TILE, SORT_TILE), np.float32), 1)), BF16)
    eye = jnp.asarray(np.eye(ROUTE_TILE, dtype=np.float32), BF16)

    m = EXPERT_BLOCK
    slack = max(DISPATCH_CHUNK, COMBINE_CHUNK)
    n_blocks = -(-(n_tok * TOP_K + N_EXPERTS * (slack + m - 1)) // m)
    n_slots = n_blocks * m
    n_tiles = n_tok // SORT_TILE

    x2 = x.reshape(n_tok, d)
    for l in range(n_layers):
        shift1, scale1, gate1, shift2, scale2, gate2 = (mod[l, :, j] for j in range(6))
        q, k, v, pool = _inproj(
            x2, shift1, scale1, norm1_g[l].reshape(1, d), w_in[l].astype(BF16),
            jnp.tile(q_norm_g[l], N_Q_HEADS).reshape(1, ATTN_WIDTH),
            jnp.tile(k_norm_g[l], N_KV_HEADS).reshape(1, KV_WIDTH),
            rope, sel, unrot, bdq, bdk, _pair_block_diag(pool_w[l]).astype(BF16), pool_b[l].reshape(1, POOL_WIDTH),
            pool_scale[l].reshape(1, POOL_WIDTH), batch=batch, seq=seq)
        attn = _attention(q, k, v, attn_sinks[l], seq=seq)
        w_out_b = w_out[l].astype(BF16)
        x2, h2, idx_t, rank_t, route, tile_cnt = _outproj(
            attn, pool, x2, gate1, shift2, scale2, norm2_g[l].reshape(1, d),
            w_out_b[:ATTN_WIDTH], w_out_b[ATTN_WIDTH:], router_w[l].T.astype(BF16),
            router_b[l].reshape(N_EXPERTS, 1),
            tri, eye, seq=seq)

        sort_cnt = tile_cnt.reshape(n_tiles, N_EXPERTS).astype(I32)
        total, pstart, pends = _region_tables(sort_cnt)
        flat = lambda a: a.reshape(-1).astype(I32)
        n_used = (pends[-1] // m).astype(I32).reshape(1)
        block_row = jnp.minimum(jnp.arange(n_blocks, dtype=I32), n_used[0] - 1) * m
        block_expert = jnp.minimum(jnp.sum(block_row[:, None] >= pends[None, :], axis=1), N_EXPERTS - 1).astype(I32)
        n_valid = jnp.clip((pstart + total)[block_expert] - block_row, 0, m).astype(I32)
        gstart = pstart[None, :] + _exclusive_cumsum(sort_cnt, 0)

        per = DISPATCH_TILE // SORT_TILE
        pair_cnt = sort_cnt.reshape(n_tiles // per, per, N_EXPERTS)
        step_cnt = jnp.sum(pair_cnt, axis=1)
        d_lo = _exclusive_cumsum(step_cnt, 1)
        d_gs = gstart[::per]
        _, d_extra, d_xe, d_xo, _ = _chunk_tables(step_cnt, DISPATCH_CHUNK, DISPATCH_TILE)
        d_rows = lambda table: flat(jnp.take_along_axis(table, d_xe, axis=1) + d_xo)
        run_base = (d_lo[:, None, :] + _exclusive_cumsum(pair_cnt, 1)).transpose(0, 2, 1).astype(F32)
        fill = jnp.concatenate([pends, pends - pstart - total]).astype(I32)
        xs = _dispatch(flat(d_extra), flat(d_lo), flat(d_gs), d_rows(d_lo), d_rows(d_gs), fill,
                       idx_t, rank_t, run_base, h2, n_slots)
        yb = _experts(block_expert, n_valid, n_used, xs, expert_w_gu,
                      expert_b_gu.reshape(n_layers, N_EXPERTS, 1, 2 * d_ff), expert_w_down,
                      expert_b_down.reshape(n_layers, N_EXPERTS, 1, d), layer=l)
        c_lo, c_extra, c_xe, c_xo, c_used = _chunk_tables(sort_cnt, COMBINE_CHUNK, SORT_TILE)
        c_rows = lambda table: flat(jnp.take_along_axis(table, c_xe, axis=1) + c_xo)
        x2 = _combine(flat(c_extra), flat(c_used), flat(gstart), flat(c_lo), c_rows(gstart), c_rows(c_lo),
                      route, c_lo.astype(F32).reshape(n_tiles, 1, N_EXPERTS), x2, gate2, yb, seq=seq)
    return x2.reshape(batch, seq, d)
```

```python
import functools

import numpy as np
import jax
import jax.numpy as jnp
from jax import lax
from jax.experimental import pallas as pl
from jax.experimental.pallas import tpu as pltpu

F32 = jnp.float32
BF16 = jnp.bfloat16
I32 = jnp.int32
U32 = jnp.uint32

HEAD_DIM = 64
N_Q_HEADS = 8
N_KV_HEADS = 2
Q_PER_KV = N_Q_HEADS // N_KV_HEADS
ATTN_WIDTH = N_Q_HEADS * HEAD_DIM
KV_WIDTH = N_KV_HEADS * HEAD_DIM
WINDOW = 128
ROT_DIM = HEAD_DIM // 4
ROT_HALF = ROT_DIM // 2
ROPE_THETA = 500000.0
POOL_WINDOWS = (2, 4, 8, 16)
N_POOL_GROUPS = len(POOL_WINDOWS)
POOL_GROUP_DIM = 128
POOL_WIDTH = N_POOL_GROUPS * POOL_GROUP_DIM
MAX_POOL_WINDOW = max(POOL_WINDOWS)
N_EXPERTS = 32
TOP_K = 4
SWIGLU_LIMIT = 7.0
SWIGLU_ALPHA = 1.702
EPS = 1e-6

LANES = 128
SUBLANES = 8
MXU_TILE = 256
VMEM_LIMIT_BYTES = 56 * 1024 * 1024

TOKEN_TILE = 1024
ATTN_TILE = 512
ROUTE_TILE = 512
SORT_TILE = 256
DISPATCH_TILE = SORT_TILE
EXPERT_BLOCK = 512
FF_CHUNK = 512
DISPATCH_CHUNK = 32
COMBINE_CHUNK = 32
COMBINE_SLAB = 512
ROUTE_RECORD = 16
FILL_CHUNK = 16


def _max_extra_chunks(step_tokens, chunk):
    return TOP_K * step_tokens // chunk


def _params(*sem):
    return pltpu.CompilerParams(dimension_semantics=sem, vmem_limit_bytes=VMEM_LIMIT_BYTES)


ROW_SUBLANES = 4
HALF_WIDTH = ROW_SUBLANES * LANES
HIGH_HALF_MASK = 0xFFFF0000


def _tile_rows(start, n):
    return pl.ds(pl.multiple_of(start * ROW_SUBLANES, ROW_SUBLANES), n * ROW_SUBLANES)


def _word_chunk(c, n, start=0):
    return pl.ds(start * ROW_SUBLANES + c, n, stride=ROW_SUBLANES)


def _pack_rows(vals):
    lo = lax.bitcast_convert_type(vals[:, :HALF_WIDTH], U32)
    hi = lax.bitcast_convert_type(vals[:, HALF_WIDTH:], U32)
    return lax.shift_right_logical(lo, jnp.uint32(16)) | hi


def _store_rows(ref, n, words):
    for c in range(ROW_SUBLANES):
        ref[_word_chunk(c, n), :] = words[:, c * LANES:(c + 1) * LANES]


def _load_rows(ref, n, keep=None, start=0):
    lo, hi = [], []
    for c in range(ROW_SUBLANES):
        w = ref[_word_chunk(c, n, start), :]
        if keep is not None:
            w = jnp.where(keep, w, jnp.uint32(0))
        lo.append(lax.bitcast_convert_type(lax.shift_left(w, jnp.uint32(16)), F32))
        hi.append(lax.bitcast_convert_type(w & jnp.uint32(HIGH_HALF_MASK), F32))
    return jnp.concatenate(lo + hi, axis=1).astype(BF16)


def _ada_kernel(c_ref, w_ref, b_ref, o_ref):
    c = c_ref[...]
    c_act = c * (1.0 / (1.0 + jnp.exp(-c)))
    o_ref[0] = jnp.dot(c_act, w_ref[0], preferred_element_type=F32,
                       precision=lax.Precision.HIGHEST) + b_ref[0]


def _ada(c, ada_w, ada_b):
    n_layers, d, d6 = ada_w.shape
    b = c.shape[0]
    tn = 1536
    return pl.pallas_call(
        _ada_kernel,
        out_shape=jax.ShapeDtypeStruct((n_layers, b, d6), F32),
        grid=(n_layers, d6 // tn),
        in_specs=[pl.BlockSpec((b, d), lambda l, j: (0, 0)),
                  pl.BlockSpec((1, d, tn), lambda l, j: (l, 0, j)),
                  pl.BlockSpec((1, 1, tn), lambda l, j: (l, 0, j))],
        out_specs=pl.BlockSpec((1, b, tn), lambda l, j: (l, 0, j)),
        compiler_params=_params("arbitrary", "arbitrary"),
        name="ada",
    )(c, ada_w, ada_b.reshape(n_layers, 1, d6))


def _rope_kernel(pos_ref, freq_ref, ch_ref, cl_ref, sh_ref, sl_ref):
    ang = pos_ref[...].astype(F32) * freq_ref[...]
    for fn, hi_ref, lo_ref in ((jnp.cos, ch_ref, cl_ref), (jnp.sin, sh_ref, sl_ref)):
        val = fn(ang)
        hi = val.astype(BF16)
        hi_ref[...] = hi
        lo_ref[...] = (val - hi.astype(F32)).astype(BF16)


def _rope_tables(positions):
    n_tok = positions.size
    inv_freq = ROPE_THETA ** (-jnp.arange(0, ROT_DIM, 2, dtype=F32) / ROT_DIM)
    rows = n_tok * ROT_HALF // LANES
    pos_rep = jnp.repeat(positions.reshape(-1), ROT_HALF).reshape(rows, LANES)
    freq = jnp.tile(inv_freq, LANES // ROT_HALF).reshape(1, LANES)
    tr = min(rows, 512)
    spec = pl.BlockSpec((tr, LANES), lambda i: (i, 0))
    outs = pl.pallas_call(
        _rope_kernel,
        out_shape=[jax.ShapeDtypeStruct((rows, LANES), BF16)] * 4,
        grid=(rows // tr,),
        in_specs=[spec, pl.BlockSpec((1, LANES), lambda i: (0, 0))],
        out_specs=[spec] * 4,
        compiler_params=_params("arbitrary"),
        name="rope",
    )(pos_rep, freq)
    return jnp.concatenate([o.reshape(n_tok, ROT_HALF) for o in outs], axis=1)


def _rope_select_matrix():
    sel = np.zeros((4 * ROT_HALF, 3 * LANES), np.float32)
    for d in range(LANES):
        dd = d % HEAD_DIM
        if dd < ROT_DIM:
            f = dd % ROT_HALF
            sel[f, d] = 1.0
            sel[ROT_HALF + f, d] = 1.0
        if dd < ROT_HALF:
            sel[2 * ROT_HALF + dd, LANES + d] = -1.0
            sel[3 * ROT_HALF + dd, LANES + d] = -1.0
        elif dd < ROT_DIM:
            sel[2 * ROT_HALF + dd - ROT_HALF, 2 * LANES + d] = 1.0
            sel[3 * ROT_HALF + dd - ROT_HALF, 2 * LANES + d] = 1.0
    unrotated = np.array([[1.0 if (d % HEAD_DIM) >= ROT_DIM else 0.0 for d in range(LANES)]], np.float32)
    return jnp.asarray(sel, BF16), jnp.asarray(unrotated, F32)


def _pair_block_diag(w):
    g, c, _ = w.shape
    zero = jnp.zeros((g // 2, c, c), w.dtype)
    top = jnp.concatenate([w[0::2], zero], axis=2)
    bottom = jnp.concatenate([zero, w[1::2]], axis=2)
    return jnp.concatenate([top, bottom], axis=1)


def _head_mean_matrix(width):
    m = np.zeros((width, width), np.float32)
    for h in range(width // HEAD_DIM):
        m[h * HEAD_DIM:(h + 1) * HEAD_DIM, h * HEAD_DIM:(h + 1) * HEAD_DIM] = 1.0 / HEAD_DIM
    return jnp.asarray(m, BF16)


def _inproj_kernel(x_ref, shift_ref, scale_ref, g1_ref, w_ref, qg_ref, kg_ref, rope_ref, sel_ref,
                   unrot_ref, bdq_ref, bdk_ref, wp_ref, bp_ref, ps_ref,
                   q_ref, k_ref, v_ref, pool_ref, carry_ref, *, ts):
    s_idx = pl.program_id(1)

    @pl.when(s_idx == 0)
    def _():
        carry_ref[...] = jnp.zeros_like(carry_ref)

    x = x_ref[...]
    ms = jnp.mean(x * x, axis=-1, keepdims=True)
    h = x * lax.rsqrt(ms + EPS) * (g1_ref[...] * (1.0 + scale_ref[...])) + shift_ref[...]
    hb = h.astype(BF16)

    tab = jnp.dot(rope_ref[...], sel_ref[...], preferred_element_type=F32)
    cos_t = tab[:, :LANES] + unrot_ref[...]
    sin_a = tab[:, LANES:2 * LANES]
    sin_b = tab[:, 2 * LANES:]

    def norm_rope(t, bd_ref, g_ref, out_scale):
        bw = bd_ref.shape[0]
        sq = (t * t).astype(BF16)
        msq = [jnp.dot(sq[:, i * bw:(i + 1) * bw], bd_ref[...], preferred_element_type=F32)
               for i in range(t.shape[1] // bw)]
        msq = msq[0] if len(msq) == 1 else jnp.concatenate(msq, axis=1)
        tn = t * lax.rsqrt(msq + EPS) * g_ref[...]
        chunks = []
        for c in range(t.shape[1] // LANES):
            tc = tn[:, c * LANES:(c + 1) * LANES]
            rot = (tc * cos_t + pltpu.roll(tc, LANES - ROT_HALF, 1) * sin_a
                   + pltpu.roll(tc, ROT_HALF, 1) * sin_b)
            chunks.append(rot * out_scale)
        return chunks[0] if len(chunks) == 1 else jnp.concatenate(chunks, axis=1)

    q = jnp.dot(hb, w_ref[:, :ATTN_WIDTH], preferred_element_type=F32)
    q_ref[...] = norm_rope(q, bdq_ref, qg_ref, HEAD_DIM ** -0.5).astype(BF16)
    k = jnp.dot(hb, w_ref[:, ATTN_WIDTH:ATTN_WIDTH + KV_WIDTH], preferred_element_type=F32)
    k_ref[...] = norm_rope(k, bdk_ref, kg_ref, 1.0).astype(BF16)
    v = jnp.dot(hb, w_ref[:, ATTN_WIDTH + KV_WIDTH:ATTN_WIDTH + 2 * KV_WIDTH], preferred_element_type=F32)
    v_ref[...] = v.astype(BF16)

    u = jnp.dot(hb, w_ref[:, ATTN_WIDTH + 2 * KV_WIDTH:], preferred_element_type=F32)
    ext = jnp.concatenate([carry_ref[...], u], axis=0)
    carry_ref[...] = u[ts - MAX_POOL_WINDOW:, :]
    count = (lax.broadcasted_iota(I32, (ts, POOL_GROUP_DIM), 0) + (s_idx * ts + 1)).astype(F32)
    outs = []
    for g, w in enumerate(POOL_WINDOWS):
        win = ext[:, g * POOL_GROUP_DIM:(g + 1) * POOL_GROUP_DIM]
        span = 1
        while span < w:
            win = win + pltpu.roll(win, span, 0)
            span *= 2
        mean = win[MAX_POOL_WINDOW:, :] * (1.0 / jnp.minimum(count, float(w)))
        outs.append((mean - u[:, g * POOL_GROUP_DIM:(g + 1) * POOL_GROUP_DIM]).astype(BF16))
    pooled = jnp.concatenate(outs, axis=1)
    pw = wp_ref.shape[1]
    mapped = [jnp.dot(pooled[:, i * pw:(i + 1) * pw], wp_ref[i], preferred_element_type=F32)
              for i in range(wp_ref.shape[0])]
    y = (jnp.concatenate(mapped, axis=1) + bp_ref[...]) * ps_ref[...]
    pool_ref[...] = y.astype(BF16)


def _inproj(x2, shift1, scale1, g1, w_in, qg, kg, rope, sel, unrot, bdq, bdk, wp, bp, ps, *, batch, seq):
    n_tok, d = x2.shape
    ts = min(TOKEN_TILE, seq)
    spt = seq // ts
    row = lambda width: pl.BlockSpec((ts, width), lambda b, s: (b * spt + s, 0))
    per_batch = pl.BlockSpec((None, 1, d), lambda b, s: (b, 0, 0))
    full = lambda a: pl.BlockSpec(a.shape, lambda b, s: (0,) * a.ndim)
    return pl.pallas_call(
        functools.partial(_inproj_kernel, ts=ts),
        out_shape=[jax.ShapeDtypeStruct((n_tok, ATTN_WIDTH), BF16),
                   jax.ShapeDtypeStruct((n_tok, KV_WIDTH), BF16),
                   jax.ShapeDtypeStruct((n_tok, KV_WIDTH), BF16),
                   jax.ShapeDtypeStruct((n_tok, POOL_WIDTH), BF16)],
        grid=(batch, spt),
        in_specs=[row(d), per_batch, per_batch, full(g1), full(w_in), full(qg), full(kg),
                  row(rope.shape[1]), full(sel), full(unrot), full(bdq), full(bdk),
                  full(wp), full(bp), full(ps)],
        out_specs=[row(ATTN_WIDTH), row(KV_WIDTH), row(KV_WIDTH), row(POOL_WIDTH)],
        scratch_shapes=[pltpu.VMEM((MAX_POOL_WINDOW, POOL_WIDTH), F32)],
        compiler_params=_params("arbitrary", "arbitrary"),
        name="inproj",
    )(x2, shift1, scale1, g1, w_in, qg, kg, rope, sel, unrot, bdq, bdk, wp, bp, ps)


def _attn_kernel(sink_ref, q_ref, kc_ref, vc_ref, kp_ref, vp_ref, o_ref, *, tq, tiles_per_seq):
    first = (pl.program_id(0) % tiles_per_seq) == 0
    kfull = jnp.concatenate([kp_ref[...], kc_ref[...]], axis=0)
    vfull = jnp.concatenate([vp_ref[...], vc_ref[...]], axis=0)
    rows = Q_PER_KV * WINDOW
    qi = lax.broadcasted_iota(I32, (rows, WINDOW), 0) % WINDOW
    from_prev = lax.broadcasted_iota(I32, (rows, WINDOW), 1) > qi
    row_head = lax.broadcasted_iota(I32, (rows, 1), 0) // WINDOW
    sinks = []
    for h in range(N_KV_HEADS):
        sink = jnp.zeros((rows, 1), F32)
        for g in range(Q_PER_KV):
            sink = jnp.where(row_head == g, sink_ref[h * Q_PER_KV + g], sink)
        sinks.append(sink)

    problems = [(j, h) for j in range(tq // WINDOW) for h in range(N_KV_HEADS)]
    scores, maxes = [], []
    for j, h in problems:
        qs = jnp.concatenate(
            [q_ref[j * WINDOW:(j + 1) * WINDOW, (h * Q_PER_KV + g) * HEAD_DIM:(h * Q_PER_KV + g + 1) * HEAD_DIM]
             for g in range(Q_PER_KV)], axis=0)
        kb = kfull[j * WINDOW:(j + 2) * WINDOW, h * HEAD_DIM:(h + 1) * HEAD_DIM]
        s = lax.dot_general(qs, kb, (((1,), (1,)), ((), ())), preferred_element_type=F32)
        s_prev = s[:, :WINDOW]
        if j == 0:
            s_prev = jnp.where(first, -jnp.inf, s_prev)
        s = jnp.where(from_prev, s_prev, s[:, WINDOW:])
        scores.append(s)
        maxes.append(jnp.maximum(jnp.max(s, axis=-1, keepdims=True), sinks[h]))
    probs, denoms = [], []
    for (j, h), s, m in zip(problems, scores, maxes):
        p = jnp.exp(s - m)
        denoms.append(jnp.sum(p, axis=-1, keepdims=True) + jnp.exp(sinks[h] - m))
        p = p.astype(BF16)
        zero = jnp.zeros_like(p)
        probs.append(jnp.concatenate([jnp.where(from_prev, p, zero), jnp.where(from_prev, zero, p)], axis=1))
    for (j, h), p, denom in zip(problems, probs, denoms):
        vb = vfull[j * WINDOW:(j + 2) * WINDOW, h * HEAD_DIM:(h + 1) * HEAD_DIM]
        o = jnp.dot(p, vb, preferred_element_type=F32) * (1.0 / denom)
        o_ref[j * WINDOW:(j + 1) * WINDOW, h * Q_PER_KV * HEAD_DIM:(h + 1) * Q_PER_KV * HEAD_DIM] = (
            jnp.concatenate([o[g * WINDOW:(g + 1) * WINDOW, :] for g in range(Q_PER_KV)], axis=1).astype(BF16))


def _attention(q, k, v, sinks, *, seq):
    n_tok = q.shape[0]
    tq = min(ATTN_TILE, seq)
    per = tq // WINDOW
    cur = lambda width: pl.BlockSpec((tq, width), lambda i: (i, 0))
    prev = pl.BlockSpec((WINDOW, KV_WIDTH), lambda i: (jnp.maximum(i * per - 1, 0), 0))
    return pl.pallas_call(
        functools.partial(_attn_kernel, tq=tq, tiles_per_seq=seq // tq),
        out_shape=jax.ShapeDtypeStruct((n_tok, ATTN_WIDTH), BF16),
        grid=(n_tok // tq,),
        in_specs=[pl.BlockSpec(memory_space=pltpu.SMEM), cur(ATTN_WIDTH), cur(KV_WIDTH), cur(KV_WIDTH), prev, prev],
        out_specs=cur(ATTN_WIDTH),
        compiler_params=_params("arbitrary"),
        name="attn",
    )(sinks, q, k, v, k, v)


def _outproj_kernel(attn_ref, pool_ref, x_ref, gate1_ref, shift2_ref, scale2_ref, g2_ref, wa_ref, wpo_ref,
                    rw_ref, rb_ref, tri_ref, eye_ref,
                    xo_ref, h2_ref, idx_ref, rank_ref, route_ref, cnt_ref):
    contract_last = (((1,), (1,)), ((), ()))
    slabs = [pl.ds(r, SORT_TILE) for r in range(0, x_ref.shape[0], SORT_TILE)]
    mixed = [jnp.dot(attn_ref[rows, :], wa_ref[...], preferred_element_type=F32)
             + jnp.dot(pool_ref[rows, :], wpo_ref[...], preferred_element_type=F32) for rows in slabs]
    xs = [x_ref[rows, :] + gate1_ref[...] * m for rows, m in zip(slabs, mixed)]
    for rows, x in zip(slabs, xs):
        xo_ref[rows, :] = x
    norm_gain = g2_ref[...] * (1.0 + scale2_ref[...])
    h2bs = []
    for rows, x in zip(slabs, xs):
        ms = jnp.mean(x * x, axis=-1, keepdims=True)
        h2b = (x * lax.rsqrt(ms + EPS) * norm_gain + shift2_ref[...]).astype(BF16)
        h2_ref[rows, :] = h2b
        h2bs.append(h2b)

    logits = jnp.concatenate([lax.dot_general(rw_ref[...], h2b, contract_last, preferred_element_type=F32)
                              for h2b in h2bs], axis=1) + rb_ref[...]
    expert = lax.broadcasted_iota(I32, logits.shape, 0)
    work = logits
    vals, picks, chosen = [], [], []
    for _ in range(TOP_K):
        m = jnp.max(work, axis=0, keepdims=True)
        pick = jnp.min(jnp.where(work == m, expert, N_EXPERTS), axis=0, keepdims=True)
        sel = expert == pick
        vals.append(m)
        picks.append(pick)
        chosen.append(sel)
        work = jnp.where(sel, -jnp.inf, work)
    exps = [jnp.exp(v - vals[0]) for v in vals]
    inv_total = 1.0 / (exps[0] + exps[1] + exps[2] + exps[3])

    multi = jnp.zeros(logits.shape, F32)
    for sel in chosen:
        multi = multi + sel.astype(F32)
    multi_b = multi.astype(BF16)
    before = jnp.dot(multi_b, tri_ref[...], preferred_element_type=F32)
    ranks = [jnp.sum(jnp.where(sel, before, 0.0), axis=0, keepdims=True) for sel in chosen]
    for kk in range(TOP_K):
        idx_ref[kk:kk + 1, :] = picks[kk]
        rank_ref[kk:kk + 1, :] = ranks[kk].astype(I32)
    record = jnp.concatenate([p.astype(F32) for p in picks] + ranks + [e * inv_total for e in exps]
                             + [jnp.zeros((ROUTE_RECORD - 3 * TOP_K, logits.shape[1]), F32)], axis=0)
    route_ref[...] = lax.dot_general(eye_ref[...], record.astype(BF16), contract_last, preferred_element_type=F32)
    ones = jnp.ones((SUBLANES, SORT_TILE), BF16)
    for s in range(multi.shape[1] // SORT_TILE):
        per_expert = lax.dot_general(ones, multi_b[:, s * SORT_TILE:(s + 1) * SORT_TILE], contract_last,
                                     preferred_element_type=F32)
        cnt_ref[s] = per_expert[0:1, :]


def _outproj(attn, pool, x2, gate1, shift2, scale2, g2, wa, wpo, rw, rb, tri, eye, *, seq):
    n_tok, d = x2.shape
    ts = tri.shape[0]
    spt = seq // ts
    row = lambda width: pl.BlockSpec((ts, width), lambda i: (i, 0))
    per_token = pl.BlockSpec((TOP_K, ts), lambda i: (0, i))
    per_batch = pl.BlockSpec((None, 1, d), lambda i: (i // spt, 0, 0))
    full = lambda a: pl.BlockSpec(a.shape, lambda i: (0,) * a.ndim)
    return pl.pallas_call(
        _outproj_kernel,
        out_shape=[jax.ShapeDtypeStruct((n_tok, d), F32),
                   jax.ShapeDtypeStruct((n_tok, d), BF16),
                   jax.ShapeDtypeStruct((TOP_K, n_tok), I32),
                   jax.ShapeDtypeStruct((TOP_K, n_tok), I32),
                   jax.ShapeDtypeStruct((n_tok, ROUTE_RECORD), F32),
                   jax.ShapeDtypeStruct((n_tok // SORT_TILE, 1, N_EXPERTS), F32)],
        grid=(n_tok // ts,),
        in_specs=[row(ATTN_WIDTH), row(POOL_WIDTH), row(d), per_batch, per_batch, per_batch, full(g2),
                  full(wa), full(wpo), full(rw), full(rb), full(tri), full(eye)],
        out_specs=[row(d), row(d), per_token, per_token, row(ROUTE_RECORD),
                   pl.BlockSpec((ts // SORT_TILE, 1, N_EXPERTS), lambda i: (i, 0, 0))],
        compiler_params=_params("arbitrary"),
        name="outproj",
    )(attn, pool, x2, gate1, shift2, scale2, g2, wa, wpo, rw, rb, tri, eye)


def _chunk_loops(cnt_ref, tile, chunk, body):
    def per_expert(e, carry):
        n_chunks = (cnt_ref[tile * N_EXPERTS + e] + chunk - 1) // chunk

        def per_chunk(i, c):
            body(e, i)
            return c

        return lax.fori_loop(0, n_chunks, per_chunk, carry)

    lax.fori_loop(0, N_EXPERTS, per_expert, 0)


def _run_copies(extra_ref, tile, make_copy, act):
    for e in range(N_EXPERTS):
        getattr(make_copy(tile * N_EXPERTS + e, True), act)()

    def per_extra(p, carry):
        getattr(make_copy(tile * make_copy.max_extra + p, False), act)()
        return carry

    lax.fori_loop(0, extra_ref[tile], per_extra, 0)


def _dispatch_kernel(extra_ref, src_ref, dst_ref, xsrc_ref, xdst_ref, fill_ref, idx_t_ref, rank_t_ref, lo_col_ref,
                     h_ref, xs_ref, stage_ref, sem, *, tt):
    t = pl.program_id(0)
    slot = t % 2
    n_rows = TOP_K * tt
    u = FILL_CHUNK
    m = EXPERT_BLOCK

    @pl.when(t == 0)
    def _():
        stage_ref[...] = jnp.zeros_like(stage_ref)

        def pad_copy(e, i):
            return pltpu.make_async_copy(stage_ref.at[1, _tile_rows(0, u)],
                                         xs_ref.at[_tile_rows(fill_ref[e] - (i + 1) * u, u)], sem)

        def tail_copy(i):
            return pltpu.make_async_copy(stage_ref.at[1, _tile_rows(0, m)],
                                         xs_ref.at[_tile_rows(fill_ref[N_EXPERTS - 1] + i * m, m)], sem)

        n_tail = (xs_ref.shape[0] // ROW_SUBLANES - fill_ref[N_EXPERTS - 1]) // m
        for act in ("start", "wait"):
            _chunk_loops(fill_ref, 1, u, lambda e, i: getattr(pad_copy(e, i), act)())
            lax.fori_loop(0, n_tail, lambda i, c: (getattr(tail_copy(i), act)(), c)[1], 0)

    expert = lax.broadcasted_iota(I32, (N_EXPERTS, tt), 0)
    token = lax.broadcasted_iota(I32, (N_EXPERTS, tt), 1)
    rows = lax.broadcasted_iota(I32, (n_rows, tt), 0)
    run_base = jnp.broadcast_to(lo_col_ref[:, 0:1], (N_EXPERTS, tt))
    for s in range(1, tt // SORT_TILE):
        run_base = jnp.where(token >= s * SORT_TILE, lo_col_ref[:, s:s + 1], run_base)
    hit = None
    for k in range(TOP_K):
        onehot = expert == idx_t_ref[k:k + 1, :]
        base = jnp.sum(jnp.where(onehot, run_base, 0.0), axis=0, keepdims=True)
        slot_row = base.astype(I32) + rank_t_ref[k:k + 1, :]
        match = rows == slot_row
        hit = match if hit is None else (hit | match)
    perm = jnp.where(hit, 1.0, 0.0).astype(BF16)
    grouped = jnp.dot(perm, h_ref[...], preferred_element_type=F32)
    _store_rows(stage_ref.at[slot], n_rows, _pack_rows(grouped))

    def copies_of(tile):
        def make_copy(j, first):
            src = src_ref[j] if first else xsrc_ref[j]
            dst = dst_ref[j] if first else xdst_ref[j]
            return pltpu.make_async_copy(stage_ref.at[tile % 2, _tile_rows(src, DISPATCH_CHUNK)],
                                         xs_ref.at[_tile_rows(dst, DISPATCH_CHUNK)], sem)
        make_copy.max_extra = _max_extra_chunks(tt, DISPATCH_CHUNK)
        return make_copy

    @pl.when(t > 0)
    def _():
        _run_copies(extra_ref, t - 1, copies_of(t - 1), "wait")

    _run_copies(extra_ref, t, copies_of(t), "start")

    @pl.when(t == pl.num_programs(0) - 1)
    def _():
        _run_copies(extra_ref, t, copies_of(t), "wait")


def _dispatch(extra, src, dst, xsrc, xdst, fill, idx_t, rank_t, lo_col, h2, n_slots):
    n_tok, d = h2.shape
    tt = DISPATCH_TILE
    grid_spec = pltpu.PrefetchScalarGridSpec(
        num_scalar_prefetch=6,
        grid=(n_tok // tt,),
        in_specs=[pl.BlockSpec((TOP_K, tt), lambda i, *_: (0, i)),
                  pl.BlockSpec((TOP_K, tt), lambda i, *_: (0, i)),
                  pl.BlockSpec((None, N_EXPERTS, tt // SORT_TILE), lambda i, *_: (i, 0, 0)),
                  pl.BlockSpec((tt, d), lambda i, *_: (i, 0))],
        out_specs=pl.BlockSpec(memory_space=pl.ANY),
        scratch_shapes=[pltpu.VMEM((2, (TOP_K * tt + DISPATCH_CHUNK) * ROW_SUBLANES, LANES), U32),
                        pltpu.SemaphoreType.DMA],
    )
    return pl.pallas_call(
        functools.partial(_dispatch_kernel, tt=tt),
        out_shape=jax.ShapeDtypeStruct((n_slots * ROW_SUBLANES, LANES), U32),
        grid_spec=grid_spec,
        compiler_params=pltpu.CompilerParams(dimension_semantics=("arbitrary",),
                                             vmem_limit_bytes=VMEM_LIMIT_BYTES, has_side_effects=True),
        name="dispatch",
    )(extra, src, dst, xsrc, xdst, fill, idx_t, rank_t, lo_col, h2)


def _experts_kernel(be_ref, nvalid_ref, nused_ref, xs_ref, wgu32_ref, bgu_ref, wd32_ref, bd_ref, yb_ref,
                    wgu_ref, wd_ref, *, d_ff):
    i = pl.program_id(0)

    @pl.when((i == 0) | (be_ref[i] != be_ref[jnp.maximum(i - 1, 0)]))
    def _():
        wgu_ref[...] = wgu32_ref[...].astype(BF16)
        wd_ref[...] = wd32_ref[...].astype(BF16)

    @pl.when(i < nused_ref[0])
    def _():
        m = xs_ref.shape[0] // ROW_SUBLANES
        row = lax.broadcasted_iota(I32, (m, 1), 0)
        xb = _load_rows(xs_ref, m, keep=row < nvalid_ref[i])
        acc = None
        for j in range(d_ff // FF_CHUNK):
            lo, hi = j * FF_CHUNK, (j + 1) * FF_CHUNK
            g = jnp.dot(xb, wgu_ref[:, lo:hi], preferred_element_type=F32) + bgu_ref[:, lo:hi]
            u = jnp.dot(xb, wgu_ref[:, d_ff + lo:d_ff + hi], preferred_element_type=F32) + bgu_ref[:, d_ff + lo:d_ff + hi]
            g = jnp.minimum(g, SWIGLU_LIMIT)
            u = jnp.clip(u, -SWIGLU_LIMIT, SWIGLU_LIMIT)
            act = g * (1.0 / (1.0 + jnp.exp(-SWIGLU_ALPHA * g))) * (u + 1.0)
            part = jnp.dot(act.astype(BF16), wd_ref[lo:hi, :], preferred_element_type=F32)
            acc = part if acc is None else acc + part
        y = (acc + bd_ref[...]).astype(BF16).astype(F32)
        _store_rows(yb_ref, m, _pack_rows(y))

    @pl.when(i >= nused_ref[0])
    def _():
        yb_ref[...] = jnp.zeros_like(yb_ref)


def _experts(block_expert, n_valid, n_used, xs, wgu, bgu, wd, bd, *, layer):
    n_slots = xs.shape[0] // ROW_SUBLANES
    d_ff, d = wd.shape[2], wd.shape[3]
    m = EXPERT_BLOCK
    blk = lambda i, be, nv, nu: (jnp.minimum(i, nu[0] - 1), 0)
    per_expert = lambda i, be, nv, nu: (layer, be[i], 0, 0)
    grid_spec = pltpu.PrefetchScalarGridSpec(
        num_scalar_prefetch=3,
        grid=(n_slots // m,),
        in_specs=[pl.BlockSpec((m * ROW_SUBLANES, LANES), blk),
                  pl.BlockSpec((None, None, d, 2 * d_ff), per_expert),
                  pl.BlockSpec((None, None, 1, 2 * d_ff), per_expert),
                  pl.BlockSpec((None, None, d_ff, d), per_expert),
                  pl.BlockSpec((None, None, 1, d), per_expert)],
        out_specs=pl.BlockSpec((m * ROW_SUBLANES, LANES), lambda i, be, nv, nu: (i, 0)),
        scratch_shapes=[pltpu.VMEM((d, 2 * d_ff), BF16), pltpu.VMEM((d_ff, d), BF16)],
    )
    return pl.pallas_call(
        functools.partial(_experts_kernel, d_ff=d_ff),
        out_shape=jax.ShapeDtypeStruct(xs.shape, U32),
        grid_spec=grid_spec,
        compiler_params=_params("arbitrary"),
        name="experts",
    )(block_expert, n_valid, n_used, xs, wgu, bgu, wd, bd)


def _combine_kernel(extra_ref, src_ref, dst_ref, xsrc_ref, xdst_ref, route_ref, lo_row_ref,
                    x_ref, gate2_ref, yb_ref, xo_ref, stage_ref, sems, *, tt, n_stage):
    t = pl.program_id(0)
    slot = t % 2

    def copies_of(tile):
        def make_copy(j, first):
            src = src_ref[j] if first else xsrc_ref[j]
            dst = dst_ref[j] if first else xdst_ref[j]
            return pltpu.make_async_copy(yb_ref.at[_tile_rows(src, COMBINE_CHUNK)],
                                         stage_ref.at[tile % 2, _tile_rows(dst, COMBINE_CHUNK)], sems.at[tile % 2])
        make_copy.max_extra = _max_extra_chunks(tt, COMBINE_CHUNK)
        return make_copy

    @pl.when(t == 0)
    def _():
        stage_ref[...] = jnp.zeros_like(stage_ref)
        _run_copies(extra_ref, 0, copies_of(0), "start")

    @pl.when(t + 1 < pl.num_programs(0))
    def _():
        _run_copies(extra_ref, t + 1, copies_of(t + 1), "start")

    lane = lax.broadcasted_iota(I32, (tt, N_EXPERTS), 1)
    picked = []
    route = route_ref[...]
    for k in range(TOP_K):
        onehot = lane == route[:, k:k + 1].astype(I32)
        base = jnp.sum(jnp.where(onehot, lo_row_ref[...], 0.0), axis=1, keepdims=True)
        col = (base + route[:, TOP_K + k:TOP_K + k + 1]).astype(I32)
        picked.append((col, route[:, 2 * TOP_K + k:2 * TOP_K + k + 1]))

    _run_copies(extra_ref, t, copies_of(t), "wait")

    def slab_product(s):
        cols = lax.broadcasted_iota(I32, (tt, COMBINE_SLAB), 1) + s * COMBINE_SLAB
        weights = jnp.zeros((tt, COMBINE_SLAB), F32)
        for col, gate in picked:
            weights = jnp.where(cols == col, gate, weights)
        staged = _load_rows(stage_ref.at[slot], COMBINE_SLAB, start=s * COMBINE_SLAB)
        return jnp.dot(weights.astype(BF16), staged, preferred_element_type=F32)

    y = slab_product(0)
    for s in range(1, n_stage // COMBINE_SLAB):
        y = y + slab_product(s)
    xo_ref[...] = x_ref[...] + gate2_ref[...] * y


def _combine(extra, src, dst, xsrc, xdst, route, lo_row, x2, gate2, yb, *, seq):
    n_tok, d = x2.shape
    tt = SORT_TILE
    spt = seq // tt
    n_stage = TOP_K * tt + N_EXPERTS * COMBINE_CHUNK
    grid_spec = pltpu.PrefetchScalarGridSpec(
        num_scalar_prefetch=5,
        grid=(n_tok // tt,),
        in_specs=[pl.BlockSpec((tt, ROUTE_RECORD), lambda i, *_: (i, 0)),
                  pl.BlockSpec((None, 1, N_EXPERTS), lambda i, *_: (i, 0, 0)),
                  pl.BlockSpec((tt, d), lambda i, *_: (i, 0)),
                  pl.BlockSpec((None, 1, d), lambda i, *_: (i // spt, 0, 0)),
                  pl.BlockSpec(memory_space=pl.ANY)],
        out_specs=pl.BlockSpec((tt, d), lambda i, *_: (i, 0)),
        scratch_shapes=[pltpu.VMEM((2, n_stage * ROW_SUBLANES, LANES), U32),
                        pltpu.SemaphoreType.DMA((2,))],
    )
    return pl.pallas_call(
        functools.partial(_combine_kernel, tt=tt, n_stage=n_stage),
        out_shape=jax.ShapeDtypeStruct((n_tok, d), F32),
        grid_spec=grid_spec,
        compiler_params=_params("arbitrary"),
        name="combine",
    )(extra, src, dst, xsrc, xdst, route, lo_row, x2, gate2, yb)


def _exclusive_cumsum(a, axis):
    return jnp.cumsum(a, axis=axis) - a


def _region_tables(sort_cnt):
    m = EXPERT_BLOCK
    total = jnp.sum(sort_cnt, axis=0)
    padded = (total + max(DISPATCH_CHUNK, COMBINE_CHUNK) + m - 1) // m * m
    pends = jnp.cumsum(padded)
    return total, pends - padded, pends


def _chunk_tables(step_cnt, chunk, step_tokens):
    n_chunks = jnp.maximum((step_cnt + chunk - 1) // chunk, 1)
    lo_chunked = _exclusive_cumsum(n_chunks * chunk, 1)
    n_extra = n_chunks - 1
    ends = jnp.cumsum(n_extra, axis=1)
    p = jnp.arange(_max_extra_chunks(step_tokens, chunk), dtype=I32)
    x_expert = jnp.minimum(jnp.sum(p[None, :, None] >= ends[:, None, :], axis=2), N_EXPERTS - 1)
    x_offset = (p[None, :] - jnp.take_along_axis(ends - n_extra, x_expert, axis=1) + 1) * chunk
    return lo_chunked, ends[:, -1], x_expert, x_offset


def kernel(x, c, positions, ada_w, ada_b, norm1_g, w_in, q_norm_g, k_norm_g, attn_sinks, pool_w, pool_b,
           pool_scale, w_out, norm2_g, router_w, router_b, expert_w_gu, expert_b_gu, expert_w_down,
           expert_b_down):
    batch, seq, d = x.shape
    n_layers = ada_w.shape[0]
    n_tok = batch * seq
    d_ff = expert_w_down.shape[2]
    assert d == SUBLANES * LANES
    assert seq % min(TOKEN_TILE, seq) == 0 and seq % min(ATTN_TILE, seq) == 0 and ATTN_TILE % WINDOW == 0
    assert seq % ROUTE_TILE == 0 and ROUTE_TILE % SORT_TILE == 0 and DISPATCH_TILE % SORT_TILE == 0

    mod = _ada(c, ada_w, ada_b).reshape(n_layers, batch, 6, 1, d)
    rope = _rope_tables(positions)
    sel, unrot = _rope_select_matrix()
    bdq, bdk = _head_mean_matrix(MXU_TILE), _head_mean_matrix(KV_WIDTH)
    tri = jnp.asarray(np.kron(np.eye(ROUTE_TILE // SORT_TILE, dtype=np.float32),
                              np.triu(np.ones((SORT_TILE, SORT_TILE), np.float32), 1)), BF16)
    eye = jnp.asarray(np.eye(ROUTE_TILE, dtype=np.float32), BF16)

    m = EXPERT_BLOCK
    slack = max(DISPATCH_CHUNK, COMBINE_CHUNK)
    n_blocks = -(-(n_tok * TOP_K + N_EXPERTS * (slack + m - 1)) // m)
    n_slots = n_blocks * m
    n_tiles = n_tok // SORT_TILE

    x2 = x.reshape(n_tok, d)
    for l in range(n_layers):
        shift1, scale1, gate1, shift2, scale2, gate2 = (mod[l, :, j] for j in range(6))
        q, k, v, pool = _inproj(
            x2, shift1, scale1, norm1_g[l].reshape(1, d), w_in[l].astype(BF16),
            jnp.tile(q_norm_g[l], N_Q_HEADS).reshape(1, ATTN_WIDTH),
            jnp.tile(k_norm_g[l], N_KV_HEADS).reshape(1, KV_WIDTH),
            rope, sel, unrot, bdq, bdk, _pair_block_diag(pool_w[l]).astype(BF16), pool_b[l].reshape(1, POOL_WIDTH),
            pool_scale[l].reshape(1, POOL_WIDTH), batch=batch, seq=seq)
        attn = _attention(q, k, v, attn_sinks[l], seq=seq)
        w_out_b = w_out[l].astype(BF16)
        x2, h2, idx_t, rank_t, route, tile_cnt = _outproj(
            attn, pool, x2, gate1, shift2, scale2, norm2_g[l].reshape(1, d),
            w_out_b[:ATTN_WIDTH], w_out_b[ATTN_WIDTH:], router_w[l].T.astype(BF16),
            router_b[l].reshape(N_EXPERTS, 1),
            tri, eye, seq=seq)

        sort_cnt = tile_cnt.reshape(n_tiles, N_EXPERTS).astype(I32)
        total, pstart, pends = _region_tables(sort_cnt)
        flat = lambda a: a.reshape(-1).astype(I32)
        n_used = (pends[-1] // m).astype(I32).reshape(1)
        block_row = jnp.minimum(jnp.arange(n_blocks, dtype=I32), n_used[0] - 1) * m
        block_expert = jnp.minimum(jnp.sum(block_row[:, None] >= pends[None, :], axis=1), N_EXPERTS - 1).astype(I32)
        n_valid = jnp.clip((pstart + total)[block_expert] - block_row, 0, m).astype(I32)
        gstart = pstart[None, :] + _exclusive_cumsum(sort_cnt, 0)

        per = DISPATCH_TILE // SORT_TILE
        pair_cnt = sort_cnt.reshape(n_tiles // per, per, N_EXPERTS)
        step_cnt = jnp.sum(pair_cnt, axis=1)
        d_lo = _exclusive_cumsum(step_cnt, 1)
        d_gs = gstart[::per]
        _, d_extra, d_xe, d_xo = _chunk_tables(step_cnt, DISPATCH_CHUNK, DISPATCH_TILE)
        d_rows = lambda table: flat(jnp.take_along_axis(table, d_xe, axis=1) + d_xo)
        run_base = (d_lo[:, None, :] + _exclusive_cumsum(pair_cnt, 1)).transpose(0, 2, 1).astype(F32)
        fill = jnp.concatenate([pends, pends - pstart - total]).astype(I32)
        xs = _dispatch(flat(d_extra), flat(d_lo), flat(d_gs), d_rows(d_lo), d_rows(d_gs), fill,
                       idx_t, rank_t, run_base, h2, n_slots)
        yb = _experts(block_expert, n_valid, n_used, xs, expert_w_gu,
                      expert_b_gu.reshape(n_layers, N_EXPERTS, 1, 2 * d_ff), expert_w_down,
                      expert_b_down.reshape(n_layers, N_EXPERTS, 1, d), layer=l)
        c_lo, c_extra, c_xe, c_xo = _chunk_tables(sort_cnt, COMBINE_CHUNK, SORT_TILE)
        c_rows = lambda table: flat(jnp.take_along_axis(table, c_xe, axis=1) + c_xo)
        x2 = _combine(flat(c_extra), flat(gstart), flat(c_lo), c_rows(gstart), c_rows(c_lo),
                      route, c_lo.astype(F32).reshape(n_tiles, 1, N_EXPERTS), x2, gate2, yb, seq=seq)
    return x2.reshape(batch, seq, d)
```

```python
import functools

import numpy as np
import jax
import jax.numpy as jnp
from jax import lax
from jax.experimental import pallas as pl
from jax.experimental.pallas import tpu as pltpu

F32 = jnp.float32
BF16 = jnp.bfloat16
I32 = jnp.int32
U32 = jnp.uint32

HEAD_DIM = 64
N_Q_HEADS = 8
N_KV_HEADS = 2
Q_PER_KV = N_Q_HEADS // N_KV_HEADS
ATTN_WIDTH = N_Q_HEADS * HEAD_DIM
KV_WIDTH = N_KV_HEADS * HEAD_DIM
WINDOW = 128
ROT_DIM = HEAD_DIM // 4
ROT_HALF = ROT_DIM // 2
ROPE_THETA = 500000.0
POOL_WINDOWS = (2, 4, 8, 16)
N_POOL_GROUPS = len(POOL_WINDOWS)
POOL_GROUP_DIM = 128
POOL_WIDTH = N_POOL_GROUPS * POOL_GROUP_DIM
MAX_POOL_WINDOW = max(POOL_WINDOWS)
N_EXPERTS = 32
TOP_K = 4
SWIGLU_LIMIT = 7.0
SWIGLU_ALPHA = 1.702
EPS = 1e-6

LANES = 128
SUBLANES = 8
MXU_TILE = 256
VMEM_LIMIT_BYTES = 56 * 1024 * 1024

TOKEN_TILE = 1024
ATTN_TILE = 512
ROUTE_TILE = 512
SORT_TILE = 256
DISPATCH_TILE = SORT_TILE
EXPERT_BLOCK = 512
FF_CHUNK = 512
DISPATCH_CHUNK = 40
COMBINE_CHUNK = 32
COMBINE_SLAB = 512
ROUTE_RECORD = 16
FILL_CHUNK = 16


def _max_extra_chunks(step_tokens, chunk):
    return -(-(TOP_K * step_tokens // chunk) // 32) * 32


def _params(*sem):
    return pltpu.CompilerParams(dimension_semantics=sem, vmem_limit_bytes=VMEM_LIMIT_BYTES)


ROW_SUBLANES = 4
HALF_WIDTH = ROW_SUBLANES * LANES
HIGH_HALF_MASK = 0xFFFF0000


def _tile_rows(start, n):
    return pl.ds(pl.multiple_of(start * ROW_SUBLANES, ROW_SUBLANES), n * ROW_SUBLANES)


def _word_chunk(c, n, start=0):
    return pl.ds(start * ROW_SUBLANES + c, n, stride=ROW_SUBLANES)


def _pack_rows(vals):
    lo = lax.bitcast_convert_type(vals[:, :HALF_WIDTH], U32)
    hi = lax.bitcast_convert_type(vals[:, HALF_WIDTH:], U32)
    return lax.shift_right_logical(lo, jnp.uint32(16)) | hi


def _store_rows(ref, n, words):
    for c in range(ROW_SUBLANES):
        ref[_word_chunk(c, n), :] = words[:, c * LANES:(c + 1) * LANES]


def _load_rows(ref, n, keep=None, start=0):
    lo, hi = [], []
    for c in range(ROW_SUBLANES):
        w = ref[_word_chunk(c, n, start), :]
        if keep is not None:
            w = jnp.where(keep, w, jnp.uint32(0))
        lo.append(lax.bitcast_convert_type(lax.shift_left(w, jnp.uint32(16)), F32))
        hi.append(lax.bitcast_convert_type(w & jnp.uint32(HIGH_HALF_MASK), F32))
    return jnp.concatenate(lo + hi, axis=1).astype(BF16)


def _ada_kernel(c_ref, w_ref, b_ref, o_ref):
    c = c_ref[...]
    c_act = c * (1.0 / (1.0 + jnp.exp(-c)))
    o_ref[0] = jnp.dot(c_act, w_ref[0], preferred_element_type=F32,
                       precision=lax.Precision.HIGHEST) + b_ref[0]


def _ada(c, ada_w, ada_b):
    n_layers, d, d6 = ada_w.shape
    b = c.shape[0]
    tn = 1536
    return pl.pallas_call(
        _ada_kernel,
        out_shape=jax.ShapeDtypeStruct((n_layers, b, d6), F32),
        grid=(n_layers, d6 // tn),
        in_specs=[pl.BlockSpec((b, d), lambda l, j: (0, 0)),
                  pl.BlockSpec((1, d, tn), lambda l, j: (l, 0, j)),
                  pl.BlockSpec((1, 1, tn), lambda l, j: (l, 0, j))],
        out_specs=pl.BlockSpec((1, b, tn), lambda l, j: (l, 0, j)),
        compiler_params=_params("arbitrary", "arbitrary"),
        name="ada",
    )(c, ada_w, ada_b.reshape(n_layers, 1, d6))


def _rope_kernel(pos_ref, freq_ref, ch_ref, cl_ref, sh_ref, sl_ref):
    ang = pos_ref[...].astype(F32) * freq_ref[...]
    for fn, hi_ref, lo_ref in ((jnp.cos, ch_ref, cl_ref), (jnp.sin, sh_ref, sl_ref)):
        val = fn(ang)
        hi = val.astype(BF16)
        hi_ref[...] = hi
        lo_ref[...] = (val - hi.astype(F32)).astype(BF16)


def _rope_tables(positions):
    n_tok = positions.size
    inv_freq = ROPE_THETA ** (-jnp.arange(0, ROT_DIM, 2, dtype=F32) / ROT_DIM)
    rows = n_tok * ROT_HALF // LANES
    pos_rep = jnp.repeat(positions.reshape(-1), ROT_HALF).reshape(rows, LANES)
    freq = jnp.tile(inv_freq, LANES // ROT_HALF).reshape(1, LANES)
    tr = min(rows, 512)
    spec = pl.BlockSpec((tr, LANES), lambda i: (i, 0))
    outs = pl.pallas_call(
        _rope_kernel,
        out_shape=[jax.ShapeDtypeStruct((rows, LANES), BF16)] * 4,
        grid=(rows // tr,),
        in_specs=[spec, pl.BlockSpec((1, LANES), lambda i: (0, 0))],
        out_specs=[spec] * 4,
        compiler_params=_params("arbitrary"),
        name="rope",
    )(pos_rep, freq)
    return jnp.concatenate([o.reshape(n_tok, ROT_HALF) for o in outs], axis=1)


def _rope_select_matrix():
    sel = np.zeros((4 * ROT_HALF, 3 * LANES), np.float32)
    for d in range(LANES):
        dd = d % HEAD_DIM
        if dd < ROT_DIM:
            f = dd % ROT_HALF
            sel[f, d] = 1.0
            sel[ROT_HALF + f, d] = 1.0
        if dd < ROT_HALF:
            sel[2 * ROT_HALF + dd, LANES + d] = -1.0
            sel[3 * ROT_HALF + dd, LANES + d] = -1.0
        elif dd < ROT_DIM:
            sel[2 * ROT_HALF + dd - ROT_HALF, 2 * LANES + d] = 1.0
            sel[3 * ROT_HALF + dd - ROT_HALF, 2 * LANES + d] = 1.0
    unrotated = np.array([[1.0 if (d % HEAD_DIM) >= ROT_DIM else 0.0 for d in range(LANES)]], np.float32)
    return jnp.asarray(sel, BF16), jnp.asarray(unrotated, F32)


def _pair_block_diag(w):
    g, c, _ = w.shape
    zero = jnp.zeros((g // 2, c, c), w.dtype)
    top = jnp.concatenate([w[0::2], zero], axis=2)
    bottom = jnp.concatenate([zero, w[1::2]], axis=2)
    return jnp.concatenate([top, bottom], axis=1)


def _head_mean_matrix(width):
    m = np.zeros((width, width), np.float32)
    for h in range(width // HEAD_DIM):
        m[h * HEAD_DIM:(h + 1) * HEAD_DIM, h * HEAD_DIM:(h + 1) * HEAD_DIM] = 1.0 / HEAD_DIM
    return jnp.asarray(m, BF16)


def _inproj_kernel(x_ref, shift_ref, scale_ref, g1_ref, w_ref, qg_ref, kg_ref, rope_ref, sel_ref,
                   unrot_ref, bdq_ref, bdk_ref, wp_ref, bp_ref, ps_ref,
                   q_ref, k_ref, v_ref, pool_ref, carry_ref, *, ts):
    s_idx = pl.program_id(1)

    @pl.when(s_idx == 0)
    def _():
        carry_ref[...] = jnp.zeros_like(carry_ref)

    x = x_ref[...]
    ms = jnp.mean(x * x, axis=-1, keepdims=True)
    h = x * lax.rsqrt(ms + EPS) * (g1_ref[...] * (1.0 + scale_ref[...])) + shift_ref[...]
    hb = h.astype(BF16)

    tab = jnp.dot(rope_ref[...], sel_ref[...], preferred_element_type=F32)
    cos_t = tab[:, :LANES] + unrot_ref[...]
    sin_a = tab[:, LANES:2 * LANES]
    sin_b = tab[:, 2 * LANES:]

    def norm_rope(t, bd_ref, g_ref, out_scale):
        bw = bd_ref.shape[0]
        sq = (t * t).astype(BF16)
        msq = [jnp.dot(sq[:, i * bw:(i + 1) * bw], bd_ref[...], preferred_element_type=F32)
               for i in range(t.shape[1] // bw)]
        msq = msq[0] if len(msq) == 1 else jnp.concatenate(msq, axis=1)
        tn = t * lax.rsqrt(msq + EPS) * g_ref[...]
        chunks = []
        for c in range(t.shape[1] // LANES):
            tc = tn[:, c * LANES:(c + 1) * LANES]
            rot = (tc * cos_t + pltpu.roll(tc, LANES - ROT_HALF, 1) * sin_a
                   + pltpu.roll(tc, ROT_HALF, 1) * sin_b)
            chunks.append(rot * out_scale)
        return chunks[0] if len(chunks) == 1 else jnp.concatenate(chunks, axis=1)

    q = jnp.dot(hb, w_ref[:, :ATTN_WIDTH], preferred_element_type=F32)
    q_ref[...] = norm_rope(q, bdq_ref, qg_ref, HEAD_DIM ** -0.5).astype(BF16)
    k = jnp.dot(hb, w_ref[:, ATTN_WIDTH:ATTN_WIDTH + KV_WIDTH], preferred_element_type=F32)
    k_ref[...] = norm_rope(k, bdk_ref, kg_ref, 1.0).astype(BF16)
    v = jnp.dot(hb, w_ref[:, ATTN_WIDTH + KV_WIDTH:ATTN_WIDTH + 2 * KV_WIDTH], preferred_element_type=F32)
    v_ref[...] = v.astype(BF16)

    u = jnp.dot(hb, w_ref[:, ATTN_WIDTH + 2 * KV_WIDTH:], preferred_element_type=F32)
    ext = jnp.concatenate([carry_ref[...], u], axis=0)
    carry_ref[...] = u[ts - MAX_POOL_WINDOW:, :]
    count = (lax.broadcasted_iota(I32, (ts, POOL_GROUP_DIM), 0) + (s_idx * ts + 1)).astype(F32)
    outs = []
    for g, w in enumerate(POOL_WINDOWS):
        win = ext[:, g * POOL_GROUP_DIM:(g + 1) * POOL_GROUP_DIM]
        span = 1
        while span < w:
            win = win + pltpu.roll(win, span, 0)
            span *= 2
        mean = win[MAX_POOL_WINDOW:, :] * (1.0 / jnp.minimum(count, float(w)))
        outs.append((mean - u[:, g * POOL_GROUP_DIM:(g + 1) * POOL_GROUP_DIM]).astype(BF16))
    pooled = jnp.concatenate(outs, axis=1)
    pw = wp_ref.shape[1]
    mapped = [jnp.dot(pooled[:, i * pw:(i + 1) * pw], wp_ref[i], preferred_element_type=F32)
              for i in range(wp_ref.shape[0])]
    y = (jnp.concatenate(mapped, axis=1) + bp_ref[...]) * ps_ref[...]
    pool_ref[...] = y.astype(BF16)


def _inproj(x2, shift1, scale1, g1, w_in, qg, kg, rope, sel, unrot, bdq, bdk, wp, bp, ps, *, batch, seq):
    n_tok, d = x2.shape
    ts = min(TOKEN_TILE, seq)
    spt = seq // ts
    row = lambda width: pl.BlockSpec((ts, width), lambda b, s: (b * spt + s, 0))
    per_batch = pl.BlockSpec((None, 1, d), lambda b, s: (b, 0, 0))
    full = lambda a: pl.BlockSpec(a.shape, lambda b, s: (0,) * a.ndim)
    return pl.pallas_call(
        functools.partial(_inproj_kernel, ts=ts),
        out_shape=[jax.ShapeDtypeStruct((n_tok, ATTN_WIDTH), BF16),
                   jax.ShapeDtypeStruct((n_tok, KV_WIDTH), BF16),
                   jax.ShapeDtypeStruct((n_tok, KV_WIDTH), BF16),
                   jax.ShapeDtypeStruct((n_tok, POOL_WIDTH), BF16)],
        grid=(batch, spt),
        in_specs=[row(d), per_batch, per_batch, full(g1), full(w_in), full(qg), full(kg),
                  row(rope.shape[1]), full(sel), full(unrot), full(bdq), full(bdk),
                  full(wp), full(bp), full(ps)],
        out_specs=[row(ATTN_WIDTH), row(KV_WIDTH), row(KV_WIDTH), row(POOL_WIDTH)],
        scratch_shapes=[pltpu.VMEM((MAX_POOL_WINDOW, POOL_WIDTH), F32)],
        compiler_params=_params("arbitrary", "arbitrary"),
        name="inproj",
    )(x2, shift1, scale1, g1, w_in, qg, kg, rope, sel, unrot, bdq, bdk, wp, bp, ps)


def _attn_kernel(sink_ref, q_ref, kc_ref, vc_ref, kp_ref, vp_ref, o_ref, *, tq, tiles_per_seq):
    first = (pl.program_id(0) % tiles_per_seq) == 0
    kfull = jnp.concatenate([kp_ref[...], kc_ref[...]], axis=0)
    vfull = jnp.concatenate([vp_ref[...], vc_ref[...]], axis=0)
    rows = Q_PER_KV * WINDOW
    qi = lax.broadcasted_iota(I32, (rows, WINDOW), 0) % WINDOW
    from_prev = lax.broadcasted_iota(I32, (rows, WINDOW), 1) > qi
    row_head = lax.broadcasted_iota(I32, (rows, 1), 0) // WINDOW
    sinks = []
    for h in range(N_KV_HEADS):
        sink = jnp.zeros((rows, 1), F32)
        for g in range(Q_PER_KV):
            sink = jnp.where(row_head == g, sink_ref[h * Q_PER_KV + g], sink)
        sinks.append(sink)

    problems = [(j, h) for j in range(tq // WINDOW) for h in range(N_KV_HEADS)]
    scores, maxes = [], []
    for j, h in problems:
        qs = jnp.concatenate(
            [q_ref[j * WINDOW:(j + 1) * WINDOW, (h * Q_PER_KV + g) * HEAD_DIM:(h * Q_PER_KV + g + 1) * HEAD_DIM]
             for g in range(Q_PER_KV)], axis=0)
        kb = kfull[j * WINDOW:(j + 2) * WINDOW, h * HEAD_DIM:(h + 1) * HEAD_DIM]
        s = lax.dot_general(qs, kb, (((1,), (1,)), ((), ())), preferred_element_type=F32)
        s_prev = s[:, :WINDOW]
        if j == 0:
            s_prev = jnp.where(first, -jnp.inf, s_prev)
        s = jnp.where(from_prev, s_prev, s[:, WINDOW:])
        scores.append(s)
        maxes.append(jnp.maximum(jnp.max(s, axis=-1, keepdims=True), sinks[h]))
    probs, denoms = [], []
    for (j, h), s, m in zip(problems, scores, maxes):
        p = jnp.exp(s - m)
        denoms.append(jnp.sum(p, axis=-1, keepdims=True) + jnp.exp(sinks[h] - m))
        p = p.astype(BF16)
        zero = jnp.zeros_like(p)
        probs.append(jnp.concatenate([jnp.where(from_prev, p, zero), jnp.where(from_prev, zero, p)], axis=1))
    for (j, h), p, denom in zip(problems, probs, denoms):
        vb = vfull[j * WINDOW:(j + 2) * WINDOW, h * HEAD_DIM:(h + 1) * HEAD_DIM]
        o = jnp.dot(p, vb, preferred_element_type=F32) * (1.0 / denom)
        o_ref[j * WINDOW:(j + 1) * WINDOW, h * Q_PER_KV * HEAD_DIM:(h + 1) * Q_PER_KV * HEAD_DIM] = (
            jnp.concatenate([o[g * WINDOW:(g + 1) * WINDOW, :] for g in range(Q_PER_KV)], axis=1).astype(BF16))


def _attention(q, k, v, sinks, *, seq):
    n_tok = q.shape[0]
    tq = min(ATTN_TILE, seq)
    per = tq // WINDOW
    cur = lambda width: pl.BlockSpec((tq, width), lambda i: (i, 0))
    prev = pl.BlockSpec((WINDOW, KV_WIDTH), lambda i: (jnp.maximum(i * per - 1, 0), 0))
    return pl.pallas_call(
        functools.partial(_attn_kernel, tq=tq, tiles_per_seq=seq // tq),
        out_shape=jax.ShapeDtypeStruct((n_tok, ATTN_WIDTH), BF16),
        grid=(n_tok // tq,),
        in_specs=[pl.BlockSpec(memory_space=pltpu.SMEM), cur(ATTN_WIDTH), cur(KV_WIDTH), cur(KV_WIDTH), prev, prev],
        out_specs=cur(ATTN_WIDTH),
        compiler_params=_params("arbitrary"),
        name="attn",
    )(sinks, q, k, v, k, v)


def _outproj_kernel(attn_ref, pool_ref, x_ref, gate1_ref, shift2_ref, scale2_ref, g2_ref, wa_ref, wpo_ref,
                    rw_ref, rb_ref, tri_ref, eye_ref,
                    xo_ref, h2_ref, idx_ref, rank_ref, route_ref, cnt_ref):
    contract_last = (((1,), (1,)), ((), ()))
    slabs = [pl.ds(r, SORT_TILE) for r in range(0, x_ref.shape[0], SORT_TILE)]
    mixed = [jnp.dot(attn_ref[rows, :], wa_ref[...], preferred_element_type=F32)
             + jnp.dot(pool_ref[rows, :], wpo_ref[...], preferred_element_type=F32) for rows in slabs]
    xs = [x_ref[rows, :] + gate1_ref[...] * m for rows, m in zip(slabs, mixed)]
    for rows, x in zip(slabs, xs):
        xo_ref[rows, :] = x
    norm_gain = g2_ref[...] * (1.0 + scale2_ref[...])
    h2bs = []
    for rows, x in zip(slabs, xs):
        ms = jnp.mean(x * x, axis=-1, keepdims=True)
        h2b = (x * lax.rsqrt(ms + EPS) * norm_gain + shift2_ref[...]).astype(BF16)
        h2_ref[rows, :] = h2b
        h2bs.append(h2b)

    logits = jnp.concatenate([lax.dot_general(rw_ref[...], h2b, contract_last, preferred_element_type=F32)
                              for h2b in h2bs], axis=1) + rb_ref[...]
    expert = lax.broadcasted_iota(I32, logits.shape, 0)
    work = logits
    vals, picks, chosen = [], [], []
    for _ in range(TOP_K):
        m = jnp.max(work, axis=0, keepdims=True)
        pick = jnp.min(jnp.where(work == m, expert, N_EXPERTS), axis=0, keepdims=True)
        sel = expert == pick
        vals.append(m)
        picks.append(pick)
        chosen.append(sel)
        work = jnp.where(sel, -jnp.inf, work)
    exps = [jnp.exp(v - vals[0]) for v in vals]
    inv_total = 1.0 / (exps[0] + exps[1] + exps[2] + exps[3])

    multi = jnp.zeros(logits.shape, F32)
    for sel in chosen:
        multi = multi + sel.astype(F32)
    multi_b = multi.astype(BF16)
    before = jnp.dot(multi_b, tri_ref[...], preferred_element_type=F32)
    ranks = [jnp.sum(jnp.where(sel, before, 0.0), axis=0, keepdims=True) for sel in chosen]
    for kk in range(TOP_K):
        idx_ref[kk:kk + 1, :] = picks[kk]
        rank_ref[kk:kk + 1, :] = ranks[kk].astype(I32)
    record = jnp.concatenate([p.astype(F32) for p in picks] + ranks + [e * inv_total for e in exps]
                             + [jnp.zeros((ROUTE_RECORD - 3 * TOP_K, logits.shape[1]), F32)], axis=0)
    route_ref[...] = lax.dot_general(eye_ref[...], record.astype(BF16), contract_last, preferred_element_type=F32)
    ones = jnp.ones((SUBLANES, SORT_TILE), BF16)
    for s in range(multi.shape[1] // SORT_TILE):
        per_expert = lax.dot_general(ones, multi_b[:, s * SORT_TILE:(s + 1) * SORT_TILE], contract_last,
                                     preferred_element_type=F32)
        cnt_ref[s] = per_expert[0:1, :]


def _outproj(attn, pool, x2, gate1, shift2, scale2, g2, wa, wpo, rw, rb, tri, eye, *, seq):
    n_tok, d = x2.shape
    ts = tri.shape[0]
    spt = seq // ts
    row = lambda width: pl.BlockSpec((ts, width), lambda i: (i, 0))
    per_token = pl.BlockSpec((TOP_K, ts), lambda i: (0, i))
    per_batch = pl.BlockSpec((None, 1, d), lambda i: (i // spt, 0, 0))
    full = lambda a: pl.BlockSpec(a.shape, lambda i: (0,) * a.ndim)
    return pl.pallas_call(
        _outproj_kernel,
        out_shape=[jax.ShapeDtypeStruct((n_tok, d), F32),
                   jax.ShapeDtypeStruct((n_tok, d), BF16),
                   jax.ShapeDtypeStruct((TOP_K, n_tok), I32),
                   jax.ShapeDtypeStruct((TOP_K, n_tok), I32),
                   jax.ShapeDtypeStruct((n_tok, ROUTE_RECORD), F32),
                   jax.ShapeDtypeStruct((n_tok // SORT_TILE, 1, N_EXPERTS), F32)],
        grid=(n_tok // ts,),
        in_specs=[row(ATTN_WIDTH), row(POOL_WIDTH), row(d), per_batch, per_batch, per_batch, full(g2),
                  full(wa), full(wpo), full(rw), full(rb), full(tri), full(eye)],
        out_specs=[row(d), row(d), per_token, per_token, row(ROUTE_RECORD),
                   pl.BlockSpec((ts // SORT_TILE, 1, N_EXPERTS), lambda i: (i, 0, 0))],
        compiler_params=_params("arbitrary"),
        name="outproj",
    )(attn, pool, x2, gate1, shift2, scale2, g2, wa, wpo, rw, rb, tri, eye)


def _chunk_loops(cnt_ref, tile, chunk, body):
    def per_expert(e, carry):
        n_chunks = (cnt_ref[tile * N_EXPERTS + e] + chunk - 1) // chunk

        def per_chunk(i, c):
            body(e, i)
            return c

        return lax.fori_loop(0, n_chunks, per_chunk, carry)

    lax.fori_loop(0, N_EXPERTS, per_expert, 0)


def _run_copies(extra_ref, tile, make_copy, act):
    for e in range(N_EXPERTS):
        getattr(make_copy(tile * N_EXPERTS + e, True), act)()

    def per_extra(p, carry):
        getattr(make_copy(tile * make_copy.max_extra + p, False), act)()
        return carry

    lax.fori_loop(0, extra_ref[tile], per_extra, 0)


def _dispatch_kernel(extra_ref, src_ref, dst_ref, xsrc_ref, xdst_ref, fill_ref, idx_t_ref, rank_t_ref, lo_col_ref,
                     h_ref, xs_ref, stage_ref, sem, *, tt):
    t = pl.program_id(0)
    slot = t % 2
    n_rows = TOP_K * tt
    u = FILL_CHUNK
    m = EXPERT_BLOCK

    @pl.when(t == 0)
    def _():
        stage_ref[...] = jnp.zeros_like(stage_ref)

        def pad_copy(e, i):
            return pltpu.make_async_copy(stage_ref.at[1, _tile_rows(0, u)],
                                         xs_ref.at[_tile_rows(fill_ref[e] - (i + 1) * u, u)], sem)

        def tail_copy(i):
            return pltpu.make_async_copy(stage_ref.at[1, _tile_rows(0, m)],
                                         xs_ref.at[_tile_rows(fill_ref[N_EXPERTS - 1] + i * m, m)], sem)

        n_tail = (xs_ref.shape[0] // ROW_SUBLANES - fill_ref[N_EXPERTS - 1]) // m
        for act in ("start", "wait"):
            _chunk_loops(fill_ref, 1, u, lambda e, i: getattr(pad_copy(e, i), act)())
            lax.fori_loop(0, n_tail, lambda i, c: (getattr(tail_copy(i), act)(), c)[1], 0)

    expert = lax.broadcasted_iota(I32, (N_EXPERTS, tt), 0)
    token = lax.broadcasted_iota(I32, (N_EXPERTS, tt), 1)
    rows = lax.broadcasted_iota(I32, (n_rows, tt), 0)
    run_base = jnp.broadcast_to(lo_col_ref[:, 0:1], (N_EXPERTS, tt))
    for s in range(1, tt // SORT_TILE):
        run_base = jnp.where(token >= s * SORT_TILE, lo_col_ref[:, s:s + 1], run_base)
    hit = None
    for k in range(TOP_K):
        onehot = expert == idx_t_ref[k:k + 1, :]
        base = jnp.sum(jnp.where(onehot, run_base, 0.0), axis=0, keepdims=True)
        slot_row = base.astype(I32) + rank_t_ref[k:k + 1, :]
        match = rows == slot_row
        hit = match if hit is None else (hit | match)
    perm = jnp.where(hit, 1.0, 0.0).astype(BF16)
    grouped = jnp.dot(perm, h_ref[...], preferred_element_type=F32)
    _store_rows(stage_ref.at[slot], n_rows, _pack_rows(grouped))

    def copies_of(tile):
        def make_copy(j, first):
            src = src_ref[j] if first else xsrc_ref[j]
            dst = dst_ref[j] if first else xdst_ref[j]
            return pltpu.make_async_copy(stage_ref.at[tile % 2, _tile_rows(src, DISPATCH_CHUNK)],
                                         xs_ref.at[_tile_rows(dst, DISPATCH_CHUNK)], sem)
        make_copy.max_extra = _max_extra_chunks(tt, DISPATCH_CHUNK)
        return make_copy

    @pl.when(t > 0)
    def _():
        _run_copies(extra_ref, t - 1, copies_of(t - 1), "wait")

    _run_copies(extra_ref, t, copies_of(t), "start")

    @pl.when(t == pl.num_programs(0) - 1)
    def _():
        _run_copies(extra_ref, t, copies_of(t), "wait")


def _dispatch(extra, src, dst, xsrc, xdst, fill, idx_t, rank_t, lo_col, h2, n_slots):
    n_tok, d = h2.shape
    tt = DISPATCH_TILE
    grid_spec = pltpu.PrefetchScalarGridSpec(
        num_scalar_prefetch=6,
        grid=(n_tok // tt,),
        in_specs=[pl.BlockSpec((TOP_K, tt), lambda i, *_: (0, i)),
                  pl.BlockSpec((TOP_K, tt), lambda i, *_: (0, i)),
                  pl.BlockSpec((None, N_EXPERTS, tt // SORT_TILE), lambda i, *_: (i, 0, 0)),
                  pl.BlockSpec((tt, d), lambda i, *_: (i, 0))],
        out_specs=pl.BlockSpec(memory_space=pl.ANY),
        scratch_shapes=[pltpu.VMEM((2, (TOP_K * tt + DISPATCH_CHUNK) * ROW_SUBLANES, LANES), U32),
                        pltpu.SemaphoreType.DMA],
    )
    return pl.pallas_call(
        functools.partial(_dispatch_kernel, tt=tt),
        out_shape=jax.ShapeDtypeStruct((n_slots * ROW_SUBLANES, LANES), U32),
        grid_spec=grid_spec,
        compiler_params=pltpu.CompilerParams(dimension_semantics=("arbitrary",),
                                             vmem_limit_bytes=VMEM_LIMIT_BYTES, has_side_effects=True),
        name="dispatch",
    )(extra, src, dst, xsrc, xdst, fill, idx_t, rank_t, lo_col, h2)


def _experts_kernel(be_ref, nvalid_ref, nused_ref, xs_ref, wgu32_ref, bgu_ref, wd32_ref, bd_ref, yb_ref,
                    wgu_ref, wd_ref, *, d_ff):
    i = pl.program_id(0)

    @pl.when((i == 0) | (be_ref[i] != be_ref[jnp.maximum(i - 1, 0)]))
    def _():
        wgu_ref[...] = wgu32_ref[...].astype(BF16)
        wd_ref[...] = wd32_ref[...].astype(BF16)

    @pl.when(i < nused_ref[0])
    def _():
        m = xs_ref.shape[0] // ROW_SUBLANES
        row = lax.broadcasted_iota(I32, (m, 1), 0)
        xb = _load_rows(xs_ref, m, keep=row < nvalid_ref[i])
        acc = None
        for j in range(d_ff // FF_CHUNK):
            lo, hi = j * FF_CHUNK, (j + 1) * FF_CHUNK
            g = jnp.dot(xb, wgu_ref[:, lo:hi], preferred_element_type=F32) + bgu_ref[:, lo:hi]
            u = jnp.dot(xb, wgu_ref[:, d_ff + lo:d_ff + hi], preferred_element_type=F32) + bgu_ref[:, d_ff + lo:d_ff + hi]
            g = jnp.minimum(g, SWIGLU_LIMIT)
            u = jnp.clip(u, -SWIGLU_LIMIT, SWIGLU_LIMIT)
            act = g * (1.0 / (1.0 + jnp.exp(-SWIGLU_ALPHA * g))) * (u + 1.0)
            part = jnp.dot(act.astype(BF16), wd_ref[lo:hi, :], preferred_element_type=F32)
            acc = part if acc is None else acc + part
        y = (acc + bd_ref[...]).astype(BF16).astype(F32)
        _store_rows(yb_ref, m, _pack_rows(y))

    @pl.when(i >= nused_ref[0])
    def _():
        yb_ref[...] = jnp.zeros_like(yb_ref)


def _experts(block_expert, n_valid, n_used, xs, wgu, bgu, wd, bd, *, layer):
    n_slots = xs.shape[0] // ROW_SUBLANES
    d_ff, d = wd.shape[2], wd.shape[3]
    m = EXPERT_BLOCK
    blk = lambda i, be, nv, nu: (jnp.minimum(i, nu[0] - 1), 0)
    per_expert = lambda i, be, nv, nu: (layer, be[i], 0, 0)
    grid_spec = pltpu.PrefetchScalarGridSpec(
        num_scalar_prefetch=3,
        grid=(n_slots // m,),
        in_specs=[pl.BlockSpec((m * ROW_SUBLANES, LANES), blk),
                  pl.BlockSpec((None, None, d, 2 * d_ff), per_expert),
                  pl.BlockSpec((None, None, 1, 2 * d_ff), per_expert),
                  pl.BlockSpec((None, None, d_ff, d), per_expert),
                  pl.BlockSpec((None, None, 1, d), per_expert)],
        out_specs=pl.BlockSpec((m * ROW_SUBLANES, LANES), lambda i, be, nv, nu: (i, 0)),
        scratch_shapes=[pltpu.VMEM((d, 2 * d_ff), BF16), pltpu.VMEM((d_ff, d), BF16)],
    )
    return pl.pallas_call(
        functools.partial(_experts_kernel, d_ff=d_ff),
        out_shape=jax.ShapeDtypeStruct(xs.shape, U32),
        grid_spec=grid_spec,
        compiler_params=_params("arbitrary"),
        name="experts",
    )(block_expert, n_valid, n_used, xs, wgu, bgu, wd, bd)


def _combine_kernel(extra_ref, src_ref, dst_ref, xsrc_ref, xdst_ref, route_ref, lo_row_ref,
                    x_ref, gate2_ref, yb_ref, xo_ref, stage_ref, sems, *, tt, n_stage):
    t = pl.program_id(0)
    slot = t % 2

    def copies_of(tile):
        def make_copy(j, first):
            src = src_ref[j] if first else xsrc_ref[j]
            dst = dst_ref[j] if first else xdst_ref[j]
            return pltpu.make_async_copy(yb_ref.at[_tile_rows(src, COMBINE_CHUNK)],
                                         stage_ref.at[tile % 2, _tile_rows(dst, COMBINE_CHUNK)], sems.at[tile % 2])
        make_copy.max_extra = _max_extra_chunks(tt, COMBINE_CHUNK)
        return make_copy

    @pl.when(t == 0)
    def _():
        stage_ref[...] = jnp.zeros_like(stage_ref)
        _run_copies(extra_ref, 0, copies_of(0), "start")

    @pl.when(t + 1 < pl.num_programs(0))
    def _():
        _run_copies(extra_ref, t + 1, copies_of(t + 1), "start")

    lane = lax.broadcasted_iota(I32, (tt, N_EXPERTS), 1)
    picked = []
    route = route_ref[...]
    for k in range(TOP_K):
        onehot = lane == route[:, k:k + 1].astype(I32)
        base = jnp.sum(jnp.where(onehot, lo_row_ref[...], 0.0), axis=1, keepdims=True)
        col = (base + route[:, TOP_K + k:TOP_K + k + 1]).astype(I32)
        picked.append((col, route[:, 2 * TOP_K + k:2 * TOP_K + k + 1]))

    _run_copies(extra_ref, t, copies_of(t), "wait")

    def slab_product(s):
        cols = lax.broadcasted_iota(I32, (tt, COMBINE_SLAB), 1) + s * COMBINE_SLAB
        weights = jnp.zeros((tt, COMBINE_SLAB), F32)
        for col, gate in picked:
            weights = jnp.where(cols == col, gate, weights)
        staged = _load_rows(stage_ref.at[slot], COMBINE_SLAB, start=s * COMBINE_SLAB)
        return jnp.dot(weights.astype(BF16), staged, preferred_element_type=F32)

    y = slab_product(0)
    for s in range(1, n_stage // COMBINE_SLAB):
        y = y + slab_product(s)
    xo_ref[...] = x_ref[...] + gate2_ref[...] * y


def _combine(extra, src, dst, xsrc, xdst, route, lo_row, x2, gate2, yb, *, seq):
    n_tok, d = x2.shape
    tt = SORT_TILE
    spt = seq // tt
    n_stage = TOP_K * tt + N_EXPERTS * COMBINE_CHUNK
    grid_spec = pltpu.PrefetchScalarGridSpec(
        num_scalar_prefetch=5,
        grid=(n_tok // tt,),
        in_specs=[pl.BlockSpec((tt, ROUTE_RECORD), lambda i, *_: (i, 0)),
                  pl.BlockSpec((None, 1, N_EXPERTS), lambda i, *_: (i, 0, 0)),
                  pl.BlockSpec((tt, d), lambda i, *_: (i, 0)),
                  pl.BlockSpec((None, 1, d), lambda i, *_: (i // spt, 0, 0)),
                  pl.BlockSpec(memory_space=pl.ANY)],
        out_specs=pl.BlockSpec((tt, d), lambda i, *_: (i, 0)),
        scratch_shapes=[pltpu.VMEM((2, n_stage * ROW_SUBLANES, LANES), U32),
                        pltpu.SemaphoreType.DMA((2,))],
    )
    return pl.pallas_call(
        functools.partial(_combine_kernel, tt=tt, n_stage=n_stage),
        out_shape=jax.ShapeDtypeStruct((n_tok, d), F32),
        grid_spec=grid_spec,
        compiler_params=_params("arbitrary"),
        name="combine",
    )(extra, src, dst, xsrc, xdst, route, lo_row, x2, gate2, yb)


def _exclusive_cumsum(a, axis):
    return jnp.cumsum(a, axis=axis) - a


def _region_tables(sort_cnt):
    m = EXPERT_BLOCK
    total = jnp.sum(sort_cnt, axis=0)
    padded = (total + max(DISPATCH_CHUNK, COMBINE_CHUNK) + m - 1) // m * m
    pends = jnp.cumsum(padded)
    return total, pends - padded, pends


def _chunk_tables(step_cnt, chunk, step_tokens):
    n_chunks = jnp.maximum((step_cnt + chunk - 1) // chunk, 1)
    lo_chunked = _exclusive_cumsum(n_chunks * chunk, 1)
    n_extra = n_chunks - 1
    ends = jnp.cumsum(n_extra, axis=1)
    p = jnp.arange(_max_extra_chunks(step_tokens, chunk), dtype=I32)
    x_expert = jnp.minimum(jnp.sum(p[None, :, None] >= ends[:, None, :], axis=2), N_EXPERTS - 1)
    x_offset = (p[None, :] - jnp.take_along_axis(ends - n_extra, x_expert, axis=1) + 1) * chunk
    return lo_chunked, ends[:, -1], x_expert, x_offset


def kernel(x, c, positions, ada_w, ada_b, norm1_g, w_in, q_norm_g, k_norm_g, attn_sinks, pool_w, pool_b,
           pool_scale, w_out, norm2_g, router_w, router_b, expert_w_gu, expert_b_gu, expert_w_down,
           expert_b_down):
    batch, seq, d = x.shape
    n_layers = ada_w.shape[0]
    n_tok = batch * seq
    d_ff = expert_w_down.shape[2]
    assert d == SUBLANES * LANES
    assert seq % min(TOKEN_TILE, seq) == 0 and seq % min(ATTN_TILE, seq) == 0 and ATTN_TILE % WINDOW == 0
    assert seq % ROUTE_TILE == 0 and ROUTE_TILE % SORT_TILE == 0 and DISPATCH_TILE % SORT_TILE == 0

    mod = _ada(c, ada_w, ada_b).reshape(n_layers, batch, 6, 1, d)
    rope = _rope_tables(positions)
    sel, unrot = _rope_select_matrix()
    bdq, bdk = _head_mean_matrix(MXU_TILE), _head_mean_matrix(KV_WIDTH)
    tri = jnp.asarray(np.kron(np.eye(ROUTE_TILE // SORT_TILE, dtype=np.float32),
                              np.triu(np.ones((SORT_TILE, SORT_TILE), np.float32), 1)), BF16)
    eye = jnp.asarray(np.eye(ROUTE_TILE, dtype=np.float32), BF16)

    m = EXPERT_BLOCK
    slack = max(DISPATCH_CHUNK, COMBINE_CHUNK)
    n_blocks = -(-(n_tok * TOP_K + N_EXPERTS * (slack + m - 1)) // m)
    n_slots = n_blocks * m
    n_tiles = n_tok // SORT_TILE

    x2 = x.reshape(n_tok, d)
    for l in range(n_layers):
        shift1, scale1, gate1, shift2, scale2, gate2 = (mod[l, :, j] for j in range(6))
        q, k, v, pool = _inproj(
            x2, shift1, scale1, norm1_g[l].reshape(1, d), w_in[l].astype(BF16),
            jnp.tile(q_norm_g[l], N_Q_HEADS).reshape(1, ATTN_WIDTH),
            jnp.tile(k_norm_g[l], N_KV_HEADS).reshape(1, KV_WIDTH),
            rope, sel, unrot, bdq, bdk, _pair_block_diag(pool_w[l]).astype(BF16), pool_b[l].reshape(1, POOL_WIDTH),
            pool_scale[l].reshape(1, POOL_WIDTH), batch=batch, seq=seq)
        attn = _attention(q, k, v, attn_sinks[l], seq=seq)
        w_out_b = w_out[l].astype(BF16)
        x2, h2, idx_t, rank_t, route, tile_cnt = _outproj(
            attn, pool, x2, gate1, shift2, scale2, norm2_g[l].reshape(1, d),
            w_out_b[:ATTN_WIDTH], w_out_b[ATTN_WIDTH:], router_w[l].T.astype(BF16),
            router_b[l].reshape(N_EXPERTS, 1),
            tri, eye, seq=seq)

        sort_cnt = tile_cnt.reshape(n_tiles, N_EXPERTS).astype(I32)
        total, pstart, pends = _region_tables(sort_cnt)
        flat = lambda a: a.reshape(-1).astype(I32)
        n_used = (pends[-1] // m).astype(I32).reshape(1)
        block_row = jnp.minimum(jnp.arange(n_blocks, dtype=I32), n_used[0] - 1) * m
        block_expert = jnp.minimum(jnp.sum(block_row[:, None] >= pends[None, :], axis=1), N_EXPERTS - 1).astype(I32)
        n_valid = jnp.clip((pstart + total)[block_expert] - block_row, 0, m).astype(I32)
        gstart = pstart[None, :] + _exclusive_cumsum(sort_cnt, 0)

        per = DISPATCH_TILE // SORT_TILE
        pair_cnt = sort_cnt.reshape(n_tiles // per, per, N_EXPERTS)
        step_cnt = jnp.sum(pair_cnt, axis=1)
        d_lo = _exclusive_cumsum(step_cnt, 1)
        d_gs = gstart[::per]
        _, d_extra, d_xe, d_xo = _chunk_tables(step_cnt, DISPATCH_CHUNK, DISPATCH_TILE)
        d_rows = lambda table: flat(jnp.take_along_axis(table, d_xe, axis=1) + d_xo)
        run_base = (d_lo[:, None, :] + _exclusive_cumsum(pair_cnt, 1)).transpose(0, 2, 1).astype(F32)
        fill = jnp.concatenate([pends, pends - pstart - total]).astype(I32)
        xs = _dispatch(flat(d_extra), flat(d_lo), flat(d_gs), d_rows(d_lo), d_rows(d_gs), fill,
                       idx_t, rank_t, run_base, h2, n_slots)
        yb = _experts(block_expert, n_valid, n_used, xs, expert_w_gu,
                      expert_b_gu.reshape(n_layers, N_EXPERTS, 1, 2 * d_ff), expert_w_down,
                      expert_b_down.reshape(n_layers, N_EXPERTS, 1, d), layer=l)
        c_lo, c_extra, c_xe, c_xo = _chunk_tables(sort_cnt, COMBINE_CHUNK, SORT_TILE)
        c_rows = lambda table: flat(jnp.take_along_axis(table, c_xe, axis=1) + c_xo)
        x2 = _combine(flat(c_extra), flat(gstart), flat(c_lo), c_rows(gstart), c_rows(c_lo),
                      route, c_lo.astype(F32).reshape(n_tiles, 1, N_EXPERTS), x2, gate2, yb, seq=seq)
    return x2.reshape(batch, seq, d)
```

```python
import functools

import numpy as np
import jax
import jax.numpy as jnp
from jax import lax
from jax.experimental import pallas as pl
from jax.experimental.pallas import tpu as pltpu

F32 = jnp.float32
BF16 = jnp.bfloat16
I32 = jnp.int32
U32 = jnp.uint32

HEAD_DIM = 64
N_Q_HEADS = 8
N_KV_HEADS = 2
Q_PER_KV = N_Q_HEADS // N_KV_HEADS
ATTN_WIDTH = N_Q_HEADS * HEAD_DIM
KV_WIDTH = N_KV_HEADS * HEAD_DIM
WINDOW = 128
ROT_DIM = HEAD_DIM // 4
ROT_HALF = ROT_DIM // 2
ROPE_THETA = 500000.0
POOL_WINDOWS = (2, 4, 8, 16)
N_POOL_GROUPS = len(POOL_WINDOWS)
POOL_GROUP_DIM = 128
POOL_WIDTH = N_POOL_GROUPS * POOL_GROUP_DIM
MAX_POOL_WINDOW = max(POOL_WINDOWS)
N_EXPERTS = 32
TOP_K = 4
SWIGLU_LIMIT = 7.0
SWIGLU_ALPHA = 1.702
EPS = 1e-6

LANES = 128
SUBLANES = 8
MXU_TILE = 256
VMEM_LIMIT_BYTES = 56 * 1024 * 1024

TOKEN_TILE = 1024
ATTN_TILE = 512
ROUTE_TILE = 512
SORT_TILE = 256
DISPATCH_TILE = SORT_TILE
EXPERT_BLOCK = 512
FF_CHUNK = 512
DISPATCH_CHUNK = 32
COMBINE_CHUNK = 32
COMBINE_SLAB = 512
ROUTE_RECORD = 16
FILL_CHUNK = 16


def _max_extra_chunks(step_tokens, chunk):
    return TOP_K * step_tokens // chunk


def _params(*sem):
    return pltpu.CompilerParams(dimension_semantics=sem, vmem_limit_bytes=VMEM_LIMIT_BYTES)


ROW_SUBLANES = 4
HALF_WIDTH = ROW_SUBLANES * LANES
HIGH_HALF_MASK = 0xFFFF0000


def _tile_rows(start, n):
    return pl.ds(pl.multiple_of(start * ROW_SUBLANES, ROW_SUBLANES), n * ROW_SUBLANES)


def _word_chunk(c, n, start=0):
    return pl.ds(start * ROW_SUBLANES + c, n, stride=ROW_SUBLANES)


def _pack_rows(vals):
    lo = lax.bitcast_convert_type(vals[:, :HALF_WIDTH], U32)
    hi = lax.bitcast_convert_type(vals[:, HALF_WIDTH:], U32)
    return lax.shift_right_logical(lo, jnp.uint32(16)) | hi


def _store_rows(ref, n, words):
    for c in range(ROW_SUBLANES):
        ref[_word_chunk(c, n), :] = words[:, c * LANES:(c + 1) * LANES]


def _load_rows(ref, n, keep=None, start=0):
    lo, hi = [], []
    for c in range(ROW_SUBLANES):
        w = ref[_word_chunk(c, n, start), :]
        if keep is not None:
            w = jnp.where(keep, w, jnp.uint32(0))
        lo.append(lax.bitcast_convert_type(lax.shift_left(w, jnp.uint32(16)), F32))
        hi.append(lax.bitcast_convert_type(w & jnp.uint32(HIGH_HALF_MASK), F32))
    return jnp.concatenate(lo + hi, axis=1).astype(BF16)


def _ada_kernel(c_ref, w_ref, b_ref, o_ref):
    c = c_ref[...]
    c_act = c * (1.0 / (1.0 + jnp.exp(-c)))
    o_ref[0] = jnp.dot(c_act, w_ref[0], preferred_element_type=F32,
                       precision=lax.Precision.HIGHEST) + b_ref[0]


def _ada(c, ada_w, ada_b):
    n_layers, d, d6 = ada_w.shape
    b = c.shape[0]
    tn = 1536
    return pl.pallas_call(
        _ada_kernel,
        out_shape=jax.ShapeDtypeStruct((n_layers, b, d6), F32),
        grid=(n_layers, d6 // tn),
        in_specs=[pl.BlockSpec((b, d), lambda l, j: (0, 0)),
                  pl.BlockSpec((1, d, tn), lambda l, j: (l, 0, j)),
                  pl.BlockSpec((1, 1, tn), lambda l, j: (l, 0, j))],
        out_specs=pl.BlockSpec((1, b, tn), lambda l, j: (l, 0, j)),
        compiler_params=_params("arbitrary", "arbitrary"),
        name="ada",
    )(c, ada_w, ada_b.reshape(n_layers, 1, d6))


def _rope_kernel(pos_ref, freq_ref, eye_ref, tab_ref):
    ang = pos_ref[...].astype(F32) * freq_ref[...]
    parts = []
    for fn in (jnp.cos, jnp.sin):
        val = fn(ang)
        hi = val.astype(BF16)
        parts += [hi, (val - hi.astype(F32)).astype(BF16)]
    record = jnp.concatenate(parts, axis=0)
    tab_ref[...] = lax.dot_general(eye_ref[...], record, (((1,), (1,)), ((), ())),
                                   preferred_element_type=F32).astype(BF16)


def _rope_tables(positions):
    n_tok = positions.size
    inv_freq = ROPE_THETA ** (-jnp.arange(0, ROT_DIM, 2, dtype=F32) / ROT_DIM)
    tr = min(n_tok, 512)
    eye = jnp.asarray(np.eye(tr, dtype=np.float32), BF16)
    return pl.pallas_call(
        _rope_kernel,
        out_shape=jax.ShapeDtypeStruct((n_tok, 4 * ROT_HALF), BF16),
        grid=(n_tok // tr,),
        in_specs=[pl.BlockSpec((1, tr), lambda i: (0, i)),
                  pl.BlockSpec((ROT_HALF, 1), lambda i: (0, 0)),
                  pl.BlockSpec((tr, tr), lambda i: (0, 0))],
        out_specs=pl.BlockSpec((tr, 4 * ROT_HALF), lambda i: (i, 0)),
        compiler_params=_params("arbitrary"),
        name="rope",
    )(positions.reshape(1, n_tok), inv_freq.reshape(ROT_HALF, 1), eye)


def _rope_select_matrix():
    sel = np.zeros((4 * ROT_HALF, 3 * LANES), np.float32)
    for d in range(LANES):
        dd = d % HEAD_DIM
        if dd < ROT_DIM:
            f = dd % ROT_HALF
            sel[f, d] = 1.0
            sel[ROT_HALF + f, d] = 1.0
        if dd < ROT_HALF:
            sel[2 * ROT_HALF + dd, LANES + d] = -1.0
            sel[3 * ROT_HALF + dd, LANES + d] = -1.0
        elif dd < ROT_DIM:
            sel[2 * ROT_HALF + dd - ROT_HALF, 2 * LANES + d] = 1.0
            sel[3 * ROT_HALF + dd - ROT_HALF, 2 * LANES + d] = 1.0
    unrotated = np.array([[1.0 if (d % HEAD_DIM) >= ROT_DIM else 0.0 for d in range(LANES)]], np.float32)
    return jnp.asarray(sel, BF16), jnp.asarray(unrotated, F32)


def _pair_block_diag(w):
    g, c, _ = w.shape
    zero = jnp.zeros((g // 2, c, c), w.dtype)
    top = jnp.concatenate([w[0::2], zero], axis=2)
    bottom = jnp.concatenate([zero, w[1::2]], axis=2)
    return jnp.concatenate([top, bottom], axis=1)


def _head_mean_matrix(width):
    m = np.zeros((width, width), np.float32)
    for h in range(width // HEAD_DIM):
        m[h * HEAD_DIM:(h + 1) * HEAD_DIM, h * HEAD_DIM:(h + 1) * HEAD_DIM] = 1.0 / HEAD_DIM
    return jnp.asarray(m, BF16)


def _inproj_kernel(x_ref, shift_ref, scale_ref, g1_ref, w_ref, qg_ref, kg_ref, rope_ref, sel_ref,
                   unrot_ref, bdq_ref, bdk_ref, wp_ref, bp_ref, ps_ref,
                   q_ref, k_ref, v_ref, pool_ref, carry_ref, *, ts):
    s_idx = pl.program_id(1)

    @pl.when(s_idx == 0)
    def _():
        carry_ref[...] = jnp.zeros_like(carry_ref)

    x = x_ref[...]
    ms = jnp.mean(x * x, axis=-1, keepdims=True)
    h = x * lax.rsqrt(ms + EPS) * (g1_ref[...] * (1.0 + scale_ref[...])) + shift_ref[...]
    hb = h.astype(BF16)

    tab = jnp.dot(rope_ref[...], sel_ref[...], preferred_element_type=F32)
    cos_t = tab[:, :LANES] + unrot_ref[...]
    sin_a = tab[:, LANES:2 * LANES]
    sin_b = tab[:, 2 * LANES:]

    def norm_rope(t, bd_ref, g_ref, out_scale):
        bw = bd_ref.shape[0]
        sq = (t * t).astype(BF16)
        msq = [jnp.dot(sq[:, i * bw:(i + 1) * bw], bd_ref[...], preferred_element_type=F32)
               for i in range(t.shape[1] // bw)]
        msq = msq[0] if len(msq) == 1 else jnp.concatenate(msq, axis=1)
        tn = t * lax.rsqrt(msq + EPS) * g_ref[...]
        chunks = []
        for c in range(t.shape[1] // LANES):
            tc = tn[:, c * LANES:(c + 1) * LANES]
            rot = (tc * cos_t + pltpu.roll(tc, LANES - ROT_HALF, 1) * sin_a
                   + pltpu.roll(tc, ROT_HALF, 1) * sin_b)
            chunks.append(rot * out_scale)
        return chunks[0] if len(chunks) == 1 else jnp.concatenate(chunks, axis=1)

    q = jnp.dot(hb, w_ref[:, :ATTN_WIDTH], preferred_element_type=F32)
    q_ref[...] = norm_rope(q, bdq_ref, qg_ref, HEAD_DIM ** -0.5).astype(BF16)
    k = jnp.dot(hb, w_ref[:, ATTN_WIDTH:ATTN_WIDTH + KV_WIDTH], preferred_element_type=F32)
    k_ref[...] = norm_rope(k, bdk_ref, kg_ref, 1.0).astype(BF16)
    v = jnp.dot(hb, w_ref[:, ATTN_WIDTH + KV_WIDTH:ATTN_WIDTH + 2 * KV_WIDTH], preferred_element_type=F32)
    v_ref[...] = v.astype(BF16)

    u = jnp.dot(hb, w_ref[:, ATTN_WIDTH + 2 * KV_WIDTH:], preferred_element_type=F32)
    ext = jnp.concatenate([carry_ref[...], u], axis=0)
    carry_ref[...] = u[ts - MAX_POOL_WINDOW:, :]
    count = (lax.broadcasted_iota(I32, (ts, POOL_GROUP_DIM), 0) + (s_idx * ts + 1)).astype(F32)
    outs = []
    for g, w in enumerate(POOL_WINDOWS):
        win = ext[:, g * POOL_GROUP_DIM:(g + 1) * POOL_GROUP_DIM]
        span = 1
        while span < w:
            win = win + pltpu.roll(win, span, 0)
            span *= 2
        mean = win[MAX_POOL_WINDOW:, :] * (1.0 / jnp.minimum(count, float(w)))
        outs.append((mean - u[:, g * POOL_GROUP_DIM:(g + 1) * POOL_GROUP_DIM]).astype(BF16))
    pooled = jnp.concatenate(outs, axis=1)
    pw = wp_ref.shape[1]
    mapped = [jnp.dot(pooled[:, i * pw:(i + 1) * pw], wp_ref[i], preferred_element_type=F32)
              for i in range(wp_ref.shape[0])]
    y = (jnp.concatenate(mapped, axis=1) + bp_ref[...]) * ps_ref[...]
    pool_ref[...] = y.astype(BF16)


def _inproj(x2, shift1, scale1, g1, w_in, qg, kg, rope, sel, unrot, bdq, bdk, wp, bp, ps, *, batch, seq):
    n_tok, d = x2.shape
    ts = min(TOKEN_TILE, seq)
    spt = seq // ts
    row = lambda width: pl.BlockSpec((ts, width), lambda b, s: (b * spt + s, 0))
    per_batch = pl.BlockSpec((None, 1, d), lambda b, s: (b, 0, 0))
    full = lambda a: pl.BlockSpec(a.shape, lambda b, s: (0,) * a.ndim)
    return pl.pallas_call(
        functools.partial(_inproj_kernel, ts=ts),
        out_shape=[jax.ShapeDtypeStruct((n_tok, ATTN_WIDTH), BF16),
                   jax.ShapeDtypeStruct((n_tok, KV_WIDTH), BF16),
                   jax.ShapeDtypeStruct((n_tok, KV_WIDTH), BF16),
                   jax.ShapeDtypeStruct((n_tok, POOL_WIDTH), BF16)],
        grid=(batch, spt),
        in_specs=[row(d), per_batch, per_batch, full(g1), full(w_in), full(qg), full(kg),
                  row(rope.shape[1]), full(sel), full(unrot), full(bdq), full(bdk),
                  full(wp), full(bp), full(ps)],
        out_specs=[row(ATTN_WIDTH), row(KV_WIDTH), row(KV_WIDTH), row(POOL_WIDTH)],
        scratch_shapes=[pltpu.VMEM((MAX_POOL_WINDOW, POOL_WIDTH), F32)],
        compiler_params=_params("arbitrary", "arbitrary"),
        name="inproj",
    )(x2, shift1, scale1, g1, w_in, qg, kg, rope, sel, unrot, bdq, bdk, wp, bp, ps)


def _attn_kernel(sink_ref, q_ref, kc_ref, vc_ref, kp_ref, vp_ref, o_ref, *, tq, tiles_per_seq):
    first = (pl.program_id(0) % tiles_per_seq) == 0
    kfull = jnp.concatenate([kp_ref[...], kc_ref[...]], axis=0)
    vfull = jnp.concatenate([vp_ref[...], vc_ref[...]], axis=0)
    rows = Q_PER_KV * WINDOW
    qi = lax.broadcasted_iota(I32, (rows, WINDOW), 0) % WINDOW
    from_prev = lax.broadcasted_iota(I32, (rows, WINDOW), 1) > qi
    row_head = lax.broadcasted_iota(I32, (rows, 1), 0) // WINDOW
    sinks = []
    for h in range(N_KV_HEADS):
        sink = jnp.zeros((rows, 1), F32)
        for g in range(Q_PER_KV):
            sink = jnp.where(row_head == g, sink_ref[h * Q_PER_KV + g], sink)
        sinks.append(sink)

    problems = [(j, h) for j in range(tq // WINDOW) for h in range(N_KV_HEADS)]
    scores, maxes = [], []
    for j, h in problems:
        qs = jnp.concatenate(
            [q_ref[j * WINDOW:(j + 1) * WINDOW, (h * Q_PER_KV + g) * HEAD_DIM:(h * Q_PER_KV + g + 1) * HEAD_DIM]
             for g in range(Q_PER_KV)], axis=0)
        kb = kfull[j * WINDOW:(j + 2) * WINDOW, h * HEAD_DIM:(h + 1) * HEAD_DIM]
        s = lax.dot_general(qs, kb, (((1,), (1,)), ((), ())), preferred_element_type=F32)
        s_prev = s[:, :WINDOW]
        if j == 0:
            s_prev = jnp.where(first, -jnp.inf, s_prev)
        s = jnp.where(from_prev, s_prev, s[:, WINDOW:])
        scores.append(s)
        maxes.append(jnp.maximum(jnp.max(s, axis=-1, keepdims=True), sinks[h]))
    probs, denoms = [], []
    for (j, h), s, m in zip(problems, scores, maxes):
        p = jnp.exp(s - m)
        denoms.append(jnp.sum(p, axis=-1, keepdims=True) + jnp.exp(sinks[h] - m))
        p = p.astype(BF16)
        zero = jnp.zeros_like(p)
        probs.append(jnp.concatenate([jnp.where(from_prev, p, zero), jnp.where(from_prev, zero, p)], axis=1))
    for (j, h), p, denom in zip(problems, probs, denoms):
        vb = vfull[j * WINDOW:(j + 2) * WINDOW, h * HEAD_DIM:(h + 1) * HEAD_DIM]
        o = jnp.dot(p, vb, preferred_element_type=F32) * (1.0 / denom)
        o_ref[j * WINDOW:(j + 1) * WINDOW, h * Q_PER_KV * HEAD_DIM:(h + 1) * Q_PER_KV * HEAD_DIM] = (
            jnp.concatenate([o[g * WINDOW:(g + 1) * WINDOW, :] for g in range(Q_PER_KV)], axis=1).astype(BF16))


def _attention(q, k, v, sinks, *, seq):
    n_tok = q.shape[0]
    tq = min(ATTN_TILE, seq)
    per = tq // WINDOW
    cur = lambda width: pl.BlockSpec((tq, width), lambda i: (i, 0))
    prev = pl.BlockSpec((WINDOW, KV_WIDTH), lambda i: (jnp.maximum(i * per - 1, 0), 0))
    return pl.pallas_call(
        functools.partial(_attn_kernel, tq=tq, tiles_per_seq=seq // tq),
        out_shape=jax.ShapeDtypeStruct((n_tok, ATTN_WIDTH), BF16),
        grid=(n_tok // tq,),
        in_specs=[pl.BlockSpec(memory_space=pltpu.SMEM), cur(ATTN_WIDTH), cur(KV_WIDTH), cur(KV_WIDTH), prev, prev],
        out_specs=cur(ATTN_WIDTH),
        compiler_params=_params("arbitrary"),
        name="attn",
    )(sinks, q, k, v, k, v)


def _outproj_kernel(attn_ref, pool_ref, x_ref, gate1_ref, shift2_ref, scale2_ref, g2_ref, wa_ref, wpo_ref,
                    rw_ref, rb_ref, tri_ref, eye_ref,
                    xo_ref, h2_ref, idx_ref, rank_ref, route_ref, cnt_ref):
    contract_last = (((1,), (1,)), ((), ()))
    slabs = [pl.ds(r, SORT_TILE) for r in range(0, x_ref.shape[0], SORT_TILE)]
    mixed = [jnp.dot(attn_ref[rows, :], wa_ref[...], preferred_element_type=F32)
             + jnp.dot(pool_ref[rows, :], wpo_ref[...], preferred_element_type=F32) for rows in slabs]
    xs = [x_ref[rows, :] + gate1_ref[...] * m for rows, m in zip(slabs, mixed)]
    for rows, x in zip(slabs, xs):
        xo_ref[rows, :] = x
    norm_gain = g2_ref[...] * (1.0 + scale2_ref[...])
    h2bs = []
    for rows, x in zip(slabs, xs):
        ms = jnp.mean(x * x, axis=-1, keepdims=True)
        h2b = (x * lax.rsqrt(ms + EPS) * norm_gain + shift2_ref[...]).astype(BF16)
        h2_ref[rows, :] = h2b
        h2bs.append(h2b)

    logits = jnp.concatenate([lax.dot_general(rw_ref[...], h2b, contract_last, preferred_element_type=F32)
                              for h2b in h2bs], axis=1) + rb_ref[...]
    expert = lax.broadcasted_iota(I32, logits.shape, 0)
    work = logits
    vals, picks, chosen = [], [], []
    for _ in range(TOP_K):
        m = jnp.max(work, axis=0, keepdims=True)
        pick = jnp.min(jnp.where(work == m, expert, N_EXPERTS), axis=0, keepdims=True)
        sel = expert == pick
        vals.append(m)
        picks.append(pick)
        chosen.append(sel)
        work = jnp.where(sel, -jnp.inf, work)
    exps = [jnp.exp(v - vals[0]) for v in vals]
    inv_total = 1.0 / (exps[0] + exps[1] + exps[2] + exps[3])

    multi = jnp.zeros(logits.shape, F32)
    for sel in chosen:
        multi = multi + sel.astype(F32)
    multi_b = multi.astype(BF16)
    before = jnp.dot(multi_b, tri_ref[...], preferred_element_type=F32)
    ranks = [jnp.sum(jnp.where(sel, before, 0.0), axis=0, keepdims=True) for sel in chosen]
    for kk in range(TOP_K):
        idx_ref[kk:kk + 1, :] = picks[kk]
        rank_ref[kk:kk + 1, :] = ranks[kk].astype(I32)
    record = jnp.concatenate([p.astype(F32) for p in picks] + ranks + [e * inv_total for e in exps]
                             + [jnp.zeros((ROUTE_RECORD - 3 * TOP_K, logits.shape[1]), F32)], axis=0)
    route_ref[...] = lax.dot_general(eye_ref[...], record.astype(BF16), contract_last, preferred_element_type=F32)
    ones = jnp.ones((SUBLANES, SORT_TILE), BF16)
    for s in range(multi.shape[1] // SORT_TILE):
        per_expert = lax.dot_general(ones, multi_b[:, s * SORT_TILE:(s + 1) * SORT_TILE], contract_last,
                                     preferred_element_type=F32)
        cnt_ref[s] = per_expert[0:1, :]


def _outproj(attn, pool, x2, gate1, shift2, scale2, g2, wa, wpo, rw, rb, tri, eye, *, seq):
    n_tok, d = x2.shape
    ts = tri.shape[0]
    spt = seq // ts
    row = lambda width: pl.BlockSpec((ts, width), lambda i: (i, 0))
    per_token = pl.BlockSpec((TOP_K, ts), lambda i: (0, i))
    per_batch = pl.BlockSpec((None, 1, d), lambda i: (i // spt, 0, 0))
    full = lambda a: pl.BlockSpec(a.shape, lambda i: (0,) * a.ndim)
    return pl.pallas_call(
        _outproj_kernel,
        out_shape=[jax.ShapeDtypeStruct((n_tok, d), F32),
                   jax.ShapeDtypeStruct((n_tok, d), BF16),
                   jax.ShapeDtypeStruct((TOP_K, n_tok), I32),
                   jax.ShapeDtypeStruct((TOP_K, n_tok), I32),
                   jax.ShapeDtypeStruct((n_tok, ROUTE_RECORD), F32),
                   jax.ShapeDtypeStruct((n_tok // SORT_TILE, 1, N_EXPERTS), F32)],
        grid=(n_tok // ts,),
        in_specs=[row(ATTN_WIDTH), row(POOL_WIDTH), row(d), per_batch, per_batch, per_batch, full(g2),
                  full(wa), full(wpo), full(rw), full(rb), full(tri), full(eye)],
        out_specs=[row(d), row(d), per_token, per_token, row(ROUTE_RECORD),
                   pl.BlockSpec((ts // SORT_TILE, 1, N_EXPERTS), lambda i: (i, 0, 0))],
        compiler_params=_params("arbitrary"),
        name="outproj",
    )(attn, pool, x2, gate1, shift2, scale2, g2, wa, wpo, rw, rb, tri, eye)


def _chunk_loops(cnt_ref, tile, chunk, body):
    def per_expert(e, carry):
        n_chunks = (cnt_ref[tile * N_EXPERTS + e] + chunk - 1) // chunk

        def per_chunk(i, c):
            body(e, i)
            return c

        return lax.fori_loop(0, n_chunks, per_chunk, carry)

    lax.fori_loop(0, N_EXPERTS, per_expert, 0)


def _run_copies(extra_ref, tile, make_copy, act):
    for e in range(N_EXPERTS):
        getattr(make_copy(tile * N_EXPERTS + e, True), act)()

    def per_extra(p, carry):
        getattr(make_copy(tile * make_copy.max_extra + p, False), act)()
        return carry

    lax.fori_loop(0, extra_ref[tile], per_extra, 0)


def _dispatch_kernel(extra_ref, src_ref, dst_ref, xsrc_ref, xdst_ref, fill_ref, idx_t_ref, rank_t_ref, lo_col_ref,
                     h_ref, xs_ref, stage_ref, sem, *, tt):
    t = pl.program_id(0)
    slot = t % 2
    n_rows = TOP_K * tt
    u = FILL_CHUNK
    m = EXPERT_BLOCK

    @pl.when(t == 0)
    def _():
        stage_ref[...] = jnp.zeros_like(stage_ref)

        def pad_copy(e, i):
            return pltpu.make_async_copy(stage_ref.at[1, _tile_rows(0, u)],
                                         xs_ref.at[_tile_rows(fill_ref[e] - (i + 1) * u, u)], sem)

        def tail_copy(i):
            return pltpu.make_async_copy(stage_ref.at[1, _tile_rows(0, m)],
                                         xs_ref.at[_tile_rows(fill_ref[N_EXPERTS - 1] + i * m, m)], sem)

        n_tail = (xs_ref.shape[0] // ROW_SUBLANES - fill_ref[N_EXPERTS - 1]) // m
        for act in ("start", "wait"):
            _chunk_loops(fill_ref, 1, u, lambda e, i: getattr(pad_copy(e, i), act)())
            lax.fori_loop(0, n_tail, lambda i, c: (getattr(tail_copy(i), act)(), c)[1], 0)

    expert = lax.broadcasted_iota(I32, (N_EXPERTS, tt), 0)
    token = lax.broadcasted_iota(I32, (N_EXPERTS, tt), 1)
    rows = lax.broadcasted_iota(I32, (n_rows, tt), 0)
    run_base = jnp.broadcast_to(lo_col_ref[:, 0:1], (N_EXPERTS, tt))
    for s in range(1, tt // SORT_TILE):
        run_base = jnp.where(token >= s * SORT_TILE, lo_col_ref[:, s:s + 1], run_base)
    hit = None
    for k in range(TOP_K):
        onehot = expert == idx_t_ref[k:k + 1, :]
        base = jnp.sum(jnp.where(onehot, run_base, 0.0), axis=0, keepdims=True)
        slot_row = base.astype(I32) + rank_t_ref[k:k + 1, :]
        match = rows == slot_row
        hit = match if hit is None else (hit | match)
    perm = jnp.where(hit, 1.0, 0.0).astype(BF16)
    grouped = jnp.dot(perm, h_ref[...], preferred_element_type=F32)
    _store_rows(stage_ref.at[slot], n_rows, _pack_rows(grouped))

    def copies_of(tile):
        def make_copy(j, first):
            src = src_ref[j] if first else xsrc_ref[j]
            dst = dst_ref[j] if first else xdst_ref[j]
            return pltpu.make_async_copy(stage_ref.at[tile % 2, _tile_rows(src, DISPATCH_CHUNK)],
                                         xs_ref.at[_tile_rows(dst, DISPATCH_CHUNK)], sem)
        make_copy.max_extra = _max_extra_chunks(tt, DISPATCH_CHUNK)
        return make_copy

    @pl.when(t > 0)
    def _():
        _run_copies(extra_ref, t - 1, copies_of(t - 1), "wait")

    _run_copies(extra_ref, t, copies_of(t), "start")

    @pl.when(t == pl.num_programs(0) - 1)
    def _():
        _run_copies(extra_ref, t, copies_of(t), "wait")


def _dispatch(extra, src, dst, xsrc, xdst, fill, idx_t, rank_t, lo_col, h2, n_slots):
    n_tok, d = h2.shape
    tt = DISPATCH_TILE
    grid_spec = pltpu.PrefetchScalarGridSpec(
        num_scalar_prefetch=6,
        grid=(n_tok // tt,),
        in_specs=[pl.BlockSpec((TOP_K, tt), lambda i, *_: (0, i)),
                  pl.BlockSpec((TOP_K, tt), lambda i, *_: (0, i)),
                  pl.BlockSpec((None, N_EXPERTS, tt // SORT_TILE), lambda i, *_: (i, 0, 0)),
                  pl.BlockSpec((tt, d), lambda i, *_: (i, 0))],
        out_specs=pl.BlockSpec(memory_space=pl.ANY),
        scratch_shapes=[pltpu.VMEM((2, (TOP_K * tt + DISPATCH_CHUNK) * ROW_SUBLANES, LANES), U32),
                        pltpu.SemaphoreType.DMA],
    )
    return pl.pallas_call(
        functools.partial(_dispatch_kernel, tt=tt),
        out_shape=jax.ShapeDtypeStruct((n_slots * ROW_SUBLANES, LANES), U32),
        grid_spec=grid_spec,
        compiler_params=pltpu.CompilerParams(dimension_semantics=("arbitrary",),
                                             vmem_limit_bytes=VMEM_LIMIT_BYTES, has_side_effects=True),
        name="dispatch",
    )(extra, src, dst, xsrc, xdst, fill, idx_t, rank_t, lo_col, h2)


def _experts_kernel(be_ref, nvalid_ref, nused_ref, xs_ref, wgu32_ref, bgu_ref, wd32_ref, bd_ref, yb_ref,
                    wgu_ref, wd_ref, *, d_ff):
    i = pl.program_id(0)

    @pl.when((i == 0) | (be_ref[i] != be_ref[jnp.maximum(i - 1, 0)]))
    def _():
        wgu_ref[...] = wgu32_ref[...].astype(BF16)
        wd_ref[...] = wd32_ref[...].astype(BF16)

    @pl.when(i < nused_ref[0])
    def _():
        m = xs_ref.shape[0] // ROW_SUBLANES
        row = lax.broadcasted_iota(I32, (m, 1), 0)
        xb = _load_rows(xs_ref, m, keep=row < nvalid_ref[i])
        acc = None
        for j in range(d_ff // FF_CHUNK):
            lo, hi = j * FF_CHUNK, (j + 1) * FF_CHUNK
            g = jnp.dot(xb, wgu_ref[:, lo:hi], preferred_element_type=F32) + bgu_ref[:, lo:hi]
            u = jnp.dot(xb, wgu_ref[:, d_ff + lo:d_ff + hi], preferred_element_type=F32) + bgu_ref[:, d_ff + lo:d_ff + hi]
            g = jnp.minimum(g, SWIGLU_LIMIT)
            u = jnp.clip(u, -SWIGLU_LIMIT, SWIGLU_LIMIT)
            act = g * (1.0 / (1.0 + jnp.exp(-SWIGLU_ALPHA * g))) * (u + 1.0)
            part = jnp.dot(act.astype(BF16), wd_ref[lo:hi, :], preferred_element_type=F32)
            acc = part if acc is None else acc + part
        y = (acc + bd_ref[...]).astype(BF16).astype(F32)
        _store_rows(yb_ref, m, _pack_rows(y))

    @pl.when(i >= nused_ref[0])
    def _():
        yb_ref[...] = jnp.zeros_like(yb_ref)


def _experts(block_expert, n_valid, n_used, xs, wgu, bgu, wd, bd, *, layer):
    n_slots = xs.shape[0] // ROW_SUBLANES
    d_ff, d = wd.shape[2], wd.shape[3]
    m = EXPERT_BLOCK
    blk = lambda i, be, nv, nu: (jnp.minimum(i, nu[0] - 1), 0)
    per_expert = lambda i, be, nv, nu: (layer, be[i], 0, 0)
    grid_spec = pltpu.PrefetchScalarGridSpec(
        num_scalar_prefetch=3,
        grid=(n_slots // m,),
        in_specs=[pl.BlockSpec((m * ROW_SUBLANES, LANES), blk),
                  pl.BlockSpec((None, None, d, 2 * d_ff), per_expert),
                  pl.BlockSpec((None, None, 1, 2 * d_ff), per_expert),
                  pl.BlockSpec((None, None, d_ff, d), per_expert),
                  pl.BlockSpec((None, None, 1, d), per_expert)],
        out_specs=pl.BlockSpec((m * ROW_SUBLANES, LANES), lambda i, be, nv, nu: (i, 0)),
        scratch_shapes=[pltpu.VMEM((d, 2 * d_ff), BF16), pltpu.VMEM((d_ff, d), BF16)],
    )
    return pl.pallas_call(
        functools.partial(_experts_kernel, d_ff=d_ff),
        out_shape=jax.ShapeDtypeStruct(xs.shape, U32),
        grid_spec=grid_spec,
        compiler_params=_params("arbitrary"),
        name="experts",
    )(block_expert, n_valid, n_used, xs, wgu, bgu, wd, bd)


def _combine_kernel(extra_ref, src_ref, dst_ref, xsrc_ref, xdst_ref, route_ref, lo_row_ref,
                    x_ref, gate2_ref, yb_ref, xo_ref, stage_ref, sems, *, tt, n_stage):
    t = pl.program_id(0)
    slot = t % 2

    def copies_of(tile):
        def make_copy(j, first):
            src = src_ref[j] if first else xsrc_ref[j]
            dst = dst_ref[j] if first else xdst_ref[j]
            return pltpu.make_async_copy(yb_ref.at[_tile_rows(src, COMBINE_CHUNK)],
                                         stage_ref.at[tile % 2, _tile_rows(dst, COMBINE_CHUNK)], sems.at[tile % 2])
        make_copy.max_extra = _max_extra_chunks(tt, COMBINE_CHUNK)
        return make_copy

    @pl.when(t == 0)
    def _():
        stage_ref[...] = jnp.zeros_like(stage_ref)
        _run_copies(extra_ref, 0, copies_of(0), "start")

    @pl.when(t + 1 < pl.num_programs(0))
    def _():
        _run_copies(extra_ref, t + 1, copies_of(t + 1), "start")

    lane = lax.broadcasted_iota(I32, (tt, N_EXPERTS), 1)
    picked = []
    route = route_ref[...]
    for k in range(TOP_K):
        onehot = lane == route[:, k:k + 1].astype(I32)
        base = jnp.sum(jnp.where(onehot, lo_row_ref[...], 0.0), axis=1, keepdims=True)
        col = (base + route[:, TOP_K + k:TOP_K + k + 1]).astype(I32)
        picked.append((col, route[:, 2 * TOP_K + k:2 * TOP_K + k + 1]))

    _run_copies(extra_ref, t, copies_of(t), "wait")

    def slab_product(s):
        cols = lax.broadcasted_iota(I32, (tt, COMBINE_SLAB), 1) + s * COMBINE_SLAB
        weights = jnp.zeros((tt, COMBINE_SLAB), F32)
        for col, gate in picked:
            weights = jnp.where(cols == col, gate, weights)
        staged = _load_rows(stage_ref.at[slot], COMBINE_SLAB, start=s * COMBINE_SLAB)
        return jnp.dot(weights.astype(BF16), staged, preferred_element_type=F32)

    y = slab_product(0)
    for s in range(1, n_stage // COMBINE_SLAB):
        y = y + slab_product(s)
    xo_ref[...] = x_ref[...] + gate2_ref[...] * y


def _combine(extra, src, dst, xsrc, xdst, route, lo_row, x2, gate2, yb, *, seq):
    n_tok, d = x2.shape
    tt = SORT_TILE
    spt = seq // tt
    n_stage = TOP_K * tt + N_EXPERTS * COMBINE_CHUNK
    grid_spec = pltpu.PrefetchScalarGridSpec(
        num_scalar_prefetch=5,
        grid=(n_tok // tt,),
        in_specs=[pl.BlockSpec((tt, ROUTE_RECORD), lambda i, *_: (i, 0)),
                  pl.BlockSpec((None, 1, N_EXPERTS), lambda i, *_: (i, 0, 0)),
                  pl.BlockSpec((tt, d), lambda i, *_: (i, 0)),
                  pl.BlockSpec((None, 1, d), lambda i, *_: (i // spt, 0, 0)),
                  pl.BlockSpec(memory_space=pl.ANY)],
        out_specs=pl.BlockSpec((tt, d), lambda i, *_: (i, 0)),
        scratch_shapes=[pltpu.VMEM((2, n_stage * ROW_SUBLANES, LANES), U32),
                        pltpu.SemaphoreType.DMA((2,))],
    )
    return pl.pallas_call(
        functools.partial(_combine_kernel, tt=tt, n_stage=n_stage),
        out_shape=jax.ShapeDtypeStruct((n_tok, d), F32),
        grid_spec=grid_spec,
        compiler_params=_params("arbitrary"),
        name="combine",
    )(extra, src, dst, xsrc, xdst, route, lo_row, x2, gate2, yb)


def _exclusive_cumsum(a, axis):
    return jnp.cumsum(a, axis=axis) - a


def _region_tables(sort_cnt):
    m = EXPERT_BLOCK
    total = jnp.sum(sort_cnt, axis=0)
    padded = (total + max(DISPATCH_CHUNK, COMBINE_CHUNK) + m - 1) // m * m
    pends = jnp.cumsum(padded)
    return total, pends - padded, pends


def _chunk_tables(step_cnt, chunk, step_tokens):
    n_chunks = jnp.maximum((step_cnt + chunk - 1) // chunk, 1)
    lo_chunked = _exclusive_cumsum(n_chunks * chunk, 1)
    n_extra = n_chunks - 1
    ends = jnp.cumsum(n_extra, axis=1)
    p = jnp.arange(_max_extra_chunks(step_tokens, chunk), dtype=I32)
    x_expert = jnp.minimum(jnp.sum(p[None, :, None] >= ends[:, None, :], axis=2), N_EXPERTS - 1)
    x_offset = (p[None, :] - jnp.take_along_axis(ends - n_extra, x_expert, axis=1) + 1) * chunk
    return lo_chunked, ends[:, -1], x_expert, x_offset


def kernel(x, c, positions, ada_w, ada_b, norm1_g, w_in, q_norm_g, k_norm_g, attn_sinks, pool_w, pool_b,
           pool_scale, w_out, norm2_g, router_w, router_b, expert_w_gu, expert_b_gu, expert_w_down,
           expert_b_down):
    batch, seq, d = x.shape
    n_layers = ada_w.shape[0]
    n_tok = batch * seq
    d_ff = expert_w_down.shape[2]
    assert d == SUBLANES * LANES
    assert seq % min(TOKEN_TILE, seq) == 0 and seq % min(ATTN_TILE, seq) == 0 and ATTN_TILE % WINDOW == 0
    assert seq % ROUTE_TILE == 0 and ROUTE_TILE % SORT_TILE == 0 and DISPATCH_TILE % SORT_TILE == 0

    mod = _ada(c, ada_w, ada_b).reshape(n_layers, batch, 6, 1, d)
    rope = _rope_tables(positions)
    sel, unrot = _rope_select_matrix()
    bdq, bdk = _head_mean_matrix(MXU_TILE), _head_mean_matrix(KV_WIDTH)
    tri = jnp.asarray(np.kron(np.eye(ROUTE_TILE // SORT_TILE, dtype=np.float32),
                              np.triu(np.ones((SORT_TILE, SORT_TILE), np.float32), 1)), BF16)
    eye = jnp.asarray(np.eye(ROUTE_TILE, dtype=np.float32), BF16)

    m = EXPERT_BLOCK
    slack = max(DISPATCH_CHUNK, COMBINE_CHUNK)
    n_blocks = -(-(n_tok * TOP_K + N_EXPERTS * (slack + m - 1)) // m)
    n_slots = n_blocks * m
    n_tiles = n_tok // SORT_TILE

    x2 = x.reshape(n_tok, d)
    for l in range(n_layers):
        shift1, scale1, gate1, shift2, scale2, gate2 = (mod[l, :, j] for j in range(6))
        q, k, v, pool = _inproj(
            x2, shift1, scale1, norm1_g[l].reshape(1, d), w_in[l].astype(BF16),
            jnp.tile(q_norm_g[l], N_Q_HEADS).reshape(1, ATTN_WIDTH),
            jnp.tile(k_norm_g[l], N_KV_HEADS).reshape(1, KV_WIDTH),
            rope, sel, unrot, bdq, bdk, _pair_block_diag(pool_w[l]).astype(BF16), pool_b[l].reshape(1, POOL_WIDTH),
            pool_scale[l].reshape(1, POOL_WIDTH), batch=batch, seq=seq)
        attn = _attention(q, k, v, attn_sinks[l], seq=seq)
        w_out_b = w_out[l].astype(BF16)
        x2, h2, idx_t, rank_t, route, tile_cnt = _outproj(
            attn, pool, x2, gate1, shift2, scale2, norm2_g[l].reshape(1, d),
            w_out_b[:ATTN_WIDTH], w_out_b[ATTN_WIDTH:], router_w[l].T.astype(BF16),
            router_b[l].reshape(N_EXPERTS, 1),
            tri, eye, seq=seq)

        sort_cnt = tile_cnt.reshape(n_tiles, N_EXPERTS).astype(I32)
        total, pstart, pends = _region_tables(sort_cnt)
        flat = lambda a: a.reshape(-1).astype(I32)
        n_used = (pends[-1] // m).astype(I32).reshape(1)
        block_row = jnp.minimum(jnp.arange(n_blocks, dtype=I32), n_used[0] - 1) * m
        block_expert = jnp.minimum(jnp.sum(block_row[:, None] >= pends[None, :], axis=1), N_EXPERTS - 1).astype(I32)
        n_valid = jnp.clip((pstart + total)[block_expert] - block_row, 0, m).astype(I32)
        gstart = pstart[None, :] + _exclusive_cumsum(sort_cnt, 0)

        per = DISPATCH_TILE // SORT_TILE
        pair_cnt = sort_cnt.reshape(n_tiles // per, per, N_EXPERTS)
        step_cnt = jnp.sum(pair_cnt, axis=1)
        d_lo = _exclusive_cumsum(step_cnt, 1)
        d_gs = gstart[::per]
        _, d_extra, d_xe, d_xo = _chunk_tables(step_cnt, DISPATCH_CHUNK, DISPATCH_TILE)
        d_rows = lambda table: flat(jnp.take_along_axis(table, d_xe, axis=1) + d_xo)
        run_base = (d_lo[:, None, :] + _exclusive_cumsum(pair_cnt, 1)).transpose(0, 2, 1).astype(F32)
        fill = jnp.concatenate([pends, pends - pstart - total]).astype(I32)
        xs = _dispatch(flat(d_extra), flat(d_lo), flat(d_gs), d_rows(d_lo), d_rows(d_gs), fill,
                       idx_t, rank_t, run_base, h2, n_slots)
        yb = _experts(block_expert, n_valid, n_used, xs, expert_w_gu,
                      expert_b_gu.reshape(n_layers, N_EXPERTS, 1, 2 * d_ff), expert_w_down,
                      expert_b_down.reshape(n_layers, N_EXPERTS, 1, d), layer=l)
        c_lo, c_extra, c_xe, c_xo = _chunk_tables(sort_cnt, COMBINE_CHUNK, SORT_TILE)
        c_rows = lambda table: flat(jnp.take_along_axis(table, c_xe, axis=1) + c_xo)
        x2 = _combine(flat(c_extra), flat(gstart), flat(c_lo), c_rows(gstart), c_rows(c_lo),
                      route, c_lo.astype(F32).reshape(n_tiles, 1, N_EXPERTS), x2, gate2, yb, seq=seq)
    return x2.reshape(batch, seq, d)
```

```python
import functools

import numpy as np
import jax
import jax.numpy as jnp
from jax import lax
from jax.experimental import pallas as pl
from jax.experimental.pallas import tpu as pltpu

F32 = jnp.float32
BF16 = jnp.bfloat16
I32 = jnp.int32
U32 = jnp.uint32

HEAD_DIM = 64
N_Q_HEADS = 8
N_KV_HEADS = 2
Q_PER_KV = N_Q_HEADS // N_KV_HEADS
ATTN_WIDTH = N_Q_HEADS * HEAD_DIM
KV_WIDTH = N_KV_HEADS * HEAD_DIM
WINDOW = 128
ROT_DIM = HEAD_DIM // 4
ROT_HALF = ROT_DIM // 2
ROPE_THETA = 500000.0
POOL_WINDOWS = (2, 4, 8, 16)
N_POOL_GROUPS = len(POOL_WINDOWS)
POOL_GROUP_DIM = 128
POOL_WIDTH = N_POOL_GROUPS * POOL_GROUP_DIM
MAX_POOL_WINDOW = max(POOL_WINDOWS)
N_EXPERTS = 32
TOP_K = 4
SWIGLU_LIMIT = 7.0
SWIGLU_ALPHA = 1.702
EPS = 1e-6

LANES = 128
SUBLANES = 8
MXU_TILE = 256
VMEM_LIMIT_BYTES = 56 * 1024 * 1024

TOKEN_TILE = 1024
ATTN_TILE = 512
ROUTE_TILE = 512
SORT_TILE = 256
DISPATCH_TILE = SORT_TILE
EXPERT_BLOCK = 512
FF_CHUNK = 512
DISPATCH_CHUNK = 32
COMBINE_CHUNK = 32
COMBINE_SLAB = 512
ROUTE_RECORD = 16
FILL_CHUNK = 16


def _max_extra_chunks(step_tokens, chunk):
    return TOP_K * step_tokens // chunk


def _params(*sem):
    return pltpu.CompilerParams(dimension_semantics=sem, vmem_limit_bytes=VMEM_LIMIT_BYTES)


ROW_SUBLANES = 4
HALF_WIDTH = ROW_SUBLANES * LANES
HIGH_HALF_MASK = 0xFFFF0000


def _tile_rows(start, n):
    return pl.ds(pl.multiple_of(start * ROW_SUBLANES, ROW_SUBLANES), n * ROW_SUBLANES)


def _word_chunk(c, n, start=0):
    return pl.ds(start * ROW_SUBLANES + c, n, stride=ROW_SUBLANES)


def _pack_rows(vals):
    lo = lax.bitcast_convert_type(vals[:, :HALF_WIDTH], U32)
    hi = lax.bitcast_convert_type(vals[:, HALF_WIDTH:], U32)
    return lax.shift_right_logical(lo, jnp.uint32(16)) | hi


def _store_rows(ref, n, words):
    for c in range(ROW_SUBLANES):
        ref[_word_chunk(c, n), :] = words[:, c * LANES:(c + 1) * LANES]


def _load_rows(ref, n, keep=None, start=0):
    lo, hi = [], []
    for c in range(ROW_SUBLANES):
        w = ref[_word_chunk(c, n, start), :]
        if keep is not None:
            w = jnp.where(keep, w, jnp.uint32(0))
        lo.append(lax.bitcast_convert_type(lax.shift_left(w, jnp.uint32(16)), F32))
        hi.append(lax.bitcast_convert_type(w & jnp.uint32(HIGH_HALF_MASK), F32))
    return jnp.concatenate(lo + hi, axis=1).astype(BF16)


def _ada_kernel(c_ref, w_ref, b_ref, o_ref):
    c = c_ref[...]
    c_act = c * (1.0 / (1.0 + jnp.exp(-c)))
    o_ref[0] = jnp.dot(c_act, w_ref[0], preferred_element_type=F32,
                       precision=lax.Precision.HIGHEST) + b_ref[0]


def _ada(c, ada_w, ada_b):
    n_layers, d, d6 = ada_w.shape
    b = c.shape[0]
    tn = 1536
    return pl.pallas_call(
        _ada_kernel,
        out_shape=jax.ShapeDtypeStruct((n_layers, b, d6), F32),
        grid=(n_layers, d6 // tn),
        in_specs=[pl.BlockSpec((b, d), lambda l, j: (0, 0)),
                  pl.BlockSpec((1, d, tn), lambda l, j: (l, 0, j)),
                  pl.BlockSpec((1, 1, tn), lambda l, j: (l, 0, j))],
        out_specs=pl.BlockSpec((1, b, tn), lambda l, j: (l, 0, j)),
        compiler_params=_params("arbitrary", "arbitrary"),
        name="ada",
    )(c, ada_w, ada_b.reshape(n_layers, 1, d6))


def _rope_kernel(pos_ref, freq_ref, eye_ref, tab_ref):
    ang = pos_ref[...].astype(F32) * freq_ref[...]
    parts = []
    for fn in (jnp.cos, jnp.sin):
        val = fn(ang)
        hi = val.astype(BF16)
        parts += [hi, (val - hi.astype(F32)).astype(BF16)]
    record = jnp.concatenate(parts, axis=0)
    tab_ref[...] = lax.dot_general(eye_ref[...], record, (((1,), (1,)), ((), ())),
                                   preferred_element_type=F32).astype(BF16)


def _rope_tables(positions):
    n_tok = positions.size
    inv_freq = ROPE_THETA ** (-jnp.arange(0, ROT_DIM, 2, dtype=F32) / ROT_DIM)
    tr = min(n_tok, 1024)
    eye = jnp.asarray(np.eye(tr, dtype=np.float32), BF16)
    return pl.pallas_call(
        _rope_kernel,
        out_shape=jax.ShapeDtypeStruct((n_tok, 4 * ROT_HALF), BF16),
        grid=(n_tok // tr,),
        in_specs=[pl.BlockSpec((1, tr), lambda i: (0, i)),
                  pl.BlockSpec((ROT_HALF, 1), lambda i: (0, 0)),
                  pl.BlockSpec((tr, tr), lambda i: (0, 0))],
        out_specs=pl.BlockSpec((tr, 4 * ROT_HALF), lambda i: (i, 0)),
        compiler_params=_params("arbitrary"),
        name="rope",
    )(positions.reshape(1, n_tok), inv_freq.reshape(ROT_HALF, 1), eye)


def _rope_select_matrix():
    sel = np.zeros((4 * ROT_HALF, 3 * LANES), np.float32)
    for d in range(LANES):
        dd = d % HEAD_DIM
        if dd < ROT_DIM:
            f = dd % ROT_HALF
            sel[f, d] = 1.0
            sel[ROT_HALF + f, d] = 1.0
        if dd < ROT_HALF:
            sel[2 * ROT_HALF + dd, LANES + d] = -1.0
            sel[3 * ROT_HALF + dd, LANES + d] = -1.0
        elif dd < ROT_DIM:
            sel[2 * ROT_HALF + dd - ROT_HALF, 2 * LANES + d] = 1.0
            sel[3 * ROT_HALF + dd - ROT_HALF, 2 * LANES + d] = 1.0
    unrotated = np.array([[1.0 if (d % HEAD_DIM) >= ROT_DIM else 0.0 for d in range(LANES)]], np.float32)
    return jnp.asarray(sel, BF16), jnp.asarray(unrotated, F32)


def _pair_block_diag(w):
    g, c, _ = w.shape
    zero = jnp.zeros((g // 2, c, c), w.dtype)
    top = jnp.concatenate([w[0::2], zero], axis=2)
    bottom = jnp.concatenate([zero, w[1::2]], axis=2)
    return jnp.concatenate([top, bottom], axis=1)


def _head_mean_matrix(width):
    m = np.zeros((width, width), np.float32)
    for h in range(width // HEAD_DIM):
        m[h * HEAD_DIM:(h + 1) * HEAD_DIM, h * HEAD_DIM:(h + 1) * HEAD_DIM] = 1.0 / HEAD_DIM
    return jnp.asarray(m, BF16)


def _inproj_kernel(x_ref, shift_ref, scale_ref, g1_ref, w_ref, qg_ref, kg_ref, rope_ref, sel_ref,
                   unrot_ref, bdq_ref, bdk_ref, wp_ref, bp_ref, ps_ref,
                   q_ref, k_ref, v_ref, pool_ref, carry_ref, *, ts):
    s_idx = pl.program_id(1)

    @pl.when(s_idx == 0)
    def _():
        carry_ref[...] = jnp.zeros_like(carry_ref)

    x = x_ref[...]
    ms = jnp.mean(x * x, axis=-1, keepdims=True)
    h = x * lax.rsqrt(ms + EPS) * (g1_ref[...] * (1.0 + scale_ref[...])) + shift_ref[...]
    hb = h.astype(BF16)

    tab = jnp.dot(rope_ref[...], sel_ref[...], preferred_element_type=F32)
    cos_t = tab[:, :LANES] + unrot_ref[...]
    sin_a = tab[:, LANES:2 * LANES]
    sin_b = tab[:, 2 * LANES:]

    def norm_rope(t, bd_ref, g_ref, out_scale):
        bw = bd_ref.shape[0]
        sq = (t * t).astype(BF16)
        msq = [jnp.dot(sq[:, i * bw:(i + 1) * bw], bd_ref[...], preferred_element_type=F32)
               for i in range(t.shape[1] // bw)]
        msq = msq[0] if len(msq) == 1 else jnp.concatenate(msq, axis=1)
        tn = t * lax.rsqrt(msq + EPS) * g_ref[...]
        chunks = []
        for c in range(t.shape[1] // LANES):
            tc = tn[:, c * LANES:(c + 1) * LANES]
            rot = (tc * cos_t + pltpu.roll(tc, LANES - ROT_HALF, 1) * sin_a
                   + pltpu.roll(tc, ROT_HALF, 1) * sin_b)
            chunks.append(rot * out_scale)
        return chunks[0] if len(chunks) == 1 else jnp.concatenate(chunks, axis=1)

    q = jnp.dot(hb, w_ref[:, :ATTN_WIDTH], preferred_element_type=F32)
    q_ref[...] = norm_rope(q, bdq_ref, qg_ref, HEAD_DIM ** -0.5).astype(BF16)
    k = jnp.dot(hb, w_ref[:, ATTN_WIDTH:ATTN_WIDTH + KV_WIDTH], preferred_element_type=F32)
    k_ref[...] = norm_rope(k, bdk_ref, kg_ref, 1.0).astype(BF16)
    v = jnp.dot(hb, w_ref[:, ATTN_WIDTH + KV_WIDTH:ATTN_WIDTH + 2 * KV_WIDTH], preferred_element_type=F32)
    v_ref[...] = v.astype(BF16)

    u = jnp.dot(hb, w_ref[:, ATTN_WIDTH + 2 * KV_WIDTH:], preferred_element_type=F32)
    ext = jnp.concatenate([carry_ref[...], u], axis=0)
    carry_ref[...] = u[ts - MAX_POOL_WINDOW:, :]
    count = (lax.broadcasted_iota(I32, (ts, POOL_GROUP_DIM), 0) + (s_idx * ts + 1)).astype(F32)
    outs = []
    for g, w in enumerate(POOL_WINDOWS):
        win = ext[:, g * POOL_GROUP_DIM:(g + 1) * POOL_GROUP_DIM]
        span = 1
        while span < w:
            win = win + pltpu.roll(win, span, 0)
            span *= 2
        mean = win[MAX_POOL_WINDOW:, :] * (1.0 / jnp.minimum(count, float(w)))
        outs.append((mean - u[:, g * POOL_GROUP_DIM:(g + 1) * POOL_GROUP_DIM]).astype(BF16))
    pooled = jnp.concatenate(outs, axis=1)
    pw = wp_ref.shape[1]
    mapped = [jnp.dot(pooled[:, i * pw:(i + 1) * pw], wp_ref[i], preferred_element_type=F32)
              for i in range(wp_ref.shape[0])]
    y = (jnp.concatenate(mapped, axis=1) + bp_ref[...]) * ps_ref[...]
    pool_ref[...] = y.astype(BF16)


def _inproj(x2, shift1, scale1, g1, w_in, qg, kg, rope, sel, unrot, bdq, bdk, wp, bp, ps, *, batch, seq):
    n_tok, d = x2.shape
    ts = min(TOKEN_TILE, seq)
    spt = seq // ts
    row = lambda width: pl.BlockSpec((ts, width), lambda b, s: (b * spt + s, 0))
    per_batch = pl.BlockSpec((None, 1, d), lambda b, s: (b, 0, 0))
    full = lambda a: pl.BlockSpec(a.shape, lambda b, s: (0,) * a.ndim)
    return pl.pallas_call(
        functools.partial(_inproj_kernel, ts=ts),
        out_shape=[jax.ShapeDtypeStruct((n_tok, ATTN_WIDTH), BF16),
                   jax.ShapeDtypeStruct((n_tok, KV_WIDTH), BF16),
                   jax.ShapeDtypeStruct((n_tok, KV_WIDTH), BF16),
                   jax.ShapeDtypeStruct((n_tok, POOL_WIDTH), BF16)],
        grid=(batch, spt),
        in_specs=[row(d), per_batch, per_batch, full(g1), full(w_in), full(qg), full(kg),
                  row(rope.shape[1]), full(sel), full(unrot), full(bdq), full(bdk),
                  full(wp), full(bp), full(ps)],
        out_specs=[row(ATTN_WIDTH), row(KV_WIDTH), row(KV_WIDTH), row(POOL_WIDTH)],
        scratch_shapes=[pltpu.VMEM((MAX_POOL_WINDOW, POOL_WIDTH), F32)],
        compiler_params=_params("arbitrary", "arbitrary"),
        name="inproj",
    )(x2, shift1, scale1, g1, w_in, qg, kg, rope, sel, unrot, bdq, bdk, wp, bp, ps)


def _attn_kernel(sink_ref, q_ref, kc_ref, vc_ref, kp_ref, vp_ref, o_ref, *, tq, tiles_per_seq):
    first = (pl.program_id(0) % tiles_per_seq) == 0
    kfull = jnp.concatenate([kp_ref[...], kc_ref[...]], axis=0)
    vfull = jnp.concatenate([vp_ref[...], vc_ref[...]], axis=0)
    rows = Q_PER_KV * WINDOW
    qi = lax.broadcasted_iota(I32, (rows, WINDOW), 0) % WINDOW
    from_prev = lax.broadcasted_iota(I32, (rows, WINDOW), 1) > qi
    row_head = lax.broadcasted_iota(I32, (rows, 1), 0) // WINDOW
    sinks = []
    for h in range(N_KV_HEADS):
        sink = jnp.zeros((rows, 1), F32)
        for g in range(Q_PER_KV):
            sink = jnp.where(row_head == g, sink_ref[h * Q_PER_KV + g], sink)
        sinks.append(sink)

    problems = [(j, h) for j in range(tq // WINDOW) for h in range(N_KV_HEADS)]
    scores, maxes = [], []
    for j, h in problems:
        qs = jnp.concatenate(
            [q_ref[j * WINDOW:(j + 1) * WINDOW, (h * Q_PER_KV + g) * HEAD_DIM:(h * Q_PER_KV + g + 1) * HEAD_DIM]
             for g in range(Q_PER_KV)], axis=0)
        kb = kfull[j * WINDOW:(j + 2) * WINDOW, h * HEAD_DIM:(h + 1) * HEAD_DIM]
        s = lax.dot_general(qs, kb, (((1,), (1,)), ((), ())), preferred_element_type=F32)
        s_prev = s[:, :WINDOW]
        if j == 0:
            s_prev = jnp.where(first, -jnp.inf, s_prev)
        s = jnp.where(from_prev, s_prev, s[:, WINDOW:])
        scores.append(s)
        maxes.append(jnp.maximum(jnp.max(s, axis=-1, keepdims=True), sinks[h]))
    probs, denoms = [], []
    for (j, h), s, m in zip(problems, scores, maxes):
        p = jnp.exp(s - m)
        denoms.append(jnp.sum(p, axis=-1, keepdims=True) + jnp.exp(sinks[h] - m))
        p = p.astype(BF16)
        zero = jnp.zeros_like(p)
        probs.append(jnp.concatenate([jnp.where(from_prev, p, zero), jnp.where(from_prev, zero, p)], axis=1))
    for (j, h), p, denom in zip(problems, probs, denoms):
        vb = vfull[j * WINDOW:(j + 2) * WINDOW, h * HEAD_DIM:(h + 1) * HEAD_DIM]
        o = jnp.dot(p, vb, preferred_element_type=F32) * (1.0 / denom)
        o_ref[j * WINDOW:(j + 1) * WINDOW, h * Q_PER_KV * HEAD_DIM:(h + 1) * Q_PER_KV * HEAD_DIM] = (
            jnp.concatenate([o[g * WINDOW:(g + 1) * WINDOW, :] for g in range(Q_PER_KV)], axis=1).astype(BF16))


def _attention(q, k, v, sinks, *, seq):
    n_tok = q.shape[0]
    tq = min(ATTN_TILE, seq)
    per = tq // WINDOW
    cur = lambda width: pl.BlockSpec((tq, width), lambda i: (i, 0))
    prev = pl.BlockSpec((WINDOW, KV_WIDTH), lambda i: (jnp.maximum(i * per - 1, 0), 0))
    return pl.pallas_call(
        functools.partial(_attn_kernel, tq=tq, tiles_per_seq=seq // tq),
        out_shape=jax.ShapeDtypeStruct((n_tok, ATTN_WIDTH), BF16),
        grid=(n_tok // tq,),
        in_specs=[pl.BlockSpec(memory_space=pltpu.SMEM), cur(ATTN_WIDTH), cur(KV_WIDTH), cur(KV_WIDTH), prev, prev],
        out_specs=cur(ATTN_WIDTH),
        compiler_params=_params("arbitrary"),
        name="attn",
    )(sinks, q, k, v, k, v)


def _outproj_kernel(attn_ref, pool_ref, x_ref, gate1_ref, shift2_ref, scale2_ref, g2_ref, wa_ref, wpo_ref,
                    rw_ref, rb_ref, tri_ref, eye_ref,
                    xo_ref, h2_ref, idx_ref, rank_ref, route_ref, cnt_ref):
    contract_last = (((1,), (1,)), ((), ()))
    slabs = [pl.ds(r, SORT_TILE) for r in range(0, x_ref.shape[0], SORT_TILE)]
    mixed = [jnp.dot(attn_ref[rows, :], wa_ref[...], preferred_element_type=F32)
             + jnp.dot(pool_ref[rows, :], wpo_ref[...], preferred_element_type=F32) for rows in slabs]
    xs = [x_ref[rows, :] + gate1_ref[...] * m for rows, m in zip(slabs, mixed)]
    for rows, x in zip(slabs, xs):
        xo_ref[rows, :] = x
    norm_gain = g2_ref[...] * (1.0 + scale2_ref[...])
    h2bs = []
    for rows, x in zip(slabs, xs):
        ms = jnp.mean(x * x, axis=-1, keepdims=True)
        h2b = (x * lax.rsqrt(ms + EPS) * norm_gain + shift2_ref[...]).astype(BF16)
        h2_ref[rows, :] = h2b
        h2bs.append(h2b)

    logits = jnp.concatenate([lax.dot_general(rw_ref[...], h2b, contract_last, preferred_element_type=F32)
                              for h2b in h2bs], axis=1) + rb_ref[...]
    expert = lax.broadcasted_iota(I32, logits.shape, 0)
    work = logits
    vals, picks, chosen = [], [], []
    for _ in range(TOP_K):
        m = jnp.max(work, axis=0, keepdims=True)
        pick = jnp.min(jnp.where(work == m, expert, N_EXPERTS), axis=0, keepdims=True)
        sel = expert == pick
        vals.append(m)
        picks.append(pick)
        chosen.append(sel)
        work = jnp.where(sel, -jnp.inf, work)
    exps = [jnp.exp(v - vals[0]) for v in vals]
    inv_total = 1.0 / (exps[0] + exps[1] + exps[2] + exps[3])

    multi = jnp.zeros(logits.shape, F32)
    for sel in chosen:
        multi = multi + sel.astype(F32)
    multi_b = multi.astype(BF16)
    before = jnp.dot(multi_b, tri_ref[...], preferred_element_type=F32)
    ranks = [jnp.sum(jnp.where(sel, before, 0.0), axis=0, keepdims=True) for sel in chosen]
    for kk in range(TOP_K):
        idx_ref[kk:kk + 1, :] = picks[kk]
        rank_ref[kk:kk + 1, :] = ranks[kk].astype(I32)
    record = jnp.concatenate([p.astype(F32) for p in picks] + ranks + [e * inv_total for e in exps]
                             + [jnp.zeros((ROUTE_RECORD - 3 * TOP_K, logits.shape[1]), F32)], axis=0)
    record = record.astype(BF16)
    for s, rows in enumerate(slabs):
        route_ref[rows, :] = lax.dot_general(eye_ref[...], record[:, s * SORT_TILE:(s + 1) * SORT_TILE], contract_last,
                                             preferred_element_type=F32)
    ones = jnp.ones((SUBLANES, SORT_TILE), BF16)
    for s in range(multi.shape[1] // SORT_TILE):
        per_expert = lax.dot_general(ones, multi_b[:, s * SORT_TILE:(s + 1) * SORT_TILE], contract_last,
                                     preferred_element_type=F32)
        cnt_ref[s] = per_expert[0:1, :]


def _outproj(attn, pool, x2, gate1, shift2, scale2, g2, wa, wpo, rw, rb, tri, eye, *, seq):
    n_tok, d = x2.shape
    ts = tri.shape[0]
    spt = seq // ts
    row = lambda width: pl.BlockSpec((ts, width), lambda i: (i, 0))
    per_token = pl.BlockSpec((TOP_K, ts), lambda i: (0, i))
    per_batch = pl.BlockSpec((None, 1, d), lambda i: (i // spt, 0, 0))
    full = lambda a: pl.BlockSpec(a.shape, lambda i: (0,) * a.ndim)
    return pl.pallas_call(
        _outproj_kernel,
        out_shape=[jax.ShapeDtypeStruct((n_tok, d), F32),
                   jax.ShapeDtypeStruct((n_tok, d), BF16),
                   jax.ShapeDtypeStruct((TOP_K, n_tok), I32),
                   jax.ShapeDtypeStruct((TOP_K, n_tok), I32),
                   jax.ShapeDtypeStruct((n_tok, ROUTE_RECORD), F32),
                   jax.ShapeDtypeStruct((n_tok // SORT_TILE, 1, N_EXPERTS), F32)],
        grid=(n_tok // ts,),
        in_specs=[row(ATTN_WIDTH), row(POOL_WIDTH), row(d), per_batch, per_batch, per_batch, full(g2),
                  full(wa), full(wpo), full(rw), full(rb), full(tri), full(eye)],
        out_specs=[row(d), row(d), per_token, per_token, row(ROUTE_RECORD),
                   pl.BlockSpec((ts // SORT_TILE, 1, N_EXPERTS), lambda i: (i, 0, 0))],
        compiler_params=_params("arbitrary"),
        name="outproj",
    )(attn, pool, x2, gate1, shift2, scale2, g2, wa, wpo, rw, rb, tri, eye)


def _chunk_loops(cnt_ref, tile, chunk, body):
    def per_expert(e, carry):
        n_chunks = (cnt_ref[tile * N_EXPERTS + e] + chunk - 1) // chunk

        def per_chunk(i, c):
            body(e, i)
            return c

        return lax.fori_loop(0, n_chunks, per_chunk, carry)

    lax.fori_loop(0, N_EXPERTS, per_expert, 0)


def _run_copies(extra_ref, tile, make_copy, act):
    for e in range(N_EXPERTS):
        getattr(make_copy(tile * N_EXPERTS + e, True), act)()

    def per_extra(p, carry):
        getattr(make_copy(tile * make_copy.max_extra + p, False), act)()
        return carry

    lax.fori_loop(0, extra_ref[tile], per_extra, 0)


def _dispatch_kernel(extra_ref, src_ref, dst_ref, xsrc_ref, xdst_ref, fill_ref, idx_t_ref, rank_t_ref, lo_col_ref,
                     h_ref, xs_ref, stage_ref, sem, *, tt):
    t = pl.program_id(0)
    slot = t % 2
    n_rows = TOP_K * tt
    u = FILL_CHUNK
    m = EXPERT_BLOCK

    @pl.when(t == 0)
    def _():
        stage_ref[...] = jnp.zeros_like(stage_ref)

        def pad_copy(e, i):
            return pltpu.make_async_copy(stage_ref.at[1, _tile_rows(0, u)],
                                         xs_ref.at[_tile_rows(fill_ref[e] - (i + 1) * u, u)], sem)

        def tail_copy(i):
            return pltpu.make_async_copy(stage_ref.at[1, _tile_rows(0, m)],
                                         xs_ref.at[_tile_rows(fill_ref[N_EXPERTS - 1] + i * m, m)], sem)

        n_tail = (xs_ref.shape[0] // ROW_SUBLANES - fill_ref[N_EXPERTS - 1]) // m
        for act in ("start", "wait"):
            _chunk_loops(fill_ref, 1, u, lambda e, i: getattr(pad_copy(e, i), act)())
            lax.fori_loop(0, n_tail, lambda i, c: (getattr(tail_copy(i), act)(), c)[1], 0)

    expert = lax.broadcasted_iota(I32, (N_EXPERTS, tt), 0)
    token = lax.broadcasted_iota(I32, (N_EXPERTS, tt), 1)
    rows = lax.broadcasted_iota(I32, (n_rows, tt), 0)
    run_base = jnp.broadcast_to(lo_col_ref[:, 0:1], (N_EXPERTS, tt))
    for s in range(1, tt // SORT_TILE):
        run_base = jnp.where(token >= s * SORT_TILE, lo_col_ref[:, s:s + 1], run_base)
    hit = None
    for k in range(TOP_K):
        onehot = expert == idx_t_ref[k:k + 1, :]
        base = jnp.sum(jnp.where(onehot, run_base, 0.0), axis=0, keepdims=True)
        slot_row = base.astype(I32) + rank_t_ref[k:k + 1, :]
        match = rows == slot_row
        hit = match if hit is None else (hit | match)
    perm = jnp.where(hit, 1.0, 0.0).astype(BF16)
    grouped = jnp.dot(perm, h_ref[...], preferred_element_type=F32)
    _store_rows(stage_ref.at[slot], n_rows, _pack_rows(grouped))

    def copies_of(tile):
        def make_copy(j, first):
            src = src_ref[j] if first else xsrc_ref[j]
            dst = dst_ref[j] if first else xdst_ref[j]
            return pltpu.make_async_copy(stage_ref.at[tile % 2, _tile_rows(src, DISPATCH_CHUNK)],
                                         xs_ref.at[_tile_rows(dst, DISPATCH_CHUNK)], sem)
        make_copy.max_extra = _max_extra_chunks(tt, DISPATCH_CHUNK)
        return make_copy

    @pl.when(t > 0)
    def _():
        _run_copies(extra_ref, t - 1, copies_of(t - 1), "wait")

    _run_copies(extra_ref, t, copies_of(t), "start")

    @pl.when(t == pl.num_programs(0) - 1)
    def _():
        _run_copies(extra_ref, t, copies_of(t), "wait")


def _dispatch(extra, src, dst, xsrc, xdst, fill, idx_t, rank_t, lo_col, h2, n_slots):
    n_tok, d = h2.shape
    tt = DISPATCH_TILE
    grid_spec = pltpu.PrefetchScalarGridSpec(
        num_scalar_prefetch=6,
        grid=(n_tok // tt,),
        in_specs=[pl.BlockSpec((TOP_K, tt), lambda i, *_: (0, i)),
                  pl.BlockSpec((TOP_K, tt), lambda i, *_: (0, i)),
                  pl.BlockSpec((None, N_EXPERTS, tt // SORT_TILE), lambda i, *_: (i, 0, 0)),
                  pl.BlockSpec((tt, d), lambda i, *_: (i, 0))],
        out_specs=pl.BlockSpec(memory_space=pl.ANY),
        scratch_shapes=[pltpu.VMEM((2, (TOP_K * tt + DISPATCH_CHUNK) * ROW_SUBLANES, LANES), U32),
                        pltpu.SemaphoreType.DMA],
    )
    return pl.pallas_call(
        functools.partial(_dispatch_kernel, tt=tt),
        out_shape=jax.ShapeDtypeStruct((n_slots * ROW_SUBLANES, LANES), U32),
        grid_spec=grid_spec,
        compiler_params=pltpu.CompilerParams(dimension_semantics=("arbitrary",),
                                             vmem_limit_bytes=VMEM_LIMIT_BYTES, has_side_effects=True),
        name="dispatch",
    )(extra, src, dst, xsrc, xdst, fill, idx_t, rank_t, lo_col, h2)


def _experts_kernel(be_ref, nvalid_ref, nused_ref, xs_ref, wgu32_ref, bgu_ref, wd32_ref, bd_ref, yb_ref,
                    wgu_ref, wd_ref, *, d_ff):
    i = pl.program_id(0)

    @pl.when((i == 0) | (be_ref[i] != be_ref[jnp.maximum(i - 1, 0)]))
    def _():
        wgu_ref[...] = wgu32_ref[...].astype(BF16)
        wd_ref[...] = wd32_ref[...].astype(BF16)

    @pl.when(i < nused_ref[0])
    def _():
        m = xs_ref.shape[0] // ROW_SUBLANES
        row = lax.broadcasted_iota(I32, (m, 1), 0)
        xb = _load_rows(xs_ref, m, keep=row < nvalid_ref[i])
        acc = None
        for j in range(d_ff // FF_CHUNK):
            lo, hi = j * FF_CHUNK, (j + 1) * FF_CHUNK
            g = jnp.dot(xb, wgu_ref[:, lo:hi], preferred_element_type=F32) + bgu_ref[:, lo:hi]
            u = jnp.dot(xb, wgu_ref[:, d_ff + lo:d_ff + hi], preferred_element_type=F32) + bgu_ref[:, d_ff + lo:d_ff + hi]
            g = jnp.minimum(g, SWIGLU_LIMIT)
            u = jnp.clip(u, -SWIGLU_LIMIT, SWIGLU_LIMIT)
            act = g * (1.0 / (1.0 + jnp.exp(-SWIGLU_ALPHA * g))) * (u + 1.0)
            part = jnp.dot(act.astype(BF16), wd_ref[lo:hi, :], preferred_element_type=F32)
            acc = part if acc is None else acc + part
        y = (acc + bd_ref[...]).astype(BF16).astype(F32)
        _store_rows(yb_ref, m, _pack_rows(y))

    @pl.when(i >= nused_ref[0])
    def _():
        yb_ref[...] = jnp.zeros_like(yb_ref)


def _experts(block_expert, n_valid, n_used, xs, wgu, bgu, wd, bd, *, layer):
    n_slots = xs.shape[0] // ROW_SUBLANES
    d_ff, d = wd.shape[2], wd.shape[3]
    m = EXPERT_BLOCK
    blk = lambda i, be, nv, nu: (jnp.minimum(i, nu[0] - 1), 0)
    per_expert = lambda i, be, nv, nu: (layer, be[i], 0, 0)
    grid_spec = pltpu.PrefetchScalarGridSpec(
        num_scalar_prefetch=3,
        grid=(n_slots // m,),
        in_specs=[pl.BlockSpec((m * ROW_SUBLANES, LANES), blk),
                  pl.BlockSpec((None, None, d, 2 * d_ff), per_expert),
                  pl.BlockSpec((None, None, 1, 2 * d_ff), per_expert),
                  pl.BlockSpec((None, None, d_ff, d), per_expert),
                  pl.BlockSpec((None, None, 1, d), per_expert)],
        out_specs=pl.BlockSpec((m * ROW_SUBLANES, LANES), lambda i, be, nv, nu: (i, 0)),
        scratch_shapes=[pltpu.VMEM((d, 2 * d_ff), BF16), pltpu.VMEM((d_ff, d), BF16)],
    )
    return pl.pallas_call(
        functools.partial(_experts_kernel, d_ff=d_ff),
        out_shape=jax.ShapeDtypeStruct(xs.shape, U32),
        grid_spec=grid_spec,
        compiler_params=_params("arbitrary"),
        name="experts",
    )(block_expert, n_valid, n_used, xs, wgu, bgu, wd, bd)


def _combine_kernel(extra_ref, src_ref, dst_ref, xsrc_ref, xdst_ref, route_ref, lo_row_ref,
                    x_ref, gate2_ref, yb_ref, xo_ref, stage_ref, sems, *, tt, n_stage):
    t = pl.program_id(0)
    slot = t % 2

    def copies_of(tile):
        def make_copy(j, first):
            src = src_ref[j] if first else xsrc_ref[j]
            dst = dst_ref[j] if first else xdst_ref[j]
            return pltpu.make_async_copy(yb_ref.at[_tile_rows(src, COMBINE_CHUNK)],
                                         stage_ref.at[tile % 2, _tile_rows(dst, COMBINE_CHUNK)], sems.at[tile % 2])
        make_copy.max_extra = _max_extra_chunks(tt, COMBINE_CHUNK)
        return make_copy

    @pl.when(t == 0)
    def _():
        stage_ref[...] = jnp.zeros_like(stage_ref)
        _run_copies(extra_ref, 0, copies_of(0), "start")

    @pl.when(t + 1 < pl.num_programs(0))
    def _():
        _run_copies(extra_ref, t + 1, copies_of(t + 1), "start")

    lane = lax.broadcasted_iota(I32, (tt, N_EXPERTS), 1)
    picked = []
    route = route_ref[...]
    for k in range(TOP_K):
        onehot = lane == route[:, k:k + 1].astype(I32)
        base = jnp.sum(jnp.where(onehot, lo_row_ref[...], 0.0), axis=1, keepdims=True)
        col = (base + route[:, TOP_K + k:TOP_K + k + 1]).astype(I32)
        picked.append((col, route[:, 2 * TOP_K + k:2 * TOP_K + k + 1]))

    _run_copies(extra_ref, t, copies_of(t), "wait")

    def slab_product(s):
        cols = lax.broadcasted_iota(I32, (tt, COMBINE_SLAB), 1) + s * COMBINE_SLAB
        weights = jnp.zeros((tt, COMBINE_SLAB), F32)
        for col, gate in picked:
            weights = jnp.where(cols == col, gate, weights)
        staged = _load_rows(stage_ref.at[slot], COMBINE_SLAB, start=s * COMBINE_SLAB)
        return jnp.dot(weights.astype(BF16), staged, preferred_element_type=F32)

    y = slab_product(0)
    for s in range(1, n_stage // COMBINE_SLAB):
        y = y + slab_product(s)
    xo_ref[...] = x_ref[...] + gate2_ref[...] * y


def _combine(extra, src, dst, xsrc, xdst, route, lo_row, x2, gate2, yb, *, seq):
    n_tok, d = x2.shape
    tt = SORT_TILE
    spt = seq // tt
    n_stage = TOP_K * tt + N_EXPERTS * COMBINE_CHUNK
    grid_spec = pltpu.PrefetchScalarGridSpec(
        num_scalar_prefetch=5,
        grid=(n_tok // tt,),
        in_specs=[pl.BlockSpec((tt, ROUTE_RECORD), lambda i, *_: (i, 0)),
                  pl.BlockSpec((None, 1, N_EXPERTS), lambda i, *_: (i, 0, 0)),
                  pl.BlockSpec((tt, d), lambda i, *_: (i, 0)),
                  pl.BlockSpec((None, 1, d), lambda i, *_: (i // spt, 0, 0)),
                  pl.BlockSpec(memory_space=pl.ANY)],
        out_specs=pl.BlockSpec((tt, d), lambda i, *_: (i, 0)),
        scratch_shapes=[pltpu.VMEM((2, n_stage * ROW_SUBLANES, LANES), U32),
                        pltpu.SemaphoreType.DMA((2,))],
    )
    return pl.pallas_call(
        functools.partial(_combine_kernel, tt=tt, n_stage=n_stage),
        out_shape=jax.ShapeDtypeStruct((n_tok, d), F32),
        grid_spec=grid_spec,
        compiler_params=_params("arbitrary"),
        name="combine",
    )(extra, src, dst, xsrc, xdst, route, lo_row, x2, gate2, yb)


def _exclusive_cumsum(a, axis):
    return jnp.cumsum(a, axis=axis) - a


def _region_tables(sort_cnt):
    m = EXPERT_BLOCK
    total = jnp.sum(sort_cnt, axis=0)
    padded = (total + max(DISPATCH_CHUNK, COMBINE_CHUNK) + m - 1) // m * m
    pends = jnp.cumsum(padded)
    return total, pends - padded, pends


def _chunk_tables(step_cnt, chunk, step_tokens):
    n_chunks = jnp.maximum((step_cnt + chunk - 1) // chunk, 1)
    lo_chunked = _exclusive_cumsum(n_chunks * chunk, 1)
    n_extra = n_chunks - 1
    ends = jnp.cumsum(n_extra, axis=1)
    p = jnp.arange(_max_extra_chunks(step_tokens, chunk), dtype=I32)
    x_expert = jnp.minimum(jnp.sum(p[None, :, None] >= ends[:, None, :], axis=2), N_EXPERTS - 1)
    x_offset = (p[None, :] - jnp.take_along_axis(ends - n_extra, x_expert, axis=1) + 1) * chunk
    return lo_chunked, ends[:, -1], x_expert, x_offset


def kernel(x, c, positions, ada_w, ada_b, norm1_g, w_in, q_norm_g, k_norm_g, attn_sinks, pool_w, pool_b,
           pool_scale, w_out, norm2_g, router_w, router_b, expert_w_gu, expert_b_gu, expert_w_down,
           expert_b_down):
    batch, seq, d = x.shape
    n_layers = ada_w.shape[0]
    n_tok = batch * seq
    d_ff = expert_w_down.shape[2]
    assert d == SUBLANES * LANES
    assert seq % min(TOKEN_TILE, seq) == 0 and seq % min(ATTN_TILE, seq) == 0 and ATTN_TILE % WINDOW == 0
    assert seq % ROUTE_TILE == 0 and ROUTE_TILE % SORT_TILE == 0 and DISPATCH_TILE % SORT_TILE == 0

    mod = _ada(c, ada_w, ada_b).reshape(n_layers, batch, 6, 1, d)
    rope = _rope_tables(positions)
    sel, unrot = _rope_select_matrix()
    bdq, bdk = _head_mean_matrix(MXU_TILE), _head_mean_matrix(KV_WIDTH)
    tri = jnp.asarray(np.kron(np.eye(ROUTE_TILE // SORT_TILE, dtype=np.float32),
                              np.triu(np.ones((SORT_TILE, SORT_TILE), np.float32), 1)), BF16)
    eye = jnp.asarray(np.eye(SORT_TILE, dtype=np.float32), BF16)

    m = EXPERT_BLOCK
    slack = max(DISPATCH_CHUNK, COMBINE_CHUNK)
    n_blocks = -(-(n_tok * TOP_K + N_EXPERTS * (slack + m - 1)) // m)
    n_slots = n_blocks * m
    n_tiles = n_tok // SORT_TILE

    x2 = x.reshape(n_tok, d)
    for l in range(n_layers):
        shift1, scale1, gate1, shift2, scale2, gate2 = (mod[l, :, j] for j in range(6))
        q, k, v, pool = _inproj(
            x2, shift1, scale1, norm1_g[l].reshape(1, d), w_in[l].astype(BF16),
            jnp.tile(q_norm_g[l], N_Q_HEADS).reshape(1, ATTN_WIDTH),
            jnp.tile(k_norm_g[l], N_KV_HEADS).reshape(1, KV_WIDTH),
            rope, sel, unrot, bdq, bdk, _pair_block_diag(pool_w[l]).astype(BF16), pool_b[l].reshape(1, POOL_WIDTH),
            pool_scale[l].reshape(1, POOL_WIDTH), batch=batch, seq=seq)
        attn = _attention(q, k, v, attn_sinks[l], seq=seq)
        w_out_b = w_out[l].astype(BF16)
        x2, h2, idx_t, rank_t, route, tile_cnt = _outproj(
            attn, pool, x2, gate1, shift2, scale2, norm2_g[l].reshape(1, d),
            w_out_b[:ATTN_WIDTH], w_out_b[ATTN_WIDTH:], router_w[l].T.astype(BF16),
            router_b[l].reshape(N_EXPERTS, 1),
            tri, eye, seq=seq)

        sort_cnt = tile_cnt.reshape(n_tiles, N_EXPERTS).astype(I32)
        total, pstart, pends = _region_tables(sort_cnt)
        flat = lambda a: a.reshape(-1).astype(I32)
        n_used = (pends[-1] // m).astype(I32).reshape(1)
        block_row = jnp.minimum(jnp.arange(n_blocks, dtype=I32), n_used[0] - 1) * m
        block_expert = jnp.minimum(jnp.sum(block_row[:, None] >= pends[None, :], axis=1), N_EXPERTS - 1).astype(I32)
        n_valid = jnp.clip((pstart + total)[block_expert] - block_row, 0, m).astype(I32)
        gstart = pstart[None, :] + _exclusive_cumsum(sort_cnt, 0)

        per = DISPATCH_TILE // SORT_TILE
        pair_cnt = sort_cnt.reshape(n_tiles // per, per, N_EXPERTS)
        step_cnt = jnp.sum(pair_cnt, axis=1)
        d_lo = _exclusive_cumsum(step_cnt, 1)
        d_gs = gstart[::per]
        _, d_extra, d_xe, d_xo = _chunk_tables(step_cnt, DISPATCH_CHUNK, DISPATCH_TILE)
        d_rows = lambda table: flat(jnp.take_along_axis(table, d_xe, axis=1) + d_xo)
        run_base = (d_lo[:, None, :] + _exclusive_cumsum(pair_cnt, 1)).transpose(0, 2, 1).astype(F32)
        fill = jnp.concatenate([pends, pends - pstart - total]).astype(I32)
        xs = _dispatch(flat(d_extra), flat(d_lo), flat(d_gs), d_rows(d_lo), d_rows(d_gs), fill,
                       idx_t, rank_t, run_base, h2, n_slots)
        yb = _experts(block_expert, n_valid, n_used, xs, expert_w_gu,
                      expert_b_gu.reshape(n_layers, N_EXPERTS, 1, 2 * d_ff), expert_w_down,
                      expert_b_down.reshape(n_layers, N_EXPERTS, 1, d), layer=l)
        c_lo, c_extra, c_xe, c_xo = _chunk_tables(sort_cnt, COMBINE_CHUNK, SORT_TILE)
        c_rows = lambda table: flat(jnp.take_along_axis(table, c_xe, axis=1) + c_xo)
        x2 = _combine(flat(c_extra), flat(gstart), flat(c_lo), c_rows(gstart), c_rows(c_lo),
                      route, c_lo.astype(F32).reshape(n_tiles, 1, N_EXPERTS), x2, gate2, yb, seq=seq)
    return x2.reshape(batch, seq, d)
```
